```python
import jax, jax.numpy as jnp
from jax import lax
import numpy as np

D_MODEL = 1024
BATCH = 16
SEQ = 2048
DEPTH = 1
DEC_BATCH = 16
DEC_SEQ = 16
PAST_LEN = 1024

CHUNK = 64
D_MIX = D_MODEL
D_A = D_MIX // 2
A_GROUPS = 4
A_GROUP_DIM = D_A // A_GROUPS
GMLP_CHUNK = 128
D_B = D_MIX - D_A
B_HEADS = 8
B_HEAD_DIM = D_B // B_HEADS
CONV_WIDTH = 4
LRU_C = 8.0
N_GROUPS = 4
EXPERTS_PER_GROUP = 8
N_EXPERTS = N_GROUPS * EXPERTS_PER_GROUP
TOP_K_INNER = 2
D_EXPERT = D_MODEL // 2
MOE_BLOCK = 128
EPS = 1e-6

kernel_name = "hymba_gmlp_rglru_hier_moe_stream_step"


def rms_norm(x, g):
    xf = x.astype(jnp.float32)
    y = xf * lax.rsqrt(jnp.mean(xf * xf, axis=-1, keepdims=True) + EPS)
    return (y * g.astype(jnp.float32)).astype(x.dtype)


def group_layer_norm(v, g, b):
    vf = v.astype(jnp.float32)
    mu = jnp.mean(vf, axis=-1, keepdims=True)
    var = jnp.mean(jnp.square(vf - mu), axis=-1, keepdims=True)
    y = (vf - mu) * lax.rsqrt(var + EPS)
    y = y * g.astype(jnp.float32).reshape(A_GROUPS, A_GROUP_DIM) + b.astype(jnp.float32).reshape(A_GROUPS, A_GROUP_DIM)
    return y.astype(v.dtype)


def gmlp_spatial_gate(u, v, ln_g, ln_b, w_s, b_s):
    B, T, _ = v.shape
    L = min(T, GMLP_CHUNK)
    vn = group_layer_norm(v.reshape(B, T, A_GROUPS, A_GROUP_DIM), ln_g, ln_b)
    mask = jnp.tril(jnp.ones((L, L), dtype=bool))
    w = jnp.where(mask, w_s[:, :L, :L], jnp.zeros((), w_s.dtype))
    vc = vn.reshape(B, T // L, L, A_GROUPS, A_GROUP_DIM)
    s = jnp.einsum('gts,bcsgd->bctgd', w, vc) + b_s[:, :L].T[None, None, :, :, None]
    out = u * s.reshape(B, T, D_A).astype(u.dtype)
    return out, vn.reshape(B, T, D_A)


def causal_depthwise_conv(xpad, w, b):
    T = xpad.shape[1] - (CONV_WIDTH - 1)
    y = b
    for k in range(CONV_WIDTH):
        y = y + xpad[:, k:k + T] * w[k]
    return y


def _lru_combine(c1, c2):
    a1, b1 = c1
    a2, b2 = c2
    return a1 * a2, a2 * b1 + b2


def rglru(xc, h0, w_a, b_a, w_x, b_x, lam):
    B, T, _ = xc.shape
    xh = xc.astype(jnp.float32).reshape(B, T, B_HEADS, B_HEAD_DIM)
    r = jax.nn.sigmoid(jnp.einsum('bthi,hij->bthj', xh, w_a.astype(jnp.float32)) + b_a.astype(jnp.float32)).reshape(B, T, D_B)
    i = jax.nn.sigmoid(jnp.einsum('bthi,hij->bthj', xh, w_x.astype(jnp.float32)) + b_x.astype(jnp.float32)).reshape(B, T, D_B)
    log_a = -LRU_C * r * jax.nn.softplus(-lam.astype(jnp.float32))
    a = jnp.exp(log_a)
    gain = jnp.sqrt(-jnp.expm1(2.0 * log_a))
    bterm = gain * i * xh.reshape(B, T, D_B)
    a_cum, h = lax.associative_scan(_lru_combine, (a, bterm), axis=1)
    h = h + a_cum * h0.astype(jnp.float32)[:, None]
    return h, h[:, -1]


def expert_ffn(xb, wg, wu, wd):
    return (jax.nn.silu(xb @ wg) * (xb @ wu)) @ wd


def hier_moe(x, rg_w, rg_b, re_w, re_b, w_gate, w_up, w_down):
    N, D = x.shape
    xf = x.astype(jnp.float32)
    g_logits = xf @ rg_w.astype(jnp.float32) + rg_b.astype(jnp.float32)
    g_prob = jax.nn.softmax(g_logits, axis=-1)
    grp = jnp.argmax(g_logits, axis=-1).astype(jnp.int32)
    p_grp = jnp.take_along_axis(g_prob, grp[:, None], axis=1)
    e_logits = (xf @ re_w.astype(jnp.float32) + re_b.astype(jnp.float32)).reshape(N, N_GROUPS, EXPERTS_PER_GROUP)
    e_in = jnp.take_along_axis(e_logits, grp[:, None, None], axis=1)[:, 0]
    top_v, top_i = lax.top_k(e_in, TOP_K_INNER)
    gates = p_grp * jax.nn.softmax(top_v, axis=-1)
    expert = grp[:, None] * EXPERTS_PER_GROUP + top_i.astype(jnp.int32)
    A = N * TOP_K_INNER
    flat_e = expert.reshape(-1)
    order = jnp.argsort(flat_e, stable=True)
    sorted_e = flat_e[order]
    tok_sorted = (order // TOP_K_INNER).astype(jnp.int32)
    sizes = jnp.bincount(flat_e, length=N_EXPERTS)
    padded = (sizes + MOE_BLOCK - 1) // MOE_BLOCK * MOE_BLOCK
    starts = jnp.cumsum(sizes) - sizes
    pad_ends = jnp.cumsum(padded)
    pad_starts = pad_ends - padded
    dest = pad_starts[sorted_e] + jnp.arange(A, dtype=jnp.int32) - starts[sorted_e]
    n_blocks = -(-A // MOE_BLOCK) + N_EXPERTS
    P = n_blocks * MOE_BLOCK
    slot_tok = jnp.full((P,), N, dtype=jnp.int32).at[dest].set(tok_sorted)
    x_ext = jnp.concatenate([x, jnp.zeros((1, D), x.dtype)], axis=0)
    xb = x_ext[slot_tok].reshape(n_blocks, MOE_BLOCK, D)
    blk_expert = jnp.minimum(jnp.searchsorted(pad_ends, jnp.arange(n_blocks) * MOE_BLOCK, side='right'), N_EXPERTS - 1)
    yb = lax.map(lambda a: expert_ffn(a[0], w_gate[a[1]], w_up[a[1]], w_down[a[1]]), (xb, blk_expert))
    y_sorted = yb.reshape(P, D)[dest]
    g_sorted = gates.reshape(-1)[order]
    y = jax.ops.segment_sum(y_sorted.astype(jnp.float32) * g_sorted[:, None], tok_sorted, num_segments=N)
    return y.astype(x.dtype)


def encoder_layer(x, conv_cache, h0, lp):
    B, T, D = x.shape
    h = rms_norm(x, lp['norm1_g'])
    z = h @ lp['w_in']
    u_a = z[..., :D_A]
    v_a = z[..., D_A:2 * D_A]
    x_b = z[..., 2 * D_A:2 * D_A + D_B]
    g_b = z[..., 2 * D_A + D_B:]
    out_a, v_rows = gmlp_spatial_gate(u_a, v_a, lp['gmlp_ln_g'], lp['gmlp_ln_b'], lp['gmlp_w_s'], lp['gmlp_b_s'])
    xpad = jnp.concatenate([conv_cache.astype(x_b.dtype), x_b], axis=1)
    xc = causal_depthwise_conv(xpad, lp['conv_w'], lp['conv_b'])
    new_conv = xpad[:, -(CONV_WIDTH - 1):]
    hseq, h_last = rglru(xc, h0, lp['lru_w_a'], lp['lru_b_a'], lp['lru_w_x'], lp['lru_b_x'], lp['lru_lambda'])
    out_b = hseq.astype(x.dtype) * jax.nn.gelu(g_b)
    x = x + jnp.concatenate([out_a, out_b], axis=-1) @ lp['w_out']
    h2 = rms_norm(x, lp['norm2_g'])
    moe = hier_moe(h2.reshape(B * T, D), lp['router_group_w'], lp['router_group_b'], lp['router_expert_w'], lp['router_expert_b'],
                   lp['expert_w_gate'], lp['expert_w_up'], lp['expert_w_down'])
    x = x + moe.reshape(B, T, D)
    return x, v_rows, new_conv, h_last


def setup_inputs(seed: int = 0) -> dict:
    key = jax.random.key(seed)
    ks = jax.random.split(key, 24)
    f32 = jnp.float32
    nrm = lambda k, shape, scale: jax.random.normal(k, shape, f32) * scale
    u = jax.random.uniform(ks[12], (DEPTH, D_B), f32, 0.9, 0.999)
    s = u ** (1.0 / LRU_C)
    lru_lambda = jnp.log(s) - jnp.log1p(-s)
    return {
        "x_prompt": nrm(ks[0], (BATCH, SEQ, D_MODEL), 1.0),
        "x_sample": nrm(ks[1], (DEC_BATCH, DEC_SEQ, D_MODEL), 1.0),
        "state_conv": nrm(ks[2], (DEPTH, DEC_BATCH, CONV_WIDTH - 1, D_B), 1.0),
        "state_rglru": nrm(ks[3], (DEPTH, DEC_BATCH, D_B), 1.0),
        "norm1_g": 1.0 + nrm(ks[4], (DEPTH, D_MODEL), 0.02),
        "w_in": nrm(ks[5], (DEPTH, D_MODEL, 2 * D_A + 2 * D_B), D_MODEL ** -0.5),
        "gmlp_ln_g": 1.0 + nrm(ks[6], (DEPTH, D_A), 0.02),
        "gmlp_ln_b": nrm(ks[7], (DEPTH, D_A), 0.02),
        "gmlp_w_s": nrm(ks[8], (DEPTH, A_GROUPS, GMLP_CHUNK, GMLP_CHUNK), GMLP_CHUNK ** -0.5),
        "gmlp_b_s": 1.0 + nrm(ks[9], (DEPTH, A_GROUPS, GMLP_CHUNK), 0.01),
        "conv_w": nrm(ks[10], (DEPTH, CONV_WIDTH, D_B), CONV_WIDTH ** -0.5),
        "conv_b": nrm(ks[11], (DEPTH, D_B), 0.01),
        "lru_w_a": nrm(ks[13], (DEPTH, B_HEADS, B_HEAD_DIM, B_HEAD_DIM), B_HEAD_DIM ** -0.5),
        "lru_b_a": nrm(ks[14], (DEPTH, B_HEADS, B_HEAD_DIM), 0.01),
        "lru_w_x": nrm(ks[15], (DEPTH, B_HEADS, B_HEAD_DIM, B_HEAD_DIM), B_HEAD_DIM ** -0.5),
        "lru_b_x": nrm(ks[16], (DEPTH, B_HEADS, B_HEAD_DIM), 0.01),
        "lru_lambda": lru_lambda,
        "w_out": nrm(ks[17], (DEPTH, D_MIX, D_MODEL), D_MIX ** -0.5),
        "norm2_g": 1.0 + nrm(ks[18], (DEPTH, D_MODEL), 0.02),
        "router_group_w": nrm(ks[19], (DEPTH, D_MODEL, N_GROUPS), D_MODEL ** -0.5),
        "router_group_b": nrm(ks[20], (DEPTH, N_GROUPS), 0.01),
        "router_expert_w": nrm(ks[21], (DEPTH, D_MODEL, N_EXPERTS), D_MODEL ** -0.5),
        "router_expert_b": nrm(ks[22], (DEPTH, N_EXPERTS), 0.01),
        "expert_w_gate": nrm(jax.random.fold_in(ks[23], 0), (DEPTH, N_EXPERTS, D_MODEL, D_EXPERT), D_MODEL ** -0.5),
        "expert_w_up": nrm(jax.random.fold_in(ks[23], 1), (DEPTH, N_EXPERTS, D_MODEL, D_EXPERT), D_MODEL ** -0.5),
        "expert_w_down": nrm(jax.random.fold_in(ks[23], 2), (DEPTH, N_EXPERTS, D_EXPERT, D_MODEL), D_EXPERT ** -0.5),
        "final_norm_g": 1.0 + nrm(jax.random.fold_in(ks[23], 3), (D_MODEL,), 0.02),
    }


def reference(x_prompt, x_sample, state_conv, state_rglru, norm1_g, w_in, gmlp_ln_g, gmlp_ln_b, gmlp_w_s, gmlp_b_s,
              conv_w, conv_b, lru_w_a, lru_b_a, lru_w_x, lru_b_x, lru_lambda, w_out, norm2_g,
              router_group_w, router_group_b, router_expert_w, router_expert_b,
              expert_w_gate, expert_w_up, expert_w_down, final_norm_g):
    yp, ys = x_prompt, x_sample
    conv_p, h_p, v_s, conv_s, h_s = [], [], [], [], []
    zero_conv = jnp.zeros((x_prompt.shape[0], CONV_WIDTH - 1, D_B), x_prompt.dtype)
    zero_h = jnp.zeros((x_prompt.shape[0], D_B), jnp.float32)
    for l in range(DEPTH):
        lp = dict(norm1_g=norm1_g[l], w_in=w_in[l], gmlp_ln_g=gmlp_ln_g[l], gmlp_ln_b=gmlp_ln_b[l],
                  gmlp_w_s=gmlp_w_s[l], gmlp_b_s=gmlp_b_s[l], conv_w=conv_w[l], conv_b=conv_b[l],
                  lru_w_a=lru_w_a[l], lru_b_a=lru_b_a[l], lru_w_x=lru_w_x[l], lru_b_x=lru_b_x[l],
                  lru_lambda=lru_lambda[l], w_out=w_out[l], norm2_g=norm2_g[l],
                  router_group_w=router_group_w[l], router_group_b=router_group_b[l],
                  router_expert_w=router_expert_w[l], router_expert_b=router_expert_b[l],
                  expert_w_gate=expert_w_gate[l], expert_w_up=expert_w_up[l], expert_w_down=expert_w_down[l])
        yp, _, cp, hp = encoder_layer(yp, zero_conv, zero_h, lp)
        conv_p.append(cp)
        h_p.append(hp)
        ys, vs, cs, hs = encoder_layer(ys, state_conv[l], state_rglru[l], lp)
        v_s.append(vs)
        conv_s.append(cs)
        h_s.append(hs)
    y_prompt = rms_norm(yp, final_norm_g)
    y_sample = rms_norm(ys, final_norm_g)
    return (y_prompt, y_sample, jnp.stack(conv_p), jnp.stack(h_p), jnp.stack(v_s), jnp.stack(conv_s), jnp.stack(h_s))
```

```python
import functools

import jax
import jax.numpy as jnp
from jax import lax
from jax.experimental import pallas as pl
from jax.experimental.pallas import tpu as pltpu

D_MODEL = 1024
D_A = 512
D_B = 512
A_GROUPS = 4
A_GROUP_DIM = 128
GMLP_CHUNK = 128
B_HEADS = 8
CONV_WIDTH = 4
LRU_C = 8.0
N_GROUPS = 4
EXPERTS_PER_GROUP = 8
N_EXPERTS = 32
D_EXPERT = 512
EPS = 1e-6

SUBLANES = 8
LANES = 128
ROUTER_ROWS = 40
TOKEN_TILE = 256
PROMPT_TILE = 512
FFN_BLOCK = 256
SLOT_ROWS = 2 * TOKEN_TILE + N_EXPERTS * SUBLANES
XB_COLS = D_MODEL + LANES
VMEM_LIMIT = 56 * 1024 * 1024

BF16 = jnp.bfloat16
F32 = jnp.float32


def _rms(x, g):
    return x * lax.rsqrt(jnp.mean(x * x, axis=-1, keepdims=True) + EPS) * g


def _dot(a, b):
    return jnp.dot(a, b, preferred_element_type=F32)


def _mixer_kernel(x_ref, hist_ref, h0_ref, n1g_ref, win_ref, lng_ref, lnb_ref, wsm_ref, bsx_ref,
                  cw_ref, cb_ref, wa_ref, ba_ref, wx_ref, bx_ref, lam_ref, wout_ref, n2g_ref,
                  wr_ref, br_ref, *rest, sb, tt, chunk, seg_len, seg_stride, chain, emit_vn, aliased):
    if aliased:
        rest = rest[3:]
    if emit_vn:
        (y1_ref, h2_ref, route_ref, cnt_ref, conv_ref, hlast_ref, vn_ref,
         xs_ref, sa_ref, sb_ref, mix_ref, hc_ref) = rest
    else:
        (y1_ref, h2_ref, route_ref, cnt_ref, conv_ref, hlast_ref,
         xs_ref, sa_ref, sb_ref, mix_ref, hc_ref) = rest
        vn_ref = None
    t_idx = pl.program_id(1)
    rows = sb * tt
    n_seg = rows // seg_len

    x = x_ref[...].reshape(rows, D_MODEL)
    hb = _rms(x, n1g_ref[...]).astype(BF16)

    u_a = _dot(hb, win_ref[:, 0:D_A])
    v_a = _dot(hb, win_ref[:, D_A:2 * D_A])
    for g in range(A_GROUPS):
        cols = slice(g * A_GROUP_DIM, (g + 1) * A_GROUP_DIM)
        vg = v_a[:, cols]
        mu = jnp.mean(vg, axis=-1, keepdims=True)
        dv = vg - mu
        var = jnp.mean(dv * dv, axis=-1, keepdims=True)
        vn = dv * lax.rsqrt(var + EPS) * lng_ref[:, cols] + lnb_ref[:, cols]
        if vn_ref is not None:
            vn_ref[:, :, cols] = vn.reshape(sb, tt, A_GROUP_DIM)
        vnb = vn.astype(BF16)
        for c in range(rows // chunk):
            rs = slice(c * chunk, (c + 1) * chunk)
            s = _dot(wsm_ref[g], vnb[rs]) + bsx_ref[:, cols]
            mix_ref[rs, cols] = (u_a[rs, cols] * s).astype(BF16)

    x_b = _dot(hb, win_ref[:, 2 * D_A:2 * D_A + D_B])

    @pl.when(t_idx == 0)
    def _():
        xs_ref[:, SUBLANES - 3:SUBLANES, :] = hist_ref[...]
        hc_ref[...] = jnp.zeros_like(hc_ref)

    @pl.when(t_idx != 0)
    def _():
        xs_ref[:, SUBLANES - 3:SUBLANES, :] = xs_ref[:, SUBLANES + tt - 3:SUBLANES + tt, :]

    xs_ref[:, SUBLANES:SUBLANES + tt, :] = x_b.reshape(sb, tt, D_B)
    conv_ref[...] = xs_ref[:, SUBLANES + tt - 3:SUBLANES + tt, :]
    xc = cb_ref[...] + x_b * cw_ref[3:4, :]
    for j in range(1, CONV_WIDTH):
        shifted = xs_ref[:, SUBLANES - j:SUBLANES - j + tt, :].reshape(rows, D_B)
        xc = xc + shifted * cw_ref[3 - j:4 - j, :]

    xcb = xc.astype(BF16)
    r = jax.nn.sigmoid(_dot(xcb, wa_ref[...]) + ba_ref[...])
    i_g = jax.nn.sigmoid(_dot(xcb, wx_ref[...]) + bx_ref[...])
    log_a = (-LRU_C * r) * jax.nn.softplus(-lam_ref[...])
    a = jnp.exp(log_a)
    gain = jnp.sqrt(1.0 - a * a)
    bterm = gain * i_g * xc

    n_lb = D_B // LANES
    for j in range(n_seg):
        for k in range(n_lb):
            lc = slice(k * LANES, (k + 1) * LANES)
            sa_ref[k, j * seg_stride:j * seg_stride + seg_len, :] = a[j * seg_len:(j + 1) * seg_len, lc]
            sb_ref[k, j * seg_stride:j * seg_stride + seg_len, :] = bterm[j * seg_len:(j + 1) * seg_len, lc]

    g_b = _dot(hb, win_ref[:, 2 * D_A + D_B:])
    gel = jax.nn.gelu(g_b)

    for grp in range(n_seg // SUBLANES):
        base = grp * SUBLANES * seg_stride
        bs = slice(grp * SUBLANES, (grp + 1) * SUBLANES)
        if chain:
            h_init = tuple(jnp.zeros((SUBLANES, LANES), F32) for _ in range(n_lb))
        else:
            h_init = tuple(h0_ref[0, bs, k * LANES:(k + 1) * LANES] for k in range(n_lb))
        a_init = tuple(jnp.ones((SUBLANES, LANES), F32) for _ in range(n_lb))

        def step(i, carry, base=base):
            hs, acs = carry
            idx = pl.ds(base + i, SUBLANES, stride=seg_stride)
            new_h, new_a = [], []
            for k in range(n_lb):
                av = sa_ref[k, idx, :]
                h = av * hs[k] + sb_ref[k, idx, :]
                sb_ref[k, idx, :] = h
                new_h.append(h)
                if chain:
                    ac = av * acs[k]
                    sa_ref[k, idx, :] = ac
                    new_a.append(ac)
                else:
                    new_a.append(acs[k])
            return tuple(new_h), tuple(new_a)

        h_end, a_end = lax.fori_loop(0, seg_len, step, (h_init, a_init), unroll=8)

        for k in range(n_lb):
            lc = slice(k * LANES, (k + 1) * LANES)
            mc = slice(D_A + k * LANES, D_A + (k + 1) * LANES)
            if chain:
                h_in = hc_ref[:, lc]
                for j in range(SUBLANES):
                    seg = slice(base + j * seg_stride, base + j * seg_stride + seg_len)
                    rs = slice((grp * SUBLANES + j) * seg_len, (grp * SUBLANES + j + 1) * seg_len)
                    h_seg = sb_ref[k, seg, :] + sa_ref[k, seg, :] * h_in
                    mix_ref[rs, mc] = (h_seg * gel[rs, lc]).astype(BF16)
                    h_in = h_end[k][j:j + 1] + a_end[k][j:j + 1] * h_in
                hc_ref[:, lc] = h_in
                hlast_ref[0, :, lc] = h_in
            else:
                for j in range(SUBLANES):
                    seg = slice(base + j * seg_stride, base + j * seg_stride + seg_len)
                    rs = slice((grp * SUBLANES + j) * seg_len, (grp * SUBLANES + j + 1) * seg_len)
                    mix_ref[rs, mc] = (sb_ref[k, seg, :] * gel[rs, lc]).astype(BF16)
                hlast_ref[0, bs, lc] = h_end[k]

    y1 = x + _dot(mix_ref[...], wout_ref[...])
    y1_ref[...] = y1
    h2 = _rms(y1, n2g_ref[...]).astype(BF16)
    h2_ref[...] = h2

    logits = lax.dot_general(wr_ref[...], h2, (((1,), (1,)), ((), ())), preferred_element_type=F32) + br_ref[...]
    gl = [logits[N_EXPERTS + k:N_EXPERTS + k + 1, :] for k in range(N_GROUPS)]
    gmax = jnp.maximum(jnp.maximum(gl[0], gl[1]), jnp.maximum(gl[2], gl[3]))
    grp_idx = jnp.where(gl[0] == gmax, 0, jnp.where(gl[1] == gmax, 1, jnp.where(gl[2] == gmax, 2, 3)))
    gsum = (jnp.exp(gl[0] - gmax) + jnp.exp(gl[1] - gmax)) + (jnp.exp(gl[2] - gmax) + jnp.exp(gl[3] - gmax))
    p_grp = 1.0 / gsum
    e_in = jnp.where(grp_idx == 0, logits[0:8],
                     jnp.where(grp_idx == 1, logits[8:16], jnp.where(grp_idx == 2, logits[16:24], logits[24:32])))
    iota8 = lax.broadcasted_iota(jnp.int32, (EXPERTS_PER_GROUP, rows), 0)
    m1 = jnp.max(e_in, axis=0, keepdims=True)
    i1 = jnp.min(jnp.where(e_in == m1, iota8, EXPERTS_PER_GROUP), axis=0, keepdims=True)
    e_rest = jnp.where(iota8 == i1, -jnp.inf, e_in)
    m2 = jnp.max(e_rest, axis=0, keepdims=True)
    i2 = jnp.min(jnp.where(e_rest == m2, iota8, EXPERTS_PER_GROUP), axis=0, keepdims=True)
    t2 = jnp.exp(m2 - m1)
    den = 1.0 + t2
    gate0 = p_grp * (1.0 / den)
    gate1 = p_grp * (t2 / den)
    e0 = (grp_idx * EXPERTS_PER_GROUP + i1).astype(F32)
    e1 = (grp_idx * EXPERTS_PER_GROUP + i2).astype(F32)
    route_ref[...] = jnp.where(iota8 == 0, e0, jnp.where(iota8 == 1, e1, jnp.where(iota8 == 2, gate0,
                               jnp.where(iota8 == 3, gate1, 0.0))))

    iota_e = lax.broadcasted_iota(jnp.int32, (N_EXPERTS, rows), 0).astype(F32)
    onehot = (iota_e == e0).astype(F32) + (iota_e == e1).astype(F32)
    lane = lax.broadcasted_iota(jnp.int32, (N_EXPERTS, LANES), 1)
    cnt = jnp.zeros((N_EXPERTS, LANES), F32)
    for s in range(rows // TOKEN_TILE):
        c_s = jnp.sum(onehot[:, s * TOKEN_TILE:(s + 1) * TOKEN_TILE], axis=1, keepdims=True)
        cnt = cnt + jnp.where(lane == s, c_s, 0.0)
    cnt_ref[...] = cnt.reshape(1, N_EXPERTS, LANES)


def _mixer_call(x, hist, h0, wts, *, n_total, row_offset, sb, tt, chunk, seg_len, seg_stride, chain,
                emit_vn, alias_in=None):
    nb, t_len, _ = x.shape
    nbb = nb // sb
    ntt = t_len // tt
    rows = sb * tt
    blk0 = row_offset // rows
    n_seg = rows // seg_len

    def full(arr):
        nd = arr.ndim
        return pl.BlockSpec(arr.shape, lambda b, t, _nd=nd: (0,) * _nd)

    in_specs = [
        pl.BlockSpec((sb, tt, D_MODEL), lambda b, t: (b, t, 0)),
        pl.BlockSpec((sb, CONV_WIDTH - 1, D_B), lambda b, t: (b, 0, 0)),
        pl.BlockSpec((1, sb, D_B), lambda b, t: (b, 0, 0)),
    ] + [full(w) for w in wts]
    args = [x, hist, h0] + list(wts)
    io_alias = {}
    if alias_in is not None:
        for k, arr in enumerate(alias_in):
            io_alias[len(args)] = k
            in_specs.append(pl.BlockSpec(memory_space=pl.ANY))
            args.append(arr)

    out_shape = [
        jax.ShapeDtypeStruct((n_total, D_MODEL), F32),
        jax.ShapeDtypeStruct((n_total, D_MODEL), BF16),
        jax.ShapeDtypeStruct((SUBLANES, n_total), F32),
        jax.ShapeDtypeStruct((nbb * ntt, N_EXPERTS, LANES), F32),
        jax.ShapeDtypeStruct((nb, CONV_WIDTH - 1, D_B), F32),
        jax.ShapeDtypeStruct((nbb, sb, D_B), F32),
    ]
    out_specs = [
        pl.BlockSpec((rows, D_MODEL), lambda b, t: (blk0 + b * ntt + t, 0)),
        pl.BlockSpec((rows, D_MODEL), lambda b, t: (blk0 + b * ntt + t, 0)),
        pl.BlockSpec((SUBLANES, rows), lambda b, t: (0, blk0 + b * ntt + t)),
        pl.BlockSpec((1, N_EXPERTS, LANES), lambda b, t: (b * ntt + t, 0, 0)),
        pl.BlockSpec((sb, CONV_WIDTH - 1, D_B), lambda b, t: (b, 0, 0)),
        pl.BlockSpec((1, sb, D_B), lambda b, t: (b, 0, 0)),
    ]
    if emit_vn:
        out_shape.append(jax.ShapeDtypeStruct((nb, t_len, D_A), F32))
        out_specs.append(pl.BlockSpec((sb, tt, D_A), lambda b, t: (b, t, 0)))

    kern = functools.partial(_mixer_kernel, sb=sb, tt=tt, chunk=chunk, seg_len=seg_len, seg_stride=seg_stride,
                             chain=chain, emit_vn=emit_vn, aliased=alias_in is not None)
    return pl.pallas_call(
        kern,
        grid=(nbb, ntt),
        in_specs=in_specs,
        out_specs=out_specs,
        out_shape=out_shape,
        scratch_shapes=[
            pltpu.VMEM((sb, SUBLANES + tt, D_B), F32),
            pltpu.VMEM((D_B // LANES, n_seg * seg_stride, LANES), F32),
            pltpu.VMEM((D_B // LANES, n_seg * seg_stride, LANES), F32),
            pltpu.VMEM((rows, D_MODEL), BF16),
            pltpu.VMEM((1, D_B), F32),
        ],
        input_output_aliases=io_alias,
        compiler_params=pltpu.CompilerParams(dimension_semantics=("arbitrary", "arbitrary"),
                                             vmem_limit_bytes=VMEM_LIMIT),
        name="mixer_chain" if chain else "mixer_step",
    )(*args)


def _tile_slots(route_ref, locoff_ref, upper_ref):
    e0 = route_ref[0:1, :]
    e1 = route_ref[1:2, :]
    iota_e = lax.broadcasted_iota(jnp.int32, (N_EXPERTS, TOKEN_TILE), 0).astype(F32)
    oh0 = (iota_e == e0).astype(F32)
    oh1 = (iota_e == e1).astype(F32)
    c0 = _dot(oh0.astype(BF16), upper_ref[...])
    c1 = _dot(oh1.astype(BF16), upper_ref[...])
    cnt0 = jnp.sum(oh0, axis=1, keepdims=True)
    base0 = locoff_ref[0]
    base1 = base0 + cnt0
    p0 = jnp.sum(oh0 * (base0 + c0), axis=0, keepdims=True)
    p1 = jnp.sum(oh1 * (base1 + c1), axis=0, keepdims=True)
    return p0.astype(jnp.int32), p1.astype(jnp.int32)


def _chunk_loop(tile, nch_ref, doff_ref, make_copy):
    def per_expert(e, loc):
        n = nch_ref[tile * N_EXPERTS + e]
        d0 = doff_ref[tile * N_EXPERTS + e]

        def per_chunk(c, _):
            make_copy(pl.multiple_of(loc + c * SUBLANES, SUBLANES), pl.multiple_of(d0 + c * SUBLANES, SUBLANES))
            return 0

        lax.fori_loop(0, n, per_chunk, 0)
        return loc + n * SUBLANES

    lax.fori_loop(0, N_EXPERTS, per_expert, 0)


def _dispatch_kernel(nch_ref, doff_ref, ntot_ref, tail_ref, ntail_ref,
                     h2_ref, route_ref, locoff_ref, upper_ref, xb_ref, buf_ref, zero_ref, sem, zsem):
    i = pl.program_id(0)
    n_tiles = pl.num_programs(0)
    slot = i % 2

    def chunk_copy(s, loc, dst):
        return pltpu.make_async_copy(buf_ref.at[s, pl.ds(loc, SUBLANES), :], xb_ref.at[pl.ds(dst, SUBLANES), :],
                                     sem.at[s])

    def wait_tile(tile, s):
        def body(c, _):
            chunk_copy(s, 0, 0).wait()
            return 0
        lax.fori_loop(0, ntot_ref[tile], body, 0)

    @pl.when(i >= 2)
    def _():
        wait_tile(i - 2, slot)

    p0, p1 = _tile_slots(route_ref, locoff_ref, upper_ref)
    iota_s = lax.broadcasted_iota(jnp.int32, (SLOT_ROWS, TOKEN_TILE), 0)
    hit0 = iota_s == p0
    hit1 = iota_s == p1
    perm = jnp.where(hit0 | hit1, 1.0, 0.0).astype(BF16)
    buf_ref[slot, :, 0:D_MODEL] = _dot(perm, h2_ref[...])
    gates = jnp.where(hit0, route_ref[2:3, :], 0.0) + jnp.where(hit1, route_ref[3:4, :], 0.0)
    gcol = jnp.sum(gates, axis=1, keepdims=True)
    buf_ref[slot, :, D_MODEL:XB_COLS] = jnp.broadcast_to(gcol, (SLOT_ROWS, LANES))

    _chunk_loop(i, nch_ref, doff_ref, lambda loc, dst: chunk_copy(slot, loc, dst).start())

    @pl.when(i == n_tiles - 1)
    def _():
        zero_ref[...] = jnp.zeros_like(zero_ref)

        def zero_copy(dst):
            return pltpu.make_async_copy(zero_ref, xb_ref.at[pl.ds(dst, SUBLANES), :], zsem.at[0])

        def per_expert(e, tot):
            n = ntail_ref[e]
            d0 = tail_ref[e]

            def per_chunk(c, _):
                zero_copy(pl.multiple_of(d0 + c * SUBLANES, SUBLANES)).start()
                return 0

            lax.fori_loop(0, n, per_chunk, 0)
            return tot + n

        total = lax.fori_loop(0, N_EXPERTS, per_expert, 0)

        @pl.when(i >= 1)
        def _():
            wait_tile(i - 1, 1 - slot)

        wait_tile(i, slot)

        def zwait(c, _):
            zero_copy(0).wait()
            return 0

        lax.fori_loop(0, total, zwait, 0)


def _dispatch_call(h2, route, locoff, upper, nch, doff, ntot, tail, ntail, *, n_tiles, p_rows):
    grid_spec = pltpu.PrefetchScalarGridSpec(
        num_scalar_prefetch=5,
        grid=(n_tiles,),
        in_specs=[
            pl.BlockSpec((TOKEN_TILE, D_MODEL), lambda i, *_: (i, 0)),
            pl.BlockSpec((SUBLANES, TOKEN_TILE), lambda i, *_: (0, i)),
            pl.BlockSpec((1, N_EXPERTS, 1), lambda i, *_: (i, 0, 0)),
            pl.BlockSpec((TOKEN_TILE, TOKEN_TILE), lambda i, *_: (0, 0)),
        ],
        out_specs=pl.BlockSpec(memory_space=pl.ANY),
        scratch_shapes=[
            pltpu.VMEM((2, SLOT_ROWS, XB_COLS), F32),
            pltpu.VMEM((SUBLANES, XB_COLS), F32),
            pltpu.SemaphoreType.DMA((2,)),
            pltpu.SemaphoreType.DMA((1,)),
        ],
    )
    return pl.pallas_call(
        _dispatch_kernel,
        grid_spec=grid_spec,
        out_shape=jax.ShapeDtypeStruct((p_rows, XB_COLS), F32),
        compiler_params=pltpu.CompilerParams(dimension_semantics=("arbitrary",), vmem_limit_bytes=VMEM_LIMIT),
        name="dispatch",
    )(nch, doff, ntot, tail, ntail, h2, route, locoff, upper)


def _ffn_kernel(blk_e_ref, nused_ref, xb_ref, wg_ref, wu_ref, wd_ref, yb_ref):
    i = pl.program_id(0)

    @pl.when(i < nused_ref[0])
    def _():
        xb = xb_ref[:, 0:D_MODEL].astype(BF16)
        gate = xb_ref[:, D_MODEL:D_MODEL + 1]
        a = _dot(xb, wg_ref[0])
        u = _dot(xb, wu_ref[0])
        mid = (jax.nn.silu(a) * u).astype(BF16)
        yb_ref[...] = _dot(mid, wd_ref[0]) * gate


def _ffn_call(xb, wg, wu, wd, blk_e, nused, *, n_blocks):
    def row_map(i, blk_e_ref, nused_ref):
        return (jnp.minimum(i, nused_ref[0] - 1), 0)

    def w_map(i, blk_e_ref, nused_ref):
        return (blk_e_ref[i], 0, 0)

    grid_spec = pltpu.PrefetchScalarGridSpec(
        num_scalar_prefetch=2,
        grid=(n_blocks,),
        in_specs=[
            pl.BlockSpec((FFN_BLOCK, XB_COLS), row_map),
            pl.BlockSpec((1, D_MODEL, D_EXPERT), w_map),
            pl.BlockSpec((1, D_MODEL, D_EXPERT), w_map),
            pl.BlockSpec((1, D_EXPERT, D_MODEL), w_map),
        ],
        out_specs=pl.BlockSpec((FFN_BLOCK, D_MODEL), row_map),
    )
    return pl.pallas_call(
        _ffn_kernel,
        grid_spec=grid_spec,
        out_shape=jax.ShapeDtypeStruct((n_blocks * FFN_BLOCK, D_MODEL), F32),
        compiler_params=pltpu.CompilerParams(dimension_semantics=("arbitrary",), vmem_limit_bytes=VMEM_LIMIT),
        name="expert_ffn",
    )(blk_e, nused, xb, wg, wu, wd)


def _combine_kernel(nch_ref, doff_ref, ntot_ref, y1_ref, route_ref, locoff_ref, upper_ref, fg_ref, yb_ref,
                    out_ref, buf_ref, sem, *, tile0):
    i = pl.program_id(0)
    n_tiles = pl.num_programs(0)
    slot = i % 2

    def chunk_copy(s, loc, src):
        return pltpu.make_async_copy(yb_ref.at[pl.ds(src, SUBLANES), :], buf_ref.at[s, pl.ds(loc, SUBLANES), :],
                                     sem.at[s])

    def issue_tile(tile, s):
        _chunk_loop(tile, nch_ref, doff_ref, lambda loc, src: chunk_copy(s, loc, src).start())

    @pl.when(i == 0)
    def _():
        buf_ref[...] = jnp.zeros_like(buf_ref)
        issue_tile(tile0, 0)

    @pl.when(i + 1 < n_tiles)
    def _():
        issue_tile(tile0 + i + 1, 1 - slot)

    p0, p1 = _tile_slots(route_ref, locoff_ref, upper_ref)
    iota_s = lax.broadcasted_iota(jnp.int32, (SLOT_ROWS, TOKEN_TILE), 0)
    perm = jnp.where((iota_s == p0) | (iota_s == p1), 1.0, 0.0).astype(BF16)

    def body(c, _):
        chunk_copy(slot, 0, 0).wait()
        return 0

    lax.fori_loop(0, ntot_ref[tile0 + i], body, 0)

    yb = buf_ref[slot]
    hi = yb.astype(BF16)
    r1 = yb - hi.astype(F32)
    mid = r1.astype(BF16)
    lo = (r1 - mid.astype(F32)).astype(BF16)
    dn = (((0,), (0,)), ((), ()))
    moe = (lax.dot_general(perm, hi, dn, preferred_element_type=F32)
           + lax.dot_general(perm, mid, dn, preferred_element_type=F32)
           + lax.dot_general(perm, lo, dn, preferred_element_type=F32))
    out_ref[...] = _rms(y1_ref[...] + moe, fg_ref[...])


def _combine_call(y1, route, locoff, upper, fg, yb, nch, doff, ntot, *, tile0, n_tiles):
    grid_spec = pltpu.PrefetchScalarGridSpec(
        num_scalar_prefetch=3,
        grid=(n_tiles,),
        in_specs=[
            pl.BlockSpec((TOKEN_TILE, D_MODEL), lambda i, *_: (tile0 + i, 0)),
            pl.BlockSpec((SUBLANES, TOKEN_TILE), lambda i, *_: (0, tile0 + i)),
            pl.BlockSpec((1, N_EXPERTS, 1), lambda i, *_: (tile0 + i, 0, 0)),
            pl.BlockSpec((TOKEN_TILE, TOKEN_TILE), lambda i, *_: (0, 0)),
            pl.BlockSpec((1, D_MODEL), lambda i, *_: (0, 0)),
            pl.BlockSpec(memory_space=pl.ANY),
        ],
        out_specs=pl.BlockSpec((TOKEN_TILE, D_MODEL), lambda i, *_: (i, 0)),
        scratch_shapes=[
            pltpu.VMEM((2, SLOT_ROWS, D_MODEL), F32),
            pltpu.SemaphoreType.DMA((2,)),
        ],
    )
    return pl.pallas_call(
        functools.partial(_combine_kernel, tile0=tile0),
        grid_spec=grid_spec,
        out_shape=jax.ShapeDtypeStruct((n_tiles * TOKEN_TILE, D_MODEL), F32),
        compiler_params=pltpu.CompilerParams(dimension_semantics=("arbitrary",), vmem_limit_bytes=VMEM_LIMIT),
        name="combine",
    )(nch, doff, ntot, y1, route, locoff, upper, fg, yb)


def _block_diag(w):
    h, d, _ = w.shape
    eye = jnp.eye(h, dtype=w.dtype)
    return (eye[:, None, :, None] * w[:, :, None, :]).reshape(h * d, h * d)


def _mixer_weights(l, length, reps, norm1_g, w_in, gmlp_ln_g, gmlp_ln_b, gmlp_w_s, gmlp_b_s, conv_w, conv_b,
                   lru_w_a, lru_b_a, lru_w_x, lru_b_x, lru_lambda, w_out, norm2_g,
                   router_group_w, router_group_b, router_expert_w, router_expert_b):
    mask = jnp.tril(jnp.ones((length, length), dtype=bool))
    ws = jnp.where(mask, gmlp_w_s[l][:, :length, :length], 0.0)
    eye = jnp.eye(reps, dtype=ws.dtype)
    wsm = (eye[None, :, None, :, None] * ws[:, None, :, None, :]).reshape(A_GROUPS, reps * length, reps * length)
    bs = jnp.tile(gmlp_b_s[l][:, :length], (1, reps))
    bsx = jnp.repeat(bs.T, A_GROUP_DIM, axis=1)
    wr = jnp.concatenate([router_expert_w[l].T, router_group_w[l].T,
                          jnp.zeros((ROUTER_ROWS - N_EXPERTS - N_GROUPS, D_MODEL), F32)], axis=0)
    br = jnp.concatenate([router_expert_b[l], router_group_b[l],
                          jnp.zeros((ROUTER_ROWS - N_EXPERTS - N_GROUPS,), F32)]).reshape(ROUTER_ROWS, 1)
    return [
        norm1_g[l].reshape(1, D_MODEL), w_in[l].astype(BF16),
        gmlp_ln_g[l].reshape(1, D_A), gmlp_ln_b[l].reshape(1, D_A), wsm.astype(BF16), bsx,
        conv_w[l], conv_b[l].reshape(1, D_B),
        _block_diag(lru_w_a[l]).astype(BF16), lru_b_a[l].reshape(1, D_B),
        _block_diag(lru_w_x[l]).astype(BF16), lru_b_x[l].reshape(1, D_B),
        lru_lambda[l].reshape(1, D_B), w_out[l].astype(BF16), norm2_g[l].reshape(1, D_MODEL),
        wr.astype(BF16), br,
    ]


def _routing_tables(cnt, n_blocks):
    seg = (cnt + SUBLANES - 1) // SUBLANES * SUBLANES
    nch = seg // SUBLANES
    tot = jnp.sum(seg, axis=0)
    padded = (tot + FFN_BLOCK - 1) // FFN_BLOCK * FFN_BLOCK
    pad_end = jnp.cumsum(padded)
    pad_start = pad_end - padded
    doff = pad_start[None, :] + jnp.cumsum(seg, axis=0) - seg
    locoff = jnp.cumsum(seg, axis=1) - seg
    ntot = jnp.sum(nch, axis=1)
    tail = pad_start + tot
    ntail = (padded - tot) // SUBLANES
    blk_end = pad_end // FFN_BLOCK
    nused = blk_end[-1]
    blk = jnp.minimum(jnp.arange(n_blocks, dtype=jnp.int32), nused - 1)
    blk_e = jnp.minimum(jnp.searchsorted(blk_end, blk, side='right'), N_EXPERTS - 1)
    i32 = lambda a: a.astype(jnp.int32)
    return (i32(nch.reshape(-1)), i32(doff.reshape(-1)), i32(ntot), i32(tail), i32(ntail),
            locoff.astype(F32)[:, :, None], i32(blk_e), i32(nused.reshape(1)))


def kernel(x_prompt, x_sample, state_conv, state_rglru, norm1_g, w_in, gmlp_ln_g, gmlp_ln_b, gmlp_w_s, gmlp_b_s,
           conv_w, conv_b, lru_w_a, lru_b_a, lru_w_x, lru_b_x, lru_lambda, w_out, norm2_g,
           router_group_w, router_group_b, router_expert_w, router_expert_b,
           expert_w_gate, expert_w_up, expert_w_down, final_norm_g):
    depth = w_in.shape[0]
    nb, t_len, _ = x_prompt.shape
    db, dt, _ = x_sample.shape
    n_prompt = nb * t_len
    n_sample = db * dt
    assert t_len % PROMPT_TILE == 0 and n_prompt % TOKEN_TILE == 0 and n_sample == TOKEN_TILE
    n_total = n_prompt + n_sample
    n_tiles = n_total // TOKEN_TILE
    p_tiles = n_prompt // TOKEN_TILE
    p_rows = 2 * n_total + (SUBLANES - 1) * N_EXPERTS * n_tiles + N_EXPERTS * (FFN_BLOCK - 1)
    n_blocks = -(-p_rows // FFN_BLOCK)
    p_rows = n_blocks * FFN_BLOCK
    upper = jnp.triu(jnp.ones((TOKEN_TILE, TOKEN_TILE), BF16), 1)

    xp, xs = x_prompt, x_sample
    conv_p, h_p, v_s, conv_s, h_s = [], [], [], [], []
    for l in range(depth):
        lw = (norm1_g, w_in, gmlp_ln_g, gmlp_ln_b, gmlp_w_s, gmlp_b_s, conv_w, conv_b, lru_w_a, lru_b_a,
              lru_w_x, lru_b_x, lru_lambda, w_out, norm2_g, router_group_w, router_group_b,
              router_expert_w, router_expert_b)
        wts_p = _mixer_weights(l, GMLP_CHUNK, 1, *lw)
        wts_s = _mixer_weights(l, dt, db, *lw)
        zero_conv = jnp.zeros((nb, CONV_WIDTH - 1, D_B), F32)
        zero_h = jnp.zeros((nb, 1, D_B), F32)
        y1, h2, route, cnt_p, cp, hp = _mixer_call(
            xp, zero_conv, zero_h, wts_p, n_total=n_total, row_offset=0, sb=1, tt=PROMPT_TILE, chunk=GMLP_CHUNK,
            seg_len=PROMPT_TILE // SUBLANES, seg_stride=PROMPT_TILE // SUBLANES + SUBLANES, chain=True,
            emit_vn=False)
        y1, h2, route, cnt_s, cs, hs, vs = _mixer_call(
            xs, state_conv[l], state_rglru[l][None], wts_s, n_total=n_total, row_offset=n_prompt, sb=db,
            tt=dt, chunk=db * dt, seg_len=dt, seg_stride=dt + SUBLANES, chain=False, emit_vn=True,
            alias_in=(y1, h2, route))
        conv_p.append(cp)
        h_p.append(hp[:, 0])
        v_s.append(vs)
        conv_s.append(cs)
        h_s.append(hs[0])

        sub = PROMPT_TILE // TOKEN_TILE
        cnt = jnp.concatenate([
            jnp.swapaxes(cnt_p[:, :, :sub], 1, 2).reshape(p_tiles, N_EXPERTS),
            cnt_s[:, :, 0]], axis=0).astype(jnp.int32)
        nch, doff, ntot, tail, ntail, locoff, blk_e, nused = _routing_tables(cnt, n_blocks)

        xb = _dispatch_call(h2, route, locoff, upper, nch, doff, ntot, tail, ntail, n_tiles=n_tiles, p_rows=p_rows)
        yb = _ffn_call(xb, expert_w_gate[l].astype(BF16), expert_w_up[l].astype(BF16),
                       expert_w_down[l].astype(BF16), blk_e, nused, n_blocks=n_blocks)
        last = l == depth - 1
        fg = (final_norm_g if last else jnp.ones_like(final_norm_g)).reshape(1, D_MODEL)
        assert last, "deeper stacks need an un-normalised combine between layers"
        yp = _combine_call(y1, route, locoff, upper, fg, yb, nch, doff, ntot, tile0=0, n_tiles=p_tiles)
        ysm = _combine_call(y1, route, locoff, upper, fg, yb, nch, doff, ntot, tile0=p_tiles,
                            n_tiles=n_tiles - p_tiles)
        xp = yp.reshape(nb, t_len, D_MODEL)
        xs = ysm.reshape(db, dt, D_MODEL)

    return (xp, xs, jnp.stack(conv_p), jnp.stack(h_p), jnp.stack(v_s), jnp.stack(conv_s), jnp.stack(h_s))
```

```python
import functools

import jax
import jax.numpy as jnp
from jax import lax
from jax.experimental import pallas as pl
from jax.experimental.pallas import tpu as pltpu

D_MODEL = 1024
D_A = 512
D_B = 512
A_GROUPS = 4
A_GROUP_DIM = 128
GMLP_CHUNK = 128
B_HEADS = 8
CONV_WIDTH = 4
LRU_C = 8.0
N_GROUPS = 4
EXPERTS_PER_GROUP = 8
N_EXPERTS = 32
D_EXPERT = 512
EPS = 1e-6

SUBLANES = 8
LANES = 128
ROUTER_ROWS = 40
TOKEN_TILE = 256
PROMPT_TILE = 512
FFN_BLOCK = 512
ZERO_CHUNKS = 8
ZERO_ROWS = ZERO_CHUNKS * SUBLANES
SLOT_ROWS = 2 * TOKEN_TILE + N_EXPERTS * SUBLANES
XB_COLS = D_MODEL + LANES
VMEM_LIMIT = 56 * 1024 * 1024

BF16 = jnp.bfloat16
F32 = jnp.float32


def _rms(x, g):
    return x * lax.rsqrt(jnp.mean(x * x, axis=-1, keepdims=True) + EPS) * g


def _dot(a, b):
    return jnp.dot(a, b, preferred_element_type=F32)


def _mixer_kernel(x_ref, hist_ref, h0_ref, n1g_ref, win_ref, lng_ref, lnb_ref, wsm_ref, bsx_ref,
                  cw_ref, cb_ref, wa_ref, ba_ref, wx_ref, bx_ref, lam_ref, wout_ref, n2g_ref,
                  wr_ref, br_ref, *rest, sb, tt, chunk, seg_len, seg_stride, chain, emit_vn, aliased):
    if aliased:
        rest = rest[3:]
    if emit_vn:
        (y1_ref, h2_ref, route_ref, cnt_ref, conv_ref, hlast_ref, vn_ref,
         xs_ref, sa_ref, sb_ref, mix_ref, hc_ref) = rest
    else:
        (y1_ref, h2_ref, route_ref, cnt_ref, conv_ref, hlast_ref,
         xs_ref, sa_ref, sb_ref, mix_ref, hc_ref) = rest
        vn_ref = None
    t_idx = pl.program_id(1)
    rows = sb * tt
    n_seg = rows // seg_len

    x = x_ref[...].reshape(rows, D_MODEL)
    hb = _rms(x, n1g_ref[...]).astype(BF16)

    u_a = _dot(hb, win_ref[:, 0:D_A])
    v_a = _dot(hb, win_ref[:, D_A:2 * D_A])
    for g in range(A_GROUPS):
        cols = slice(g * A_GROUP_DIM, (g + 1) * A_GROUP_DIM)
        vg = v_a[:, cols]
        mu = jnp.mean(vg, axis=-1, keepdims=True)
        dv = vg - mu
        var = jnp.mean(dv * dv, axis=-1, keepdims=True)
        vn = dv * lax.rsqrt(var + EPS) * lng_ref[:, cols] + lnb_ref[:, cols]
        if vn_ref is not None:
            vn_ref[:, :, cols] = vn.reshape(sb, tt, A_GROUP_DIM)
        vnb = vn.astype(BF16)
        for c in range(rows // chunk):
            rs = slice(c * chunk, (c + 1) * chunk)
            s = _dot(wsm_ref[g], vnb[rs]) + bsx_ref[:, cols]
            mix_ref[rs, cols] = (u_a[rs, cols] * s).astype(BF16)

    x_b = _dot(hb, win_ref[:, 2 * D_A:2 * D_A + D_B])

    @pl.when(t_idx == 0)
    def _():
        xs_ref[:, SUBLANES - 3:SUBLANES, :] = hist_ref[...]
        hc_ref[...] = jnp.zeros_like(hc_ref)

    @pl.when(t_idx != 0)
    def _():
        xs_ref[:, SUBLANES - 3:SUBLANES, :] = xs_ref[:, SUBLANES + tt - 3:SUBLANES + tt, :]

    xs_ref[:, SUBLANES:SUBLANES + tt, :] = x_b.reshape(sb, tt, D_B)
    conv_ref[...] = xs_ref[:, SUBLANES + tt - 3:SUBLANES + tt, :]
    xc = cb_ref[...] + x_b * cw_ref[3:4, :]
    for j in range(1, CONV_WIDTH):
        shifted = xs_ref[:, SUBLANES - j:SUBLANES - j + tt, :].reshape(rows, D_B)
        xc = xc + shifted * cw_ref[3 - j:4 - j, :]

    xcb = xc.astype(BF16)
    r = jax.nn.sigmoid(_dot(xcb, wa_ref[...]) + ba_ref[...])
    i_g = jax.nn.sigmoid(_dot(xcb, wx_ref[...]) + bx_ref[...])
    log_a = (-LRU_C * r) * jax.nn.softplus(-lam_ref[...])
    a = jnp.exp(log_a)
    gain = jnp.sqrt(1.0 - a * a)
    bterm = gain * i_g * xc

    n_lb = D_B // LANES
    for j in range(n_seg):
        for k in range(n_lb):
            lc = slice(k * LANES, (k + 1) * LANES)
            sa_ref[k, j * seg_stride:j * seg_stride + seg_len, :] = a[j * seg_len:(j + 1) * seg_len, lc]
            sb_ref[k, j * seg_stride:j * seg_stride + seg_len, :] = bterm[j * seg_len:(j + 1) * seg_len, lc]

    g_b = _dot(hb, win_ref[:, 2 * D_A + D_B:])
    gel = jax.nn.gelu(g_b)

    for grp in range(n_seg // SUBLANES):
        base = grp * SUBLANES * seg_stride
        bs = slice(grp * SUBLANES, (grp + 1) * SUBLANES)
        if chain:
            h_init = tuple(jnp.zeros((SUBLANES, LANES), F32) for _ in range(n_lb))
        else:
            h_init = tuple(h0_ref[0, bs, k * LANES:(k + 1) * LANES] for k in range(n_lb))
        a_init = tuple(jnp.ones((SUBLANES, LANES), F32) for _ in range(n_lb))

        def step(i, carry, base=base):
            hs, acs = carry
            idx = pl.ds(base + i, SUBLANES, stride=seg_stride)
            new_h, new_a = [], []
            for k in range(n_lb):
                av = sa_ref[k, idx, :]
                h = av * hs[k] + sb_ref[k, idx, :]
                sb_ref[k, idx, :] = h
                new_h.append(h)
                if chain:
                    ac = av * acs[k]
                    sa_ref[k, idx, :] = ac
                    new_a.append(ac)
                else:
                    new_a.append(acs[k])
            return tuple(new_h), tuple(new_a)

        h_end, a_end = lax.fori_loop(0, seg_len, step, (h_init, a_init), unroll=8)

        for k in range(n_lb):
            lc = slice(k * LANES, (k + 1) * LANES)
            mc = slice(D_A + k * LANES, D_A + (k + 1) * LANES)
            if chain:
                h_in = hc_ref[:, lc]
                for j in range(SUBLANES):
                    seg = slice(base + j * seg_stride, base + j * seg_stride + seg_len)
                    rs = slice((grp * SUBLANES + j) * seg_len, (grp * SUBLANES + j + 1) * seg_len)
                    h_seg = sb_ref[k, seg, :] + sa_ref[k, seg, :] * h_in
                    mix_ref[rs, mc] = (h_seg * gel[rs, lc]).astype(BF16)
                    h_in = h_end[k][j:j + 1] + a_end[k][j:j + 1] * h_in
                hc_ref[:, lc] = h_in
                hlast_ref[0, :, lc] = h_in
            else:
                for j in range(SUBLANES):
                    seg = slice(base + j * seg_stride, base + j * seg_stride + seg_len)
                    rs = slice((grp * SUBLANES + j) * seg_len, (grp * SUBLANES + j + 1) * seg_len)
                    mix_ref[rs, mc] = (sb_ref[k, seg, :] * gel[rs, lc]).astype(BF16)
                hlast_ref[0, bs, lc] = h_end[k]

    y1 = x + _dot(mix_ref[...], wout_ref[...])
    y1_ref[...] = y1
    h2 = _rms(y1, n2g_ref[...]).astype(BF16)
    h2_ref[...] = h2

    logits = lax.dot_general(wr_ref[...], h2, (((1,), (1,)), ((), ())), preferred_element_type=F32) + br_ref[...]
    gl = [logits[N_EXPERTS + k:N_EXPERTS + k + 1, :] for k in range(N_GROUPS)]
    gmax = jnp.maximum(jnp.maximum(gl[0], gl[1]), jnp.maximum(gl[2], gl[3]))
    grp_idx = jnp.where(gl[0] == gmax, 0, jnp.where(gl[1] == gmax, 1, jnp.where(gl[2] == gmax, 2, 3)))
    gsum = (jnp.exp(gl[0] - gmax) + jnp.exp(gl[1] - gmax)) + (jnp.exp(gl[2] - gmax) + jnp.exp(gl[3] - gmax))
    p_grp = 1.0 / gsum
    e_in = jnp.where(grp_idx == 0, logits[0:8],
                     jnp.where(grp_idx == 1, logits[8:16], jnp.where(grp_idx == 2, logits[16:24], logits[24:32])))
    iota8 = lax.broadcasted_iota(jnp.int32, (EXPERTS_PER_GROUP, rows), 0)
    m1 = jnp.max(e_in, axis=0, keepdims=True)
    i1 = jnp.min(jnp.where(e_in == m1, iota8, EXPERTS_PER_GROUP), axis=0, keepdims=True)
    e_rest = jnp.where(iota8 == i1, -jnp.inf, e_in)
    m2 = jnp.max(e_rest, axis=0, keepdims=True)
    i2 = jnp.min(jnp.where(e_rest == m2, iota8, EXPERTS_PER_GROUP), axis=0, keepdims=True)
    t2 = jnp.exp(m2 - m1)
    den = 1.0 + t2
    gate0 = p_grp * (1.0 / den)
    gate1 = p_grp * (t2 / den)
    e0 = (grp_idx * EXPERTS_PER_GROUP + i1).astype(F32)
    e1 = (grp_idx * EXPERTS_PER_GROUP + i2).astype(F32)
    route_ref[...] = jnp.where(iota8 == 0, e0, jnp.where(iota8 == 1, e1, jnp.where(iota8 == 2, gate0,
                               jnp.where(iota8 == 3, gate1, 0.0))))

    iota_e = lax.broadcasted_iota(jnp.int32, (N_EXPERTS, rows), 0).astype(F32)
    onehot = (iota_e == e0).astype(F32) + (iota_e == e1).astype(F32)
    lane = lax.broadcasted_iota(jnp.int32, (N_EXPERTS, LANES), 1)
    cnt = jnp.zeros((N_EXPERTS, LANES), F32)
    for s in range(rows // TOKEN_TILE):
        c_s = jnp.sum(onehot[:, s * TOKEN_TILE:(s + 1) * TOKEN_TILE], axis=1, keepdims=True)
        cnt = cnt + jnp.where(lane == s, c_s, 0.0)
    cnt_ref[...] = cnt.reshape(1, N_EXPERTS, LANES)


def _mixer_call(x, hist, h0, wts, *, n_total, row_offset, sb, tt, chunk, seg_len, seg_stride, chain,
                emit_vn, alias_in=None):
    nb, t_len, _ = x.shape
    nbb = nb // sb
    ntt = t_len // tt
    rows = sb * tt
    blk0 = row_offset // rows
    n_seg = rows // seg_len

    def full(arr):
        nd = arr.ndim
        return pl.BlockSpec(arr.shape, lambda b, t, _nd=nd: (0,) * _nd)

    in_specs = [
        pl.BlockSpec((sb, tt, D_MODEL), lambda b, t: (b, t, 0)),
        pl.BlockSpec((sb, CONV_WIDTH - 1, D_B), lambda b, t: (b, 0, 0)),
        pl.BlockSpec((1, sb, D_B), lambda b, t: (b, 0, 0)),
    ] + [full(w) for w in wts]
    args = [x, hist, h0] + list(wts)
    io_alias = {}
    if alias_in is not None:
        for k, arr in enumerate(alias_in):
            io_alias[len(args)] = k
            in_specs.append(pl.BlockSpec(memory_space=pl.ANY))
            args.append(arr)

    out_shape = [
        jax.ShapeDtypeStruct((n_total, D_MODEL), F32),
        jax.ShapeDtypeStruct((n_total, D_MODEL), BF16),
        jax.ShapeDtypeStruct((SUBLANES, n_total), F32),
        jax.ShapeDtypeStruct((nbb * ntt, N_EXPERTS, LANES), F32),
        jax.ShapeDtypeStruct((nb, CONV_WIDTH - 1, D_B), F32),
        jax.ShapeDtypeStruct((nbb, sb, D_B), F32),
    ]
    out_specs = [
        pl.BlockSpec((rows, D_MODEL), lambda b, t: (blk0 + b * ntt + t, 0)),
        pl.BlockSpec((rows, D_MODEL), lambda b, t: (blk0 + b * ntt + t, 0)),
        pl.BlockSpec((SUBLANES, rows), lambda b, t: (0, blk0 + b * ntt + t)),
        pl.BlockSpec((1, N_EXPERTS, LANES), lambda b, t: (b * ntt + t, 0, 0)),
        pl.BlockSpec((sb, CONV_WIDTH - 1, D_B), lambda b, t: (b, 0, 0)),
        pl.BlockSpec((1, sb, D_B), lambda b, t: (b, 0, 0)),
    ]
    if emit_vn:
        out_shape.append(jax.ShapeDtypeStruct((nb, t_len, D_A), F32))
        out_specs.append(pl.BlockSpec((sb, tt, D_A), lambda b, t: (b, t, 0)))

    kern = functools.partial(_mixer_kernel, sb=sb, tt=tt, chunk=chunk, seg_len=seg_len, seg_stride=seg_stride,
                             chain=chain, emit_vn=emit_vn, aliased=alias_in is not None)
    return pl.pallas_call(
        kern,
        grid=(nbb, ntt),
        in_specs=in_specs,
        out_specs=out_specs,
        out_shape=out_shape,
        scratch_shapes=[
            pltpu.VMEM((sb, SUBLANES + tt, D_B), F32),
            pltpu.VMEM((D_B // LANES, n_seg * seg_stride, LANES), F32),
            pltpu.VMEM((D_B // LANES, n_seg * seg_stride, LANES), F32),
            pltpu.VMEM((rows, D_MODEL), BF16),
            pltpu.VMEM((1, D_B), F32),
        ],
        input_output_aliases=io_alias,
        compiler_params=pltpu.CompilerParams(dimension_semantics=("arbitrary", "arbitrary"),
                                             vmem_limit_bytes=VMEM_LIMIT),
        name="mixer_chain" if chain else "mixer_step",
    )(*args)


def _tile_slots(route_ref, locoff_ref, upper_ref):
    e0 = route_ref[0:1, :]
    e1 = route_ref[1:2, :]
    iota_e = lax.broadcasted_iota(jnp.int32, (N_EXPERTS, TOKEN_TILE), 0).astype(F32)
    oh0 = (iota_e == e0).astype(F32)
    oh1 = (iota_e == e1).astype(F32)
    c0 = _dot(oh0.astype(BF16), upper_ref[...])
    c1 = _dot(oh1.astype(BF16), upper_ref[...])
    cnt0 = jnp.sum(oh0, axis=1, keepdims=True)
    base0 = locoff_ref[0]
    base1 = base0 + cnt0
    p0 = jnp.sum(oh0 * (base0 + c0), axis=0, keepdims=True)
    p1 = jnp.sum(oh1 * (base1 + c1), axis=0, keepdims=True)
    return p0.astype(jnp.int32), p1.astype(jnp.int32)


def _chunk_loop(tile, nch_ref, doff_ref, make_copy):
    def per_expert(e, loc):
        n = nch_ref[tile * N_EXPERTS + e]
        d0 = doff_ref[tile * N_EXPERTS + e]

        def per_chunk(c, _):
            make_copy(pl.multiple_of(loc + c * SUBLANES, SUBLANES), pl.multiple_of(d0 + c * SUBLANES, SUBLANES))
            return 0

        lax.fori_loop(0, n, per_chunk, 0)
        return loc + n * SUBLANES

    lax.fori_loop(0, N_EXPERTS, per_expert, 0)


def _dispatch_kernel(nch_ref, doff_ref, ntot_ref, tail_ref, ntail_ref,
                     h2_ref, route_ref, locoff_ref, upper_ref, xb_ref, buf_ref, zero_ref, sem, zsem):
    i = pl.program_id(0)
    n_tiles = pl.num_programs(0)
    slot = i % 2

    def chunk_copy(s, loc, dst):
        return pltpu.make_async_copy(buf_ref.at[s, pl.ds(loc, SUBLANES), :], xb_ref.at[pl.ds(dst, SUBLANES), :],
                                     sem.at[s])

    def wait_tile(tile, s):
        n_rows = ntot_ref[tile] * SUBLANES

        @pl.when(n_rows > 0)
        def _():
            pltpu.make_async_copy(buf_ref.at[s, pl.ds(0, n_rows), :], xb_ref.at[pl.ds(0, n_rows), :],
                                  sem.at[s]).wait()

    @pl.when(i >= 2)
    def _():
        wait_tile(i - 2, slot)

    p0, p1 = _tile_slots(route_ref, locoff_ref, upper_ref)
    iota_s = lax.broadcasted_iota(jnp.int32, (SLOT_ROWS, TOKEN_TILE), 0)
    hit0 = iota_s == p0
    hit1 = iota_s == p1
    perm = jnp.where(hit0 | hit1, 1.0, 0.0).astype(BF16)
    buf_ref[slot, :, 0:D_MODEL] = _dot(perm, h2_ref[...])
    gates = jnp.where(hit0, route_ref[2:3, :], 0.0) + jnp.where(hit1, route_ref[3:4, :], 0.0)
    gcol = jnp.sum(gates, axis=1, keepdims=True)
    buf_ref[slot, :, D_MODEL:XB_COLS] = jnp.broadcast_to(gcol, (SLOT_ROWS, LANES))

    _chunk_loop(i, nch_ref, doff_ref, lambda loc, dst: chunk_copy(slot, loc, dst).start())

    @pl.when(i == n_tiles - 1)
    def _():
        zero_ref[...] = jnp.zeros_like(zero_ref)

        def zero_copy(dst, rows, k):
            return pltpu.make_async_copy(zero_ref.at[pl.ds(0, rows), :], xb_ref.at[pl.ds(dst, rows), :], zsem.at[k])

        def per_expert(e, tot):
            n_big, n_small = tot
            n = ntail_ref[e]
            d0 = tail_ref[e]
            nb = n // ZERO_CHUNKS
            ns = n - nb * ZERO_CHUNKS

            def big(c, _):
                zero_copy(pl.multiple_of(d0 + c * ZERO_ROWS, SUBLANES), ZERO_ROWS, 0).start()
                return 0

            def small(c, _):
                zero_copy(pl.multiple_of(d0 + nb * ZERO_ROWS + c * SUBLANES, SUBLANES), SUBLANES, 1).start()
                return 0

            lax.fori_loop(0, nb, big, 0)
            lax.fori_loop(0, ns, small, 0)
            return n_big + nb, n_small + ns

        n_big, n_small = lax.fori_loop(0, N_EXPERTS, per_expert, (0, 0))

        @pl.when(i >= 1)
        def _():
            wait_tile(i - 1, 1 - slot)

        wait_tile(i, slot)

        def wait_big(c, _):
            zero_copy(0, ZERO_ROWS, 0).wait()
            return 0

        def wait_small(c, _):
            zero_copy(0, SUBLANES, 1).wait()
            return 0

        lax.fori_loop(0, n_big, wait_big, 0)
        lax.fori_loop(0, n_small, wait_small, 0)


def _dispatch_call(h2, route, locoff, upper, nch, doff, ntot, tail, ntail, *, n_tiles, p_rows):
    grid_spec = pltpu.PrefetchScalarGridSpec(
        num_scalar_prefetch=5,
        grid=(n_tiles,),
        in_specs=[
            pl.BlockSpec((TOKEN_TILE, D_MODEL), lambda i, *_: (i, 0)),
            pl.BlockSpec((SUBLANES, TOKEN_TILE), lambda i, *_: (0, i)),
            pl.BlockSpec((1, N_EXPERTS, 1), lambda i, *_: (i, 0, 0)),
            pl.BlockSpec((TOKEN_TILE, TOKEN_TILE), lambda i, *_: (0, 0)),
        ],
        out_specs=pl.BlockSpec(memory_space=pl.ANY),
        scratch_shapes=[
            pltpu.VMEM((2, SLOT_ROWS, XB_COLS), F32),
            pltpu.VMEM((ZERO_ROWS, XB_COLS), F32),
            pltpu.SemaphoreType.DMA((2,)),
            pltpu.SemaphoreType.DMA((2,)),
        ],
    )
    return pl.pallas_call(
        _dispatch_kernel,
        grid_spec=grid_spec,
        out_shape=jax.ShapeDtypeStruct((p_rows, XB_COLS), F32),
        compiler_params=pltpu.CompilerParams(dimension_semantics=("arbitrary",), vmem_limit_bytes=VMEM_LIMIT),
        name="dispatch",
    )(nch, doff, ntot, tail, ntail, h2, route, locoff, upper)


def _ffn_kernel(blk_e_ref, nused_ref, xb_ref, wg_ref, wu_ref, wd_ref, yb_ref, wgb_ref, wub_ref, wdb_ref):
    i = pl.program_id(0)
    new_expert = (i == 0) | (blk_e_ref[i] != blk_e_ref[jnp.maximum(i - 1, 0)])

    @pl.when(new_expert)
    def _():
        wgb_ref[...] = wg_ref[0].astype(BF16)
        wub_ref[...] = wu_ref[0].astype(BF16)
        wdb_ref[...] = wd_ref[0].astype(BF16)

    @pl.when(i < nused_ref[0])
    def _():
        xb = xb_ref[:, 0:D_MODEL].astype(BF16)
        gate = xb_ref[:, D_MODEL:D_MODEL + 1]
        a = _dot(xb, wgb_ref[...])
        u = _dot(xb, wub_ref[...])
        mid = (jax.nn.silu(a) * u).astype(BF16)
        yb_ref[...] = _dot(mid, wdb_ref[...]) * gate


def _ffn_call(xb, wg, wu, wd, blk_e, nused, *, n_blocks):
    def row_map(i, blk_e_ref, nused_ref):
        return (jnp.minimum(i, nused_ref[0] - 1), 0)

    def w_map(i, blk_e_ref, nused_ref):
        return (blk_e_ref[i], 0, 0)

    grid_spec = pltpu.PrefetchScalarGridSpec(
        num_scalar_prefetch=2,
        grid=(n_blocks,),
        in_specs=[
            pl.BlockSpec((FFN_BLOCK, XB_COLS), row_map),
            pl.BlockSpec((1, D_MODEL, D_EXPERT), w_map),
            pl.BlockSpec((1, D_MODEL, D_EXPERT), w_map),
            pl.BlockSpec((1, D_EXPERT, D_MODEL), w_map),
        ],
        out_specs=pl.BlockSpec((FFN_BLOCK, D_MODEL), row_map),
        scratch_shapes=[
            pltpu.VMEM((D_MODEL, D_EXPERT), BF16),
            pltpu.VMEM((D_MODEL, D_EXPERT), BF16),
            pltpu.VMEM((D_EXPERT, D_MODEL), BF16),
        ],
    )
    return pl.pallas_call(
        _ffn_kernel,
        grid_spec=grid_spec,
        out_shape=jax.ShapeDtypeStruct((n_blocks * FFN_BLOCK, D_MODEL), F32),
        compiler_params=pltpu.CompilerParams(dimension_semantics=("arbitrary",), vmem_limit_bytes=VMEM_LIMIT),
        name="expert_ffn",
    )(blk_e, nused, xb, wg, wu, wd)


def _combine_kernel(nch_ref, doff_ref, ntot_ref, y1_ref, route_ref, locoff_ref, upper_ref, fg_ref, yb_ref,
                    out_ref, buf_ref, sem, *, tile0):
    i = pl.program_id(0)
    n_tiles = pl.num_programs(0)
    slot = i % 2

    def chunk_copy(s, loc, src):
        return pltpu.make_async_copy(yb_ref.at[pl.ds(src, SUBLANES), :], buf_ref.at[s, pl.ds(loc, SUBLANES), :],
                                     sem.at[s])

    def issue_tile(tile, s):
        _chunk_loop(tile, nch_ref, doff_ref, lambda loc, src: chunk_copy(s, loc, src).start())

    @pl.when(i == 0)
    def _():
        buf_ref[...] = jnp.zeros_like(buf_ref)
        issue_tile(tile0, 0)

    @pl.when(i + 1 < n_tiles)
    def _():
        issue_tile(tile0 + i + 1, 1 - slot)

    p0, p1 = _tile_slots(route_ref, locoff_ref, upper_ref)
    iota_s = lax.broadcasted_iota(jnp.int32, (SLOT_ROWS, TOKEN_TILE), 0)
    perm = jnp.where((iota_s == p0) | (iota_s == p1), 1.0, 0.0).astype(BF16)

    n_rows = ntot_ref[tile0 + i] * SUBLANES

    @pl.when(n_rows > 0)
    def _():
        pltpu.make_async_copy(yb_ref.at[pl.ds(0, n_rows), :], buf_ref.at[slot, pl.ds(0, n_rows), :],
                              sem.at[slot]).wait()

    yb = buf_ref[slot]
    hi = yb.astype(BF16)
    r1 = yb - hi.astype(F32)
    mid = r1.astype(BF16)
    lo = (r1 - mid.astype(F32)).astype(BF16)
    dn = (((0,), (0,)), ((), ()))
    moe = (lax.dot_general(perm, hi, dn, preferred_element_type=F32)
           + lax.dot_general(perm, mid, dn, preferred_element_type=F32)
           + lax.dot_general(perm, lo, dn, preferred_element_type=F32))
    out_ref[...] = _rms(y1_ref[...] + moe, fg_ref[...])


def _combine_call(y1, route, locoff, upper, fg, yb, nch, doff, ntot, *, tile0, n_tiles):
    grid_spec = pltpu.PrefetchScalarGridSpec(
        num_scalar_prefetch=3,
        grid=(n_tiles,),
        in_specs=[
            pl.BlockSpec((TOKEN_TILE, D_MODEL), lambda i, *_: (tile0 + i, 0)),
            pl.BlockSpec((SUBLANES, TOKEN_TILE), lambda i, *_: (0, tile0 + i)),
            pl.BlockSpec((1, N_EXPERTS, 1), lambda i, *_: (tile0 + i, 0, 0)),
            pl.BlockSpec((TOKEN_TILE, TOKEN_TILE), lambda i, *_: (0, 0)),
            pl.BlockSpec((1, D_MODEL), lambda i, *_: (0, 0)),
            pl.BlockSpec(memory_space=pl.ANY),
        ],
        out_specs=pl.BlockSpec((TOKEN_TILE, D_MODEL), lambda i, *_: (i, 0)),
        scratch_shapes=[
            pltpu.VMEM((2, SLOT_ROWS, D_MODEL), F32),
            pltpu.SemaphoreType.DMA((2,)),
        ],
    )
    return pl.pallas_call(
        functools.partial(_combine_kernel, tile0=tile0),
        grid_spec=grid_spec,
        out_shape=jax.ShapeDtypeStruct((n_tiles * TOKEN_TILE, D_MODEL), F32),
        compiler_params=pltpu.CompilerParams(dimension_semantics=("arbitrary",), vmem_limit_bytes=VMEM_LIMIT),
        name="combine",
    )(nch, doff, ntot, y1, route, locoff, upper, fg, yb)


def _block_diag(w):
    h, d, _ = w.shape
    eye = jnp.eye(h, dtype=w.dtype)
    return (eye[:, None, :, None] * w[:, :, None, :]).reshape(h * d, h * d)


def _mixer_weights(l, length, reps, norm1_g, w_in, gmlp_ln_g, gmlp_ln_b, gmlp_w_s, gmlp_b_s, conv_w, conv_b,
                   lru_w_a, lru_b_a, lru_w_x, lru_b_x, lru_lambda, w_out, norm2_g,
                   router_group_w, router_group_b, router_expert_w, router_expert_b):
    mask = jnp.tril(jnp.ones((length, length), dtype=bool))
    ws = jnp.where(mask, gmlp_w_s[l][:, :length, :length], 0.0)
    eye = jnp.eye(reps, dtype=ws.dtype)
    wsm = (eye[None, :, None, :, None] * ws[:, None, :, None, :]).reshape(A_GROUPS, reps * length, reps * length)
    bs = jnp.tile(gmlp_b_s[l][:, :length], (1, reps))
    bsx = jnp.repeat(bs.T, A_GROUP_DIM, axis=1)
    wr = jnp.concatenate([router_expert_w[l].T, router_group_w[l].T,
                          jnp.zeros((ROUTER_ROWS - N_EXPERTS - N_GROUPS, D_MODEL), F32)], axis=0)
    br = jnp.concatenate([router_expert_b[l], router_group_b[l],
                          jnp.zeros((ROUTER_ROWS - N_EXPERTS - N_GROUPS,), F32)]).reshape(ROUTER_ROWS, 1)
    return [
        norm1_g[l].reshape(1, D_MODEL), w_in[l].astype(BF16),
        gmlp_ln_g[l].reshape(1, D_A), gmlp_ln_b[l].reshape(1, D_A), wsm.astype(BF16), bsx,
        conv_w[l], conv_b[l].reshape(1, D_B),
        _block_diag(lru_w_a[l]).astype(BF16), lru_b_a[l].reshape(1, D_B),
        _block_diag(lru_w_x[l]).astype(BF16), lru_b_x[l].reshape(1, D_B),
        lru_lambda[l].reshape(1, D_B), w_out[l].astype(BF16), norm2_g[l].reshape(1, D_MODEL),
        wr.astype(BF16), br,
    ]


def _routing_tables(cnt, n_blocks):
    n_tiles = cnt.shape[0]
    seg = (cnt + SUBLANES - 1) // SUBLANES * SUBLANES
    nch = seg // SUBLANES
    tot = jnp.sum(seg, axis=0)
    padded = (tot + FFN_BLOCK - 1) // FFN_BLOCK * FFN_BLOCK
    e_before = jnp.arange(N_EXPERTS)[None, :] < jnp.arange(N_EXPERTS)[:, None]
    t_before = jnp.arange(n_tiles)[None, :] < jnp.arange(n_tiles)[:, None]
    pad_start = jnp.sum(jnp.where(e_before, padded[None, :], 0), axis=1)
    pad_end = pad_start + padded
    doff = pad_start[None, :] + jnp.sum(jnp.where(t_before[:, :, None], seg[None, :, :], 0), axis=1)
    locoff = jnp.sum(jnp.where(e_before[None, :, :], seg[:, None, :], 0), axis=2)
    ntot = jnp.sum(nch, axis=1)
    tail = pad_start + tot
    ntail = (padded - tot) // SUBLANES
    blk_end = pad_end // FFN_BLOCK
    nused = blk_end[-1]
    blk = jnp.minimum(jnp.arange(n_blocks, dtype=jnp.int32), nused - 1)
    blk_e = jnp.minimum(jnp.sum((blk_end[None, :] <= blk[:, None]).astype(jnp.int32), axis=1), N_EXPERTS - 1)
    i32 = lambda a: a.astype(jnp.int32)
    return (i32(nch.reshape(-1)), i32(doff.reshape(-1)), i32(ntot), i32(tail), i32(ntail),
            locoff.astype(F32)[:, :, None], i32(blk_e), i32(nused.reshape(1)))


def kernel(x_prompt, x_sample, state_conv, state_rglru, norm1_g, w_in, gmlp_ln_g, gmlp_ln_b, gmlp_w_s, gmlp_b_s,
           conv_w, conv_b, lru_w_a, lru_b_a, lru_w_x, lru_b_x, lru_lambda, w_out, norm2_g,
           router_group_w, router_group_b, router_expert_w, router_expert_b,
           expert_w_gate, expert_w_up, expert_w_down, final_norm_g):
    depth = w_in.shape[0]
    nb, t_len, _ = x_prompt.shape
    db, dt, _ = x_sample.shape
    n_prompt = nb * t_len
    n_sample = db * dt
    assert t_len % PROMPT_TILE == 0 and n_prompt % TOKEN_TILE == 0 and n_sample == TOKEN_TILE
    n_total = n_prompt + n_sample
    n_tiles = n_total // TOKEN_TILE
    p_tiles = n_prompt // TOKEN_TILE
    p_rows = 2 * n_total + (SUBLANES - 1) * N_EXPERTS * n_tiles + N_EXPERTS * (FFN_BLOCK - 1)
    n_blocks = -(-p_rows // FFN_BLOCK)
    p_rows = n_blocks * FFN_BLOCK
    upper = jnp.triu(jnp.ones((TOKEN_TILE, TOKEN_TILE), BF16), 1)

    xp, xs = x_prompt, x_sample
    conv_p, h_p, v_s, conv_s, h_s = [], [], [], [], []
    for l in range(depth):
        lw = (norm1_g, w_in, gmlp_ln_g, gmlp_ln_b, gmlp_w_s, gmlp_b_s, conv_w, conv_b, lru_w_a, lru_b_a,
              lru_w_x, lru_b_x, lru_lambda, w_out, norm2_g, router_group_w, router_group_b,
              router_expert_w, router_expert_b)
        wts_p = _mixer_weights(l, GMLP_CHUNK, 1, *lw)
        wts_s = _mixer_weights(l, dt, db, *lw)
        zero_conv = jnp.zeros((nb, CONV_WIDTH - 1, D_B), F32)
        zero_h = jnp.zeros((nb, 1, D_B), F32)
        y1, h2, route, cnt_p, cp, hp = _mixer_call(
            xp, zero_conv, zero_h, wts_p, n_total=n_total, row_offset=0, sb=1, tt=PROMPT_TILE, chunk=GMLP_CHUNK,
            seg_len=PROMPT_TILE // SUBLANES, seg_stride=PROMPT_TILE // SUBLANES + SUBLANES, chain=True,
            emit_vn=False)
        y1, h2, route, cnt_s, cs, hs, vs = _mixer_call(
            xs, state_conv[l], state_rglru[l][None], wts_s, n_total=n_total, row_offset=n_prompt, sb=db,
            tt=dt, chunk=db * dt, seg_len=dt, seg_stride=dt + SUBLANES, chain=False, emit_vn=True,
            alias_in=(y1, h2, route))
        conv_p.append(cp)
        h_p.append(hp[:, 0])
        v_s.append(vs)
        conv_s.append(cs)
        h_s.append(hs[0])

        sub = PROMPT_TILE // TOKEN_TILE
        cnt = jnp.concatenate([
            jnp.swapaxes(cnt_p[:, :, :sub], 1, 2).reshape(p_tiles, N_EXPERTS),
            cnt_s[:, :, 0]], axis=0).astype(jnp.int32)
        nch, doff, ntot, tail, ntail, locoff, blk_e, nused = _routing_tables(cnt, n_blocks)

        xb = _dispatch_call(h2, route, locoff, upper, nch, doff, ntot, tail, ntail, n_tiles=n_tiles, p_rows=p_rows)
        yb = _ffn_call(xb, expert_w_gate[l], expert_w_up[l], expert_w_down[l], blk_e, nused, n_blocks=n_blocks)
        last = l == depth - 1
        fg = (final_norm_g if last else jnp.ones_like(final_norm_g)).reshape(1, D_MODEL)
        assert last, "deeper stacks need an un-normalised combine between layers"
        yp = _combine_call(y1, route, locoff, upper, fg, yb, nch, doff, ntot, tile0=0, n_tiles=p_tiles)
        ysm = _combine_call(y1, route, locoff, upper, fg, yb, nch, doff, ntot, tile0=p_tiles,
                            n_tiles=n_tiles - p_tiles)
        xp = yp.reshape(nb, t_len, D_MODEL)
        xs = ysm.reshape(db, dt, D_MODEL)

    return (xp, xs, jnp.stack(conv_p), jnp.stack(h_p), jnp.stack(v_s), jnp.stack(conv_s), jnp.stack(h_s))
```

```python
import functools

import jax
import jax.numpy as jnp
from jax import lax
from jax.experimental import pallas as pl
from jax.experimental.pallas import tpu as pltpu

D_MODEL = 1024
D_A = 512
D_B = 512
A_GROUPS = 4
A_GROUP_DIM = 128
GMLP_CHUNK = 128
B_HEADS = 8
CONV_WIDTH = 4
LRU_C = 8.0
N_GROUPS = 4
EXPERTS_PER_GROUP = 8
N_EXPERTS = 32
D_EXPERT = 512
EPS = 1e-6

SUBLANES = 8
LANES = 128
ROUTER_ROWS = 40
TOKEN_TILE = 256
PROMPT_TILE = 512
FFN_BLOCK = 512
ZERO_CHUNKS = 8
ZERO_ROWS = ZERO_CHUNKS * SUBLANES
SLOT_ROWS = 2 * TOKEN_TILE + N_EXPERTS * SUBLANES
PACK_COLS = D_MODEL // 2
XB_COLS = PACK_COLS + LANES
U32 = jnp.uint32
HI_MASK = 0xFFFF0000
VMEM_LIMIT = 56 * 1024 * 1024

BF16 = jnp.bfloat16
F32 = jnp.float32


def _rms(x, g):
    return x * lax.rsqrt(jnp.mean(x * x, axis=-1, keepdims=True) + EPS) * g


def _dot(a, b):
    return jnp.dot(a, b, preferred_element_type=F32)


def _pack_bf16_pairs(x):
    bits = lax.bitcast_convert_type(x, U32)
    return (bits[:, PACK_COLS:] & U32(HI_MASK)) | (bits[:, :PACK_COLS] >> 16)


def _unpack_bf16_pairs(w):
    lo = lax.bitcast_convert_type(w << 16, F32).astype(BF16)
    hi = lax.bitcast_convert_type(w & U32(HI_MASK), F32).astype(BF16)
    return jnp.concatenate([lo, hi], axis=1)


def _mixer_kernel(x_ref, hist_ref, h0_ref, n1g_ref, win_ref, lng_ref, lnb_ref, wsm_ref, bsx_ref,
                  cw_ref, cb_ref, wa_ref, ba_ref, wx_ref, bx_ref, lam_ref, wout_ref, n2g_ref,
                  wr_ref, br_ref, *rest, sb, tt, chunk, seg_len, seg_stride, chain, emit_vn, aliased):
    if aliased:
        rest = rest[3:]
    if emit_vn:
        (y1_ref, h2_ref, route_ref, cnt_ref, conv_ref, hlast_ref, vn_ref,
         xs_ref, sa_ref, sb_ref, mix_ref, hc_ref) = rest
    else:
        (y1_ref, h2_ref, route_ref, cnt_ref, conv_ref, hlast_ref,
         xs_ref, sa_ref, sb_ref, mix_ref, hc_ref) = rest
        vn_ref = None
    t_idx = pl.program_id(1)
    rows = sb * tt
    n_seg = rows // seg_len

    x = x_ref[...].reshape(rows, D_MODEL)
    hb = _rms(x, n1g_ref[...]).astype(BF16)

    u_a = _dot(hb, win_ref[:, 0:D_A])
    v_a = _dot(hb, win_ref[:, D_A:2 * D_A])
    for g in range(A_GROUPS):
        cols = slice(g * A_GROUP_DIM, (g + 1) * A_GROUP_DIM)
        vg = v_a[:, cols]
        mu = jnp.mean(vg, axis=-1, keepdims=True)
        dv = vg - mu
        var = jnp.mean(dv * dv, axis=-1, keepdims=True)
        vn = dv * lax.rsqrt(var + EPS) * lng_ref[:, cols] + lnb_ref[:, cols]
        if vn_ref is not None:
            vn_ref[:, :, cols] = vn.reshape(sb, tt, A_GROUP_DIM)
        vnb = vn.astype(BF16)
        for c in range(rows // chunk):
            rs = slice(c * chunk, (c + 1) * chunk)
            s = _dot(wsm_ref[g], vnb[rs]) + bsx_ref[:, cols]
            mix_ref[rs, cols] = (u_a[rs, cols] * s).astype(BF16)

    x_b = _dot(hb, win_ref[:, 2 * D_A:2 * D_A + D_B])

    @pl.when(t_idx == 0)
    def _():
        xs_ref[:, SUBLANES - 3:SUBLANES, :] = hist_ref[...]
        hc_ref[...] = jnp.zeros_like(hc_ref)

    @pl.when(t_idx != 0)
    def _():
        xs_ref[:, SUBLANES - 3:SUBLANES, :] = xs_ref[:, SUBLANES + tt - 3:SUBLANES + tt, :]

    xs_ref[:, SUBLANES:SUBLANES + tt, :] = x_b.reshape(sb, tt, D_B)
    conv_ref[...] = xs_ref[:, SUBLANES + tt - 3:SUBLANES + tt, :]
    xc = cb_ref[...] + x_b * cw_ref[3:4, :]
    for j in range(1, CONV_WIDTH):
        shifted = xs_ref[:, SUBLANES - j:SUBLANES - j + tt, :].reshape(rows, D_B)
        xc = xc + shifted * cw_ref[3 - j:4 - j, :]

    xcb = xc.astype(BF16)
    r = jax.nn.sigmoid(_dot(xcb, wa_ref[...]) + ba_ref[...])
    i_g = jax.nn.sigmoid(_dot(xcb, wx_ref[...]) + bx_ref[...])
    log_a = (-LRU_C * r) * jax.nn.softplus(-lam_ref[...])
    a = jnp.exp(log_a)
    gain = jnp.sqrt(1.0 - a * a)
    bterm = gain * i_g * xc

    n_lb = D_B // LANES
    for j in range(n_seg):
        for k in range(n_lb):
            lc = slice(k * LANES, (k + 1) * LANES)
            sa_ref[k, j * seg_stride:j * seg_stride + seg_len, :] = a[j * seg_len:(j + 1) * seg_len, lc]
            sb_ref[k, j * seg_stride:j * seg_stride + seg_len, :] = bterm[j * seg_len:(j + 1) * seg_len, lc]

    g_b = _dot(hb, win_ref[:, 2 * D_A + D_B:])
    gel = jax.nn.gelu(g_b)

    for grp in range(n_seg // SUBLANES):
        base = grp * SUBLANES * seg_stride
        bs = slice(grp * SUBLANES, (grp + 1) * SUBLANES)
        if chain:
            h_init = tuple(jnp.zeros((SUBLANES, LANES), F32) for _ in range(n_lb))
        else:
            h_init = tuple(h0_ref[0, bs, k * LANES:(k + 1) * LANES] for k in range(n_lb))
        a_init = tuple(jnp.ones((SUBLANES, LANES), F32) for _ in range(n_lb))

        def step(i, carry, base=base):
            hs, acs = carry
            idx = pl.ds(base + i, SUBLANES, stride=seg_stride)
            new_h, new_a = [], []
            for k in range(n_lb):
                av = sa_ref[k, idx, :]
                h = av * hs[k] + sb_ref[k, idx, :]
                sb_ref[k, idx, :] = h
                new_h.append(h)
                if chain:
                    ac = av * acs[k]
                    sa_ref[k, idx, :] = ac
                    new_a.append(ac)
                else:
                    new_a.append(acs[k])
            return tuple(new_h), tuple(new_a)

        h_end, a_end = lax.fori_loop(0, seg_len, step, (h_init, a_init), unroll=8)

        for k in range(n_lb):
            lc = slice(k * LANES, (k + 1) * LANES)
            mc = slice(D_A + k * LANES, D_A + (k + 1) * LANES)
            if chain:
                h_in = hc_ref[:, lc]
                for j in range(SUBLANES):
                    seg = slice(base + j * seg_stride, base + j * seg_stride + seg_len)
                    rs = slice((grp * SUBLANES + j) * seg_len, (grp * SUBLANES + j + 1) * seg_len)
                    h_seg = sb_ref[k, seg, :] + sa_ref[k, seg, :] * h_in
                    mix_ref[rs, mc] = (h_seg * gel[rs, lc]).astype(BF16)
                    h_in = h_end[k][j:j + 1] + a_end[k][j:j + 1] * h_in
                hc_ref[:, lc] = h_in
                hlast_ref[0, :, lc] = h_in
            else:
                for j in range(SUBLANES):
                    seg = slice(base + j * seg_stride, base + j * seg_stride + seg_len)
                    rs = slice((grp * SUBLANES + j) * seg_len, (grp * SUBLANES + j + 1) * seg_len)
                    mix_ref[rs, mc] = (sb_ref[k, seg, :] * gel[rs, lc]).astype(BF16)
                hlast_ref[0, bs, lc] = h_end[k]

    y1 = x + _dot(mix_ref[...], wout_ref[...])
    y1_ref[...] = y1
    h2 = _rms(y1, n2g_ref[...]).astype(BF16)
    h2_ref[...] = h2

    logits = lax.dot_general(wr_ref[...], h2, (((1,), (1,)), ((), ())), preferred_element_type=F32) + br_ref[...]
    gl = [logits[N_EXPERTS + k:N_EXPERTS + k + 1, :] for k in range(N_GROUPS)]
    gmax = jnp.maximum(jnp.maximum(gl[0], gl[1]), jnp.maximum(gl[2], gl[3]))
    grp_idx = jnp.where(gl[0] == gmax, 0, jnp.where(gl[1] == gmax, 1, jnp.where(gl[2] == gmax, 2, 3)))
    gsum = (jnp.exp(gl[0] - gmax) + jnp.exp(gl[1] - gmax)) + (jnp.exp(gl[2] - gmax) + jnp.exp(gl[3] - gmax))
    p_grp = 1.0 / gsum
    e_in = jnp.where(grp_idx == 0, logits[0:8],
                     jnp.where(grp_idx == 1, logits[8:16], jnp.where(grp_idx == 2, logits[16:24], logits[24:32])))
    iota8 = lax.broadcasted_iota(jnp.int32, (EXPERTS_PER_GROUP, rows), 0)
    m1 = jnp.max(e_in, axis=0, keepdims=True)
    i1 = jnp.min(jnp.where(e_in == m1, iota8, EXPERTS_PER_GROUP), axis=0, keepdims=True)
    e_rest = jnp.where(iota8 == i1, -jnp.inf, e_in)
    m2 = jnp.max(e_rest, axis=0, keepdims=True)
    i2 = jnp.min(jnp.where(e_rest == m2, iota8, EXPERTS_PER_GROUP), axis=0, keepdims=True)
    t2 = jnp.exp(m2 - m1)
    den = 1.0 + t2
    gate0 = p_grp * (1.0 / den)
    gate1 = p_grp * (t2 / den)
    e0 = (grp_idx * EXPERTS_PER_GROUP + i1).astype(F32)
    e1 = (grp_idx * EXPERTS_PER_GROUP + i2).astype(F32)
    route_ref[...] = jnp.where(iota8 == 0, e0, jnp.where(iota8 == 1, e1, jnp.where(iota8 == 2, gate0,
                               jnp.where(iota8 == 3, gate1, 0.0))))

    iota_e = lax.broadcasted_iota(jnp.int32, (N_EXPERTS, rows), 0).astype(F32)
    onehot = (iota_e == e0).astype(F32) + (iota_e == e1).astype(F32)
    lane = lax.broadcasted_iota(jnp.int32, (N_EXPERTS, LANES), 1)
    cnt = jnp.zeros((N_EXPERTS, LANES), F32)
    for s in range(rows // TOKEN_TILE):
        c_s = jnp.sum(onehot[:, s * TOKEN_TILE:(s + 1) * TOKEN_TILE], axis=1, keepdims=True)
        cnt = cnt + jnp.where(lane == s, c_s, 0.0)
    cnt_ref[...] = cnt.reshape(1, N_EXPERTS, LANES)


def _mixer_call(x, hist, h0, wts, *, n_total, row_offset, sb, tt, chunk, seg_len, seg_stride, chain,
                emit_vn, alias_in=None):
    nb, t_len, _ = x.shape
    nbb = nb // sb
    ntt = t_len // tt
    rows = sb * tt
    blk0 = row_offset // rows
    n_seg = rows // seg_len

    def full(arr):
        nd = arr.ndim
        return pl.BlockSpec(arr.shape, lambda b, t, _nd=nd: (0,) * _nd)

    in_specs = [
        pl.BlockSpec((sb, tt, D_MODEL), lambda b, t: (b, t, 0)),
        pl.BlockSpec((sb, CONV_WIDTH - 1, D_B), lambda b, t: (b, 0, 0)),
        pl.BlockSpec((1, sb, D_B), lambda b, t: (b, 0, 0)),
    ] + [full(w) for w in wts]
    args = [x, hist, h0] + list(wts)
    io_alias = {}
    if alias_in is not None:
        for k, arr in enumerate(alias_in):
            io_alias[len(args)] = k
            in_specs.append(pl.BlockSpec(memory_space=pl.ANY))
            args.append(arr)

    out_shape = [
        jax.ShapeDtypeStruct((n_total, D_MODEL), F32),
        jax.ShapeDtypeStruct((n_total, D_MODEL), BF16),
        jax.ShapeDtypeStruct((SUBLANES, n_total), F32),
        jax.ShapeDtypeStruct((nbb * ntt, N_EXPERTS, LANES), F32),
        jax.ShapeDtypeStruct((nb, CONV_WIDTH - 1, D_B), F32),
        jax.ShapeDtypeStruct((nbb, sb, D_B), F32),
    ]
    out_specs = [
        pl.BlockSpec((rows, D_MODEL), lambda b, t: (blk0 + b * ntt + t, 0)),
        pl.BlockSpec((rows, D_MODEL), lambda b, t: (blk0 + b * ntt + t, 0)),
        pl.BlockSpec((SUBLANES, rows), lambda b, t: (0, blk0 + b * ntt + t)),
        pl.BlockSpec((1, N_EXPERTS, LANES), lambda b, t: (b * ntt + t, 0, 0)),
        pl.BlockSpec((sb, CONV_WIDTH - 1, D_B), lambda b, t: (b, 0, 0)),
        pl.BlockSpec((1, sb, D_B), lambda b, t: (b, 0, 0)),
    ]
    if emit_vn:
        out_shape.append(jax.ShapeDtypeStruct((nb, t_len, D_A), F32))
        out_specs.append(pl.BlockSpec((sb, tt, D_A), lambda b, t: (b, t, 0)))

    kern = functools.partial(_mixer_kernel, sb=sb, tt=tt, chunk=chunk, seg_len=seg_len, seg_stride=seg_stride,
                             chain=chain, emit_vn=emit_vn, aliased=alias_in is not None)
    return pl.pallas_call(
        kern,
        grid=(nbb, ntt),
        in_specs=in_specs,
        out_specs=out_specs,
        out_shape=out_shape,
        scratch_shapes=[
            pltpu.VMEM((sb, SUBLANES + tt, D_B), F32),
            pltpu.VMEM((D_B // LANES, n_seg * seg_stride, LANES), F32),
            pltpu.VMEM((D_B // LANES, n_seg * seg_stride, LANES), F32),
            pltpu.VMEM((rows, D_MODEL), BF16),
            pltpu.VMEM((1, D_B), F32),
        ],
        input_output_aliases=io_alias,
        compiler_params=pltpu.CompilerParams(dimension_semantics=("arbitrary", "arbitrary"),
                                             vmem_limit_bytes=VMEM_LIMIT),
        name="mixer_chain" if chain else "mixer_step",
    )(*args)


def _tile_slots(route_ref, locoff_ref, upper_ref):
    e0 = route_ref[0:1, :]
    e1 = route_ref[1:2, :]
    iota_e = lax.broadcasted_iota(jnp.int32, (N_EXPERTS, TOKEN_TILE), 0).astype(F32)
    oh0 = (iota_e == e0).astype(F32)
    oh1 = (iota_e == e1).astype(F32)
    c0 = _dot(oh0.astype(BF16), upper_ref[...])
    c1 = _dot(oh1.astype(BF16), upper_ref[...])
    cnt0 = jnp.sum(oh0, axis=1, keepdims=True)
    base0 = locoff_ref[0]
    base1 = base0 + cnt0
    p0 = jnp.sum(oh0 * (base0 + c0), axis=0, keepdims=True)
    p1 = jnp.sum(oh1 * (base1 + c1), axis=0, keepdims=True)
    return p0.astype(jnp.int32), p1.astype(jnp.int32)


def _chunk_loop(tile, nch_ref, doff_ref, make_copy):
    def per_expert(e, loc):
        n = nch_ref[tile * N_EXPERTS + e]
        d0 = doff_ref[tile * N_EXPERTS + e]

        def per_chunk(c, _):
            make_copy(pl.multiple_of(loc + c * SUBLANES, SUBLANES), pl.multiple_of(d0 + c * SUBLANES, SUBLANES))
            return 0

        lax.fori_loop(0, n, per_chunk, 0)
        return loc + n * SUBLANES

    lax.fori_loop(0, N_EXPERTS, per_expert, 0)


def _dispatch_kernel(nch_ref, doff_ref, ntot_ref, tail_ref, ntail_ref,
                     h2_ref, route_ref, locoff_ref, upper_ref, xb_ref, buf_ref, zero_ref, sem, zsem):
    i = pl.program_id(0)
    n_tiles = pl.num_programs(0)
    slot = i % 2

    def chunk_copy(s, loc, dst):
        return pltpu.make_async_copy(buf_ref.at[s, pl.ds(loc, SUBLANES), :], xb_ref.at[pl.ds(dst, SUBLANES), :],
                                     sem.at[s])

    def wait_tile(tile, s):
        n_rows = ntot_ref[tile] * SUBLANES

        @pl.when(n_rows > 0)
        def _():
            pltpu.make_async_copy(buf_ref.at[s, pl.ds(0, n_rows), :], xb_ref.at[pl.ds(0, n_rows), :],
                                  sem.at[s]).wait()

    @pl.when(i >= 2)
    def _():
        wait_tile(i - 2, slot)

    p0, p1 = _tile_slots(route_ref, locoff_ref, upper_ref)
    iota_s = lax.broadcasted_iota(jnp.int32, (SLOT_ROWS, TOKEN_TILE), 0)
    hit0 = iota_s == p0
    hit1 = iota_s == p1
    perm = jnp.where(hit0 | hit1, 1.0, 0.0).astype(BF16)
    buf_ref[slot, :, 0:PACK_COLS] = _pack_bf16_pairs(_dot(perm, h2_ref[...]))
    gates = jnp.where(hit0, route_ref[2:3, :], 0.0) + jnp.where(hit1, route_ref[3:4, :], 0.0)
    gcol = jnp.sum(gates, axis=1, keepdims=True)
    buf_ref[slot, :, PACK_COLS:XB_COLS] = lax.bitcast_convert_type(jnp.broadcast_to(gcol, (SLOT_ROWS, LANES)), U32)

    _chunk_loop(i, nch_ref, doff_ref, lambda loc, dst: chunk_copy(slot, loc, dst).start())

    @pl.when(i == n_tiles - 1)
    def _():
        zero_ref[...] = jnp.zeros_like(zero_ref)

        def zero_copy(dst, rows, k):
            return pltpu.make_async_copy(zero_ref.at[pl.ds(0, rows), :], xb_ref.at[pl.ds(dst, rows), :], zsem.at[k])

        def per_expert(e, tot):
            n_big, n_small = tot
            n = ntail_ref[e]
            d0 = tail_ref[e]
            nb = n // ZERO_CHUNKS
            ns = n - nb * ZERO_CHUNKS

            def big(c, _):
                zero_copy(pl.multiple_of(d0 + c * ZERO_ROWS, SUBLANES), ZERO_ROWS, 0).start()
                return 0

            def small(c, _):
                zero_copy(pl.multiple_of(d0 + nb * ZERO_ROWS + c * SUBLANES, SUBLANES), SUBLANES, 1).start()
                return 0

            lax.fori_loop(0, nb, big, 0)
            lax.fori_loop(0, ns, small, 0)
            return n_big + nb, n_small + ns

        n_big, n_small = lax.fori_loop(0, N_EXPERTS, per_expert, (0, 0))

        @pl.when(i >= 1)
        def _():
            wait_tile(i - 1, 1 - slot)

        wait_tile(i, slot)

        def wait_big(c, _):
            zero_copy(0, ZERO_ROWS, 0).wait()
            return 0

        def wait_small(c, _):
            zero_copy(0, SUBLANES, 1).wait()
            return 0

        lax.fori_loop(0, n_big, wait_big, 0)
        lax.fori_loop(0, n_small, wait_small, 0)


def _dispatch_call(h2, route, locoff, upper, nch, doff, ntot, tail, ntail, *, n_tiles, p_rows):
    grid_spec = pltpu.PrefetchScalarGridSpec(
        num_scalar_prefetch=5,
        grid=(n_tiles,),
        in_specs=[
            pl.BlockSpec((TOKEN_TILE, D_MODEL), lambda i, *_: (i, 0)),
            pl.BlockSpec((SUBLANES, TOKEN_TILE), lambda i, *_: (0, i)),
            pl.BlockSpec((1, N_EXPERTS, 1), lambda i, *_: (i, 0, 0)),
            pl.BlockSpec((TOKEN_TILE, TOKEN_TILE), lambda i, *_: (0, 0)),
        ],
        out_specs=pl.BlockSpec(memory_space=pl.ANY),
        scratch_shapes=[
            pltpu.VMEM((2, SLOT_ROWS, XB_COLS), U32),
            pltpu.VMEM((ZERO_ROWS, XB_COLS), U32),
            pltpu.SemaphoreType.DMA((2,)),
            pltpu.SemaphoreType.DMA((2,)),
        ],
    )
    return pl.pallas_call(
        _dispatch_kernel,
        grid_spec=grid_spec,
        out_shape=jax.ShapeDtypeStruct((p_rows, XB_COLS), U32),
        compiler_params=pltpu.CompilerParams(dimension_semantics=("arbitrary",), vmem_limit_bytes=VMEM_LIMIT),
        name="dispatch",
    )(nch, doff, ntot, tail, ntail, h2, route, locoff, upper)


def _ffn_kernel(blk_e_ref, nused_ref, xb_ref, wg_ref, wu_ref, wd_ref, yb_ref, wgb_ref, wub_ref, wdb_ref):
    i = pl.program_id(0)
    new_expert = (i == 0) | (blk_e_ref[i] != blk_e_ref[jnp.maximum(i - 1, 0)])

    @pl.when(new_expert)
    def _():
        wgb_ref[...] = wg_ref[0].astype(BF16)
        wub_ref[...] = wu_ref[0].astype(BF16)
        wdb_ref[...] = wd_ref[0].astype(BF16)

    @pl.when(i < nused_ref[0])
    def _():
        xb = _unpack_bf16_pairs(xb_ref[:, 0:PACK_COLS])
        gate = lax.bitcast_convert_type(xb_ref[:, PACK_COLS:PACK_COLS + 1], F32)
        a = _dot(xb, wgb_ref[...])
        u = _dot(xb, wub_ref[...])
        mid = (jax.nn.silu(a) * u).astype(BF16)
        y = _dot(mid, wdb_ref[...]) * gate
        yb_ref[...] = _pack_bf16_pairs(y.astype(BF16).astype(F32))


def _ffn_call(xb, wg, wu, wd, blk_e, nused, *, n_blocks):
    def row_map(i, blk_e_ref, nused_ref):
        return (jnp.minimum(i, nused_ref[0] - 1), 0)

    def w_map(i, blk_e_ref, nused_ref):
        return (blk_e_ref[i], 0, 0)

    grid_spec = pltpu.PrefetchScalarGridSpec(
        num_scalar_prefetch=2,
        grid=(n_blocks,),
        in_specs=[
            pl.BlockSpec((FFN_BLOCK, XB_COLS), row_map),
            pl.BlockSpec((1, D_MODEL, D_EXPERT), w_map),
            pl.BlockSpec((1, D_MODEL, D_EXPERT), w_map),
            pl.BlockSpec((1, D_EXPERT, D_MODEL), w_map),
        ],
        out_specs=pl.BlockSpec((FFN_BLOCK, PACK_COLS), row_map),
        scratch_shapes=[
            pltpu.VMEM((D_MODEL, D_EXPERT), BF16),
            pltpu.VMEM((D_MODEL, D_EXPERT), BF16),
            pltpu.VMEM((D_EXPERT, D_MODEL), BF16),
        ],
    )
    return pl.pallas_call(
        _ffn_kernel,
        grid_spec=grid_spec,
        out_shape=jax.ShapeDtypeStruct((n_blocks * FFN_BLOCK, PACK_COLS), U32),
        compiler_params=pltpu.CompilerParams(dimension_semantics=("arbitrary",), vmem_limit_bytes=VMEM_LIMIT),
        name="expert_ffn",
    )(blk_e, nused, xb, wg, wu, wd)


def _combine_kernel(nch_ref, doff_ref, ntot_ref, y1_ref, route_ref, locoff_ref, upper_ref, fg_ref, yb_ref,
                    out_ref, buf_ref, sem, *, tile0):
    i = pl.program_id(0)
    n_tiles = pl.num_programs(0)
    slot = i % 2

    def chunk_copy(s, loc, src):
        return pltpu.make_async_copy(yb_ref.at[pl.ds(src, SUBLANES), :], buf_ref.at[s, pl.ds(loc, SUBLANES), :],
                                     sem.at[s])

    def issue_tile(tile, s):
        _chunk_loop(tile, nch_ref, doff_ref, lambda loc, src: chunk_copy(s, loc, src).start())

    @pl.when(i == 0)
    def _():
        buf_ref[...] = jnp.zeros_like(buf_ref)
        issue_tile(tile0, 0)

    @pl.when(i + 1 < n_tiles)
    def _():
        issue_tile(tile0 + i + 1, 1 - slot)

    p0, p1 = _tile_slots(route_ref, locoff_ref, upper_ref)
    iota_s = lax.broadcasted_iota(jnp.int32, (SLOT_ROWS, TOKEN_TILE), 0)
    perm = jnp.where((iota_s == p0) | (iota_s == p1), 1.0, 0.0).astype(BF16)

    n_rows = ntot_ref[tile0 + i] * SUBLANES

    @pl.when(n_rows > 0)
    def _():
        pltpu.make_async_copy(yb_ref.at[pl.ds(0, n_rows), :], buf_ref.at[slot, pl.ds(0, n_rows), :],
                              sem.at[slot]).wait()

    yb = _unpack_bf16_pairs(buf_ref[slot])
    moe = lax.dot_general(perm, yb, (((0,), (0,)), ((), ())), preferred_element_type=F32)
    out_ref[...] = _rms(y1_ref[...] + moe, fg_ref[...])


def _combine_call(y1, route, locoff, upper, fg, yb, nch, doff, ntot, *, tile0, n_tiles):
    grid_spec = pltpu.PrefetchScalarGridSpec(
        num_scalar_prefetch=3,
        grid=(n_tiles,),
        in_specs=[
            pl.BlockSpec((TOKEN_TILE, D_MODEL), lambda i, *_: (tile0 + i, 0)),
            pl.BlockSpec((SUBLANES, TOKEN_TILE), lambda i, *_: (0, tile0 + i)),
            pl.BlockSpec((1, N_EXPERTS, 1), lambda i, *_: (tile0 + i, 0, 0)),
            pl.BlockSpec((TOKEN_TILE, TOKEN_TILE), lambda i, *_: (0, 0)),
            pl.BlockSpec((1, D_MODEL), lambda i, *_: (0, 0)),
            pl.BlockSpec(memory_space=pl.ANY),
        ],
        out_specs=pl.BlockSpec((TOKEN_TILE, D_MODEL), lambda i, *_: (i, 0)),
        scratch_shapes=[
            pltpu.VMEM((2, SLOT_ROWS, PACK_COLS), U32),
            pltpu.SemaphoreType.DMA((2,)),
        ],
    )
    return pl.pallas_call(
        functools.partial(_combine_kernel, tile0=tile0),
        grid_spec=grid_spec,
        out_shape=jax.ShapeDtypeStruct((n_tiles * TOKEN_TILE, D_MODEL), F32),
        compiler_params=pltpu.CompilerParams(dimension_semantics=("arbitrary",), vmem_limit_bytes=VMEM_LIMIT),
        name="combine",
    )(nch, doff, ntot, y1, route, locoff, upper, fg, yb)


def _block_diag(w):
    h, d, _ = w.shape
    eye = jnp.eye(h, dtype=w.dtype)
    return (eye[:, None, :, None] * w[:, :, None, :]).reshape(h * d, h * d)


def _mixer_weights(l, length, reps, norm1_g, w_in, gmlp_ln_g, gmlp_ln_b, gmlp_w_s, gmlp_b_s, conv_w, conv_b,
                   lru_w_a, lru_b_a, lru_w_x, lru_b_x, lru_lambda, w_out, norm2_g,
                   router_group_w, router_group_b, router_expert_w, router_expert_b):
    mask = jnp.tril(jnp.ones((length, length), dtype=bool))
    ws = jnp.where(mask, gmlp_w_s[l][:, :length, :length], 0.0)
    eye = jnp.eye(reps, dtype=ws.dtype)
    wsm = (eye[None, :, None, :, None] * ws[:, None, :, None, :]).reshape(A_GROUPS, reps * length, reps * length)
    bs = jnp.tile(gmlp_b_s[l][:, :length], (1, reps))
    bsx = jnp.repeat(bs.T, A_GROUP_DIM, axis=1)
    wr = jnp.concatenate([router_expert_w[l].T, router_group_w[l].T,
                          jnp.zeros((ROUTER_ROWS - N_EXPERTS - N_GROUPS, D_MODEL), F32)], axis=0)
    br = jnp.concatenate([router_expert_b[l], router_group_b[l],
                          jnp.zeros((ROUTER_ROWS - N_EXPERTS - N_GROUPS,), F32)]).reshape(ROUTER_ROWS, 1)
    return [
        norm1_g[l].reshape(1, D_MODEL), w_in[l].astype(BF16),
        gmlp_ln_g[l].reshape(1, D_A), gmlp_ln_b[l].reshape(1, D_A), wsm.astype(BF16), bsx,
        conv_w[l], conv_b[l].reshape(1, D_B),
        _block_diag(lru_w_a[l]).astype(BF16), lru_b_a[l].reshape(1, D_B),
        _block_diag(lru_w_x[l]).astype(BF16), lru_b_x[l].reshape(1, D_B),
        lru_lambda[l].reshape(1, D_B), w_out[l].astype(BF16), norm2_g[l].reshape(1, D_MODEL),
        wr.astype(BF16), br,
    ]


def _routing_tables(cnt, n_blocks):
    n_tiles = cnt.shape[0]
    seg = (cnt + SUBLANES - 1) // SUBLANES * SUBLANES
    nch = seg // SUBLANES
    tot = jnp.sum(seg, axis=0)
    padded = (tot + FFN_BLOCK - 1) // FFN_BLOCK * FFN_BLOCK
    e_before = jnp.arange(N_EXPERTS)[None, :] < jnp.arange(N_EXPERTS)[:, None]
    t_before = jnp.arange(n_tiles)[None, :] < jnp.arange(n_tiles)[:, None]
    pad_start = jnp.sum(jnp.where(e_before, padded[None, :], 0), axis=1)
    pad_end = pad_start + padded
    doff = pad_start[None, :] + jnp.sum(jnp.where(t_before[:, :, None], seg[None, :, :], 0), axis=1)
    locoff = jnp.sum(jnp.where(e_before[None, :, :], seg[:, None, :], 0), axis=2)
    ntot = jnp.sum(nch, axis=1)
    tail = pad_start + tot
    ntail = (padded - tot) // SUBLANES
    blk_end = pad_end // FFN_BLOCK
    nused = blk_end[-1]
    blk = jnp.minimum(jnp.arange(n_blocks, dtype=jnp.int32), nused - 1)
    blk_e = jnp.minimum(jnp.sum((blk_end[None, :] <= blk[:, None]).astype(jnp.int32), axis=1), N_EXPERTS - 1)
    i32 = lambda a: a.astype(jnp.int32)
    return (i32(nch.reshape(-1)), i32(doff.reshape(-1)), i32(ntot), i32(tail), i32(ntail),
            locoff.astype(F32)[:, :, None], i32(blk_e), i32(nused.reshape(1)))


def kernel(x_prompt, x_sample, state_conv, state_rglru, norm1_g, w_in, gmlp_ln_g, gmlp_ln_b, gmlp_w_s, gmlp_b_s,
           conv_w, conv_b, lru_w_a, lru_b_a, lru_w_x, lru_b_x, lru_lambda, w_out, norm2_g,
           router_group_w, router_group_b, router_expert_w, router_expert_b,
           expert_w_gate, expert_w_up, expert_w_down, final_norm_g):
    depth = w_in.shape[0]
    nb, t_len, _ = x_prompt.shape
    db, dt, _ = x_sample.shape
    n_prompt = nb * t_len
    n_sample = db * dt
    assert t_len % PROMPT_TILE == 0 and n_prompt % TOKEN_TILE == 0 and n_sample == TOKEN_TILE
    n_total = n_prompt + n_sample
    n_tiles = n_total // TOKEN_TILE
    p_tiles = n_prompt // TOKEN_TILE
    p_rows = 2 * n_total + (SUBLANES - 1) * N_EXPERTS * n_tiles + N_EXPERTS * (FFN_BLOCK - 1)
    n_blocks = -(-p_rows // FFN_BLOCK)
    p_rows = n_blocks * FFN_BLOCK
    upper = jnp.triu(jnp.ones((TOKEN_TILE, TOKEN_TILE), BF16), 1)

    xp, xs = x_prompt, x_sample
    conv_p, h_p, v_s, conv_s, h_s = [], [], [], [], []
    for l in range(depth):
        lw = (norm1_g, w_in, gmlp_ln_g, gmlp_ln_b, gmlp_w_s, gmlp_b_s, conv_w, conv_b, lru_w_a, lru_b_a,
              lru_w_x, lru_b_x, lru_lambda, w_out, norm2_g, router_group_w, router_group_b,
              router_expert_w, router_expert_b)
        wts_p = _mixer_weights(l, GMLP_CHUNK, 1, *lw)
        wts_s = _mixer_weights(l, dt, db, *lw)
        zero_conv = jnp.zeros((nb, CONV_WIDTH - 1, D_B), F32)
        zero_h = jnp.zeros((nb, 1, D_B), F32)
        y1, h2, route, cnt_p, cp, hp = _mixer_call(
            xp, zero_conv, zero_h, wts_p, n_total=n_total, row_offset=0, sb=1, tt=PROMPT_TILE, chunk=GMLP_CHUNK,
            seg_len=PROMPT_TILE // SUBLANES, seg_stride=PROMPT_TILE // SUBLANES + SUBLANES, chain=True,
            emit_vn=False)
        y1, h2, route, cnt_s, cs, hs, vs = _mixer_call(
            xs, state_conv[l], state_rglru[l][None], wts_s, n_total=n_total, row_offset=n_prompt, sb=db,
            tt=dt, chunk=db * dt, seg_len=dt, seg_stride=dt + SUBLANES, chain=False, emit_vn=True,
            alias_in=(y1, h2, route))
        conv_p.append(cp)
        h_p.append(hp[:, 0])
        v_s.append(vs)
        conv_s.append(cs)
        h_s.append(hs[0])

        sub = PROMPT_TILE // TOKEN_TILE
        cnt = jnp.concatenate([
            jnp.swapaxes(cnt_p[:, :, :sub], 1, 2).reshape(p_tiles, N_EXPERTS),
            cnt_s[:, :, 0]], axis=0).astype(jnp.int32)
        nch, doff, ntot, tail, ntail, locoff, blk_e, nused = _routing_tables(cnt, n_blocks)

        xb = _dispatch_call(h2, route, locoff, upper, nch, doff, ntot, tail, ntail, n_tiles=n_tiles, p_rows=p_rows)
        yb = _ffn_call(xb, expert_w_gate[l], expert_w_up[l], expert_w_down[l], blk_e, nused, n_blocks=n_blocks)
        last = l == depth - 1
        fg = (final_norm_g if last else jnp.ones_like(final_norm_g)).reshape(1, D_MODEL)
        assert last, "deeper stacks need an un-normalised combine between layers"
        yp = _combine_call(y1, route, locoff, upper, fg, yb, nch, doff, ntot, tile0=0, n_tiles=p_tiles)
        ysm = _combine_call(y1, route, locoff, upper, fg, yb, nch, doff, ntot, tile0=p_tiles,
                            n_tiles=n_tiles - p_tiles)
        xp = yp.reshape(nb, t_len, D_MODEL)
        xs = ysm.reshape(db, dt, D_MODEL)

    return (xp, xs, jnp.stack(conv_p), jnp.stack(h_p), jnp.stack(v_s), jnp.stack(conv_s), jnp.stack(h_s))
```

```python
import functools

import jax
import jax.numpy as jnp
from jax import lax
from jax.experimental import pallas as pl
from jax.experimental.pallas import tpu as pltpu

D_MODEL = 1024
D_A = 512
D_B = 512
A_GROUPS = 4
A_GROUP_DIM = 128
GMLP_CHUNK = 128
B_HEADS = 8
CONV_WIDTH = 4
LRU_C = 8.0
N_GROUPS = 4
EXPERTS_PER_GROUP = 8
N_EXPERTS = 32
D_EXPERT = 512
EPS = 1e-6

SUBLANES = 8
LANES = 128
ROUTER_ROWS = 40
TOKEN_TILE = 256
PROMPT_TILE = 512
PROMPT_SUB = 1
PROMPT_SEG = PROMPT_TILE // PROMPT_SUB // SUBLANES
FFN_BLOCK = 512
FFN_CHUNK = 512
ZERO_CHUNKS = 8
ZERO_ROWS = ZERO_CHUNKS * SUBLANES
SLOT_ROWS = 2 * TOKEN_TILE + N_EXPERTS * SUBLANES
PACK_COLS = D_MODEL // 2
XB_COLS = PACK_COLS + LANES
U32 = jnp.uint32
HI_MASK = 0xFFFF0000
VMEM_LIMIT = 56 * 1024 * 1024

BF16 = jnp.bfloat16
F32 = jnp.float32


def _rms(x, g):
    return x * lax.rsqrt(jnp.mean(x * x, axis=-1, keepdims=True) + EPS) * g


def _dot(a, b):
    return jnp.dot(a, b, preferred_element_type=F32)


def _pack_bf16_pairs(x):
    bits = lax.bitcast_convert_type(x, U32)
    return (bits[:, PACK_COLS:] & U32(HI_MASK)) | (bits[:, :PACK_COLS] >> 16)


def _unpack_bf16_pairs(w):
    lo = lax.bitcast_convert_type(w << 16, F32).astype(BF16)
    hi = lax.bitcast_convert_type(w & U32(HI_MASK), F32).astype(BF16)
    return jnp.concatenate([lo, hi], axis=1)


def _mixer_kernel(x_ref, hist_ref, h0_ref, n1g_ref, win_ref, lng_ref, lnb_ref, wsm_ref, bsx_ref,
                  cw_ref, cb_ref, wa_ref, ba_ref, wx_ref, bx_ref, lam_ref, wout_ref, n2g_ref,
                  wr_ref, br_ref, *rest, sb, tt, n_sub, chunk, seg_len, seg_stride, chain, emit_vn, aliased):
    if aliased:
        rest = rest[3:]
    if emit_vn:
        (y1_ref, h2_ref, route_ref, cnt_ref, conv_ref, hlast_ref, vn_ref,
         xs_ref, sa_ref, sb_ref, mix_ref, hc_ref) = rest
    else:
        (y1_ref, h2_ref, route_ref, cnt_ref, conv_ref, hlast_ref,
         xs_ref, sa_ref, sb_ref, mix_ref, hc_ref) = rest
        vn_ref = None
    t_idx = pl.program_id(1)

    @pl.when(t_idx == 0)
    def _():
        xs_ref[:, SUBLANES - 3:SUBLANES, :] = hist_ref[...]
        hc_ref[...] = jnp.zeros_like(hc_ref)

    @pl.when(t_idx != 0)
    def _():
        xs_ref[:, SUBLANES - 3:SUBLANES, :] = xs_ref[:, SUBLANES + tt - 3:SUBLANES + tt, :]

    cnt = jnp.zeros((N_EXPERTS, LANES), F32)
    for sub in range(n_sub):
        cnt = _mixer_rows(sub, cnt, x_ref, h0_ref, n1g_ref, win_ref, lng_ref, lnb_ref, wsm_ref, bsx_ref, cw_ref,
                          cb_ref, wa_ref, ba_ref, wx_ref, bx_ref, lam_ref, wout_ref, n2g_ref, wr_ref, br_ref,
                          y1_ref, h2_ref, route_ref, hlast_ref, vn_ref, xs_ref, sa_ref, sb_ref, mix_ref, hc_ref,
                          sb=sb, tt=tt, n_sub=n_sub, chunk=chunk, seg_len=seg_len, seg_stride=seg_stride,
                          chain=chain)
    conv_ref[...] = xs_ref[:, SUBLANES + tt - 3:SUBLANES + tt, :]
    cnt_ref[...] = cnt.reshape(1, N_EXPERTS, LANES)


def _mixer_rows(sub, cnt, x_ref, h0_ref, n1g_ref, win_ref, lng_ref, lnb_ref, wsm_ref, bsx_ref, cw_ref, cb_ref,
                wa_ref, ba_ref, wx_ref, bx_ref, lam_ref, wout_ref, n2g_ref, wr_ref, br_ref,
                y1_ref, h2_ref, route_ref, hlast_ref, vn_ref, xs_ref, sa_ref, sb_ref, mix_ref, hc_ref,
                *, sb, tt, n_sub, chunk, seg_len, seg_stride, chain):
    tsub = tt // n_sub
    t0 = sub * tsub
    rows = sb * tsub
    r0 = sub * rows
    n_seg = rows // seg_len
    half = D_B // 2

    x = x_ref[:, t0:t0 + tsub, :].reshape(rows, D_MODEL)
    hb = _rms(x, n1g_ref[...]).astype(BF16)

    v_a = _dot(hb, win_ref[:, D_A:2 * D_A])
    x_b = _dot(hb, win_ref[:, 2 * D_A:2 * D_A + D_B])
    mixed = []
    for g in range(A_GROUPS):
        cols = slice(g * A_GROUP_DIM, (g + 1) * A_GROUP_DIM)
        vg = v_a[:, cols]
        mu = jnp.mean(vg, axis=-1, keepdims=True)
        dv = vg - mu
        var = jnp.mean(dv * dv, axis=-1, keepdims=True)
        vn = dv * lax.rsqrt(var + EPS) * lng_ref[:, cols] + lnb_ref[:, cols]
        if vn_ref is not None:
            vn_ref[:, t0:t0 + tsub, cols] = vn.reshape(sb, tsub, A_GROUP_DIM)
        vnb = vn.astype(BF16)
        mixed.append([_dot(wsm_ref[g], vnb[c * chunk:(c + 1) * chunk]) for c in range(rows // chunk)])
    u_a = _dot(hb, win_ref[:, 0:D_A])

    xs_ref[:, SUBLANES + t0:SUBLANES + t0 + tsub, :] = x_b.reshape(sb, tsub, D_B)
    xc = cb_ref[...] + x_b * cw_ref[3:4, :]
    for j in range(1, CONV_WIDTH):
        shifted = xs_ref[:, SUBLANES + t0 - j:SUBLANES + t0 - j + tsub, :].reshape(rows, D_B)
        xc = xc + shifted * cw_ref[3 - j:4 - j, :]

    xcb = xc.astype(BF16)
    r_lin = jnp.concatenate([_dot(xcb[:, :half], wa_ref[0]), _dot(xcb[:, half:], wa_ref[1])], axis=1)
    i_lin = jnp.concatenate([_dot(xcb[:, :half], wx_ref[0]), _dot(xcb[:, half:], wx_ref[1])], axis=1)

    for g in range(A_GROUPS):
        cols = slice(g * A_GROUP_DIM, (g + 1) * A_GROUP_DIM)
        for c in range(rows // chunk):
            rs = slice(c * chunk, (c + 1) * chunk)
            s = mixed[g][c] + bsx_ref[:, cols]
            mix_ref[r0 + c * chunk:r0 + (c + 1) * chunk, cols] = (u_a[rs, cols] * s).astype(BF16)

    g_b = _dot(hb, win_ref[:, 2 * D_A + D_B:])
    r = jax.nn.sigmoid(r_lin + ba_ref[...])
    i_g = jax.nn.sigmoid(i_lin + bx_ref[...])
    log_a = (-LRU_C * r) * jax.nn.softplus(-lam_ref[...])
    a = jnp.exp(log_a)
    gain = jnp.sqrt(1.0 - a * a)
    bterm = gain * i_g * xc

    n_lb = D_B // LANES
    sbase = sub * n_seg * seg_stride
    for j in range(n_seg):
        for k in range(n_lb):
            lc = slice(k * LANES, (k + 1) * LANES)
            dst = slice(sbase + j * seg_stride, sbase + j * seg_stride + seg_len)
            sa_ref[k, dst, :] = a[j * seg_len:(j + 1) * seg_len, lc]
            sb_ref[k, dst, :] = bterm[j * seg_len:(j + 1) * seg_len, lc]

    y_a = x + _dot(mix_ref[r0:r0 + rows, 0:D_A], wout_ref[0:D_A, :])
    gel = jax.nn.gelu(g_b)

    for grp in range(n_seg // SUBLANES):
        base = sbase + grp * SUBLANES * seg_stride
        bs = slice(grp * SUBLANES, (grp + 1) * SUBLANES)
        if chain:
            h_init = tuple(jnp.zeros((SUBLANES, LANES), F32) for _ in range(n_lb))
        else:
            h_init = tuple(h0_ref[0, bs, k * LANES:(k + 1) * LANES] for k in range(n_lb))
        a_init = tuple(jnp.ones((SUBLANES, LANES), F32) for _ in range(n_lb))

        def step(i, carry, base=base):
            hs, acs = carry
            idx = pl.ds(base + i, SUBLANES, stride=seg_stride)
            new_h, new_a = [], []
            for k in range(n_lb):
                av = sa_ref[k, idx, :]
                h = av * hs[k] + sb_ref[k, idx, :]
                sb_ref[k, idx, :] = h
                new_h.append(h)
                if chain:
                    ac = av * acs[k]
                    sa_ref[k, idx, :] = ac
                    new_a.append(ac)
                else:
                    new_a.append(acs[k])
            return tuple(new_h), tuple(new_a)

        carry = (h_init, a_init)
        for i in range(seg_len):
            carry = step(i, carry)
        h_end, a_end = carry

        for k in range(n_lb):
            lc = slice(k * LANES, (k + 1) * LANES)
            mc = slice(D_A + k * LANES, D_A + (k + 1) * LANES)
            if chain:
                h_in = hc_ref[:, lc]
                for j in range(SUBLANES):
                    seg = slice(base + j * seg_stride, base + j * seg_stride + seg_len)
                    rs = slice((grp * SUBLANES + j) * seg_len, (grp * SUBLANES + j + 1) * seg_len)
                    ms = slice(r0 + rs.start, r0 + rs.stop)
                    h_seg = sb_ref[k, seg, :] + sa_ref[k, seg, :] * h_in
                    mix_ref[ms, mc] = (h_seg * gel[rs, lc]).astype(BF16)
                    h_in = h_end[k][j:j + 1] + a_end[k][j:j + 1] * h_in
                hc_ref[:, lc] = h_in
                hlast_ref[0, :, lc] = h_in
            else:
                for j in range(SUBLANES):
                    seg = slice(base + j * seg_stride, base + j * seg_stride + seg_len)
                    rs = slice((grp * SUBLANES + j) * seg_len, (grp * SUBLANES + j + 1) * seg_len)
                    ms = slice(r0 + rs.start, r0 + rs.stop)
                    mix_ref[ms, mc] = (sb_ref[k, seg, :] * gel[rs, lc]).astype(BF16)
                hlast_ref[0, bs, lc] = h_end[k]

    y1 = y_a + _dot(mix_ref[r0:r0 + rows, D_A:], wout_ref[D_A:, :])
    y1_ref[r0:r0 + rows, :] = y1
    h2 = _rms(y1, n2g_ref[...]).astype(BF16)
    h2_ref[r0:r0 + rows, :] = h2

    logits = lax.dot_general(wr_ref[...], h2, (((1,), (1,)), ((), ())), preferred_element_type=F32) + br_ref[...]
    gl = [logits[N_EXPERTS + k:N_EXPERTS + k + 1, :] for k in range(N_GROUPS)]
    gmax = jnp.maximum(jnp.maximum(gl[0], gl[1]), jnp.maximum(gl[2], gl[3]))
    grp_idx = jnp.where(gl[0] == gmax, 0, jnp.where(gl[1] == gmax, 1, jnp.where(gl[2] == gmax, 2, 3)))
    gsum = (jnp.exp(gl[0] - gmax) + jnp.exp(gl[1] - gmax)) + (jnp.exp(gl[2] - gmax) + jnp.exp(gl[3] - gmax))
    p_grp = 1.0 / gsum
    e_in = jnp.where(grp_idx == 0, logits[0:8],
                     jnp.where(grp_idx == 1, logits[8:16], jnp.where(grp_idx == 2, logits[16:24], logits[24:32])))
    iota8 = lax.broadcasted_iota(jnp.int32, (EXPERTS_PER_GROUP, rows), 0)
    m1 = jnp.max(e_in, axis=0, keepdims=True)
    i1 = jnp.min(jnp.where(e_in == m1, iota8, EXPERTS_PER_GROUP), axis=0, keepdims=True)
    e_rest = jnp.where(iota8 == i1, -jnp.inf, e_in)
    m2 = jnp.max(e_rest, axis=0, keepdims=True)
    i2 = jnp.min(jnp.where(e_rest == m2, iota8, EXPERTS_PER_GROUP), axis=0, keepdims=True)
    t2 = jnp.exp(m2 - m1)
    den = 1.0 + t2
    gate0 = p_grp * (1.0 / den)
    gate1 = p_grp * (t2 / den)
    e0 = (grp_idx * EXPERTS_PER_GROUP + i1).astype(F32)
    e1 = (grp_idx * EXPERTS_PER_GROUP + i2).astype(F32)
    route_ref[:, r0:r0 + rows] = jnp.where(iota8 == 0, e0, jnp.where(iota8 == 1, e1, jnp.where(
        iota8 == 2, gate0, jnp.where(iota8 == 3, gate1, 0.0))))

    iota_e = lax.broadcasted_iota(jnp.int32, (N_EXPERTS, rows), 0).astype(F32)
    onehot = (iota_e == e0).astype(F32) + (iota_e == e1).astype(F32)
    lane = lax.broadcasted_iota(jnp.int32, (N_EXPERTS, LANES), 1)
    for s in range(rows // TOKEN_TILE):
        c_s = jnp.sum(onehot[:, s * TOKEN_TILE:(s + 1) * TOKEN_TILE], axis=1, keepdims=True)
        cnt = cnt + jnp.where(lane == r0 // TOKEN_TILE + s, c_s, 0.0)
    return cnt


def _mixer_call(x, hist, h0, wts, *, n_total, row_offset, sb, tt, n_sub, chunk, seg_len, seg_stride, chain,
                emit_vn, alias_in=None):
    nb, t_len, _ = x.shape
    nbb = nb // sb
    ntt = t_len // tt
    rows = sb * tt
    blk0 = row_offset // rows
    n_seg = rows // seg_len

    def full(arr):
        nd = arr.ndim
        return pl.BlockSpec(arr.shape, lambda b, t, _nd=nd: (0,) * _nd)

    in_specs = [
        pl.BlockSpec((sb, tt, D_MODEL), lambda b, t: (b, t, 0)),
        pl.BlockSpec((sb, CONV_WIDTH - 1, D_B), lambda b, t: (b, 0, 0)),
        pl.BlockSpec((1, sb, D_B), lambda b, t: (b, 0, 0)),
    ] + [full(w) for w in wts]
    args = [x, hist, h0] + list(wts)
    io_alias = {}
    if alias_in is not None:
        for k, arr in enumerate(alias_in):
            io_alias[len(args)] = k
            in_specs.append(pl.BlockSpec(memory_space=pl.ANY))
            args.append(arr)

    out_shape = [
        jax.ShapeDtypeStruct((n_total, D_MODEL), F32),
        jax.ShapeDtypeStruct((n_total, D_MODEL), BF16),
        jax.ShapeDtypeStruct((SUBLANES, n_total), F32),
        jax.ShapeDtypeStruct((nbb * ntt, N_EXPERTS, LANES), F32),
        jax.ShapeDtypeStruct((nb, CONV_WIDTH - 1, D_B), F32),
        jax.ShapeDtypeStruct((nbb, sb, D_B), F32),
    ]
    out_specs = [
        pl.BlockSpec((rows, D_MODEL), lambda b, t: (blk0 + b * ntt + t, 0)),
        pl.BlockSpec((rows, D_MODEL), lambda b, t: (blk0 + b * ntt + t, 0)),
        pl.BlockSpec((SUBLANES, rows), lambda b, t: (0, blk0 + b * ntt + t)),
        pl.BlockSpec((1, N_EXPERTS, LANES), lambda b, t: (b * ntt + t, 0, 0)),
        pl.BlockSpec((sb, CONV_WIDTH - 1, D_B), lambda b, t: (b, 0, 0)),
        pl.BlockSpec((1, sb, D_B), lambda b, t: (b, 0, 0)),
    ]
    if emit_vn:
        out_shape.append(jax.ShapeDtypeStruct((nb, t_len, D_A), F32))
        out_specs.append(pl.BlockSpec((sb, tt, D_A), lambda b, t: (b, t, 0)))

    assert n_sub == 1 or sb == 1
    kern = functools.partial(_mixer_kernel, sb=sb, tt=tt, n_sub=n_sub, chunk=chunk, seg_len=seg_len,
                             seg_stride=seg_stride,
                             chain=chain, emit_vn=emit_vn, aliased=alias_in is not None)
    return pl.pallas_call(
        kern,
        grid=(nbb, ntt),
        in_specs=in_specs,
        out_specs=out_specs,
        out_shape=out_shape,
        scratch_shapes=[
            pltpu.VMEM((sb, SUBLANES + tt, D_B), F32),
            pltpu.VMEM((D_B // LANES, n_seg * seg_stride, LANES), F32),
            pltpu.VMEM((D_B // LANES, n_seg * seg_stride, LANES), F32),
            pltpu.VMEM((rows, D_MODEL), BF16),
            pltpu.VMEM((1, D_B), F32),
        ],
        input_output_aliases=io_alias,
        compiler_params=pltpu.CompilerParams(dimension_semantics=("arbitrary", "arbitrary"),
                                             vmem_limit_bytes=VMEM_LIMIT),
        name="mixer_chain" if chain else "mixer_step",
    )(*args)


def _tile_slots(route_ref, locoff_ref, upper_ref):
    e0 = route_ref[0:1, :]
    e1 = route_ref[1:2, :]
    iota_e = lax.broadcasted_iota(jnp.int32, (N_EXPERTS, TOKEN_TILE), 0).astype(F32)
    oh0 = (iota_e == e0).astype(F32)
    oh1 = (iota_e == e1).astype(F32)
    c0 = _dot(oh0.astype(BF16), upper_ref[...])
    c1 = _dot(oh1.astype(BF16), upper_ref[...])
    cnt0 = jnp.sum(oh0, axis=1, keepdims=True)
    base0 = locoff_ref[0]
    base1 = base0 + cnt0
    p0 = jnp.sum(oh0 * (base0 + c0), axis=0, keepdims=True)
    p1 = jnp.sum(oh1 * (base1 + c1), axis=0, keepdims=True)
    return p0.astype(jnp.int32), p1.astype(jnp.int32)


def _chunk_loop(tile, nch_ref, doff_ref, make_copy):
    def per_expert(e, loc):
        n = nch_ref[tile * N_EXPERTS + e]
        d0 = doff_ref[tile * N_EXPERTS + e]

        def per_chunk(c, _):
            make_copy(pl.multiple_of(loc + c * SUBLANES, SUBLANES), pl.multiple_of(d0 + c * SUBLANES, SUBLANES))
            return 0

        lax.fori_loop(0, n, per_chunk, 0)
        return loc + n * SUBLANES

    lax.fori_loop(0, N_EXPERTS, per_expert, 0)


def _dispatch_kernel(nch_ref, doff_ref, ntot_ref, tail_ref, ntail_ref,
                     h2_ref, route_ref, locoff_ref, upper_ref, xb_ref, buf_ref, zero_ref, sem, zsem):
    i = pl.program_id(0)
    n_tiles = pl.num_programs(0)
    slot = i % 2

    def chunk_copy(s, loc, dst):
        return pltpu.make_async_copy(buf_ref.at[s, pl.ds(loc, SUBLANES), :], xb_ref.at[pl.ds(dst, SUBLANES), :],
                                     sem.at[s])

    def wait_tile(tile, s):
        n_rows = ntot_ref[tile] * SUBLANES

        @pl.when(n_rows > 0)
        def _():
            pltpu.make_async_copy(buf_ref.at[s, pl.ds(0, n_rows), :], xb_ref.at[pl.ds(0, n_rows), :],
                                  sem.at[s]).wait()

    @pl.when(i >= 2)
    def _():
        wait_tile(i - 2, slot)

    p0, p1 = _tile_slots(route_ref, locoff_ref, upper_ref)
    iota_s = lax.broadcasted_iota(jnp.int32, (SLOT_ROWS, TOKEN_TILE), 0)
    hit0 = iota_s == p0
    hit1 = iota_s == p1
    perm = jnp.where(hit0 | hit1, 1.0, 0.0).astype(BF16)
    buf_ref[slot, :, 0:PACK_COLS] = _pack_bf16_pairs(_dot(perm, h2_ref[...]))
    gates = jnp.where(hit0, route_ref[2:3, :], 0.0) + jnp.where(hit1, route_ref[3:4, :], 0.0)
    gcol = jnp.sum(gates, axis=1, keepdims=True)
    buf_ref[slot, :, PACK_COLS:XB_COLS] = lax.bitcast_convert_type(jnp.broadcast_to(gcol, (SLOT_ROWS, LANES)), U32)

    _chunk_loop(i, nch_ref, doff_ref, lambda loc, dst: chunk_copy(slot, loc, dst).start())

    @pl.when(i == n_tiles - 1)
    def _():
        zero_ref[...] = jnp.zeros_like(zero_ref)

        def zero_copy(dst, rows, k):
            return pltpu.make_async_copy(zero_ref.at[pl.ds(0, rows), :], xb_ref.at[pl.ds(dst, rows), :], zsem.at[k])

        def per_expert(e, tot):
            n_big, n_small = tot
            n = ntail_ref[e]
            d0 = tail_ref[e]
            nb = n // ZERO_CHUNKS
            ns = n - nb * ZERO_CHUNKS

            def big(c, _):
                zero_copy(pl.multiple_of(d0 + c * ZERO_ROWS, SUBLANES), ZERO_ROWS, 0).start()
                return 0

            def small(c, _):
                zero_copy(pl.multiple_of(d0 + nb * ZERO_ROWS + c * SUBLANES, SUBLANES), SUBLANES, 1).start()
                return 0

            lax.fori_loop(0, nb, big, 0)
            lax.fori_loop(0, ns, small, 0)
            return n_big + nb, n_small + ns

        n_big, n_small = lax.fori_loop(0, N_EXPERTS, per_expert, (0, 0))

        @pl.when(i >= 1)
        def _():
            wait_tile(i - 1, 1 - slot)

        wait_tile(i, slot)

        def wait_big(c, _):
            zero_copy(0, ZERO_ROWS, 0).wait()
            return 0

        def wait_small(c, _):
            zero_copy(0, SUBLANES, 1).wait()
            return 0

        lax.fori_loop(0, n_big, wait_big, 0)
        lax.fori_loop(0, n_small, wait_small, 0)


def _dispatch_call(h2, route, locoff, upper, nch, doff, ntot, tail, ntail, *, n_tiles, p_rows):
    grid_spec = pltpu.PrefetchScalarGridSpec(
        num_scalar_prefetch=5,
        grid=(n_tiles,),
        in_specs=[
            pl.BlockSpec((TOKEN_TILE, D_MODEL), lambda i, *_: (i, 0)),
            pl.BlockSpec((SUBLANES, TOKEN_TILE), lambda i, *_: (0, i)),
            pl.BlockSpec((1, N_EXPERTS, 1), lambda i, *_: (i, 0, 0)),
            pl.BlockSpec((TOKEN_TILE, TOKEN_TILE), lambda i, *_: (0, 0)),
        ],
        out_specs=pl.BlockSpec(memory_space=pl.ANY),
        scratch_shapes=[
            pltpu.VMEM((2, SLOT_ROWS, XB_COLS), U32),
            pltpu.VMEM((ZERO_ROWS, XB_COLS), U32),
            pltpu.SemaphoreType.DMA((2,)),
            pltpu.SemaphoreType.DMA((2,)),
        ],
    )
    return pl.pallas_call(
        _dispatch_kernel,
        grid_spec=grid_spec,
        out_shape=jax.ShapeDtypeStruct((p_rows, XB_COLS), U32),
        compiler_params=pltpu.CompilerParams(dimension_semantics=("arbitrary",), vmem_limit_bytes=VMEM_LIMIT),
        name="dispatch",
    )(nch, doff, ntot, tail, ntail, h2, route, locoff, upper)


def _ffn_kernel(blk_e_ref, nused_ref, xb_ref, wg_ref, wu_ref, wd_ref, yb_ref, wgb_ref, wub_ref, wdb_ref):
    i = pl.program_id(0)
    new_expert = (i == 0) | (blk_e_ref[i] != blk_e_ref[jnp.maximum(i - 1, 0)])

    @pl.when(new_expert)
    def _():
        wgb_ref[...] = wg_ref[0].astype(BF16)
        wub_ref[...] = wu_ref[0].astype(BF16)
        wdb_ref[...] = wd_ref[0].astype(BF16)

    @pl.when(i < nused_ref[0])
    def _():
        for c in range(FFN_BLOCK // FFN_CHUNK):
            rs = slice(c * FFN_CHUNK, (c + 1) * FFN_CHUNK)
            xb = _unpack_bf16_pairs(xb_ref[rs, 0:PACK_COLS])
            gate = lax.bitcast_convert_type(xb_ref[rs, PACK_COLS:PACK_COLS + 1], F32)
            a = _dot(xb, wgb_ref[...])
            u = _dot(xb, wub_ref[...])
            mid = (jax.nn.silu(a) * u).astype(BF16)
            y = _dot(mid, wdb_ref[...]) * gate
            yb_ref[rs, :] = _pack_bf16_pairs(y.astype(BF16).astype(F32))


def _ffn_call(xb, wg, wu, wd, blk_e, nused, *, n_blocks):
    def row_map(i, blk_e_ref, nused_ref):
        return (jnp.minimum(i, nused_ref[0] - 1), 0)

    def w_map(i, blk_e_ref, nused_ref):
        return (blk_e_ref[i], 0, 0)

    grid_spec = pltpu.PrefetchScalarGridSpec(
        num_scalar_prefetch=2,
        grid=(n_blocks,),
        in_specs=[
            pl.BlockSpec((FFN_BLOCK, XB_COLS), row_map),
            pl.BlockSpec((1, D_MODEL, D_EXPERT), w_map),
            pl.BlockSpec((1, D_MODEL, D_EXPERT), w_map),
            pl.BlockSpec((1, D_EXPERT, D_MODEL), w_map),
        ],
        out_specs=pl.BlockSpec((FFN_BLOCK, PACK_COLS), row_map),
        scratch_shapes=[
            pltpu.VMEM((D_MODEL, D_EXPERT), BF16),
            pltpu.VMEM((D_MODEL, D_EXPERT), BF16),
            pltpu.VMEM((D_EXPERT, D_MODEL), BF16),
        ],
    )
    return pl.pallas_call(
        _ffn_kernel,
        grid_spec=grid_spec,
        out_shape=jax.ShapeDtypeStruct((n_blocks * FFN_BLOCK, PACK_COLS), U32),
        compiler_params=pltpu.CompilerParams(dimension_semantics=("arbitrary",), vmem_limit_bytes=VMEM_LIMIT),
        name="expert_ffn",
    )(blk_e, nused, xb, wg, wu, wd)


def _combine_kernel(nch_ref, doff_ref, ntot_ref, y1_ref, route_ref, locoff_ref, upper_ref, fg_ref, yb_ref,
                    out_ref, buf_ref, sem, *, tile0):
    i = pl.program_id(0)
    n_tiles = pl.num_programs(0)
    slot = i % 2

    def chunk_copy(s, loc, src):
        return pltpu.make_async_copy(yb_ref.at[pl.ds(src, SUBLANES), :], buf_ref.at[s, pl.ds(loc, SUBLANES), :],
                                     sem.at[s])

    def issue_tile(tile, s):
        _chunk_loop(tile, nch_ref, doff_ref, lambda loc, src: chunk_copy(s, loc, src).start())

    @pl.when(i == 0)
    def _():
        buf_ref[...] = jnp.zeros_like(buf_ref)
        issue_tile(tile0, 0)

    @pl.when(i + 1 < n_tiles)
    def _():
        issue_tile(tile0 + i + 1, 1 - slot)

    p0, p1 = _tile_slots(route_ref, locoff_ref, upper_ref)
    iota_s = lax.broadcasted_iota(jnp.int32, (SLOT_ROWS, TOKEN_TILE), 0)
    perm = jnp.where((iota_s == p0) | (iota_s == p1), 1.0, 0.0).astype(BF16)

    n_rows = ntot_ref[tile0 + i] * SUBLANES

    @pl.when(n_rows > 0)
    def _():
        pltpu.make_async_copy(yb_ref.at[pl.ds(0, n_rows), :], buf_ref.at[slot, pl.ds(0, n_rows), :],
                              sem.at[slot]).wait()

    yb = _unpack_bf16_pairs(buf_ref[slot])
    moe = lax.dot_general(perm, yb, (((0,), (0,)), ((), ())), preferred_element_type=F32)
    out_ref[...] = _rms(y1_ref[...] + moe, fg_ref[...])


def _combine_call(y1, route, locoff, upper, fg, yb, nch, doff, ntot, *, tile0, n_tiles):
    grid_spec = pltpu.PrefetchScalarGridSpec(
        num_scalar_prefetch=3,
        grid=(n_tiles,),
        in_specs=[
            pl.BlockSpec((TOKEN_TILE, D_MODEL), lambda i, *_: (tile0 + i, 0)),
            pl.BlockSpec((SUBLANES, TOKEN_TILE), lambda i, *_: (0, tile0 + i)),
            pl.BlockSpec((1, N_EXPERTS, 1), lambda i, *_: (tile0 + i, 0, 0)),
            pl.BlockSpec((TOKEN_TILE, TOKEN_TILE), lambda i, *_: (0, 0)),
            pl.BlockSpec((1, D_MODEL), lambda i, *_: (0, 0)),
            pl.BlockSpec(memory_space=pl.ANY),
        ],
        out_specs=pl.BlockSpec((TOKEN_TILE, D_MODEL), lambda i, *_: (i, 0)),
        scratch_shapes=[
            pltpu.VMEM((2, SLOT_ROWS, PACK_COLS), U32),
            pltpu.SemaphoreType.DMA((2,)),
        ],
    )
    return pl.pallas_call(
        functools.partial(_combine_kernel, tile0=tile0),
        grid_spec=grid_spec,
        out_shape=jax.ShapeDtypeStruct((n_tiles * TOKEN_TILE, D_MODEL), F32),
        compiler_params=pltpu.CompilerParams(dimension_semantics=("arbitrary",), vmem_limit_bytes=VMEM_LIMIT),
        name="combine",
    )(nch, doff, ntot, y1, route, locoff, upper, fg, yb)


def _block_diag(w):
    h, d, _ = w.shape
    eye = jnp.eye(h, dtype=w.dtype)
    return (eye[:, None, :, None] * w[:, :, None, :]).reshape(h * d, h * d)


def _head_blocks(w):
    half = B_HEADS // 2
    return jnp.stack([_block_diag(w[:half]), _block_diag(w[half:])]).astype(BF16)


def _mixer_weights(l, length, reps, norm1_g, w_in, gmlp_ln_g, gmlp_ln_b, gmlp_w_s, gmlp_b_s, conv_w, conv_b,
                   lru_w_a, lru_b_a, lru_w_x, lru_b_x, lru_lambda, w_out, norm2_g,
                   router_group_w, router_group_b, router_expert_w, router_expert_b):
    mask = jnp.tril(jnp.ones((length, length), dtype=bool))
    ws = jnp.where(mask, gmlp_w_s[l][:, :length, :length], 0.0)
    eye = jnp.eye(reps, dtype=ws.dtype)
    wsm = (eye[None, :, None, :, None] * ws[:, None, :, None, :]).reshape(A_GROUPS, reps * length, reps * length)
    bs = jnp.tile(gmlp_b_s[l][:, :length], (1, reps))
    bsx = jnp.repeat(bs.T, A_GROUP_DIM, axis=1)
    wr = jnp.concatenate([router_expert_w[l].T, router_group_w[l].T,
                          jnp.zeros((ROUTER_ROWS - N_EXPERTS - N_GROUPS, D_MODEL), F32)], axis=0)
    br = jnp.concatenate([router_expert_b[l], router_group_b[l],
                          jnp.zeros((ROUTER_ROWS - N_EXPERTS - N_GROUPS,), F32)]).reshape(ROUTER_ROWS, 1)
    return [
        norm1_g[l].reshape(1, D_MODEL), w_in[l].astype(BF16),
        gmlp_ln_g[l].reshape(1, D_A), gmlp_ln_b[l].reshape(1, D_A), wsm.astype(BF16), bsx,
        conv_w[l], conv_b[l].reshape(1, D_B),
        _head_blocks(lru_w_a[l]), lru_b_a[l].reshape(1, D_B),
        _head_blocks(lru_w_x[l]), lru_b_x[l].reshape(1, D_B),
        lru_lambda[l].reshape(1, D_B), w_out[l].astype(BF16), norm2_g[l].reshape(1, D_MODEL),
        wr.astype(BF16), br,
    ]


def _routing_tables(cnt, n_blocks):
    n_tiles = cnt.shape[0]
    seg = (cnt + SUBLANES - 1) // SUBLANES * SUBLANES
    nch = seg // SUBLANES
    tot = jnp.sum(seg, axis=0)
    padded = (tot + FFN_BLOCK - 1) // FFN_BLOCK * FFN_BLOCK
    e_before = jnp.arange(N_EXPERTS)[None, :] < jnp.arange(N_EXPERTS)[:, None]
    t_before = jnp.arange(n_tiles)[None, :] < jnp.arange(n_tiles)[:, None]
    pad_start = jnp.sum(jnp.where(e_before, padded[None, :], 0), axis=1)
    pad_end = pad_start + padded
    doff = pad_start[None, :] + jnp.sum(jnp.where(t_before[:, :, None], seg[None, :, :], 0), axis=1)
    locoff = jnp.sum(jnp.where(e_before[None, :, :], seg[:, None, :], 0), axis=2)
    ntot = jnp.sum(nch, axis=1)
    tail = pad_start + tot
    ntail = (padded - tot) // SUBLANES
    blk_end = pad_end // FFN_BLOCK
    nused = blk_end[-1]
    blk = jnp.minimum(jnp.arange(n_blocks, dtype=jnp.int32), nused - 1)
    blk_e = jnp.minimum(jnp.sum((blk_end[None, :] <= blk[:, None]).astype(jnp.int32), axis=1), N_EXPERTS - 1)
    i32 = lambda a: a.astype(jnp.int32)
    return (i32(nch.reshape(-1)), i32(doff.reshape(-1)), i32(ntot), i32(tail), i32(ntail),
            locoff.astype(F32)[:, :, None], i32(blk_e), i32(nused.reshape(1)))


def kernel(x_prompt, x_sample, state_conv, state_rglru, norm1_g, w_in, gmlp_ln_g, gmlp_ln_b, gmlp_w_s, gmlp_b_s,
           conv_w, conv_b, lru_w_a, lru_b_a, lru_w_x, lru_b_x, lru_lambda, w_out, norm2_g,
           router_group_w, router_group_b, router_expert_w, router_expert_b,
           expert_w_gate, expert_w_up, expert_w_down, final_norm_g):
    depth = w_in.shape[0]
    nb, t_len, _ = x_prompt.shape
    db, dt, _ = x_sample.shape
    n_prompt = nb * t_len
    n_sample = db * dt
    assert t_len % PROMPT_TILE == 0 and n_prompt % TOKEN_TILE == 0 and n_sample == TOKEN_TILE
    n_total = n_prompt + n_sample
    n_tiles = n_total // TOKEN_TILE
    p_tiles = n_prompt // TOKEN_TILE
    p_rows = 2 * n_total + (SUBLANES - 1) * N_EXPERTS * n_tiles + N_EXPERTS * (FFN_BLOCK - 1)
    n_blocks = -(-p_rows // FFN_BLOCK)
    p_rows = n_blocks * FFN_BLOCK
    upper = jnp.triu(jnp.ones((TOKEN_TILE, TOKEN_TILE), BF16), 1)

    xp, xs = x_prompt, x_sample
    conv_p, h_p, v_s, conv_s, h_s = [], [], [], [], []
    for l in range(depth):
        lw = (norm1_g, w_in, gmlp_ln_g, gmlp_ln_b, gmlp_w_s, gmlp_b_s, conv_w, conv_b, lru_w_a, lru_b_a,
              lru_w_x, lru_b_x, lru_lambda, w_out, norm2_g, router_group_w, router_group_b,
              router_expert_w, router_expert_b)
        wts_p = _mixer_weights(l, GMLP_CHUNK, 1, *lw)
        wts_s = _mixer_weights(l, dt, db, *lw)
        zero_conv = jnp.zeros((nb, CONV_WIDTH - 1, D_B), F32)
        zero_h = jnp.zeros((nb, 1, D_B), F32)
        y1, h2, route, cnt_p, cp, hp = _mixer_call(
            xp, zero_conv, zero_h, wts_p, n_total=n_total, row_offset=0, sb=1, tt=PROMPT_TILE, n_sub=PROMPT_SUB,
            chunk=GMLP_CHUNK, seg_len=PROMPT_SEG, seg_stride=PROMPT_SEG + SUBLANES, chain=True, emit_vn=False)
        y1, h2, route, cnt_s, cs, hs, vs = _mixer_call(
            xs, state_conv[l], state_rglru[l][None], wts_s, n_total=n_total, row_offset=n_prompt, sb=db,
            tt=dt, n_sub=1, chunk=db * dt, seg_len=dt, seg_stride=dt + SUBLANES, chain=False, emit_vn=True,
            alias_in=(y1, h2, route))
        conv_p.append(cp)
        h_p.append(hp[:, 0])
        v_s.append(vs)
        conv_s.append(cs)
        h_s.append(hs[0])

        sub = PROMPT_TILE // TOKEN_TILE
        cnt = jnp.concatenate([
            jnp.swapaxes(cnt_p[:, :, :sub], 1, 2).reshape(p_tiles, N_EXPERTS),
            cnt_s[:, :, 0]], axis=0).astype(jnp.int32)
        nch, doff, ntot, tail, ntail, locoff, blk_e, nused = _routing_tables(cnt, n_blocks)

        xb = _dispatch_call(h2, route, locoff, upper, nch, doff, ntot, tail, ntail, n_tiles=n_tiles, p_rows=p_rows)
        yb = _ffn_call(xb, expert_w_gate[l], expert_w_up[l], expert_w_down[l], blk_e, nused, n_blocks=n_blocks)
        last = l == depth - 1
        fg = (final_norm_g if last else jnp.ones_like(final_norm_g)).reshape(1, D_MODEL)
        assert last, "deeper stacks need an un-normalised combine between layers"
        yp = _combine_call(y1, route, locoff, upper, fg, yb, nch, doff, ntot, tile0=0, n_tiles=p_tiles)
        ysm = _combine_call(y1, route, locoff, upper, fg, yb, nch, doff, ntot, tile0=p_tiles,
                            n_tiles=n_tiles - p_tiles)
        xp = yp.reshape(nb, t_len, D_MODEL)
        xs = ysm.reshape(db, dt, D_MODEL)

    return (xp, xs, jnp.stack(conv_p), jnp.stack(h_p), jnp.stack(v_s), jnp.stack(conv_s), jnp.stack(h_s))
```

```python
import functools

import jax
import jax.numpy as jnp
from jax import lax
from jax.experimental import pallas as pl
from jax.experimental.pallas import tpu as pltpu

D_MODEL = 1024
D_A = 512
D_B = 512
A_GROUPS = 4
A_GROUP_DIM = 128
GMLP_CHUNK = 128
B_HEADS = 8
CONV_WIDTH = 4
LRU_C = 8.0
N_GROUPS = 4
EXPERTS_PER_GROUP = 8
N_EXPERTS = 32
D_EXPERT = 512
EPS = 1e-6

SUBLANES = 8
LANES = 128
ROUTER_ROWS = 40
TOKEN_TILE = 256
PROMPT_TILE = 512
PROMPT_SUB = 1
PROMPT_SEG = PROMPT_TILE // PROMPT_SUB // SUBLANES
FFN_BLOCK = 512
FFN_CHUNK = 512
ZERO_CHUNKS = 8
ZERO_ROWS = ZERO_CHUNKS * SUBLANES
SLOT_ROWS = 2 * TOKEN_TILE + N_EXPERTS * SUBLANES
TILE_CHUNKS = SLOT_ROWS // SUBLANES
TILES_PER_STEP = 3
STEP_TOKENS = TILES_PER_STEP * TOKEN_TILE
PACK_COLS = D_MODEL // 2
XB_COLS = PACK_COLS + LANES
U32 = jnp.uint32
HI_MASK = 0xFFFF0000
VMEM_LIMIT = 56 * 1024 * 1024

BF16 = jnp.bfloat16
F32 = jnp.float32


def _rms(x, g):
    return x * lax.rsqrt(jnp.mean(x * x, axis=-1, keepdims=True) + EPS) * g


def _dot(a, b):
    return jnp.dot(a, b, preferred_element_type=F32)


def _pack_bf16_pairs(x):
    bits = lax.bitcast_convert_type(x, U32)
    return (bits[:, PACK_COLS:] & U32(HI_MASK)) | (bits[:, :PACK_COLS] >> 16)


def _unpack_bf16_pairs(w):
    lo = lax.bitcast_convert_type(w << 16, F32).astype(BF16)
    hi = lax.bitcast_convert_type(w & U32(HI_MASK), F32).astype(BF16)
    return jnp.concatenate([lo, hi], axis=1)


def _mixer_kernel(x_ref, hist_ref, h0_ref, n1g_ref, win_ref, lng_ref, lnb_ref, wsm_ref, bsx_ref,
                  cw_ref, cb_ref, wa_ref, ba_ref, wx_ref, bx_ref, lam_ref, wout_ref, n2g_ref,
                  wr_ref, br_ref, *rest, sb, tt, n_sub, chunk, seg_len, seg_stride, chain, emit_vn, aliased):
    if aliased:
        rest = rest[3:]
    if emit_vn:
        (y1_ref, h2_ref, route_ref, cnt_ref, conv_ref, hlast_ref, vn_ref,
         xs_ref, sa_ref, sb_ref, mix_ref, hc_ref) = rest
    else:
        (y1_ref, h2_ref, route_ref, cnt_ref, conv_ref, hlast_ref,
         xs_ref, sa_ref, sb_ref, mix_ref, hc_ref) = rest
        vn_ref = None
    t_idx = pl.program_id(1)

    @pl.when(t_idx == 0)
    def _():
        xs_ref[:, SUBLANES - 3:SUBLANES, :] = hist_ref[...]
        hc_ref[...] = jnp.zeros_like(hc_ref)

    @pl.when(t_idx != 0)
    def _():
        xs_ref[:, SUBLANES - 3:SUBLANES, :] = xs_ref[:, SUBLANES + tt - 3:SUBLANES + tt, :]

    cnt = jnp.zeros((N_EXPERTS, LANES), F32)
    for sub in range(n_sub):
        cnt = _mixer_rows(sub, cnt, x_ref, h0_ref, n1g_ref, win_ref, lng_ref, lnb_ref, wsm_ref, bsx_ref, cw_ref,
                          cb_ref, wa_ref, ba_ref, wx_ref, bx_ref, lam_ref, wout_ref, n2g_ref, wr_ref, br_ref,
                          y1_ref, h2_ref, route_ref, hlast_ref, vn_ref, xs_ref, sa_ref, sb_ref, mix_ref, hc_ref,
                          sb=sb, tt=tt, n_sub=n_sub, chunk=chunk, seg_len=seg_len, seg_stride=seg_stride,
                          chain=chain)
    conv_ref[...] = xs_ref[:, SUBLANES + tt - 3:SUBLANES + tt, :]
    cnt_ref[...] = cnt.reshape(1, N_EXPERTS, LANES)


def _mixer_rows(sub, cnt, x_ref, h0_ref, n1g_ref, win_ref, lng_ref, lnb_ref, wsm_ref, bsx_ref, cw_ref, cb_ref,
                wa_ref, ba_ref, wx_ref, bx_ref, lam_ref, wout_ref, n2g_ref, wr_ref, br_ref,
                y1_ref, h2_ref, route_ref, hlast_ref, vn_ref, xs_ref, sa_ref, sb_ref, mix_ref, hc_ref,
                *, sb, tt, n_sub, chunk, seg_len, seg_stride, chain):
    tsub = tt // n_sub
    t0 = sub * tsub
    rows = sb * tsub
    r0 = sub * rows
    n_seg = rows // seg_len
    half = D_B // 2

    x = x_ref[:, t0:t0 + tsub, :].reshape(rows, D_MODEL)
    hb = _rms(x, n1g_ref[...]).astype(BF16)

    v_a = _dot(hb, win_ref[:, D_A:2 * D_A])
    x_b = _dot(hb, win_ref[:, 2 * D_A:2 * D_A + D_B])
    mixed = []
    for g in range(A_GROUPS):
        cols = slice(g * A_GROUP_DIM, (g + 1) * A_GROUP_DIM)
        vg = v_a[:, cols]
        mu = jnp.mean(vg, axis=-1, keepdims=True)
        dv = vg - mu
        var = jnp.mean(dv * dv, axis=-1, keepdims=True)
        vn = dv * lax.rsqrt(var + EPS) * lng_ref[:, cols] + lnb_ref[:, cols]
        if vn_ref is not None:
            vn_ref[:, t0:t0 + tsub, cols] = vn.reshape(sb, tsub, A_GROUP_DIM)
        vnb = vn.astype(BF16)
        mixed.append([_dot(wsm_ref[g], vnb[c * chunk:(c + 1) * chunk]) for c in range(rows // chunk)])
    u_a = _dot(hb, win_ref[:, 0:D_A])

    xs_ref[:, SUBLANES + t0:SUBLANES + t0 + tsub, :] = x_b.reshape(sb, tsub, D_B)
    xc = cb_ref[...] + x_b * cw_ref[3:4, :]
    for j in range(1, CONV_WIDTH):
        shifted = xs_ref[:, SUBLANES + t0 - j:SUBLANES + t0 - j + tsub, :].reshape(rows, D_B)
        xc = xc + shifted * cw_ref[3 - j:4 - j, :]

    xcb = xc.astype(BF16)
    r_lin = jnp.concatenate([_dot(xcb[:, :half], wa_ref[0]), _dot(xcb[:, half:], wa_ref[1])], axis=1)
    i_lin = jnp.concatenate([_dot(xcb[:, :half], wx_ref[0]), _dot(xcb[:, half:], wx_ref[1])], axis=1)

    for g in range(A_GROUPS):
        cols = slice(g * A_GROUP_DIM, (g + 1) * A_GROUP_DIM)
        for c in range(rows // chunk):
            rs = slice(c * chunk, (c + 1) * chunk)
            s = mixed[g][c] + bsx_ref[:, cols]
            mix_ref[r0 + c * chunk:r0 + (c + 1) * chunk, cols] = (u_a[rs, cols] * s).astype(BF16)

    g_b = _dot(hb, win_ref[:, 2 * D_A + D_B:])
    r = jax.nn.sigmoid(r_lin + ba_ref[...])
    i_g = jax.nn.sigmoid(i_lin + bx_ref[...])
    log_a = (-LRU_C * r) * jax.nn.softplus(-lam_ref[...])
    a = jnp.exp(log_a)
    gain = jnp.sqrt(1.0 - a * a)
    bterm = gain * i_g * xc

    n_lb = D_B // LANES
    sbase = sub * n_seg * seg_stride
    for j in range(n_seg):
        for k in range(n_lb):
            lc = slice(k * LANES, (k + 1) * LANES)
            dst = slice(sbase + j * seg_stride, sbase + j * seg_stride + seg_len)
            sa_ref[k, dst, :] = a[j * seg_len:(j + 1) * seg_len, lc]
            sb_ref[k, dst, :] = bterm[j * seg_len:(j + 1) * seg_len, lc]

    y_a = x + _dot(mix_ref[r0:r0 + rows, 0:D_A], wout_ref[0:D_A, :])
    gel = jax.nn.gelu(g_b)

    for grp in range(n_seg // SUBLANES):
        base = sbase + grp * SUBLANES * seg_stride
        bs = slice(grp * SUBLANES, (grp + 1) * SUBLANES)
        if chain:
            h_init = tuple(jnp.zeros((SUBLANES, LANES), F32) for _ in range(n_lb))
        else:
            h_init = tuple(h0_ref[0, bs, k * LANES:(k + 1) * LANES] for k in range(n_lb))
        a_init = tuple(jnp.ones((SUBLANES, LANES), F32) for _ in range(n_lb))

        def step(i, carry, base=base):
            hs, acs = carry
            idx = pl.ds(base + i, SUBLANES, stride=seg_stride)
            new_h, new_a = [], []
            for k in range(n_lb):
                av = sa_ref[k, idx, :]
                h = av * hs[k] + sb_ref[k, idx, :]
                sb_ref[k, idx, :] = h
                new_h.append(h)
                if chain:
                    ac = av * acs[k]
                    sa_ref[k, idx, :] = ac
                    new_a.append(ac)
                else:
                    new_a.append(acs[k])
            return tuple(new_h), tuple(new_a)

        carry = (h_init, a_init)
        for i in range(seg_len):
            carry = step(i, carry)
        h_end, a_end = carry

        for k in range(n_lb):
            lc = slice(k * LANES, (k + 1) * LANES)
            mc = slice(D_A + k * LANES, D_A + (k + 1) * LANES)
            if chain:
                h_in = hc_ref[:, lc]
                for j in range(SUBLANES):
                    seg = slice(base + j * seg_stride, base + j * seg_stride + seg_len)
                    rs = slice((grp * SUBLANES + j) * seg_len, (grp * SUBLANES + j + 1) * seg_len)
                    ms = slice(r0 + rs.start, r0 + rs.stop)
                    h_seg = sb_ref[k, seg, :] + sa_ref[k, seg, :] * h_in
                    mix_ref[ms, mc] = (h_seg * gel[rs, lc]).astype(BF16)
                    h_in = h_end[k][j:j + 1] + a_end[k][j:j + 1] * h_in
                hc_ref[:, lc] = h_in
                hlast_ref[0, :, lc] = h_in
            else:
                for j in range(SUBLANES):
                    seg = slice(base + j * seg_stride, base + j * seg_stride + seg_len)
                    rs = slice((grp * SUBLANES + j) * seg_len, (grp * SUBLANES + j + 1) * seg_len)
                    ms = slice(r0 + rs.start, r0 + rs.stop)
                    mix_ref[ms, mc] = (sb_ref[k, seg, :] * gel[rs, lc]).astype(BF16)
                hlast_ref[0, bs, lc] = h_end[k]

    y1 = y_a + _dot(mix_ref[r0:r0 + rows, D_A:], wout_ref[D_A:, :])
    y1_ref[r0:r0 + rows, :] = y1
    h2 = _rms(y1, n2g_ref[...]).astype(BF16)
    h2_ref[r0:r0 + rows, :] = h2

    logits = lax.dot_general(wr_ref[...], h2, (((1,), (1,)), ((), ())), preferred_element_type=F32) + br_ref[...]
    gl = [logits[N_EXPERTS + k:N_EXPERTS + k + 1, :] for k in range(N_GROUPS)]
    gmax = jnp.maximum(jnp.maximum(gl[0], gl[1]), jnp.maximum(gl[2], gl[3]))
    grp_idx = jnp.where(gl[0] == gmax, 0, jnp.where(gl[1] == gmax, 1, jnp.where(gl[2] == gmax, 2, 3)))
    gsum = (jnp.exp(gl[0] - gmax) + jnp.exp(gl[1] - gmax)) + (jnp.exp(gl[2] - gmax) + jnp.exp(gl[3] - gmax))
    p_grp = 1.0 / gsum
    e_in = jnp.where(grp_idx == 0, logits[0:8],
                     jnp.where(grp_idx == 1, logits[8:16], jnp.where(grp_idx == 2, logits[16:24], logits[24:32])))
    iota8 = lax.broadcasted_iota(jnp.int32, (EXPERTS_PER_GROUP, rows), 0)
    m1 = jnp.max(e_in, axis=0, keepdims=True)
    i1 = jnp.min(jnp.where(e_in == m1, iota8, EXPERTS_PER_GROUP), axis=0, keepdims=True)
    e_rest = jnp.where(iota8 == i1, -jnp.inf, e_in)
    m2 = jnp.max(e_rest, axis=0, keepdims=True)
    i2 = jnp.min(jnp.where(e_rest == m2, iota8, EXPERTS_PER_GROUP), axis=0, keepdims=True)
    t2 = jnp.exp(m2 - m1)
    den = 1.0 + t2
    gate0 = p_grp * (1.0 / den)
    gate1 = p_grp * (t2 / den)
    e0 = (grp_idx * EXPERTS_PER_GROUP + i1).astype(F32)
    e1 = (grp_idx * EXPERTS_PER_GROUP + i2).astype(F32)
    route_ref[:, r0:r0 + rows] = jnp.where(iota8 == 0, e0, jnp.where(iota8 == 1, e1, jnp.where(
        iota8 == 2, gate0, jnp.where(iota8 == 3, gate1, 0.0))))

    iota_e = lax.broadcasted_iota(jnp.int32, (N_EXPERTS, rows), 0).astype(F32)
    onehot = (iota_e == e0).astype(F32) + (iota_e == e1).astype(F32)
    lane = lax.broadcasted_iota(jnp.int32, (N_EXPERTS, LANES), 1)
    for s in range(rows // TOKEN_TILE):
        c_s = jnp.sum(onehot[:, s * TOKEN_TILE:(s + 1) * TOKEN_TILE], axis=1, keepdims=True)
        cnt = cnt + jnp.where(lane == r0 // TOKEN_TILE + s, c_s, 0.0)
    return cnt


def _mixer_call(x, hist, h0, wts, *, n_total, row_offset, sb, tt, n_sub, chunk, seg_len, seg_stride, chain,
                emit_vn, alias_in=None):
    nb, t_len, _ = x.shape
    nbb = nb // sb
    ntt = t_len // tt
    rows = sb * tt
    blk0 = row_offset // rows
    n_seg = rows // seg_len

    def full(arr):
        nd = arr.ndim
        return pl.BlockSpec(arr.shape, lambda b, t, _nd=nd: (0,) * _nd)

    in_specs = [
        pl.BlockSpec((sb, tt, D_MODEL), lambda b, t: (b, t, 0)),
        pl.BlockSpec((sb, CONV_WIDTH - 1, D_B), lambda b, t: (b, 0, 0)),
        pl.BlockSpec((1, sb, D_B), lambda b, t: (b, 0, 0)),
    ] + [full(w) for w in wts]
    args = [x, hist, h0] + list(wts)
    io_alias = {}
    if alias_in is not None:
        for k, arr in enumerate(alias_in):
            io_alias[len(args)] = k
            in_specs.append(pl.BlockSpec(memory_space=pl.ANY))
            args.append(arr)

    out_shape = [
        jax.ShapeDtypeStruct((n_total, D_MODEL), F32),
        jax.ShapeDtypeStruct((n_total, D_MODEL), BF16),
        jax.ShapeDtypeStruct((SUBLANES, n_total), F32),
        jax.ShapeDtypeStruct((nbb * ntt, N_EXPERTS, LANES), F32),
        jax.ShapeDtypeStruct((nb, CONV_WIDTH - 1, D_B), F32),
        jax.ShapeDtypeStruct((nbb, sb, D_B), F32),
    ]
    out_specs = [
        pl.BlockSpec((rows, D_MODEL), lambda b, t: (blk0 + b * ntt + t, 0)),
        pl.BlockSpec((rows, D_MODEL), lambda b, t: (blk0 + b * ntt + t, 0)),
        pl.BlockSpec((SUBLANES, rows), lambda b, t: (0, blk0 + b * ntt + t)),
        pl.BlockSpec((1, N_EXPERTS, LANES), lambda b, t: (b * ntt + t, 0, 0)),
        pl.BlockSpec((sb, CONV_WIDTH - 1, D_B), lambda b, t: (b, 0, 0)),
        pl.BlockSpec((1, sb, D_B), lambda b, t: (b, 0, 0)),
    ]
    if emit_vn:
        out_shape.append(jax.ShapeDtypeStruct((nb, t_len, D_A), F32))
        out_specs.append(pl.BlockSpec((sb, tt, D_A), lambda b, t: (b, t, 0)))

    assert n_sub == 1 or sb == 1
    kern = functools.partial(_mixer_kernel, sb=sb, tt=tt, n_sub=n_sub, chunk=chunk, seg_len=seg_len,
                             seg_stride=seg_stride,
                             chain=chain, emit_vn=emit_vn, aliased=alias_in is not None)
    return pl.pallas_call(
        kern,
        grid=(nbb, ntt),
        in_specs=in_specs,
        out_specs=out_specs,
        out_shape=out_shape,
        scratch_shapes=[
            pltpu.VMEM((sb, SUBLANES + tt, D_B), F32),
            pltpu.VMEM((D_B // LANES, n_seg * seg_stride, LANES), F32),
            pltpu.VMEM((D_B // LANES, n_seg * seg_stride, LANES), F32),
            pltpu.VMEM((rows, D_MODEL), BF16),
            pltpu.VMEM((1, D_B), F32),
        ],
        input_output_aliases=io_alias,
        compiler_params=pltpu.CompilerParams(dimension_semantics=("arbitrary", "arbitrary"),
                                             vmem_limit_bytes=VMEM_LIMIT),
        name="mixer_chain" if chain else "mixer_step",
    )(*args)


def _tile_slots(route_ref, locoff_ref, upper_ref, u):
    lanes = slice(u * TOKEN_TILE, (u + 1) * TOKEN_TILE)
    e0 = route_ref[0:1, lanes]
    e1 = route_ref[1:2, lanes]
    iota_e = lax.broadcasted_iota(jnp.int32, (N_EXPERTS, TOKEN_TILE), 0).astype(F32)
    oh0 = (iota_e == e0).astype(F32)
    oh1 = (iota_e == e1).astype(F32)
    c0 = _dot(oh0.astype(BF16), upper_ref[...])
    c1 = _dot(oh1.astype(BF16), upper_ref[...])
    cnt0 = jnp.sum(oh0, axis=1, keepdims=True)
    base0 = locoff_ref[u]
    base1 = base0 + cnt0
    p0 = jnp.sum(oh0 * (base0 + c0), axis=0, keepdims=True)
    p1 = jnp.sum(oh1 * (base1 + c1), axis=0, keepdims=True)
    return p0.astype(jnp.int32), p1.astype(jnp.int32)


def _chunk_loop(tile, ntot_ref, crow_ref, make_copy):
    def per_chunk(q, _):
        row = crow_ref[tile * TILE_CHUNKS + q]
        make_copy(pl.multiple_of(q * SUBLANES, SUBLANES), pl.multiple_of(row, SUBLANES))
        return 0

    lax.fori_loop(0, ntot_ref[tile], per_chunk, 0)


def _dispatch_kernel(crow_ref, ntot_ref, tail_ref, ntail_ref,
                     h2_ref, route_ref, locoff_ref, upper_ref, xb_ref, buf_ref, zero_ref, sem, zsem):
    i = pl.program_id(0)
    n_steps = pl.num_programs(0)
    slot = i % 2

    def chunk_copy(s, u, loc, dst):
        return pltpu.make_async_copy(buf_ref.at[s, u, pl.ds(loc, SUBLANES), :],
                                     xb_ref.at[pl.ds(dst, SUBLANES), :], sem.at[s])

    def wait_step(step, s):
        for u in range(TILES_PER_STEP):
            n_rows = ntot_ref[step * TILES_PER_STEP + u] * SUBLANES

            @pl.when(n_rows > 0)
            def _():
                pltpu.make_async_copy(buf_ref.at[s, u, pl.ds(0, n_rows), :], xb_ref.at[pl.ds(0, n_rows), :],
                                      sem.at[s]).wait()

    @pl.when(i >= 2)
    def _():
        wait_step(i - 2, slot)

    iota_s = lax.broadcasted_iota(jnp.int32, (SLOT_ROWS, TOKEN_TILE), 0)
    for u in range(TILES_PER_STEP):
        lanes = slice(u * TOKEN_TILE, (u + 1) * TOKEN_TILE)
        p0, p1 = _tile_slots(route_ref, locoff_ref, upper_ref, u)
        hit0 = iota_s == p0
        hit1 = iota_s == p1
        perm = jnp.where(hit0 | hit1, 1.0, 0.0).astype(BF16)
        sorted_rows = _dot(perm, h2_ref[u * TOKEN_TILE:(u + 1) * TOKEN_TILE, :])
        buf_ref[slot, u, :, 0:PACK_COLS] = _pack_bf16_pairs(sorted_rows)
        gates = jnp.where(hit0, route_ref[2:3, lanes], 0.0) + jnp.where(hit1, route_ref[3:4, lanes], 0.0)
        gcol = jnp.sum(gates, axis=1, keepdims=True)
        buf_ref[slot, u, :, PACK_COLS:XB_COLS] = lax.bitcast_convert_type(
            jnp.broadcast_to(gcol, (SLOT_ROWS, LANES)), U32)

    for u in range(TILES_PER_STEP):
        _chunk_loop(i * TILES_PER_STEP + u, ntot_ref, crow_ref,
                    lambda loc, dst, u=u: chunk_copy(slot, u, loc, dst).start())

    @pl.when(i == n_steps - 1)
    def _():
        zero_ref[...] = jnp.zeros_like(zero_ref)

        def zero_copy(dst, rows, k):
            return pltpu.make_async_copy(zero_ref.at[pl.ds(0, rows), :], xb_ref.at[pl.ds(dst, rows), :], zsem.at[k])

        def per_expert(e, tot):
            n_big, n_small = tot
            n = ntail_ref[e]
            d0 = tail_ref[e]
            nb = n // ZERO_CHUNKS
            ns = n - nb * ZERO_CHUNKS

            def big(c, _):
                zero_copy(pl.multiple_of(d0 + c * ZERO_ROWS, SUBLANES), ZERO_ROWS, 0).start()
                return 0

            def small(c, _):
                zero_copy(pl.multiple_of(d0 + nb * ZERO_ROWS + c * SUBLANES, SUBLANES), SUBLANES, 1).start()
                return 0

            lax.fori_loop(0, nb, big, 0)
            lax.fori_loop(0, ns, small, 0)
            return n_big + nb, n_small + ns

        n_big, n_small = lax.fori_loop(0, N_EXPERTS, per_expert, (0, 0))

        @pl.when(i >= 1)
        def _():
            wait_step(i - 1, 1 - slot)

        wait_step(i, slot)

        def wait_big(c, _):
            zero_copy(0, ZERO_ROWS, 0).wait()
            return 0

        def wait_small(c, _):
            zero_copy(0, SUBLANES, 1).wait()
            return 0

        lax.fori_loop(0, n_big, wait_big, 0)
        lax.fori_loop(0, n_small, wait_small, 0)


def _dispatch_call(h2, route, locoff, upper, crow, ntot, tail, ntail, *, n_tiles, p_rows):
    grid_spec = pltpu.PrefetchScalarGridSpec(
        num_scalar_prefetch=4,
        grid=(n_tiles // TILES_PER_STEP,),
        in_specs=[
            pl.BlockSpec((STEP_TOKENS, D_MODEL), lambda i, *_: (i, 0)),
            pl.BlockSpec((SUBLANES, STEP_TOKENS), lambda i, *_: (0, i)),
            pl.BlockSpec((TILES_PER_STEP, N_EXPERTS, 1), lambda i, *_: (i, 0, 0)),
            pl.BlockSpec((TOKEN_TILE, TOKEN_TILE), lambda i, *_: (0, 0)),
        ],
        out_specs=pl.BlockSpec(memory_space=pl.ANY),
        scratch_shapes=[
            pltpu.VMEM((2, TILES_PER_STEP, SLOT_ROWS, XB_COLS), U32),
            pltpu.VMEM((ZERO_ROWS, XB_COLS), U32),
            pltpu.SemaphoreType.DMA((2,)),
            pltpu.SemaphoreType.DMA((2,)),
        ],
    )
    return pl.pallas_call(
        _dispatch_kernel,
        grid_spec=grid_spec,
        out_shape=jax.ShapeDtypeStruct((p_rows, XB_COLS), U32),
        compiler_params=pltpu.CompilerParams(dimension_semantics=("arbitrary",), vmem_limit_bytes=VMEM_LIMIT),
        name="dispatch",
    )(crow, ntot, tail, ntail, h2, route, locoff, upper)


def _ffn_kernel(blk_e_ref, nused_ref, xb_ref, wg_ref, wu_ref, wd_ref, yb_ref, wgb_ref, wub_ref, wdb_ref):
    i = pl.program_id(0)
    new_expert = (i == 0) | (blk_e_ref[i] != blk_e_ref[jnp.maximum(i - 1, 0)])

    @pl.when(new_expert)
    def _():
        wgb_ref[...] = wg_ref[0].astype(BF16)
        wub_ref[...] = wu_ref[0].astype(BF16)
        wdb_ref[...] = wd_ref[0].astype(BF16)

    @pl.when(i < nused_ref[0])
    def _():
        for c in range(FFN_BLOCK // FFN_CHUNK):
            rs = slice(c * FFN_CHUNK, (c + 1) * FFN_CHUNK)
            xb = _unpack_bf16_pairs(xb_ref[rs, 0:PACK_COLS])
            gate = lax.bitcast_convert_type(xb_ref[rs, PACK_COLS:PACK_COLS + 1], F32)
            a = _dot(xb, wgb_ref[...])
            u = _dot(xb, wub_ref[...])
            mid = (jax.nn.silu(a) * u).astype(BF16)
            y = _dot(mid, wdb_ref[...]) * gate
            yb_ref[rs, :] = _pack_bf16_pairs(y.astype(BF16).astype(F32))


def _ffn_call(xb, wg, wu, wd, blk_e, nused, *, n_blocks):
    def row_map(i, blk_e_ref, nused_ref):
        return (jnp.minimum(i, nused_ref[0] - 1), 0)

    def w_map(i, blk_e_ref, nused_ref):
        return (blk_e_ref[i], 0, 0)

    grid_spec = pltpu.PrefetchScalarGridSpec(
        num_scalar_prefetch=2,
        grid=(n_blocks,),
        in_specs=[
            pl.BlockSpec((FFN_BLOCK, XB_COLS), row_map),
            pl.BlockSpec((1, D_MODEL, D_EXPERT), w_map),
            pl.BlockSpec((1, D_MODEL, D_EXPERT), w_map),
            pl.BlockSpec((1, D_EXPERT, D_MODEL), w_map),
        ],
        out_specs=pl.BlockSpec((FFN_BLOCK, PACK_COLS), row_map),
        scratch_shapes=[
            pltpu.VMEM((D_MODEL, D_EXPERT), BF16),
            pltpu.VMEM((D_MODEL, D_EXPERT), BF16),
            pltpu.VMEM((D_EXPERT, D_MODEL), BF16),
        ],
    )
    return pl.pallas_call(
        _ffn_kernel,
        grid_spec=grid_spec,
        out_shape=jax.ShapeDtypeStruct((n_blocks * FFN_BLOCK, PACK_COLS), U32),
        compiler_params=pltpu.CompilerParams(dimension_semantics=("arbitrary",), vmem_limit_bytes=VMEM_LIMIT),
        name="expert_ffn",
    )(blk_e, nused, xb, wg, wu, wd)


def _combine_kernel(crow_ref, ntot_ref, y1_ref, route_ref, locoff_ref, upper_ref, fg_ref, yb_ref,
                    yp_ref, ys_ref, buf_ref, sem):
    i = pl.program_id(0)
    n_steps = pl.num_programs(0)
    slot = i % 2

    def chunk_copy(s, u, loc, src):
        return pltpu.make_async_copy(yb_ref.at[pl.ds(src, SUBLANES), :],
                                     buf_ref.at[s, u, pl.ds(loc, SUBLANES), :], sem.at[s])

    def issue_step(step, s):
        for u in range(TILES_PER_STEP):
            _chunk_loop(step * TILES_PER_STEP + u, ntot_ref, crow_ref,
                        lambda loc, src, u=u: chunk_copy(s, u, loc, src).start())

    @pl.when(i == 0)
    def _():
        buf_ref[...] = jnp.zeros_like(buf_ref)
        issue_step(0, 0)

    @pl.when(i + 1 < n_steps)
    def _():
        issue_step(i + 1, 1 - slot)

    iota_s = lax.broadcasted_iota(jnp.int32, (SLOT_ROWS, TOKEN_TILE), 0)
    perms = []
    for u in range(TILES_PER_STEP):
        p0, p1 = _tile_slots(route_ref, locoff_ref, upper_ref, u)
        perms.append(jnp.where((iota_s == p0) | (iota_s == p1), 1.0, 0.0).astype(BF16))

    for u in range(TILES_PER_STEP):
        n_rows = ntot_ref[i * TILES_PER_STEP + u] * SUBLANES

        @pl.when(n_rows > 0)
        def _():
            pltpu.make_async_copy(yb_ref.at[pl.ds(0, n_rows), :], buf_ref.at[slot, u, pl.ds(0, n_rows), :],
                                  sem.at[slot]).wait()

    for u in range(TILES_PER_STEP):
        rs = slice(u * TOKEN_TILE, (u + 1) * TOKEN_TILE)
        yb = _unpack_bf16_pairs(buf_ref[slot, u])
        moe = lax.dot_general(perms[u], yb, (((0,), (0,)), ((), ())), preferred_element_type=F32)
        y_tile = _rms(y1_ref[rs, :] + moe, fg_ref[...])
        yp_ref[rs, :] = y_tile

    @pl.when(i == n_steps - 1)
    def _():
        ys_ref[...] = y_tile


def _combine_call(y1, route, locoff, upper, fg, yb, crow, ntot, *, n_tiles, n_prompt, n_sample):
    n_steps = n_tiles // TILES_PER_STEP
    assert n_sample == TOKEN_TILE and n_prompt + n_sample == n_steps * STEP_TOKENS
    grid_spec = pltpu.PrefetchScalarGridSpec(
        num_scalar_prefetch=2,
        grid=(n_steps,),
        in_specs=[
            pl.BlockSpec((STEP_TOKENS, D_MODEL), lambda i, *_: (i, 0)),
            pl.BlockSpec((SUBLANES, STEP_TOKENS), lambda i, *_: (0, i)),
            pl.BlockSpec((TILES_PER_STEP, N_EXPERTS, 1), lambda i, *_: (i, 0, 0)),
            pl.BlockSpec((TOKEN_TILE, TOKEN_TILE), lambda i, *_: (0, 0)),
            pl.BlockSpec((1, D_MODEL), lambda i, *_: (0, 0)),
            pl.BlockSpec(memory_space=pl.ANY),
        ],
        out_specs=[
            pl.BlockSpec((STEP_TOKENS, D_MODEL), lambda i, *_: (i, 0)),
            pl.BlockSpec((n_sample, D_MODEL), lambda i, *_: (0, 0)),
        ],
        scratch_shapes=[
            pltpu.VMEM((2, TILES_PER_STEP, SLOT_ROWS, PACK_COLS), U32),
            pltpu.SemaphoreType.DMA((2,)),
        ],
    )
    return pl.pallas_call(
        _combine_kernel,
        grid_spec=grid_spec,
        out_shape=[jax.ShapeDtypeStruct((n_prompt, D_MODEL), F32), jax.ShapeDtypeStruct((n_sample, D_MODEL), F32)],
        compiler_params=pltpu.CompilerParams(dimension_semantics=("arbitrary",), vmem_limit_bytes=VMEM_LIMIT),
        name="combine",
    )(crow, ntot, y1, route, locoff, upper, fg, yb)


def _block_diag(w):
    h, d, _ = w.shape
    eye = jnp.eye(h, dtype=w.dtype)
    return (eye[:, None, :, None] * w[:, :, None, :]).reshape(h * d, h * d)


def _head_blocks(w):
    half = B_HEADS // 2
    return jnp.stack([_block_diag(w[:half]), _block_diag(w[half:])]).astype(BF16)


def _mixer_weights(l, length, reps, norm1_g, w_in, gmlp_ln_g, gmlp_ln_b, gmlp_w_s, gmlp_b_s, conv_w, conv_b,
                   lru_w_a, lru_b_a, lru_w_x, lru_b_x, lru_lambda, w_out, norm2_g,
                   router_group_w, router_group_b, router_expert_w, router_expert_b):
    mask = jnp.tril(jnp.ones((length, length), dtype=bool))
    ws = jnp.where(mask, gmlp_w_s[l][:, :length, :length], 0.0)
    eye = jnp.eye(reps, dtype=ws.dtype)
    wsm = (eye[None, :, None, :, None] * ws[:, None, :, None, :]).reshape(A_GROUPS, reps * length, reps * length)
    bs = jnp.tile(gmlp_b_s[l][:, :length], (1, reps))
    bsx = jnp.repeat(bs.T, A_GROUP_DIM, axis=1)
    wr = jnp.concatenate([router_expert_w[l].T, router_group_w[l].T,
                          jnp.zeros((ROUTER_ROWS - N_EXPERTS - N_GROUPS, D_MODEL), F32)], axis=0)
    br = jnp.concatenate([router_expert_b[l], router_group_b[l],
                          jnp.zeros((ROUTER_ROWS - N_EXPERTS - N_GROUPS,), F32)]).reshape(ROUTER_ROWS, 1)
    return [
        norm1_g[l].reshape(1, D_MODEL), w_in[l].astype(BF16),
        gmlp_ln_g[l].reshape(1, D_A), gmlp_ln_b[l].reshape(1, D_A), wsm.astype(BF16), bsx,
        conv_w[l], conv_b[l].reshape(1, D_B),
        _head_blocks(lru_w_a[l]), lru_b_a[l].reshape(1, D_B),
        _head_blocks(lru_w_x[l]), lru_b_x[l].reshape(1, D_B),
        lru_lambda[l].reshape(1, D_B), w_out[l].astype(BF16), norm2_g[l].reshape(1, D_MODEL),
        wr.astype(BF16), br,
    ]


def _routing_tables(cnt, n_blocks):
    n_tiles = cnt.shape[0]
    seg = (cnt + SUBLANES - 1) // SUBLANES * SUBLANES
    nch = seg // SUBLANES
    tot = jnp.sum(seg, axis=0)
    padded = (tot + FFN_BLOCK - 1) // FFN_BLOCK * FFN_BLOCK
    e_before = jnp.arange(N_EXPERTS)[None, :] < jnp.arange(N_EXPERTS)[:, None]
    t_before = jnp.arange(n_tiles)[None, :] < jnp.arange(n_tiles)[:, None]
    pad_start = jnp.sum(jnp.where(e_before, padded[None, :], 0), axis=1)
    pad_end = pad_start + padded
    doff = pad_start[None, :] + jnp.sum(jnp.where(t_before[:, :, None], seg[None, :, :], 0), axis=1)
    locoff = jnp.sum(jnp.where(e_before[None, :, :], seg[:, None, :], 0), axis=2)
    ntot = jnp.sum(nch, axis=1)
    first = locoff // SUBLANES
    q = jnp.arange(TILE_CHUNKS)
    owner = jnp.sum((first + nch)[:, None, :] <= q[None, :, None], axis=2)
    own = owner[:, :, None] == jnp.arange(N_EXPERTS)[None, None, :]
    crow = jnp.sum(jnp.where(own, doff[:, None, :] + SUBLANES * (q[None, :, None] - first[:, None, :]), 0), axis=2)
    tail = pad_start + tot
    ntail = (padded - tot) // SUBLANES
    blk_end = pad_end // FFN_BLOCK
    nused = blk_end[-1]
    blk = jnp.minimum(jnp.arange(n_blocks, dtype=jnp.int32), nused - 1)
    blk_e = jnp.minimum(jnp.sum((blk_end[None, :] <= blk[:, None]).astype(jnp.int32), axis=1), N_EXPERTS - 1)
    i32 = lambda a: a.astype(jnp.int32)
    return (i32(crow.reshape(-1)), i32(ntot), i32(tail), i32(ntail),
            locoff.astype(F32)[:, :, None], i32(blk_e), i32(nused.reshape(1)))


def kernel(x_prompt, x_sample, state_conv, state_rglru, norm1_g, w_in, gmlp_ln_g, gmlp_ln_b, gmlp_w_s, gmlp_b_s,
           conv_w, conv_b, lru_w_a, lru_b_a, lru_w_x, lru_b_x, lru_lambda, w_out, norm2_g,
           router_group_w, router_group_b, router_expert_w, router_expert_b,
           expert_w_gate, expert_w_up, expert_w_down, final_norm_g):
    depth = w_in.shape[0]
    nb, t_len, _ = x_prompt.shape
    db, dt, _ = x_sample.shape
    n_prompt = nb * t_len
    n_sample = db * dt
    assert t_len % PROMPT_TILE == 0 and n_prompt % TOKEN_TILE == 0 and n_sample == TOKEN_TILE
    n_total = n_prompt + n_sample
    assert n_total % STEP_TOKENS == 0
    n_tiles = n_total // TOKEN_TILE
    p_tiles = n_prompt // TOKEN_TILE
    p_rows = 2 * n_total + (SUBLANES - 1) * N_EXPERTS * n_tiles + N_EXPERTS * (FFN_BLOCK - 1)
    n_blocks = -(-p_rows // FFN_BLOCK)
    p_rows = n_blocks * FFN_BLOCK
    upper = jnp.triu(jnp.ones((TOKEN_TILE, TOKEN_TILE), BF16), 1)

    xp, xs = x_prompt, x_sample
    conv_p, h_p, v_s, conv_s, h_s = [], [], [], [], []
    for l in range(depth):
        lw = (norm1_g, w_in, gmlp_ln_g, gmlp_ln_b, gmlp_w_s, gmlp_b_s, conv_w, conv_b, lru_w_a, lru_b_a,
              lru_w_x, lru_b_x, lru_lambda, w_out, norm2_g, router_group_w, router_group_b,
              router_expert_w, router_expert_b)
        wts_p = _mixer_weights(l, GMLP_CHUNK, 1, *lw)
        wts_s = _mixer_weights(l, dt, db, *lw)
        zero_conv = jnp.zeros((nb, CONV_WIDTH - 1, D_B), F32)
        zero_h = jnp.zeros((nb, 1, D_B), F32)
        y1, h2, route, cnt_p, cp, hp = _mixer_call(
            xp, zero_conv, zero_h, wts_p, n_total=n_total, row_offset=0, sb=1, tt=PROMPT_TILE, n_sub=PROMPT_SUB,
            chunk=GMLP_CHUNK, seg_len=PROMPT_SEG, seg_stride=PROMPT_SEG + SUBLANES, chain=True, emit_vn=False)
        y1, h2, route, cnt_s, cs, hs, vs = _mixer_call(
            xs, state_conv[l], state_rglru[l][None], wts_s, n_total=n_total, row_offset=n_prompt, sb=db,
            tt=dt, n_sub=1, chunk=db * dt, seg_len=dt, seg_stride=dt + SUBLANES, chain=False, emit_vn=True,
            alias_in=(y1, h2, route))
        conv_p.append(cp)
        h_p.append(hp[:, 0])
        v_s.append(vs)
        conv_s.append(cs)
        h_s.append(hs[0])

        sub = PROMPT_TILE // TOKEN_TILE
        cnt = jnp.concatenate([
            jnp.swapaxes(cnt_p[:, :, :sub], 1, 2).reshape(p_tiles, N_EXPERTS),
            cnt_s[:, :, 0]], axis=0).astype(jnp.int32)
        crow, ntot, tail, ntail, locoff, blk_e, nused = _routing_tables(cnt, n_blocks)

        xb = _dispatch_call(h2, route, locoff, upper, crow, ntot, tail, ntail, n_tiles=n_tiles, p_rows=p_rows)
        yb = _ffn_call(xb, expert_w_gate[l], expert_w_up[l], expert_w_down[l], blk_e, nused, n_blocks=n_blocks)
        assert l == depth - 1, "deeper stacks need an un-normalised combine between layers"
        fg = final_norm_g.reshape(1, D_MODEL)
        yp, ysm = _combine_call(y1, route, locoff, upper, fg, yb, crow, ntot, n_tiles=n_tiles, n_prompt=n_prompt,
                                n_sample=n_sample)
        xp = yp.reshape(nb, t_len, D_MODEL)
        xs = ysm.reshape(db, dt, D_MODEL)

    return (xp, xs, jnp.stack(conv_p), jnp.stack(h_p), jnp.stack(v_s), jnp.stack(conv_s), jnp.stack(h_s))
```

```python
import functools

import jax
import jax.numpy as jnp
from jax import lax
from jax.experimental import pallas as pl
from jax.experimental.pallas import tpu as pltpu

D_MODEL = 1024
D_A = 512
D_B = 512
A_GROUPS = 4
A_GROUP_DIM = 128
GMLP_CHUNK = 128
B_HEADS = 8
CONV_WIDTH = 4
LRU_C = 8.0
N_GROUPS = 4
EXPERTS_PER_GROUP = 8
N_EXPERTS = 32
D_EXPERT = 512
EPS = 1e-6

SUBLANES = 8
LANES = 128
ROUTER_ROWS = 40
TOKEN_TILE = 256
PROMPT_TILE = 1024
PROMPT_SUB = 1
MIX_LAG = 6
PROMPT_SEG = PROMPT_TILE // PROMPT_SUB // SUBLANES
FFN_BLOCK = 1024
FFN_CHUNK = 1024
ZERO_CHUNKS = 8
ZERO_ROWS = ZERO_CHUNKS * SUBLANES
SLOT_ROWS = 2 * TOKEN_TILE + N_EXPERTS * SUBLANES
TILE_CHUNKS = SLOT_ROWS // SUBLANES
TILES_PER_STEP = 3
STEP_TOKENS = TILES_PER_STEP * TOKEN_TILE
PACK_COLS = D_MODEL // 2
XB_COLS = PACK_COLS + LANES
U32 = jnp.uint32
HI_MASK = 0xFFFF0000
VMEM_LIMIT = 56 * 1024 * 1024

BF16 = jnp.bfloat16
F32 = jnp.float32


def _rms(x, g):
    return x * lax.rsqrt(jnp.mean(x * x, axis=-1, keepdims=True) + EPS) * g


def _dot(a, b):
    return jnp.dot(a, b, preferred_element_type=F32)


def _pack_bf16_pairs(x):
    bits = lax.bitcast_convert_type(x, U32)
    return (bits[:, PACK_COLS:] & U32(HI_MASK)) | (bits[:, :PACK_COLS] >> 16)


def _unpack_bf16_pairs(w):
    lo = lax.bitcast_convert_type(w << 16, F32).astype(BF16)
    hi = lax.bitcast_convert_type(w & U32(HI_MASK), F32).astype(BF16)
    return jnp.concatenate([lo, hi], axis=1)


def _mixer_kernel(x_ref, hist_ref, h0_ref, n1g_ref, win_ref, lng_ref, lnb_ref, wsm_ref, bsx_ref,
                  cw_ref, cb_ref, wa_ref, ba_ref, wx_ref, bx_ref, lam_ref, wout_ref, n2g_ref,
                  wr_ref, br_ref, *rest, sb, tt, n_sub, chunk, seg_len, seg_stride, chain, emit_vn, aliased):
    if aliased:
        rest = rest[3:]
    if emit_vn:
        (y1_ref, h2_ref, route_ref, cnt_ref, conv_ref, hlast_ref, vn_ref,
         xs_ref, sa_ref, sb_ref, mix_ref, hc_ref) = rest
    else:
        (y1_ref, h2_ref, route_ref, cnt_ref, conv_ref, hlast_ref,
         xs_ref, sa_ref, sb_ref, mix_ref, hc_ref) = rest
        vn_ref = None
    t_idx = pl.program_id(1)

    @pl.when(t_idx == 0)
    def _():
        xs_ref[:, SUBLANES - 3:SUBLANES, :] = hist_ref[...]
        hc_ref[...] = jnp.zeros_like(hc_ref)

    @pl.when(t_idx != 0)
    def _():
        xs_ref[:, SUBLANES - 3:SUBLANES, :] = xs_ref[:, SUBLANES + tt - 3:SUBLANES + tt, :]

    counts = []
    groups = [_mixer_rows(sub, counts, x_ref, h0_ref, n1g_ref, win_ref, lng_ref, lnb_ref, wsm_ref, bsx_ref,
                          cw_ref, cb_ref, wa_ref, ba_ref, wx_ref, bx_ref, lam_ref, wout_ref, n2g_ref, wr_ref,
                          br_ref, y1_ref, h2_ref, route_ref, hlast_ref, vn_ref, xs_ref, sa_ref, sb_ref, mix_ref,
                          hc_ref, sb=sb, tt=tt, n_sub=n_sub, chunk=chunk, seg_len=seg_len,
                          seg_stride=seg_stride, chain=chain)
              for sub in range(n_sub)]
    live = [True] * n_sub
    tick = 0
    while any(live):
        for sub in range(n_sub):
            if live[sub] and tick >= sub * MIX_LAG:
                live[sub] = next(groups[sub], "done") != "done"
        tick += 1
    conv_ref[...] = xs_ref[:, SUBLANES + tt - 3:SUBLANES + tt, :]
    cnt_ref[...] = sum(counts).reshape(1, N_EXPERTS, LANES)


def _mixer_rows(sub, counts, x_ref, h0_ref, n1g_ref, win_ref, lng_ref, lnb_ref, wsm_ref, bsx_ref, cw_ref, cb_ref,
                wa_ref, ba_ref, wx_ref, bx_ref, lam_ref, wout_ref, n2g_ref, wr_ref, br_ref,
                y1_ref, h2_ref, route_ref, hlast_ref, vn_ref, xs_ref, sa_ref, sb_ref, mix_ref, hc_ref,
                *, sb, tt, n_sub, chunk, seg_len, seg_stride, chain):
    tsub = tt // n_sub
    t0 = sub * tsub
    rows = sb * tsub
    r0 = sub * rows
    n_seg = rows // seg_len
    half = D_B // 2

    x = x_ref[:, t0:t0 + tsub, :].reshape(rows, D_MODEL)
    hb = _rms(x, n1g_ref[...]).astype(BF16)
    yield

    v_a = _dot(hb, win_ref[:, D_A:2 * D_A])
    x_b = _dot(hb, win_ref[:, 2 * D_A:2 * D_A + D_B])
    yield
    mixed = []
    for g in range(A_GROUPS):
        cols = slice(g * A_GROUP_DIM, (g + 1) * A_GROUP_DIM)
        vg = v_a[:, cols]
        mu = jnp.mean(vg, axis=-1, keepdims=True)
        dv = vg - mu
        var = jnp.mean(dv * dv, axis=-1, keepdims=True)
        vn = dv * lax.rsqrt(var + EPS) * lng_ref[:, cols] + lnb_ref[:, cols]
        if vn_ref is not None:
            vn_ref[:, t0:t0 + tsub, cols] = vn.reshape(sb, tsub, A_GROUP_DIM)
        vnb = vn.astype(BF16)
        mixed.append([_dot(wsm_ref[g], vnb[c * chunk:(c + 1) * chunk]) for c in range(rows // chunk)])
    yield
    u_a = _dot(hb, win_ref[:, 0:D_A])

    xs_ref[:, SUBLANES + t0:SUBLANES + t0 + tsub, :] = x_b.reshape(sb, tsub, D_B)
    xc = cb_ref[...] + x_b * cw_ref[3:4, :]
    for j in range(1, CONV_WIDTH):
        shifted = xs_ref[:, SUBLANES + t0 - j:SUBLANES + t0 - j + tsub, :].reshape(rows, D_B)
        xc = xc + shifted * cw_ref[3 - j:4 - j, :]
    yield

    xcb = xc.astype(BF16)
    r_lin = jnp.concatenate([_dot(xcb[:, :half], wa_ref[0]), _dot(xcb[:, half:], wa_ref[1])], axis=1)
    i_lin = jnp.concatenate([_dot(xcb[:, :half], wx_ref[0]), _dot(xcb[:, half:], wx_ref[1])], axis=1)

    for g in range(A_GROUPS):
        cols = slice(g * A_GROUP_DIM, (g + 1) * A_GROUP_DIM)
        for c in range(rows // chunk):
            rs = slice(c * chunk, (c + 1) * chunk)
            s = mixed[g][c] + bsx_ref[:, cols]
            mix_ref[r0 + c * chunk:r0 + (c + 1) * chunk, cols] = (u_a[rs, cols] * s).astype(BF16)

    g_b = _dot(hb, win_ref[:, 2 * D_A + D_B:])
    yield
    r = jax.nn.sigmoid(r_lin + ba_ref[...])
    i_g = jax.nn.sigmoid(i_lin + bx_ref[...])
    log_a = (-LRU_C * r) * jax.nn.softplus(-lam_ref[...])
    a = jnp.exp(log_a)
    gain = jnp.sqrt(1.0 - a * a)
    bterm = gain * i_g * xc

    n_lb = D_B // LANES
    sbase = sub * n_seg * seg_stride
    for j in range(n_seg):
        for k in range(n_lb):
            lc = slice(k * LANES, (k + 1) * LANES)
            dst = slice(sbase + j * seg_stride, sbase + j * seg_stride + seg_len)
            sa_ref[k, dst, :] = a[j * seg_len:(j + 1) * seg_len, lc]
            sb_ref[k, dst, :] = bterm[j * seg_len:(j + 1) * seg_len, lc]
    yield

    y_a = x + _dot(mix_ref[r0:r0 + rows, 0:D_A], wout_ref[0:D_A, :])
    gel = jax.nn.gelu(g_b)
    yield

    for grp in range(n_seg // SUBLANES):
        base = sbase + grp * SUBLANES * seg_stride
        bs = slice(grp * SUBLANES, (grp + 1) * SUBLANES)
        if chain:
            h_init = tuple(jnp.zeros((SUBLANES, LANES), F32) for _ in range(n_lb))
        else:
            h_init = tuple(h0_ref[0, bs, k * LANES:(k + 1) * LANES] for k in range(n_lb))
        a_init = tuple(jnp.ones((SUBLANES, LANES), F32) for _ in range(n_lb))

        def step(i, carry, base=base):
            hs, acs = carry
            idx = pl.ds(base + i, SUBLANES, stride=seg_stride)
            new_h, new_a = [], []
            for k in range(n_lb):
                av = sa_ref[k, idx, :]
                h = av * hs[k] + sb_ref[k, idx, :]
                sb_ref[k, idx, :] = h
                new_h.append(h)
                if chain:
                    ac = av * acs[k]
                    sa_ref[k, idx, :] = ac
                    new_a.append(ac)
                else:
                    new_a.append(acs[k])
            return tuple(new_h), tuple(new_a)

        carry = (h_init, a_init)
        for i in range(seg_len):
            carry = step(i, carry)
        h_end, a_end = carry
        yield

        for k in range(n_lb):
            lc = slice(k * LANES, (k + 1) * LANES)
            mc = slice(D_A + k * LANES, D_A + (k + 1) * LANES)
            if chain:
                h_in = hc_ref[:, lc]
                for j in range(SUBLANES):
                    seg = slice(base + j * seg_stride, base + j * seg_stride + seg_len)
                    rs = slice((grp * SUBLANES + j) * seg_len, (grp * SUBLANES + j + 1) * seg_len)
                    ms = slice(r0 + rs.start, r0 + rs.stop)
                    h_seg = sb_ref[k, seg, :] + sa_ref[k, seg, :] * h_in
                    mix_ref[ms, mc] = (h_seg * gel[rs, lc]).astype(BF16)
                    h_in = h_end[k][j:j + 1] + a_end[k][j:j + 1] * h_in
                hc_ref[:, lc] = h_in
                hlast_ref[0, :, lc] = h_in
            else:
                for j in range(SUBLANES):
                    seg = slice(base + j * seg_stride, base + j * seg_stride + seg_len)
                    rs = slice((grp * SUBLANES + j) * seg_len, (grp * SUBLANES + j + 1) * seg_len)
                    ms = slice(r0 + rs.start, r0 + rs.stop)
                    mix_ref[ms, mc] = (sb_ref[k, seg, :] * gel[rs, lc]).astype(BF16)
                hlast_ref[0, bs, lc] = h_end[k]
        yield

    y1 = y_a + _dot(mix_ref[r0:r0 + rows, D_A:], wout_ref[D_A:, :])
    y1_ref[r0:r0 + rows, :] = y1
    yield
    h2 = _rms(y1, n2g_ref[...]).astype(BF16)
    h2_ref[r0:r0 + rows, :] = h2
    yield

    logits = lax.dot_general(wr_ref[...], h2, (((1,), (1,)), ((), ())), preferred_element_type=F32) + br_ref[...]
    gl = [logits[N_EXPERTS + k:N_EXPERTS + k + 1, :] for k in range(N_GROUPS)]
    gmax = jnp.maximum(jnp.maximum(gl[0], gl[1]), jnp.maximum(gl[2], gl[3]))
    grp_idx = jnp.where(gl[0] == gmax, 0, jnp.where(gl[1] == gmax, 1, jnp.where(gl[2] == gmax, 2, 3)))
    gsum = (jnp.exp(gl[0] - gmax) + jnp.exp(gl[1] - gmax)) + (jnp.exp(gl[2] - gmax) + jnp.exp(gl[3] - gmax))
    p_grp = 1.0 / gsum
    e_in = jnp.where(grp_idx == 0, logits[0:8],
                     jnp.where(grp_idx == 1, logits[8:16], jnp.where(grp_idx == 2, logits[16:24], logits[24:32])))
    iota8 = lax.broadcasted_iota(jnp.int32, (EXPERTS_PER_GROUP, rows), 0)
    m1 = jnp.max(e_in, axis=0, keepdims=True)
    i1 = jnp.min(jnp.where(e_in == m1, iota8, EXPERTS_PER_GROUP), axis=0, keepdims=True)
    e_rest = jnp.where(iota8 == i1, -jnp.inf, e_in)
    m2 = jnp.max(e_rest, axis=0, keepdims=True)
    i2 = jnp.min(jnp.where(e_rest == m2, iota8, EXPERTS_PER_GROUP), axis=0, keepdims=True)
    t2 = jnp.exp(m2 - m1)
    den = 1.0 + t2
    gate0 = p_grp * (1.0 / den)
    gate1 = p_grp * (t2 / den)
    e0 = (grp_idx * EXPERTS_PER_GROUP + i1).astype(F32)
    e1 = (grp_idx * EXPERTS_PER_GROUP + i2).astype(F32)
    route_ref[:, r0:r0 + rows] = jnp.where(iota8 == 0, e0, jnp.where(iota8 == 1, e1, jnp.where(
        iota8 == 2, gate0, jnp.where(iota8 == 3, gate1, 0.0))))

    iota_e = lax.broadcasted_iota(jnp.int32, (N_EXPERTS, rows), 0).astype(F32)
    onehot = (iota_e == e0).astype(F32) + (iota_e == e1).astype(F32)
    lane = lax.broadcasted_iota(jnp.int32, (N_EXPERTS, LANES), 1)
    for s in range(rows // TOKEN_TILE):
        c_s = jnp.sum(onehot[:, s * TOKEN_TILE:(s + 1) * TOKEN_TILE], axis=1, keepdims=True)
        counts.append(jnp.where(lane == r0 // TOKEN_TILE + s, c_s, 0.0))


def _mixer_call(x, hist, h0, wts, *, n_total, row_offset, sb, tt, n_sub, chunk, seg_len, seg_stride, chain,
                emit_vn, alias_in=None):
    nb, t_len, _ = x.shape
    nbb = nb // sb
    ntt = t_len // tt
    rows = sb * tt
    blk0 = row_offset // rows
    n_seg = rows // seg_len

    def full(arr):
        nd = arr.ndim
        return pl.BlockSpec(arr.shape, lambda b, t, _nd=nd: (0,) * _nd)

    in_specs = [
        pl.BlockSpec((sb, tt, D_MODEL), lambda b, t: (b, t, 0)),
        pl.BlockSpec((sb, CONV_WIDTH - 1, D_B), lambda b, t: (b, 0, 0)),
        pl.BlockSpec((1, sb, D_B), lambda b, t: (b, 0, 0)),
    ] + [full(w) for w in wts]
    args = [x, hist, h0] + list(wts)
    io_alias = {}
    if alias_in is not None:
        for k, arr in enumerate(alias_in):
            io_alias[len(args)] = k
            in_specs.append(pl.BlockSpec(memory_space=pl.ANY))
            args.append(arr)

    out_shape = [
        jax.ShapeDtypeStruct((n_total, D_MODEL), F32),
        jax.ShapeDtypeStruct((n_total, D_MODEL), BF16),
        jax.ShapeDtypeStruct((SUBLANES, n_total), F32),
        jax.ShapeDtypeStruct((nbb * ntt, N_EXPERTS, LANES), F32),
        jax.ShapeDtypeStruct((nb, CONV_WIDTH - 1, D_B), F32),
        jax.ShapeDtypeStruct((nbb, sb, D_B), F32),
    ]
    out_specs = [
        pl.BlockSpec((rows, D_MODEL), lambda b, t: (blk0 + b * ntt + t, 0)),
        pl.BlockSpec((rows, D_MODEL), lambda b, t: (blk0 + b * ntt + t, 0)),
        pl.BlockSpec((SUBLANES, rows), lambda b, t: (0, blk0 + b * ntt + t)),
        pl.BlockSpec((1, N_EXPERTS, LANES), lambda b, t: (b * ntt + t, 0, 0)),
        pl.BlockSpec((sb, CONV_WIDTH - 1, D_B), lambda b, t: (b, 0, 0)),
        pl.BlockSpec((1, sb, D_B), lambda b, t: (b, 0, 0)),
    ]
    if emit_vn:
        out_shape.append(jax.ShapeDtypeStruct((nb, t_len, D_A), F32))
        out_specs.append(pl.BlockSpec((sb, tt, D_A), lambda b, t: (b, t, 0)))

    assert n_sub == 1 or sb == 1
    kern = functools.partial(_mixer_kernel, sb=sb, tt=tt, n_sub=n_sub, chunk=chunk, seg_len=seg_len,
                             seg_stride=seg_stride,
                             chain=chain, emit_vn=emit_vn, aliased=alias_in is not None)
    return pl.pallas_call(
        kern,
        grid=(nbb, ntt),
        in_specs=in_specs,
        out_specs=out_specs,
        out_shape=out_shape,
        scratch_shapes=[
            pltpu.VMEM((sb, SUBLANES + tt, D_B), F32),
            pltpu.VMEM((D_B // LANES, n_seg * seg_stride, LANES), F32),
            pltpu.VMEM((D_B // LANES, n_seg * seg_stride, LANES), F32),
            pltpu.VMEM((rows, D_MODEL), BF16),
            pltpu.VMEM((1, D_B), F32),
        ],
        input_output_aliases=io_alias,
        compiler_params=pltpu.CompilerParams(dimension_semantics=("arbitrary", "arbitrary"),
                                             vmem_limit_bytes=VMEM_LIMIT),
        name="mixer_chain" if chain else "mixer_step",
    )(*args)


def _tile_slots(route_ref, locoff_ref, upper_ref, u):
    lanes = slice(u * TOKEN_TILE, (u + 1) * TOKEN_TILE)
    e0 = route_ref[0:1, lanes]
    e1 = route_ref[1:2, lanes]
    iota_e = lax.broadcasted_iota(jnp.int32, (N_EXPERTS, TOKEN_TILE), 0).astype(F32)
    oh0 = (iota_e == e0).astype(F32)
    oh1 = (iota_e == e1).astype(F32)
    c0 = _dot(oh0.astype(BF16), upper_ref[...])
    c1 = _dot(oh1.astype(BF16), upper_ref[...])
    cnt0 = jnp.sum(oh0, axis=1, keepdims=True)
    base0 = locoff_ref[u]
    base1 = base0 + cnt0
    p0 = jnp.sum(oh0 * (base0 + c0), axis=0, keepdims=True)
    p1 = jnp.sum(oh1 * (base1 + c1), axis=0, keepdims=True)
    return p0.astype(jnp.int32), p1.astype(jnp.int32)


def _chunk_loop(tile, ntot_ref, crow_ref, make_copy):
    def per_chunk(q, _):
        row = crow_ref[tile * TILE_CHUNKS + q]
        make_copy(pl.multiple_of(q * SUBLANES, SUBLANES), pl.multiple_of(row, SUBLANES))
        return 0

    lax.fori_loop(0, ntot_ref[tile], per_chunk, 0)


def _dispatch_kernel(crow_ref, ntot_ref, tail_ref, ntail_ref,
                     h2_ref, route_ref, locoff_ref, upper_ref, xb_ref, buf_ref, zero_ref, sem, zsem):
    i = pl.program_id(0)
    n_steps = pl.num_programs(0)
    slot = i % 2

    def chunk_copy(s, u, loc, dst):
        return pltpu.make_async_copy(buf_ref.at[s, u, pl.ds(loc, SUBLANES), :],
                                     xb_ref.at[pl.ds(dst, SUBLANES), :], sem.at[s])

    def wait_step(step, s):
        for u in range(TILES_PER_STEP):
            n_rows = ntot_ref[step * TILES_PER_STEP + u] * SUBLANES

            @pl.when(n_rows > 0)
            def _():
                pltpu.make_async_copy(buf_ref.at[s, u, pl.ds(0, n_rows), :], xb_ref.at[pl.ds(0, n_rows), :],
                                      sem.at[s]).wait()

    @pl.when(i >= 2)
    def _():
        wait_step(i - 2, slot)

    iota_s = lax.broadcasted_iota(jnp.int32, (SLOT_ROWS, TOKEN_TILE), 0)
    for u in range(TILES_PER_STEP):
        lanes = slice(u * TOKEN_TILE, (u + 1) * TOKEN_TILE)
        p0, p1 = _tile_slots(route_ref, locoff_ref, upper_ref, u)
        hit0 = iota_s == p0
        hit1 = iota_s == p1
        perm = jnp.where(hit0 | hit1, 1.0, 0.0).astype(BF16)
        sorted_rows = _dot(perm, h2_ref[u * TOKEN_TILE:(u + 1) * TOKEN_TILE, :])
        buf_ref[slot, u, :, 0:PACK_COLS] = _pack_bf16_pairs(sorted_rows)
        gates = jnp.where(hit0, route_ref[2:3, lanes], 0.0) + jnp.where(hit1, route_ref[3:4, lanes], 0.0)
        gcol = jnp.sum(gates, axis=1, keepdims=True)
        buf_ref[slot, u, :, PACK_COLS:XB_COLS] = lax.bitcast_convert_type(
            jnp.broadcast_to(gcol, (SLOT_ROWS, LANES)), U32)

    for u in range(TILES_PER_STEP):
        _chunk_loop(i * TILES_PER_STEP + u, ntot_ref, crow_ref,
                    lambda loc, dst, u=u: chunk_copy(slot, u, loc, dst).start())

    @pl.when(i == n_steps - 1)
    def _():
        zero_ref[...] = jnp.zeros_like(zero_ref)

        def zero_copy(dst, rows, k):
            return pltpu.make_async_copy(zero_ref.at[pl.ds(0, rows), :], xb_ref.at[pl.ds(dst, rows), :], zsem.at[k])

        def per_expert(e, tot):
            n_big, n_small = tot
            n = ntail_ref[e]
            d0 = tail_ref[e]
            nb = n // ZERO_CHUNKS
            ns = n - nb * ZERO_CHUNKS

            def big(c, _):
                zero_copy(pl.multiple_of(d0 + c * ZERO_ROWS, SUBLANES), ZERO_ROWS, 0).start()
                return 0

            def small(c, _):
                zero_copy(pl.multiple_of(d0 + nb * ZERO_ROWS + c * SUBLANES, SUBLANES), SUBLANES, 1).start()
                return 0

            lax.fori_loop(0, nb, big, 0)
            lax.fori_loop(0, ns, small, 0)
            return n_big + nb, n_small + ns

        n_big, n_small = lax.fori_loop(0, N_EXPERTS, per_expert, (0, 0))

        @pl.when(i >= 1)
        def _():
            wait_step(i - 1, 1 - slot)

        wait_step(i, slot)

        def wait_big(c, _):
            zero_copy(0, ZERO_ROWS, 0).wait()
            return 0

        def wait_small(c, _):
            zero_copy(0, SUBLANES, 1).wait()
            return 0

        lax.fori_loop(0, n_big, wait_big, 0)
        lax.fori_loop(0, n_small, wait_small, 0)


def _dispatch_call(h2, route, locoff, upper, crow, ntot, tail, ntail, *, n_tiles, p_rows):
    grid_spec = pltpu.PrefetchScalarGridSpec(
        num_scalar_prefetch=4,
        grid=(n_tiles // TILES_PER_STEP,),
        in_specs=[
            pl.BlockSpec((STEP_TOKENS, D_MODEL), lambda i, *_: (i, 0)),
            pl.BlockSpec((SUBLANES, STEP_TOKENS), lambda i, *_: (0, i)),
            pl.BlockSpec((TILES_PER_STEP, N_EXPERTS, 1), lambda i, *_: (i, 0, 0)),
            pl.BlockSpec((TOKEN_TILE, TOKEN_TILE), lambda i, *_: (0, 0)),
        ],
        out_specs=pl.BlockSpec(memory_space=pl.ANY),
        scratch_shapes=[
            pltpu.VMEM((2, TILES_PER_STEP, SLOT_ROWS, XB_COLS), U32),
            pltpu.VMEM((ZERO_ROWS, XB_COLS), U32),
            pltpu.SemaphoreType.DMA((2,)),
            pltpu.SemaphoreType.DMA((2,)),
        ],
    )
    return pl.pallas_call(
        _dispatch_kernel,
        grid_spec=grid_spec,
        out_shape=jax.ShapeDtypeStruct((p_rows, XB_COLS), U32),
        compiler_params=pltpu.CompilerParams(dimension_semantics=("arbitrary",), vmem_limit_bytes=VMEM_LIMIT),
        name="dispatch",
    )(crow, ntot, tail, ntail, h2, route, locoff, upper)


def _ffn_kernel(blk_e_ref, nused_ref, xb_ref, wg_ref, wu_ref, wd_ref, yb_ref, wgb_ref, wub_ref, wdb_ref):
    i = pl.program_id(0)
    new_expert = (i == 0) | (blk_e_ref[i] != blk_e_ref[jnp.maximum(i - 1, 0)])

    @pl.when(new_expert)
    def _():
        wgb_ref[...] = wg_ref[0].astype(BF16)
        wub_ref[...] = wu_ref[0].astype(BF16)
        wdb_ref[...] = wd_ref[0].astype(BF16)

    @pl.when(i < nused_ref[0])
    def _():
        for c in range(FFN_BLOCK // FFN_CHUNK):
            rs = slice(c * FFN_CHUNK, (c + 1) * FFN_CHUNK)
            xb = _unpack_bf16_pairs(xb_ref[rs, 0:PACK_COLS])
            gate = lax.bitcast_convert_type(xb_ref[rs, PACK_COLS:PACK_COLS + 1], F32)
            a = _dot(xb, wgb_ref[...])
            u = _dot(xb, wub_ref[...])
            mid = (jax.nn.silu(a) * u).astype(BF16)
            y = _dot(mid, wdb_ref[...]) * gate
            yb_ref[rs, :] = _pack_bf16_pairs(y.astype(BF16).astype(F32))


def _ffn_call(xb, wg, wu, wd, blk_e, nused, *, n_blocks):
    def row_map(i, blk_e_ref, nused_ref):
        return (jnp.minimum(i, nused_ref[0] - 1), 0)

    def w_map(i, blk_e_ref, nused_ref):
        return (blk_e_ref[i], 0, 0)

    grid_spec = pltpu.PrefetchScalarGridSpec(
        num_scalar_prefetch=2,
        grid=(n_blocks,),
        in_specs=[
            pl.BlockSpec((FFN_BLOCK, XB_COLS), row_map),
            pl.BlockSpec((1, D_MODEL, D_EXPERT), w_map),
            pl.BlockSpec((1, D_MODEL, D_EXPERT), w_map),
            pl.BlockSpec((1, D_EXPERT, D_MODEL), w_map),
        ],
        out_specs=pl.BlockSpec((FFN_BLOCK, PACK_COLS), row_map),
        scratch_shapes=[
            pltpu.VMEM((D_MODEL, D_EXPERT), BF16),
            pltpu.VMEM((D_MODEL, D_EXPERT), BF16),
            pltpu.VMEM((D_EXPERT, D_MODEL), BF16),
        ],
    )
    return pl.pallas_call(
        _ffn_kernel,
        grid_spec=grid_spec,
        out_shape=jax.ShapeDtypeStruct((n_blocks * FFN_BLOCK, PACK_COLS), U32),
        compiler_params=pltpu.CompilerParams(dimension_semantics=("arbitrary",), vmem_limit_bytes=VMEM_LIMIT),
        name="expert_ffn",
    )(blk_e, nused, xb, wg, wu, wd)


def _combine_kernel(crow_ref, ntot_ref, y1_ref, route_ref, locoff_ref, upper_ref, fg_ref, yb_ref,
                    yp_ref, ys_ref, buf_ref, sem):
    i = pl.program_id(0)
    n_steps = pl.num_programs(0)
    slot = i % 2

    def chunk_copy(s, u, loc, src):
        return pltpu.make_async_copy(yb_ref.at[pl.ds(src, SUBLANES), :],
                                     buf_ref.at[s, u, pl.ds(loc, SUBLANES), :], sem.at[s])

    def issue_step(step, s):
        for u in range(TILES_PER_STEP):
            _chunk_loop(step * TILES_PER_STEP + u, ntot_ref, crow_ref,
                        lambda loc, src, u=u: chunk_copy(s, u, loc, src).start())

    @pl.when(i == 0)
    def _():
        buf_ref[...] = jnp.zeros_like(buf_ref)
        issue_step(0, 0)

    @pl.when(i + 1 < n_steps)
    def _():
        issue_step(i + 1, 1 - slot)

    iota_s = lax.broadcasted_iota(jnp.int32, (SLOT_ROWS, TOKEN_TILE), 0)
    perms = []
    for u in range(TILES_PER_STEP):
        p0, p1 = _tile_slots(route_ref, locoff_ref, upper_ref, u)
        perms.append(jnp.where((iota_s == p0) | (iota_s == p1), 1.0, 0.0).astype(BF16))

    for u in range(TILES_PER_STEP):
        n_rows = ntot_ref[i * TILES_PER_STEP + u] * SUBLANES

        @pl.when(n_rows > 0)
        def _():
            pltpu.make_async_copy(yb_ref.at[pl.ds(0, n_rows), :], buf_ref.at[slot, u, pl.ds(0, n_rows), :],
                                  sem.at[slot]).wait()

    for u in range(TILES_PER_STEP):
        rs = slice(u * TOKEN_TILE, (u + 1) * TOKEN_TILE)
        yb = _unpack_bf16_pairs(buf_ref[slot, u])
        moe = lax.dot_general(perms[u], yb, (((0,), (0,)), ((), ())), preferred_element_type=F32)
        y_tile = _rms(y1_ref[rs, :] + moe, fg_ref[...])
        yp_ref[rs, :] = y_tile

    @pl.when(i == n_steps - 1)
    def _():
        ys_ref[...] = y_tile


def _combine_call(y1, route, locoff, upper, fg, yb, crow, ntot, *, n_tiles, n_prompt, n_sample):
    n_steps = n_tiles // TILES_PER_STEP
    assert n_sample == TOKEN_TILE and n_prompt + n_sample == n_steps * STEP_TOKENS
    grid_spec = pltpu.PrefetchScalarGridSpec(
        num_scalar_prefetch=2,
        grid=(n_steps,),
        in_specs=[
            pl.BlockSpec((STEP_TOKENS, D_MODEL), lambda i, *_: (i, 0)),
            pl.BlockSpec((SUBLANES, STEP_TOKENS), lambda i, *_: (0, i)),
            pl.BlockSpec((TILES_PER_STEP, N_EXPERTS, 1), lambda i, *_: (i, 0, 0)),
            pl.BlockSpec((TOKEN_TILE, TOKEN_TILE), lambda i, *_: (0, 0)),
            pl.BlockSpec((1, D_MODEL), lambda i, *_: (0, 0)),
            pl.BlockSpec(memory_space=pl.ANY),
        ],
        out_specs=[
            pl.BlockSpec((STEP_TOKENS, D_MODEL), lambda i, *_: (i, 0)),
            pl.BlockSpec((n_sample, D_MODEL), lambda i, *_: (0, 0)),
        ],
        scratch_shapes=[
            pltpu.VMEM((2, TILES_PER_STEP, SLOT_ROWS, PACK_COLS), U32),
            pltpu.SemaphoreType.DMA((2,)),
        ],
    )
    return pl.pallas_call(
        _combine_kernel,
        grid_spec=grid_spec,
        out_shape=[jax.ShapeDtypeStruct((n_prompt, D_MODEL), F32), jax.ShapeDtypeStruct((n_sample, D_MODEL), F32)],
        compiler_params=pltpu.CompilerParams(dimension_semantics=("arbitrary",), vmem_limit_bytes=VMEM_LIMIT),
        name="combine",
    )(crow, ntot, y1, route, locoff, upper, fg, yb)


def _block_diag(w):
    h, d, _ = w.shape
    eye = jnp.eye(h, dtype=w.dtype)
    return (eye[:, None, :, None] * w[:, :, None, :]).reshape(h * d, h * d)


def _head_blocks(w):
    half = B_HEADS // 2
    return jnp.stack([_block_diag(w[:half]), _block_diag(w[half:])]).astype(BF16)


def _mixer_weights(l, length, reps, norm1_g, w_in, gmlp_ln_g, gmlp_ln_b, gmlp_w_s, gmlp_b_s, conv_w, conv_b,
                   lru_w_a, lru_b_a, lru_w_x, lru_b_x, lru_lambda, w_out, norm2_g,
                   router_group_w, router_group_b, router_expert_w, router_expert_b):
    mask = jnp.tril(jnp.ones((length, length), dtype=bool))
    ws = jnp.where(mask, gmlp_w_s[l][:, :length, :length], 0.0)
    eye = jnp.eye(reps, dtype=ws.dtype)
    wsm = (eye[None, :, None, :, None] * ws[:, None, :, None, :]).reshape(A_GROUPS, reps * length, reps * length)
    bs = jnp.tile(gmlp_b_s[l][:, :length], (1, reps))
    bsx = jnp.repeat(bs.T, A_GROUP_DIM, axis=1)
    wr = jnp.concatenate([router_expert_w[l].T, router_group_w[l].T,
                          jnp.zeros((ROUTER_ROWS - N_EXPERTS - N_GROUPS, D_MODEL), F32)], axis=0)
    br = jnp.concatenate([router_expert_b[l], router_group_b[l],
                          jnp.zeros((ROUTER_ROWS - N_EXPERTS - N_GROUPS,), F32)]).reshape(ROUTER_ROWS, 1)
    return [
        norm1_g[l].reshape(1, D_MODEL), w_in[l].astype(BF16),
        gmlp_ln_g[l].reshape(1, D_A), gmlp_ln_b[l].reshape(1, D_A), wsm.astype(BF16), bsx,
        conv_w[l], conv_b[l].reshape(1, D_B),
        _head_blocks(lru_w_a[l]), lru_b_a[l].reshape(1, D_B),
        _head_blocks(lru_w_x[l]), lru_b_x[l].reshape(1, D_B),
        lru_lambda[l].reshape(1, D_B), w_out[l].astype(BF16), norm2_g[l].reshape(1, D_MODEL),
        wr.astype(BF16), br,
    ]


def _routing_tables(cnt, n_blocks):
    n_tiles = cnt.shape[0]
    seg = (cnt + SUBLANES - 1) // SUBLANES * SUBLANES
    nch = seg // SUBLANES
    tot = jnp.sum(seg, axis=0)
    padded = (tot + FFN_BLOCK - 1) // FFN_BLOCK * FFN_BLOCK
    e_before = jnp.arange(N_EXPERTS)[None, :] < jnp.arange(N_EXPERTS)[:, None]
    t_before = jnp.arange(n_tiles)[None, :] < jnp.arange(n_tiles)[:, None]
    pad_start = jnp.sum(jnp.where(e_before, padded[None, :], 0), axis=1)
    pad_end = pad_start + padded
    doff = pad_start[None, :] + jnp.sum(jnp.where(t_before[:, :, None], seg[None, :, :], 0), axis=1)
    locoff = jnp.sum(jnp.where(e_before[None, :, :], seg[:, None, :], 0), axis=2)
    ntot = jnp.sum(nch, axis=1)
    first = locoff // SUBLANES
    q = jnp.arange(TILE_CHUNKS)
    owner = jnp.sum((first + nch)[:, None, :] <= q[None, :, None], axis=2)
    own = owner[:, :, None] == jnp.arange(N_EXPERTS)[None, None, :]
    crow = jnp.sum(jnp.where(own, doff[:, None, :] + SUBLANES * (q[None, :, None] - first[:, None, :]), 0), axis=2)
    tail = pad_start + tot
    ntail = (padded - tot) // SUBLANES
    blk_end = pad_end // FFN_BLOCK
    nused = blk_end[-1]
    blk = jnp.minimum(jnp.arange(n_blocks, dtype=jnp.int32), nused - 1)
    blk_e = jnp.minimum(jnp.sum((blk_end[None, :] <= blk[:, None]).astype(jnp.int32), axis=1), N_EXPERTS - 1)
    i32 = lambda a: a.astype(jnp.int32)
    return (i32(crow.reshape(-1)), i32(ntot), i32(tail), i32(ntail),
            locoff.astype(F32)[:, :, None], i32(blk_e), i32(nused.reshape(1)))


def kernel(x_prompt, x_sample, state_conv, state_rglru, norm1_g, w_in, gmlp_ln_g, gmlp_ln_b, gmlp_w_s, gmlp_b_s,
           conv_w, conv_b, lru_w_a, lru_b_a, lru_w_x, lru_b_x, lru_lambda, w_out, norm2_g,
           router_group_w, router_group_b, router_expert_w, router_expert_b,
           expert_w_gate, expert_w_up, expert_w_down, final_norm_g):
    depth = w_in.shape[0]
    nb, t_len, _ = x_prompt.shape
    db, dt, _ = x_sample.shape
    n_prompt = nb * t_len
    n_sample = db * dt
    assert t_len % PROMPT_TILE == 0 and n_prompt % TOKEN_TILE == 0 and n_sample == TOKEN_TILE
    n_total = n_prompt + n_sample
    assert n_total % STEP_TOKENS == 0
    n_tiles = n_total // TOKEN_TILE
    p_tiles = n_prompt // TOKEN_TILE
    p_rows = 2 * n_total + (SUBLANES - 1) * N_EXPERTS * n_tiles + N_EXPERTS * (FFN_BLOCK - 1)
    n_blocks = -(-p_rows // FFN_BLOCK)
    p_rows = n_blocks * FFN_BLOCK
    upper = jnp.triu(jnp.ones((TOKEN_TILE, TOKEN_TILE), BF16), 1)

    xp, xs = x_prompt, x_sample
    conv_p, h_p, v_s, conv_s, h_s = [], [], [], [], []
    for l in range(depth):
        lw = (norm1_g, w_in, gmlp_ln_g, gmlp_ln_b, gmlp_w_s, gmlp_b_s, conv_w, conv_b, lru_w_a, lru_b_a,
              lru_w_x, lru_b_x, lru_lambda, w_out, norm2_g, router_group_w, router_group_b,
              router_expert_w, router_expert_b)
        wts_p = _mixer_weights(l, GMLP_CHUNK, 1, *lw)
        wts_s = _mixer_weights(l, dt, db, *lw)
        zero_conv = jnp.zeros((nb, CONV_WIDTH - 1, D_B), F32)
        zero_h = jnp.zeros((nb, 1, D_B), F32)
        y1, h2, route, cnt_p, cp, hp = _mixer_call(
            xp, zero_conv, zero_h, wts_p, n_total=n_total, row_offset=0, sb=1, tt=PROMPT_TILE, n_sub=PROMPT_SUB,
            chunk=GMLP_CHUNK, seg_len=PROMPT_SEG, seg_stride=PROMPT_SEG + SUBLANES, chain=True, emit_vn=False)
        y1, h2, route, cnt_s, cs, hs, vs = _mixer_call(
            xs, state_conv[l], state_rglru[l][None], wts_s, n_total=n_total, row_offset=n_prompt, sb=db,
            tt=dt, n_sub=1, chunk=db * dt, seg_len=dt, seg_stride=dt + SUBLANES, chain=False, emit_vn=True,
            alias_in=(y1, h2, route))
        conv_p.append(cp)
        h_p.append(hp[:, 0])
        v_s.append(vs)
        conv_s.append(cs)
        h_s.append(hs[0])

        sub = PROMPT_TILE // TOKEN_TILE
        cnt = jnp.concatenate([
            jnp.swapaxes(cnt_p[:, :, :sub], 1, 2).reshape(p_tiles, N_EXPERTS),
            cnt_s[:, :, 0]], axis=0).astype(jnp.int32)
        crow, ntot, tail, ntail, locoff, blk_e, nused = _routing_tables(cnt, n_blocks)

        xb = _dispatch_call(h2, route, locoff, upper, crow, ntot, tail, ntail, n_tiles=n_tiles, p_rows=p_rows)
        yb = _ffn_call(xb, expert_w_gate[l], expert_w_up[l], expert_w_down[l], blk_e, nused, n_blocks=n_blocks)
        assert l == depth - 1, "deeper stacks need an un-normalised combine between layers"
        fg = final_norm_g.reshape(1, D_MODEL)
        yp, ysm = _combine_call(y1, route, locoff, upper, fg, yb, crow, ntot, n_tiles=n_tiles, n_prompt=n_prompt,
                                n_sample=n_sample)
        xp = yp.reshape(nb, t_len, D_MODEL)
        xs = ysm.reshape(db, dt, D_MODEL)

    return (xp, xs, jnp.stack(conv_p), jnp.stack(h_p), jnp.stack(v_s), jnp.stack(conv_s), jnp.stack(h_s))
```

```python
import functools

import jax
import jax.numpy as jnp
from jax import lax
from jax.experimental import pallas as pl
from jax.experimental.pallas import tpu as pltpu

D_MODEL = 1024
D_A = 512
D_B = 512
A_GROUPS = 4
A_GROUP_DIM = 128
GMLP_CHUNK = 128
B_HEADS = 8
CONV_WIDTH = 4
LRU_C = 8.0
N_GROUPS = 4
EXPERTS_PER_GROUP = 8
N_EXPERTS = 32
D_EXPERT = 512
EPS = 1e-6

SUBLANES = 8
LANES = 128
ROUTER_ROWS = 40
TOKEN_TILE = 256
PROMPT_TILE = 1024
PROMPT_SUB = 1
MIX_LAG = 6
PROMPT_SEG = PROMPT_TILE // PROMPT_SUB // SUBLANES
FFN_BLOCK = 1024
FFN_CHUNK = 1024
ZERO_CHUNKS = 8
ZERO_ROWS = ZERO_CHUNKS * SUBLANES
SLOT_ROWS = 2 * TOKEN_TILE + N_EXPERTS * SUBLANES
TILE_CHUNKS = SLOT_ROWS // SUBLANES
COPY_CLASSES = ((4, TILE_CHUNKS // 4), (3, N_EXPERTS), (2, N_EXPERTS), (1, N_EXPERTS))
TILES_PER_STEP = 3
STEP_TOKENS = TILES_PER_STEP * TOKEN_TILE
PACK_COLS = D_MODEL // 2
XB_COLS = PACK_COLS + LANES
U32 = jnp.uint32
HI_MASK = 0xFFFF0000
VMEM_LIMIT = 56 * 1024 * 1024

BF16 = jnp.bfloat16
F32 = jnp.float32


def _rms(x, g):
    return x * lax.rsqrt(jnp.mean(x * x, axis=-1, keepdims=True) + EPS) * g


def _dot(a, b):
    return jnp.dot(a, b, preferred_element_type=F32)


def _pack_bf16_pairs(x):
    bits = lax.bitcast_convert_type(x, U32)
    return (bits[:, PACK_COLS:] & U32(HI_MASK)) | (bits[:, :PACK_COLS] >> 16)


def _unpack_bf16_pairs(w):
    lo = lax.bitcast_convert_type(w << 16, F32).astype(BF16)
    hi = lax.bitcast_convert_type(w & U32(HI_MASK), F32).astype(BF16)
    return jnp.concatenate([lo, hi], axis=1)


def _mixer_kernel(x_ref, hist_ref, h0_ref, n1g_ref, win_ref, lng_ref, lnb_ref, wsm_ref, bsx_ref,
                  cw_ref, cb_ref, wa_ref, ba_ref, wx_ref, bx_ref, lam_ref, wout_ref, n2g_ref,
                  wr_ref, br_ref, *rest, sb, tt, n_sub, chunk, seg_len, seg_stride, chain, emit_vn, aliased):
    if aliased:
        rest = rest[3:]
    if emit_vn:
        (y1_ref, h2_ref, route_ref, cnt_ref, conv_ref, hlast_ref, vn_ref,
         xs_ref, sa_ref, sb_ref, mix_ref, hc_ref) = rest
    else:
        (y1_ref, h2_ref, route_ref, cnt_ref, conv_ref, hlast_ref,
         xs_ref, sa_ref, sb_ref, mix_ref, hc_ref) = rest
        vn_ref = None
    t_idx = pl.program_id(1)

    @pl.when(t_idx == 0)
    def _():
        xs_ref[:, SUBLANES - 3:SUBLANES, :] = hist_ref[...]
        hc_ref[...] = jnp.zeros_like(hc_ref)

    @pl.when(t_idx != 0)
    def _():
        xs_ref[:, SUBLANES - 3:SUBLANES, :] = xs_ref[:, SUBLANES + tt - 3:SUBLANES + tt, :]

    counts = []
    groups = [_mixer_rows(sub, counts, x_ref, h0_ref, n1g_ref, win_ref, lng_ref, lnb_ref, wsm_ref, bsx_ref,
                          cw_ref, cb_ref, wa_ref, ba_ref, wx_ref, bx_ref, lam_ref, wout_ref, n2g_ref, wr_ref,
                          br_ref, y1_ref, h2_ref, route_ref, hlast_ref, vn_ref, xs_ref, sa_ref, sb_ref, mix_ref,
                          hc_ref, sb=sb, tt=tt, n_sub=n_sub, chunk=chunk, seg_len=seg_len,
                          seg_stride=seg_stride, chain=chain)
              for sub in range(n_sub)]
    live = [True] * n_sub
    tick = 0
    while any(live):
        for sub in range(n_sub):
            if live[sub] and tick >= sub * MIX_LAG:
                live[sub] = next(groups[sub], "done") != "done"
        tick += 1
    conv_ref[...] = xs_ref[:, SUBLANES + tt - 3:SUBLANES + tt, :]
    cnt_ref[...] = sum(counts).reshape(1, N_EXPERTS, LANES)


def _mixer_rows(sub, counts, x_ref, h0_ref, n1g_ref, win_ref, lng_ref, lnb_ref, wsm_ref, bsx_ref, cw_ref, cb_ref,
                wa_ref, ba_ref, wx_ref, bx_ref, lam_ref, wout_ref, n2g_ref, wr_ref, br_ref,
                y1_ref, h2_ref, route_ref, hlast_ref, vn_ref, xs_ref, sa_ref, sb_ref, mix_ref, hc_ref,
                *, sb, tt, n_sub, chunk, seg_len, seg_stride, chain):
    tsub = tt // n_sub
    t0 = sub * tsub
    rows = sb * tsub
    r0 = sub * rows
    n_seg = rows // seg_len
    half = D_B // 2

    x = x_ref[:, t0:t0 + tsub, :].reshape(rows, D_MODEL)
    hb = _rms(x, n1g_ref[...]).astype(BF16)
    yield

    v_a = _dot(hb, win_ref[:, D_A:2 * D_A])
    x_b = _dot(hb, win_ref[:, 2 * D_A:2 * D_A + D_B])
    yield
    mixed = []
    for g in range(A_GROUPS):
        cols = slice(g * A_GROUP_DIM, (g + 1) * A_GROUP_DIM)
        vg = v_a[:, cols]
        mu = jnp.mean(vg, axis=-1, keepdims=True)
        dv = vg - mu
        var = jnp.mean(dv * dv, axis=-1, keepdims=True)
        vn = dv * lax.rsqrt(var + EPS) * lng_ref[:, cols] + lnb_ref[:, cols]
        if vn_ref is not None:
            vn_ref[:, t0:t0 + tsub, cols] = vn.reshape(sb, tsub, A_GROUP_DIM)
        vnb = vn.astype(BF16)
        mixed.append([_dot(wsm_ref[g], vnb[c * chunk:(c + 1) * chunk]) for c in range(rows // chunk)])
    yield
    u_a = _dot(hb, win_ref[:, 0:D_A])

    xs_ref[:, SUBLANES + t0:SUBLANES + t0 + tsub, :] = x_b.reshape(sb, tsub, D_B)
    xc = cb_ref[...] + x_b * cw_ref[3:4, :]
    for j in range(1, CONV_WIDTH):
        shifted = xs_ref[:, SUBLANES + t0 - j:SUBLANES + t0 - j + tsub, :].reshape(rows, D_B)
        xc = xc + shifted * cw_ref[3 - j:4 - j, :]
    yield

    xcb = xc.astype(BF16)
    r_lin = jnp.concatenate([_dot(xcb[:, :half], wa_ref[0]), _dot(xcb[:, half:], wa_ref[1])], axis=1)
    i_lin = jnp.concatenate([_dot(xcb[:, :half], wx_ref[0]), _dot(xcb[:, half:], wx_ref[1])], axis=1)

    for g in range(A_GROUPS):
        cols = slice(g * A_GROUP_DIM, (g + 1) * A_GROUP_DIM)
        for c in range(rows // chunk):
            rs = slice(c * chunk, (c + 1) * chunk)
            s = mixed[g][c] + bsx_ref[:, cols]
            mix_ref[r0 + c * chunk:r0 + (c + 1) * chunk, cols] = (u_a[rs, cols] * s).astype(BF16)

    g_b = _dot(hb, win_ref[:, 2 * D_A + D_B:])
    yield
    r = jax.nn.sigmoid(r_lin + ba_ref[...])
    i_g = jax.nn.sigmoid(i_lin + bx_ref[...])
    log_a = (-LRU_C * r) * jax.nn.softplus(-lam_ref[...])
    a = jnp.exp(log_a)
    gain = jnp.sqrt(1.0 - a * a)
    bterm = gain * i_g * xc

    n_lb = D_B // LANES
    sbase = sub * n_seg * seg_stride
    for j in range(n_seg):
        for k in range(n_lb):
            lc = slice(k * LANES, (k + 1) * LANES)
            dst = slice(sbase + j * seg_stride, sbase + j * seg_stride + seg_len)
            sa_ref[k, dst, :] = a[j * seg_len:(j + 1) * seg_len, lc]
            sb_ref[k, dst, :] = bterm[j * seg_len:(j + 1) * seg_len, lc]
    yield

    y_a = x + _dot(mix_ref[r0:r0 + rows, 0:D_A], wout_ref[0:D_A, :])
    gel = jax.nn.gelu(g_b)
    yield

    for grp in range(n_seg // SUBLANES):
        base = sbase + grp * SUBLANES * seg_stride
        bs = slice(grp * SUBLANES, (grp + 1) * SUBLANES)
        if chain:
            h_init = tuple(jnp.zeros((SUBLANES, LANES), F32) for _ in range(n_lb))
        else:
            h_init = tuple(h0_ref[0, bs, k * LANES:(k + 1) * LANES] for k in range(n_lb))
        a_init = tuple(jnp.ones((SUBLANES, LANES), F32) for _ in range(n_lb))

        def step(i, carry, base=base):
            hs, acs = carry
            idx = pl.ds(base + i, SUBLANES, stride=seg_stride)
            new_h, new_a = [], []
            for k in range(n_lb):
                av = sa_ref[k, idx, :]
                h = av * hs[k] + sb_ref[k, idx, :]
                sb_ref[k, idx, :] = h
                new_h.append(h)
                if chain:
                    ac = av * acs[k]
                    sa_ref[k, idx, :] = ac
                    new_a.append(ac)
                else:
                    new_a.append(acs[k])
            return tuple(new_h), tuple(new_a)

        carry = (h_init, a_init)
        for i in range(seg_len):
            carry = step(i, carry)
        h_end, a_end = carry
        yield

        for k in range(n_lb):
            lc = slice(k * LANES, (k + 1) * LANES)
            mc = slice(D_A + k * LANES, D_A + (k + 1) * LANES)
            if chain:
                h_in = hc_ref[:, lc]
                for j in range(SUBLANES):
                    seg = slice(base + j * seg_stride, base + j * seg_stride + seg_len)
                    rs = slice((grp * SUBLANES + j) * seg_len, (grp * SUBLANES + j + 1) * seg_len)
                    ms = slice(r0 + rs.start, r0 + rs.stop)
                    h_seg = sb_ref[k, seg, :] + sa_ref[k, seg, :] * h_in
                    mix_ref[ms, mc] = (h_seg * gel[rs, lc]).astype(BF16)
                    h_in = h_end[k][j:j + 1] + a_end[k][j:j + 1] * h_in
                hc_ref[:, lc] = h_in
                hlast_ref[0, :, lc] = h_in
            else:
                for j in range(SUBLANES):
                    seg = slice(base + j * seg_stride, base + j * seg_stride + seg_len)
                    rs = slice((grp * SUBLANES + j) * seg_len, (grp * SUBLANES + j + 1) * seg_len)
                    ms = slice(r0 + rs.start, r0 + rs.stop)
                    mix_ref[ms, mc] = (sb_ref[k, seg, :] * gel[rs, lc]).astype(BF16)
                hlast_ref[0, bs, lc] = h_end[k]
        yield

    y1 = y_a + _dot(mix_ref[r0:r0 + rows, D_A:], wout_ref[D_A:, :])
    y1_ref[r0:r0 + rows, :] = y1
    yield
    h2 = _rms(y1, n2g_ref[...]).astype(BF16)
    h2_ref[r0:r0 + rows, :] = h2
    yield

    logits = lax.dot_general(wr_ref[...], h2, (((1,), (1,)), ((), ())), preferred_element_type=F32) + br_ref[...]
    gl = [logits[N_EXPERTS + k:N_EXPERTS + k + 1, :] for k in range(N_GROUPS)]
    gmax = jnp.maximum(jnp.maximum(gl[0], gl[1]), jnp.maximum(gl[2], gl[3]))
    grp_idx = jnp.where(gl[0] == gmax, 0, jnp.where(gl[1] == gmax, 1, jnp.where(gl[2] == gmax, 2, 3)))
    gsum = (jnp.exp(gl[0] - gmax) + jnp.exp(gl[1] - gmax)) + (jnp.exp(gl[2] - gmax) + jnp.exp(gl[3] - gmax))
    p_grp = 1.0 / gsum
    e_in = jnp.where(grp_idx == 0, logits[0:8],
                     jnp.where(grp_idx == 1, logits[8:16], jnp.where(grp_idx == 2, logits[16:24], logits[24:32])))
    iota8 = lax.broadcasted_iota(jnp.int32, (EXPERTS_PER_GROUP, rows), 0)
    m1 = jnp.max(e_in, axis=0, keepdims=True)
    i1 = jnp.min(jnp.where(e_in == m1, iota8, EXPERTS_PER_GROUP), axis=0, keepdims=True)
    e_rest = jnp.where(iota8 == i1, -jnp.inf, e_in)
    m2 = jnp.max(e_rest, axis=0, keepdims=True)
    i2 = jnp.min(jnp.where(e_rest == m2, iota8, EXPERTS_PER_GROUP), axis=0, keepdims=True)
    t2 = jnp.exp(m2 - m1)
    den = 1.0 + t2
    gate0 = p_grp * (1.0 / den)
    gate1 = p_grp * (t2 / den)
    e0 = (grp_idx * EXPERTS_PER_GROUP + i1).astype(F32)
    e1 = (grp_idx * EXPERTS_PER_GROUP + i2).astype(F32)
    route_ref[:, r0:r0 + rows] = jnp.where(iota8 == 0, e0, jnp.where(iota8 == 1, e1, jnp.where(
        iota8 == 2, gate0, jnp.where(iota8 == 3, gate1, 0.0))))

    iota_e = lax.broadcasted_iota(jnp.int32, (N_EXPERTS, rows), 0).astype(F32)
    onehot = (iota_e == e0).astype(F32) + (iota_e == e1).astype(F32)
    lane = lax.broadcasted_iota(jnp.int32, (N_EXPERTS, LANES), 1)
    for s in range(rows // TOKEN_TILE):
        c_s = jnp.sum(onehot[:, s * TOKEN_TILE:(s + 1) * TOKEN_TILE], axis=1, keepdims=True)
        counts.append(jnp.where(lane == r0 // TOKEN_TILE + s, c_s, 0.0))


def _mixer_call(x, hist, h0, wts, *, n_total, row_offset, sb, tt, n_sub, chunk, seg_len, seg_stride, chain,
                emit_vn, alias_in=None):
    nb, t_len, _ = x.shape
    nbb = nb // sb
    ntt = t_len // tt
    rows = sb * tt
    blk0 = row_offset // rows
    n_seg = rows // seg_len

    def full(arr):
        nd = arr.ndim
        return pl.BlockSpec(arr.shape, lambda b, t, _nd=nd: (0,) * _nd)

    in_specs = [
        pl.BlockSpec((sb, tt, D_MODEL), lambda b, t: (b, t, 0)),
        pl.BlockSpec((sb, CONV_WIDTH - 1, D_B), lambda b, t: (b, 0, 0)),
        pl.BlockSpec((1, sb, D_B), lambda b, t: (b, 0, 0)),
    ] + [full(w) for w in wts]
    args = [x, hist, h0] + list(wts)
    io_alias = {}
    if alias_in is not None:
        for k, arr in enumerate(alias_in):
            io_alias[len(args)] = k
            in_specs.append(pl.BlockSpec(memory_space=pl.ANY))
            args.append(arr)

    out_shape = [
        jax.ShapeDtypeStruct((n_total, D_MODEL), F32),
        jax.ShapeDtypeStruct((n_total, D_MODEL), BF16),
        jax.ShapeDtypeStruct((SUBLANES, n_total), F32),
        jax.ShapeDtypeStruct((nbb * ntt, N_EXPERTS, LANES), F32),
        jax.ShapeDtypeStruct((nb, CONV_WIDTH - 1, D_B), F32),
        jax.ShapeDtypeStruct((nbb, sb, D_B), F32),
    ]
    out_specs = [
        pl.BlockSpec((rows, D_MODEL), lambda b, t: (blk0 + b * ntt + t, 0)),
        pl.BlockSpec((rows, D_MODEL), lambda b, t: (blk0 + b * ntt + t, 0)),
        pl.BlockSpec((SUBLANES, rows), lambda b, t: (0, blk0 + b * ntt + t)),
        pl.BlockSpec((1, N_EXPERTS, LANES), lambda b, t: (b * ntt + t, 0, 0)),
        pl.BlockSpec((sb, CONV_WIDTH - 1, D_B), lambda b, t: (b, 0, 0)),
        pl.BlockSpec((1, sb, D_B), lambda b, t: (b, 0, 0)),
    ]
    if emit_vn:
        out_shape.append(jax.ShapeDtypeStruct((nb, t_len, D_A), F32))
        out_specs.append(pl.BlockSpec((sb, tt, D_A), lambda b, t: (b, t, 0)))

    assert n_sub == 1 or sb == 1
    kern = functools.partial(_mixer_kernel, sb=sb, tt=tt, n_sub=n_sub, chunk=chunk, seg_len=seg_len,
                             seg_stride=seg_stride,
                             chain=chain, emit_vn=emit_vn, aliased=alias_in is not None)
    return pl.pallas_call(
        kern,
        grid=(nbb, ntt),
        in_specs=in_specs,
        out_specs=out_specs,
        out_shape=out_shape,
        scratch_shapes=[
            pltpu.VMEM((sb, SUBLANES + tt, D_B), F32),
            pltpu.VMEM((D_B // LANES, n_seg * seg_stride, LANES), F32),
            pltpu.VMEM((D_B // LANES, n_seg * seg_stride, LANES), F32),
            pltpu.VMEM((rows, D_MODEL), BF16),
            pltpu.VMEM((1, D_B), F32),
        ],
        input_output_aliases=io_alias,
        compiler_params=pltpu.CompilerParams(dimension_semantics=("arbitrary", "arbitrary"),
                                             vmem_limit_bytes=VMEM_LIMIT),
        name="mixer_chain" if chain else "mixer_step",
    )(*args)


def _tile_slots(route_ref, locoff_ref, upper_ref, u):
    lanes = slice(u * TOKEN_TILE, (u + 1) * TOKEN_TILE)
    e0 = route_ref[0:1, lanes]
    e1 = route_ref[1:2, lanes]
    iota_e = lax.broadcasted_iota(jnp.int32, (N_EXPERTS, TOKEN_TILE), 0).astype(F32)
    oh0 = (iota_e == e0).astype(F32)
    oh1 = (iota_e == e1).astype(F32)
    c0 = _dot(oh0.astype(BF16), upper_ref[...])
    c1 = _dot(oh1.astype(BF16), upper_ref[...])
    cnt0 = jnp.sum(oh0, axis=1, keepdims=True)
    base0 = locoff_ref[u]
    base1 = base0 + cnt0
    p0 = jnp.sum(oh0 * (base0 + c0), axis=0, keepdims=True)
    p1 = jnp.sum(oh1 * (base1 + c1), axis=0, keepdims=True)
    return p0.astype(jnp.int32), p1.astype(jnp.int32)


def _copy_ops(tile, n_tiles, nops_ref, oloc_ref, odst_ref, make_copy):
    base = 0
    for k, (chunks, cap) in enumerate(COPY_CLASSES):
        def per_op(s, _, chunks=chunks, cap=cap, base=base):
            idx = base + tile * cap + s
            make_copy(pl.multiple_of(oloc_ref[idx], SUBLANES), pl.multiple_of(odst_ref[idx], SUBLANES),
                      chunks * SUBLANES)
            return 0

        lax.fori_loop(0, nops_ref[k * n_tiles + tile], per_op, 0)
        base += n_tiles * cap


def _dispatch_kernel(nops_ref, oloc_ref, odst_ref, ntot_ref, tail_ref, ntail_ref,
                     h2_ref, route_ref, locoff_ref, upper_ref, xb_ref, buf_ref, zero_ref, sem, zsem, *, n_tiles):
    i = pl.program_id(0)
    n_steps = pl.num_programs(0)
    slot = i % 2

    def chunk_copy(s, u, loc, dst, rows=SUBLANES):
        return pltpu.make_async_copy(buf_ref.at[s, u, pl.ds(loc, rows), :],
                                     xb_ref.at[pl.ds(dst, rows), :], sem.at[s])

    def wait_step(step, s):
        for u in range(TILES_PER_STEP):
            n_rows = ntot_ref[step * TILES_PER_STEP + u] * SUBLANES

            @pl.when(n_rows > 0)
            def _():
                pltpu.make_async_copy(buf_ref.at[s, u, pl.ds(0, n_rows), :], xb_ref.at[pl.ds(0, n_rows), :],
                                      sem.at[s]).wait()

    @pl.when(i >= 2)
    def _():
        wait_step(i - 2, slot)

    iota_s = lax.broadcasted_iota(jnp.int32, (SLOT_ROWS, TOKEN_TILE), 0)
    for u in range(TILES_PER_STEP):
        lanes = slice(u * TOKEN_TILE, (u + 1) * TOKEN_TILE)
        p0, p1 = _tile_slots(route_ref, locoff_ref, upper_ref, u)
        hit0 = iota_s == p0
        hit1 = iota_s == p1
        perm = jnp.where(hit0 | hit1, 1.0, 0.0).astype(BF16)
        sorted_rows = _dot(perm, h2_ref[u * TOKEN_TILE:(u + 1) * TOKEN_TILE, :])
        buf_ref[slot, u, :, 0:PACK_COLS] = _pack_bf16_pairs(sorted_rows)
        gates = jnp.where(hit0, route_ref[2:3, lanes], 0.0) + jnp.where(hit1, route_ref[3:4, lanes], 0.0)
        gcol = jnp.sum(gates, axis=1, keepdims=True)
        buf_ref[slot, u, :, PACK_COLS:XB_COLS] = lax.bitcast_convert_type(
            jnp.broadcast_to(gcol, (SLOT_ROWS, LANES)), U32)

    for u in range(TILES_PER_STEP):
        _copy_ops(i * TILES_PER_STEP + u, n_tiles, nops_ref, oloc_ref, odst_ref,
                  lambda loc, dst, rows, u=u: chunk_copy(slot, u, loc, dst, rows).start())

    @pl.when(i == n_steps - 1)
    def _():
        zero_ref[...] = jnp.zeros_like(zero_ref)

        def zero_copy(dst, rows, k):
            return pltpu.make_async_copy(zero_ref.at[pl.ds(0, rows), :], xb_ref.at[pl.ds(dst, rows), :], zsem.at[k])

        def per_expert(e, tot):
            n_big, n_small = tot
            n = ntail_ref[e]
            d0 = tail_ref[e]
            nb = n // ZERO_CHUNKS
            ns = n - nb * ZERO_CHUNKS

            def big(c, _):
                zero_copy(pl.multiple_of(d0 + c * ZERO_ROWS, SUBLANES), ZERO_ROWS, 0).start()
                return 0

            def small(c, _):
                zero_copy(pl.multiple_of(d0 + nb * ZERO_ROWS + c * SUBLANES, SUBLANES), SUBLANES, 1).start()
                return 0

            lax.fori_loop(0, nb, big, 0)
            lax.fori_loop(0, ns, small, 0)
            return n_big + nb, n_small + ns

        n_big, n_small = lax.fori_loop(0, N_EXPERTS, per_expert, (0, 0))

        @pl.when(i >= 1)
        def _():
            wait_step(i - 1, 1 - slot)

        wait_step(i, slot)

        def wait_big(c, _):
            zero_copy(0, ZERO_ROWS, 0).wait()
            return 0

        def wait_small(c, _):
            zero_copy(0, SUBLANES, 1).wait()
            return 0

        lax.fori_loop(0, n_big, wait_big, 0)
        lax.fori_loop(0, n_small, wait_small, 0)


def _dispatch_call(h2, route, locoff, upper, ops, ntot, tail, ntail, *, n_tiles, p_rows):
    grid_spec = pltpu.PrefetchScalarGridSpec(
        num_scalar_prefetch=6,
        grid=(n_tiles // TILES_PER_STEP,),
        in_specs=[
            pl.BlockSpec((STEP_TOKENS, D_MODEL), lambda i, *_: (i, 0)),
            pl.BlockSpec((SUBLANES, STEP_TOKENS), lambda i, *_: (0, i)),
            pl.BlockSpec((TILES_PER_STEP, N_EXPERTS, 1), lambda i, *_: (i, 0, 0)),
            pl.BlockSpec((TOKEN_TILE, TOKEN_TILE), lambda i, *_: (0, 0)),
        ],
        out_specs=pl.BlockSpec(memory_space=pl.ANY),
        scratch_shapes=[
            pltpu.VMEM((2, TILES_PER_STEP, SLOT_ROWS, XB_COLS), U32),
            pltpu.VMEM((ZERO_ROWS, XB_COLS), U32),
            pltpu.SemaphoreType.DMA((2,)),
            pltpu.SemaphoreType.DMA((2,)),
        ],
    )
    return pl.pallas_call(
        functools.partial(_dispatch_kernel, n_tiles=n_tiles),
        grid_spec=grid_spec,
        out_shape=jax.ShapeDtypeStruct((p_rows, XB_COLS), U32),
        compiler_params=pltpu.CompilerParams(dimension_semantics=("arbitrary",), vmem_limit_bytes=VMEM_LIMIT),
        name="dispatch",
    )(*ops, ntot, tail, ntail, h2, route, locoff, upper)


def _ffn_kernel(blk_e_ref, nused_ref, xb_ref, wg_ref, wu_ref, wd_ref, yb_ref, wgb_ref, wub_ref, wdb_ref):
    i = pl.program_id(0)
    new_expert = (i == 0) | (blk_e_ref[i] != blk_e_ref[jnp.maximum(i - 1, 0)])

    @pl.when(new_expert)
    def _():
        wgb_ref[...] = wg_ref[0].astype(BF16)
        wub_ref[...] = wu_ref[0].astype(BF16)
        wdb_ref[...] = wd_ref[0].astype(BF16)

    @pl.when(i < nused_ref[0])
    def _():
        for c in range(FFN_BLOCK // FFN_CHUNK):
            rs = slice(c * FFN_CHUNK, (c + 1) * FFN_CHUNK)
            xb = _unpack_bf16_pairs(xb_ref[rs, 0:PACK_COLS])
            gate = lax.bitcast_convert_type(xb_ref[rs, PACK_COLS:PACK_COLS + 1], F32)
            a = _dot(xb, wgb_ref[...])
            u = _dot(xb, wub_ref[...])
            mid = (jax.nn.silu(a) * u).astype(BF16)
            y = _dot(mid, wdb_ref[...]) * gate
            yb_ref[rs, :] = _pack_bf16_pairs(y.astype(BF16).astype(F32))


def _ffn_call(xb, wg, wu, wd, blk_e, nused, *, n_blocks):
    def row_map(i, blk_e_ref, nused_ref):
        return (jnp.minimum(i, nused_ref[0] - 1), 0)

    def w_map(i, blk_e_ref, nused_ref):
        return (blk_e_ref[i], 0, 0)

    grid_spec = pltpu.PrefetchScalarGridSpec(
        num_scalar_prefetch=2,
        grid=(n_blocks,),
        in_specs=[
            pl.BlockSpec((FFN_BLOCK, XB_COLS), row_map),
            pl.BlockSpec((1, D_MODEL, D_EXPERT), w_map),
            pl.BlockSpec((1, D_MODEL, D_EXPERT), w_map),
            pl.BlockSpec((1, D_EXPERT, D_MODEL), w_map),
        ],
        out_specs=pl.BlockSpec((FFN_BLOCK, PACK_COLS), row_map),
        scratch_shapes=[
            pltpu.VMEM((D_MODEL, D_EXPERT), BF16),
            pltpu.VMEM((D_MODEL, D_EXPERT), BF16),
            pltpu.VMEM((D_EXPERT, D_MODEL), BF16),
        ],
    )
    return pl.pallas_call(
        _ffn_kernel,
        grid_spec=grid_spec,
        out_shape=jax.ShapeDtypeStruct((n_blocks * FFN_BLOCK, PACK_COLS), U32),
        compiler_params=pltpu.CompilerParams(dimension_semantics=("arbitrary",), vmem_limit_bytes=VMEM_LIMIT),
        name="expert_ffn",
    )(blk_e, nused, xb, wg, wu, wd)


def _combine_kernel(nops_ref, oloc_ref, odst_ref, ntot_ref, y1_ref, route_ref, locoff_ref, upper_ref, fg_ref,
                    yb_ref, yp_ref, ys_ref, buf_ref, sem, *, n_tiles):
    i = pl.program_id(0)
    n_steps = pl.num_programs(0)
    slot = i % 2

    def chunk_copy(s, u, loc, src, rows):
        return pltpu.make_async_copy(yb_ref.at[pl.ds(src, rows), :],
                                     buf_ref.at[s, u, pl.ds(loc, rows), :], sem.at[s])

    def issue_step(step, s):
        for u in range(TILES_PER_STEP):
            _copy_ops(step * TILES_PER_STEP + u, n_tiles, nops_ref, oloc_ref, odst_ref,
                      lambda loc, src, rows, u=u: chunk_copy(s, u, loc, src, rows).start())

    @pl.when(i == 0)
    def _():
        buf_ref[...] = jnp.zeros_like(buf_ref)
        issue_step(0, 0)

    @pl.when(i + 1 < n_steps)
    def _():
        issue_step(i + 1, 1 - slot)

    iota_s = lax.broadcasted_iota(jnp.int32, (SLOT_ROWS, TOKEN_TILE), 0)
    perms = []
    for u in range(TILES_PER_STEP):
        p0, p1 = _tile_slots(route_ref, locoff_ref, upper_ref, u)
        perms.append(jnp.where((iota_s == p0) | (iota_s == p1), 1.0, 0.0).astype(BF16))

    for u in range(TILES_PER_STEP):
        n_rows = ntot_ref[i * TILES_PER_STEP + u] * SUBLANES

        @pl.when(n_rows > 0)
        def _():
            pltpu.make_async_copy(yb_ref.at[pl.ds(0, n_rows), :], buf_ref.at[slot, u, pl.ds(0, n_rows), :],
                                  sem.at[slot]).wait()

    for u in range(TILES_PER_STEP):
        rs = slice(u * TOKEN_TILE, (u + 1) * TOKEN_TILE)
        yb = _unpack_bf16_pairs(buf_ref[slot, u])
        moe = lax.dot_general(perms[u], yb, (((0,), (0,)), ((), ())), preferred_element_type=F32)
        y_tile = _rms(y1_ref[rs, :] + moe, fg_ref[...])
        yp_ref[rs, :] = y_tile

    @pl.when(i == n_steps - 1)
    def _():
        ys_ref[...] = y_tile


def _combine_call(y1, route, locoff, upper, fg, yb, ops, ntot, *, n_tiles, n_prompt, n_sample):
    n_steps = n_tiles // TILES_PER_STEP
    assert n_sample == TOKEN_TILE and n_prompt + n_sample == n_steps * STEP_TOKENS
    grid_spec = pltpu.PrefetchScalarGridSpec(
        num_scalar_prefetch=4,
        grid=(n_steps,),
        in_specs=[
            pl.BlockSpec((STEP_TOKENS, D_MODEL), lambda i, *_: (i, 0)),
            pl.BlockSpec((SUBLANES, STEP_TOKENS), lambda i, *_: (0, i)),
            pl.BlockSpec((TILES_PER_STEP, N_EXPERTS, 1), lambda i, *_: (i, 0, 0)),
            pl.BlockSpec((TOKEN_TILE, TOKEN_TILE), lambda i, *_: (0, 0)),
            pl.BlockSpec((1, D_MODEL), lambda i, *_: (0, 0)),
            pl.BlockSpec(memory_space=pl.ANY),
        ],
        out_specs=[
            pl.BlockSpec((STEP_TOKENS, D_MODEL), lambda i, *_: (i, 0)),
            pl.BlockSpec((n_sample, D_MODEL), lambda i, *_: (0, 0)),
        ],
        scratch_shapes=[
            pltpu.VMEM((2, TILES_PER_STEP, SLOT_ROWS, PACK_COLS), U32),
            pltpu.SemaphoreType.DMA((2,)),
        ],
    )
    return pl.pallas_call(
        functools.partial(_combine_kernel, n_tiles=n_tiles),
        grid_spec=grid_spec,
        out_shape=[jax.ShapeDtypeStruct((n_prompt, D_MODEL), F32), jax.ShapeDtypeStruct((n_sample, D_MODEL), F32)],
        compiler_params=pltpu.CompilerParams(dimension_semantics=("arbitrary",), vmem_limit_bytes=VMEM_LIMIT),
        name="combine",
    )(*ops, ntot, y1, route, locoff, upper, fg, yb)


def _block_diag(w):
    h, d, _ = w.shape
    eye = jnp.eye(h, dtype=w.dtype)
    return (eye[:, None, :, None] * w[:, :, None, :]).reshape(h * d, h * d)


def _head_blocks(w):
    half = B_HEADS // 2
    return jnp.stack([_block_diag(w[:half]), _block_diag(w[half:])]).astype(BF16)


def _mixer_weights(l, length, reps, norm1_g, w_in, gmlp_ln_g, gmlp_ln_b, gmlp_w_s, gmlp_b_s, conv_w, conv_b,
                   lru_w_a, lru_b_a, lru_w_x, lru_b_x, lru_lambda, w_out, norm2_g,
                   router_group_w, router_group_b, router_expert_w, router_expert_b):
    mask = jnp.tril(jnp.ones((length, length), dtype=bool))
    ws = jnp.where(mask, gmlp_w_s[l][:, :length, :length], 0.0)
    eye = jnp.eye(reps, dtype=ws.dtype)
    wsm = (eye[None, :, None, :, None] * ws[:, None, :, None, :]).reshape(A_GROUPS, reps * length, reps * length)
    bs = jnp.tile(gmlp_b_s[l][:, :length], (1, reps))
    bsx = jnp.repeat(bs.T, A_GROUP_DIM, axis=1)
    wr = jnp.concatenate([router_expert_w[l].T, router_group_w[l].T,
                          jnp.zeros((ROUTER_ROWS - N_EXPERTS - N_GROUPS, D_MODEL), F32)], axis=0)
    br = jnp.concatenate([router_expert_b[l], router_group_b[l],
                          jnp.zeros((ROUTER_ROWS - N_EXPERTS - N_GROUPS,), F32)]).reshape(ROUTER_ROWS, 1)
    return [
        norm1_g[l].reshape(1, D_MODEL), w_in[l].astype(BF16),
        gmlp_ln_g[l].reshape(1, D_A), gmlp_ln_b[l].reshape(1, D_A), wsm.astype(BF16), bsx,
        conv_w[l], conv_b[l].reshape(1, D_B),
        _head_blocks(lru_w_a[l]), lru_b_a[l].reshape(1, D_B),
        _head_blocks(lru_w_x[l]), lru_b_x[l].reshape(1, D_B),
        lru_lambda[l].reshape(1, D_B), w_out[l].astype(BF16), norm2_g[l].reshape(1, D_MODEL),
        wr.astype(BF16), br,
    ]


def _routing_tables(cnt, n_blocks):
    n_tiles = cnt.shape[0]
    seg = (cnt + SUBLANES - 1) // SUBLANES * SUBLANES
    nch = seg // SUBLANES
    tot = jnp.sum(seg, axis=0)
    padded = (tot + FFN_BLOCK - 1) // FFN_BLOCK * FFN_BLOCK
    e_before = jnp.arange(N_EXPERTS)[None, :] < jnp.arange(N_EXPERTS)[:, None]
    t_before = jnp.arange(n_tiles)[None, :] < jnp.arange(n_tiles)[:, None]
    pad_start = jnp.sum(jnp.where(e_before, padded[None, :], 0), axis=1)
    pad_end = pad_start + padded
    doff = pad_start[None, :] + jnp.sum(jnp.where(t_before[:, :, None], seg[None, :, :], 0), axis=1)
    locoff = jnp.sum(jnp.where(e_before[None, :, :], seg[:, None, :], 0), axis=2)
    ntot = jnp.sum(nch, axis=1)
    whole = nch // COPY_CLASSES[0][0]
    done = whole * COPY_CLASSES[0][0] * SUBLANES
    nops, olocs, odsts = [], [], []
    for chunks, cap in COPY_CLASSES:
        if chunks == COPY_CLASSES[0][0]:
            m, loc0, dst0 = whole, locoff, doff
        else:
            m, loc0, dst0 = (nch - whole * COPY_CLASSES[0][0] == chunks).astype(jnp.int32), locoff + done, doff + done
        start = jnp.sum(jnp.where(e_before[None, :, :], m[:, None, :], 0), axis=2)
        s = jnp.arange(cap)
        owner = jnp.sum((start + m)[:, None, :] <= s[None, :, None], axis=2)
        own = owner[:, :, None] == jnp.arange(N_EXPERTS)[None, None, :]
        step = chunks * SUBLANES * (s[None, :, None] - start[:, None, :])
        olocs.append(jnp.sum(jnp.where(own, loc0[:, None, :] + step, 0), axis=2).reshape(-1))
        odsts.append(jnp.sum(jnp.where(own, dst0[:, None, :] + step, 0), axis=2).reshape(-1))
        nops.append(jnp.sum(m, axis=1))
    tail = pad_start + tot
    ntail = (padded - tot) // SUBLANES
    blk_end = pad_end // FFN_BLOCK
    nused = blk_end[-1]
    blk = jnp.minimum(jnp.arange(n_blocks, dtype=jnp.int32), nused - 1)
    blk_e = jnp.minimum(jnp.sum((blk_end[None, :] <= blk[:, None]).astype(jnp.int32), axis=1), N_EXPERTS - 1)
    i32 = lambda a: a.astype(jnp.int32)
    ops = (i32(jnp.concatenate(nops)), i32(jnp.concatenate(olocs)), i32(jnp.concatenate(odsts)))
    return (ops, i32(ntot), i32(tail), i32(ntail),
            locoff.astype(F32)[:, :, None], i32(blk_e), i32(nused.reshape(1)))


def kernel(x_prompt, x_sample, state_conv, state_rglru, norm1_g, w_in, gmlp_ln_g, gmlp_ln_b, gmlp_w_s, gmlp_b_s,
           conv_w, conv_b, lru_w_a, lru_b_a, lru_w_x, lru_b_x, lru_lambda, w_out, norm2_g,
           router_group_w, router_group_b, router_expert_w, router_expert_b,
           expert_w_gate, expert_w_up, expert_w_down, final_norm_g):
    depth = w_in.shape[0]
    nb, t_len, _ = x_prompt.shape
    db, dt, _ = x_sample.shape
    n_prompt = nb * t_len
    n_sample = db * dt
    assert t_len % PROMPT_TILE == 0 and n_prompt % TOKEN_TILE == 0 and n_sample == TOKEN_TILE
    n_total = n_prompt + n_sample
    assert n_total % STEP_TOKENS == 0
    n_tiles = n_total // TOKEN_TILE
    p_tiles = n_prompt // TOKEN_TILE
    p_rows = 2 * n_total + (SUBLANES - 1) * N_EXPERTS * n_tiles + N_EXPERTS * (FFN_BLOCK - 1)
    n_blocks = -(-p_rows // FFN_BLOCK)
    p_rows = n_blocks * FFN_BLOCK
    upper = jnp.triu(jnp.ones((TOKEN_TILE, TOKEN_TILE), BF16), 1)

    xp, xs = x_prompt, x_sample
    conv_p, h_p, v_s, conv_s, h_s = [], [], [], [], []
    for l in range(depth):
        lw = (norm1_g, w_in, gmlp_ln_g, gmlp_ln_b, gmlp_w_s, gmlp_b_s, conv_w, conv_b, lru_w_a, lru_b_a,
              lru_w_x, lru_b_x, lru_lambda, w_out, norm2_g, router_group_w, router_group_b,
              router_expert_w, router_expert_b)
        wts_p = _mixer_weights(l, GMLP_CHUNK, 1, *lw)
        wts_s = _mixer_weights(l, dt, db, *lw)
        zero_conv = jnp.zeros((nb, CONV_WIDTH - 1, D_B), F32)
        zero_h = jnp.zeros((nb, 1, D_B), F32)
        y1, h2, route, cnt_p, cp, hp = _mixer_call(
            xp, zero_conv, zero_h, wts_p, n_total=n_total, row_offset=0, sb=1, tt=PROMPT_TILE, n_sub=PROMPT_SUB,
            chunk=GMLP_CHUNK, seg_len=PROMPT_SEG, seg_stride=PROMPT_SEG + SUBLANES, chain=True, emit_vn=False)
        y1, h2, route, cnt_s, cs, hs, vs = _mixer_call(
            xs, state_conv[l], state_rglru[l][None], wts_s, n_total=n_total, row_offset=n_prompt, sb=db,
            tt=dt, n_sub=1, chunk=db * dt, seg_len=dt, seg_stride=dt + SUBLANES, chain=False, emit_vn=True,
            alias_in=(y1, h2, route))
        conv_p.append(cp)
        h_p.append(hp[:, 0])
        v_s.append(vs)
        conv_s.append(cs)
        h_s.append(hs[0])

        sub = PROMPT_TILE // TOKEN_TILE
        cnt = jnp.concatenate([
            jnp.swapaxes(cnt_p[:, :, :sub], 1, 2).reshape(p_tiles, N_EXPERTS),
            cnt_s[:, :, 0]], axis=0).astype(jnp.int32)
        ops, ntot, tail, ntail, locoff, blk_e, nused = _routing_tables(cnt, n_blocks)

        xb = _dispatch_call(h2, route, locoff, upper, ops, ntot, tail, ntail, n_tiles=n_tiles, p_rows=p_rows)
        yb = _ffn_call(xb, expert_w_gate[l], expert_w_up[l], expert_w_down[l], blk_e, nused, n_blocks=n_blocks)
        assert l == depth - 1, "deeper stacks need an un-normalised combine between layers"
        fg = final_norm_g.reshape(1, D_MODEL)
        yp, ysm = _combine_call(y1, route, locoff, upper, fg, yb, ops, ntot, n_tiles=n_tiles, n_prompt=n_prompt,
                                n_sample=n_sample)
        xp = yp.reshape(nb, t_len, D_MODEL)
        xs = ysm.reshape(db, dt, D_MODEL)

    return (xp, xs, jnp.stack(conv_p), jnp.stack(h_p), jnp.stack(v_s), jnp.stack(conv_s), jnp.stack(h_s))
```

```python
import functools

import jax
import jax.numpy as jnp
from jax import lax
from jax.experimental import pallas as pl
from jax.experimental.pallas import tpu as pltpu

D_MODEL = 1024
D_A = 512
D_B = 512
A_GROUPS = 4
A_GROUP_DIM = 128
GMLP_CHUNK = 128
B_HEADS = 8
CONV_WIDTH = 4
LRU_C = 8.0
N_GROUPS = 4
EXPERTS_PER_GROUP = 8
N_EXPERTS = 32
D_EXPERT = 512
EPS = 1e-6

SUBLANES = 8
LANES = 128
ROUTER_ROWS = 40
TOKEN_TILE = 256
PROMPT_TILE = 1024
PROMPT_SUB = 1
MIX_LAG = 6
PROMPT_SEG = PROMPT_TILE // PROMPT_SUB // SUBLANES
FFN_BLOCK = 1024
FFN_CHUNK = 256
ZERO_CHUNKS = 8
ZERO_ROWS = ZERO_CHUNKS * SUBLANES
SLOT_ROWS = 2 * TOKEN_TILE + N_EXPERTS * SUBLANES
TILE_CHUNKS = SLOT_ROWS // SUBLANES
COPY_CLASSES = ((4, TILE_CHUNKS // 4), (3, N_EXPERTS), (2, N_EXPERTS), (1, N_EXPERTS))
TILES_PER_STEP = 3
STEP_TOKENS = TILES_PER_STEP * TOKEN_TILE
PACK_COLS = D_MODEL // 2
XB_COLS = PACK_COLS + LANES
U32 = jnp.uint32
HI_MASK = 0xFFFF0000
VMEM_LIMIT = 56 * 1024 * 1024

BF16 = jnp.bfloat16
F32 = jnp.float32


def _rms(x, g):
    return x * lax.rsqrt(jnp.mean(x * x, axis=-1, keepdims=True) + EPS) * g


def _dot(a, b):
    return jnp.dot(a, b, preferred_element_type=F32)


def _pack_bf16_pairs(x):
    bits = lax.bitcast_convert_type(x, U32)
    return (bits[:, PACK_COLS:] & U32(HI_MASK)) | (bits[:, :PACK_COLS] >> 16)


def _unpack_bf16_pairs(w):
    lo = lax.bitcast_convert_type(w << 16, F32).astype(BF16)
    hi = lax.bitcast_convert_type(w & U32(HI_MASK), F32).astype(BF16)
    return jnp.concatenate([lo, hi], axis=1)


def _mixer_kernel(x_ref, hist_ref, h0_ref, n1g_ref, win_ref, lng_ref, lnb_ref, wsm_ref, bsx_ref,
                  cw_ref, cb_ref, wa_ref, ba_ref, wx_ref, bx_ref, lam_ref, wout_ref, n2g_ref,
                  wr_ref, br_ref, *rest, sb, tt, n_sub, chunk, seg_len, seg_stride, chain, emit_vn, aliased):
    if aliased:
        rest = rest[3:]
    if emit_vn:
        (y1_ref, h2_ref, route_ref, cnt_ref, conv_ref, hlast_ref, vn_ref,
         xs_ref, sa_ref, sb_ref, mix_ref, hc_ref) = rest
    else:
        (y1_ref, h2_ref, route_ref, cnt_ref, conv_ref, hlast_ref,
         xs_ref, sa_ref, sb_ref, mix_ref, hc_ref) = rest
        vn_ref = None
    t_idx = pl.program_id(1)

    @pl.when(t_idx == 0)
    def _():
        xs_ref[:, SUBLANES - 3:SUBLANES, :] = hist_ref[...]
        hc_ref[...] = jnp.zeros_like(hc_ref)

    @pl.when(t_idx != 0)
    def _():
        xs_ref[:, SUBLANES - 3:SUBLANES, :] = xs_ref[:, SUBLANES + tt - 3:SUBLANES + tt, :]

    counts = []
    groups = [_mixer_rows(sub, counts, x_ref, h0_ref, n1g_ref, win_ref, lng_ref, lnb_ref, wsm_ref, bsx_ref,
                          cw_ref, cb_ref, wa_ref, ba_ref, wx_ref, bx_ref, lam_ref, wout_ref, n2g_ref, wr_ref,
                          br_ref, y1_ref, h2_ref, route_ref, hlast_ref, vn_ref, xs_ref, sa_ref, sb_ref, mix_ref,
                          hc_ref, sb=sb, tt=tt, n_sub=n_sub, chunk=chunk, seg_len=seg_len,
                          seg_stride=seg_stride, chain=chain)
              for sub in range(n_sub)]
    live = [True] * n_sub
    tick = 0
    while any(live):
        for sub in range(n_sub):
            if live[sub] and tick >= sub * MIX_LAG:
                live[sub] = next(groups[sub], "done") != "done"
        tick += 1
    conv_ref[...] = xs_ref[:, SUBLANES + tt - 3:SUBLANES + tt, :]
    cnt_ref[...] = sum(counts).reshape(1, N_EXPERTS, LANES)


def _mixer_rows(sub, counts, x_ref, h0_ref, n1g_ref, win_ref, lng_ref, lnb_ref, wsm_ref, bsx_ref, cw_ref, cb_ref,
                wa_ref, ba_ref, wx_ref, bx_ref, lam_ref, wout_ref, n2g_ref, wr_ref, br_ref,
                y1_ref, h2_ref, route_ref, hlast_ref, vn_ref, xs_ref, sa_ref, sb_ref, mix_ref, hc_ref,
                *, sb, tt, n_sub, chunk, seg_len, seg_stride, chain):
    tsub = tt // n_sub
    t0 = sub * tsub
    rows = sb * tsub
    r0 = sub * rows
    n_seg = rows // seg_len
    half = D_B // 2

    x = x_ref[:, t0:t0 + tsub, :].reshape(rows, D_MODEL)
    hb = _rms(x, n1g_ref[...]).astype(BF16)
    yield

    v_a = _dot(hb, win_ref[:, D_A:2 * D_A])
    x_b = _dot(hb, win_ref[:, 2 * D_A:2 * D_A + D_B])
    yield
    mixed = []
    for g in range(A_GROUPS):
        cols = slice(g * A_GROUP_DIM, (g + 1) * A_GROUP_DIM)
        vg = v_a[:, cols]
        mu = jnp.mean(vg, axis=-1, keepdims=True)
        dv = vg - mu
        var = jnp.mean(dv * dv, axis=-1, keepdims=True)
        vn = dv * lax.rsqrt(var + EPS) * lng_ref[:, cols] + lnb_ref[:, cols]
        if vn_ref is not None:
            vn_ref[:, t0:t0 + tsub, cols] = vn.reshape(sb, tsub, A_GROUP_DIM)
        vnb = vn.astype(BF16)
        mixed.append([_dot(wsm_ref[g], vnb[c * chunk:(c + 1) * chunk]) for c in range(rows // chunk)])
    yield
    u_a = _dot(hb, win_ref[:, 0:D_A])

    xs_ref[:, SUBLANES + t0:SUBLANES + t0 + tsub, :] = x_b.reshape(sb, tsub, D_B)
    xc = cb_ref[...] + x_b * cw_ref[3:4, :]
    for j in range(1, CONV_WIDTH):
        shifted = xs_ref[:, SUBLANES + t0 - j:SUBLANES + t0 - j + tsub, :].reshape(rows, D_B)
        xc = xc + shifted * cw_ref[3 - j:4 - j, :]
    yield

    xcb = xc.astype(BF16)
    r_lin = jnp.concatenate([_dot(xcb[:, :half], wa_ref[0]), _dot(xcb[:, half:], wa_ref[1])], axis=1)
    i_lin = jnp.concatenate([_dot(xcb[:, :half], wx_ref[0]), _dot(xcb[:, half:], wx_ref[1])], axis=1)

    for g in range(A_GROUPS):
        cols = slice(g * A_GROUP_DIM, (g + 1) * A_GROUP_DIM)
        for c in range(rows // chunk):
            rs = slice(c * chunk, (c + 1) * chunk)
            s = mixed[g][c] + bsx_ref[:, cols]
            mix_ref[r0 + c * chunk:r0 + (c + 1) * chunk, cols] = (u_a[rs, cols] * s).astype(BF16)

    g_b = _dot(hb, win_ref[:, 2 * D_A + D_B:])
    yield
    r = jax.nn.sigmoid(r_lin + ba_ref[...])
    i_g = jax.nn.sigmoid(i_lin + bx_ref[...])
    log_a = (-LRU_C * r) * jax.nn.softplus(-lam_ref[...])
    a = jnp.exp(log_a)
    gain = jnp.sqrt(1.0 - a * a)
    bterm = gain * i_g * xc

    n_lb = D_B // LANES
    sbase = sub * n_seg * seg_stride
    for j in range(n_seg):
        for k in range(n_lb):
            lc = slice(k * LANES, (k + 1) * LANES)
            dst = slice(sbase + j * seg_stride, sbase + j * seg_stride + seg_len)
            sa_ref[k, dst, :] = a[j * seg_len:(j + 1) * seg_len, lc]
            sb_ref[k, dst, :] = bterm[j * seg_len:(j + 1) * seg_len, lc]
    yield

    y_a = x + _dot(mix_ref[r0:r0 + rows, 0:D_A], wout_ref[0:D_A, :])
    gel = jax.nn.gelu(g_b)
    yield

    for grp in range(n_seg // SUBLANES):
        base = sbase + grp * SUBLANES * seg_stride
        bs = slice(grp * SUBLANES, (grp + 1) * SUBLANES)
        if chain:
            h_init = tuple(jnp.zeros((SUBLANES, LANES), F32) for _ in range(n_lb))
        else:
            h_init = tuple(h0_ref[0, bs, k * LANES:(k + 1) * LANES] for k in range(n_lb))
        a_init = tuple(jnp.ones((SUBLANES, LANES), F32) for _ in range(n_lb))

        def step(i, carry, base=base):
            hs, acs = carry
            idx = pl.ds(base + i, SUBLANES, stride=seg_stride)
            new_h, new_a = [], []
            for k in range(n_lb):
                av = sa_ref[k, idx, :]
                h = av * hs[k] + sb_ref[k, idx, :]
                sb_ref[k, idx, :] = h
                new_h.append(h)
                if chain:
                    ac = av * acs[k]
                    sa_ref[k, idx, :] = ac
                    new_a.append(ac)
                else:
                    new_a.append(acs[k])
            return tuple(new_h), tuple(new_a)

        carry = (h_init, a_init)
        for i in range(seg_len):
            carry = step(i, carry)
        h_end, a_end = carry
        yield

        for k in range(n_lb):
            lc = slice(k * LANES, (k + 1) * LANES)
            mc = slice(D_A + k * LANES, D_A + (k + 1) * LANES)
            if chain:
                h_in = hc_ref[:, lc]
                for j in range(SUBLANES):
                    seg = slice(base + j * seg_stride, base + j * seg_stride + seg_len)
                    rs = slice((grp * SUBLANES + j) * seg_len, (grp * SUBLANES + j + 1) * seg_len)
                    ms = slice(r0 + rs.start, r0 + rs.stop)
                    h_seg = sb_ref[k, seg, :] + sa_ref[k, seg, :] * h_in
                    mix_ref[ms, mc] = (h_seg * gel[rs, lc]).astype(BF16)
                    h_in = h_end[k][j:j + 1] + a_end[k][j:j + 1] * h_in
                hc_ref[:, lc] = h_in
                hlast_ref[0, :, lc] = h_in
            else:
                for j in range(SUBLANES):
                    seg = slice(base + j * seg_stride, base + j * seg_stride + seg_len)
                    rs = slice((grp * SUBLANES + j) * seg_len, (grp * SUBLANES + j + 1) * seg_len)
                    ms = slice(r0 + rs.start, r0 + rs.stop)
                    mix_ref[ms, mc] = (sb_ref[k, seg, :] * gel[rs, lc]).astype(BF16)
                hlast_ref[0, bs, lc] = h_end[k]
        yield

    y1 = y_a + _dot(mix_ref[r0:r0 + rows, D_A:], wout_ref[D_A:, :])
    y1_ref[r0:r0 + rows, :] = y1
    yield
    h2 = _rms(y1, n2g_ref[...]).astype(BF16)
    h2_ref[r0:r0 + rows, :] = h2
    yield

    logits = lax.dot_general(wr_ref[...], h2, (((1,), (1,)), ((), ())), preferred_element_type=F32) + br_ref[...]
    gl = [logits[N_EXPERTS + k:N_EXPERTS + k + 1, :] for k in range(N_GROUPS)]
    gmax = jnp.maximum(jnp.maximum(gl[0], gl[1]), jnp.maximum(gl[2], gl[3]))
    grp_idx = jnp.where(gl[0] == gmax, 0, jnp.where(gl[1] == gmax, 1, jnp.where(gl[2] == gmax, 2, 3)))
    gsum = (jnp.exp(gl[0] - gmax) + jnp.exp(gl[1] - gmax)) + (jnp.exp(gl[2] - gmax) + jnp.exp(gl[3] - gmax))
    p_grp = 1.0 / gsum
    e_in = jnp.where(grp_idx == 0, logits[0:8],
                     jnp.where(grp_idx == 1, logits[8:16], jnp.where(grp_idx == 2, logits[16:24], logits[24:32])))
    iota8 = lax.broadcasted_iota(jnp.int32, (EXPERTS_PER_GROUP, rows), 0)
    m1 = jnp.max(e_in, axis=0, keepdims=True)
    i1 = jnp.min(jnp.where(e_in == m1, iota8, EXPERTS_PER_GROUP), axis=0, keepdims=True)
    e_rest = jnp.where(iota8 == i1, -jnp.inf, e_in)
    m2 = jnp.max(e_rest, axis=0, keepdims=True)
    i2 = jnp.min(jnp.where(e_rest == m2, iota8, EXPERTS_PER_GROUP), axis=0, keepdims=True)
    t2 = jnp.exp(m2 - m1)
    den = 1.0 + t2
    gate0 = p_grp * (1.0 / den)
    gate1 = p_grp * (t2 / den)
    e0 = (grp_idx * EXPERTS_PER_GROUP + i1).astype(F32)
    e1 = (grp_idx * EXPERTS_PER_GROUP + i2).astype(F32)
    route_ref[:, r0:r0 + rows] = jnp.where(iota8 == 0, e0, jnp.where(iota8 == 1, e1, jnp.where(
        iota8 == 2, gate0, jnp.where(iota8 == 3, gate1, 0.0))))

    iota_e = lax.broadcasted_iota(jnp.int32, (N_EXPERTS, rows), 0).astype(F32)
    onehot = (iota_e == e0).astype(F32) + (iota_e == e1).astype(F32)
    lane = lax.broadcasted_iota(jnp.int32, (N_EXPERTS, LANES), 1)
    for s in range(rows // TOKEN_TILE):
        c_s = jnp.sum(onehot[:, s * TOKEN_TILE:(s + 1) * TOKEN_TILE], axis=1, keepdims=True)
        counts.append(jnp.where(lane == r0 // TOKEN_TILE + s, c_s, 0.0))


def _mixer_call(x, hist, h0, wts, *, n_total, row_offset, sb, tt, n_sub, chunk, seg_len, seg_stride, chain,
                emit_vn, alias_in=None):
    nb, t_len, _ = x.shape
    nbb = nb // sb
    ntt = t_len // tt
    rows = sb * tt
    blk0 = row_offset // rows
    n_seg = rows // seg_len

    def full(arr):
        nd = arr.ndim
        return pl.BlockSpec(arr.shape, lambda b, t, _nd=nd: (0,) * _nd)

    in_specs = [
        pl.BlockSpec((sb, tt, D_MODEL), lambda b, t: (b, t, 0)),
        pl.BlockSpec((sb, CONV_WIDTH - 1, D_B), lambda b, t: (b, 0, 0)),
        pl.BlockSpec((1, sb, D_B), lambda b, t: (b, 0, 0)),
    ] + [full(w) for w in wts]
    args = [x, hist, h0] + list(wts)
    io_alias = {}
    if alias_in is not None:
        for k, arr in enumerate(alias_in):
            io_alias[len(args)] = k
            in_specs.append(pl.BlockSpec(memory_space=pl.ANY))
            args.append(arr)

    out_shape = [
        jax.ShapeDtypeStruct((n_total, D_MODEL), F32),
        jax.ShapeDtypeStruct((n_total, D_MODEL), BF16),
        jax.ShapeDtypeStruct((SUBLANES, n_total), F32),
        jax.ShapeDtypeStruct((nbb * ntt, N_EXPERTS, LANES), F32),
        jax.ShapeDtypeStruct((nb, CONV_WIDTH - 1, D_B), F32),
        jax.ShapeDtypeStruct((nbb, sb, D_B), F32),
    ]
    out_specs = [
        pl.BlockSpec((rows, D_MODEL), lambda b, t: (blk0 + b * ntt + t, 0)),
        pl.BlockSpec((rows, D_MODEL), lambda b, t: (blk0 + b * ntt + t, 0)),
        pl.BlockSpec((SUBLANES, rows), lambda b, t: (0, blk0 + b * ntt + t)),
        pl.BlockSpec((1, N_EXPERTS, LANES), lambda b, t: (b * ntt + t, 0, 0)),
        pl.BlockSpec((sb, CONV_WIDTH - 1, D_B), lambda b, t: (b, 0, 0)),
        pl.BlockSpec((1, sb, D_B), lambda b, t: (b, 0, 0)),
    ]
    if emit_vn:
        out_shape.append(jax.ShapeDtypeStruct((nb, t_len, D_A), F32))
        out_specs.append(pl.BlockSpec((sb, tt, D_A), lambda b, t: (b, t, 0)))

    assert n_sub == 1 or sb == 1
    kern = functools.partial(_mixer_kernel, sb=sb, tt=tt, n_sub=n_sub, chunk=chunk, seg_len=seg_len,
                             seg_stride=seg_stride,
                             chain=chain, emit_vn=emit_vn, aliased=alias_in is not None)
    return pl.pallas_call(
        kern,
        grid=(nbb, ntt),
        in_specs=in_specs,
        out_specs=out_specs,
        out_shape=out_shape,
        scratch_shapes=[
            pltpu.VMEM((sb, SUBLANES + tt, D_B), F32),
            pltpu.VMEM((D_B // LANES, n_seg * seg_stride, LANES), F32),
            pltpu.VMEM((D_B // LANES, n_seg * seg_stride, LANES), F32),
            pltpu.VMEM((rows, D_MODEL), BF16),
            pltpu.VMEM((1, D_B), F32),
        ],
        input_output_aliases=io_alias,
        compiler_params=pltpu.CompilerParams(dimension_semantics=("arbitrary", "arbitrary"),
                                             vmem_limit_bytes=VMEM_LIMIT),
        name="mixer_chain" if chain else "mixer_step",
    )(*args)


def _tile_slots(route_ref, locoff_ref, upper_ref, u):
    lanes = slice(u * TOKEN_TILE, (u + 1) * TOKEN_TILE)
    e0 = route_ref[0:1, lanes]
    e1 = route_ref[1:2, lanes]
    iota_e = lax.broadcasted_iota(jnp.int32, (N_EXPERTS, TOKEN_TILE), 0).astype(F32)
    oh0 = (iota_e == e0).astype(F32)
    oh1 = (iota_e == e1).astype(F32)
    c0 = _dot(oh0.astype(BF16), upper_ref[...])
    c1 = _dot(oh1.astype(BF16), upper_ref[...])
    cnt0 = jnp.sum(oh0, axis=1, keepdims=True)
    base0 = locoff_ref[u]
    base1 = base0 + cnt0
    p0 = jnp.sum(oh0 * (base0 + c0), axis=0, keepdims=True)
    p1 = jnp.sum(oh1 * (base1 + c1), axis=0, keepdims=True)
    return p0.astype(jnp.int32), p1.astype(jnp.int32)


def _copy_ops(tile, n_tiles, nops_ref, oloc_ref, odst_ref, make_copy):
    base = 0
    for k, (chunks, cap) in enumerate(COPY_CLASSES):
        def per_op(s, _, chunks=chunks, cap=cap, base=base):
            idx = base + tile * cap + s
            make_copy(pl.multiple_of(oloc_ref[idx], SUBLANES), pl.multiple_of(odst_ref[idx], SUBLANES),
                      chunks * SUBLANES)
            return 0

        lax.fori_loop(0, nops_ref[k * n_tiles + tile], per_op, 0)
        base += n_tiles * cap


def _dispatch_kernel(nops_ref, oloc_ref, odst_ref, ntot_ref, tail_ref, ntail_ref,
                     h2_ref, route_ref, locoff_ref, upper_ref, xb_ref, buf_ref, zero_ref, sem, zsem, *, n_tiles):
    i = pl.program_id(0)
    n_steps = pl.num_programs(0)
    slot = i % 2

    def chunk_copy(s, u, loc, dst, rows=SUBLANES):
        return pltpu.make_async_copy(buf_ref.at[s, u, pl.ds(loc, rows), :],
                                     xb_ref.at[pl.ds(dst, rows), :], sem.at[s])

    def wait_step(step, s):
        for u in range(TILES_PER_STEP):
            n_rows = ntot_ref[step * TILES_PER_STEP + u] * SUBLANES

            @pl.when(n_rows > 0)
            def _():
                pltpu.make_async_copy(buf_ref.at[s, u, pl.ds(0, n_rows), :], xb_ref.at[pl.ds(0, n_rows), :],
                                      sem.at[s]).wait()

    @pl.when(i >= 2)
    def _():
        wait_step(i - 2, slot)

    iota_s = lax.broadcasted_iota(jnp.int32, (SLOT_ROWS, TOKEN_TILE), 0)
    for u in range(TILES_PER_STEP):
        lanes = slice(u * TOKEN_TILE, (u + 1) * TOKEN_TILE)
        p0, p1 = _tile_slots(route_ref, locoff_ref, upper_ref, u)
        hit0 = iota_s == p0
        hit1 = iota_s == p1
        perm = jnp.where(hit0 | hit1, 1.0, 0.0).astype(BF16)
        sorted_rows = _dot(perm, h2_ref[u * TOKEN_TILE:(u + 1) * TOKEN_TILE, :])
        buf_ref[slot, u, :, 0:PACK_COLS] = _pack_bf16_pairs(sorted_rows)
        gates = jnp.where(hit0, route_ref[2:3, lanes], 0.0) + jnp.where(hit1, route_ref[3:4, lanes], 0.0)
        gcol = jnp.sum(gates, axis=1, keepdims=True)
        buf_ref[slot, u, :, PACK_COLS:XB_COLS] = lax.bitcast_convert_type(
            jnp.broadcast_to(gcol, (SLOT_ROWS, LANES)), U32)

    for u in range(TILES_PER_STEP):
        _copy_ops(i * TILES_PER_STEP + u, n_tiles, nops_ref, oloc_ref, odst_ref,
                  lambda loc, dst, rows, u=u: chunk_copy(slot, u, loc, dst, rows).start())

    @pl.when(i == n_steps - 1)
    def _():
        zero_ref[...] = jnp.zeros_like(zero_ref)

        def zero_copy(dst, rows, k):
            return pltpu.make_async_copy(zero_ref.at[pl.ds(0, rows), :], xb_ref.at[pl.ds(dst, rows), :], zsem.at[k])

        def per_expert(e, tot):
            n_big, n_small = tot
            n = ntail_ref[e]
            d0 = tail_ref[e]
            nb = n // ZERO_CHUNKS
            ns = n - nb * ZERO_CHUNKS

            def big(c, _):
                zero_copy(pl.multiple_of(d0 + c * ZERO_ROWS, SUBLANES), ZERO_ROWS, 0).start()
                return 0

            def small(c, _):
                zero_copy(pl.multiple_of(d0 + nb * ZERO_ROWS + c * SUBLANES, SUBLANES), SUBLANES, 1).start()
                return 0

            lax.fori_loop(0, nb, big, 0)
            lax.fori_loop(0, ns, small, 0)
            return n_big + nb, n_small + ns

        n_big, n_small = lax.fori_loop(0, N_EXPERTS, per_expert, (0, 0))

        @pl.when(i >= 1)
        def _():
            wait_step(i - 1, 1 - slot)

        wait_step(i, slot)

        def wait_big(c, _):
            zero_copy(0, ZERO_ROWS, 0).wait()
            return 0

        def wait_small(c, _):
            zero_copy(0, SUBLANES, 1).wait()
            return 0

        lax.fori_loop(0, n_big, wait_big, 0)
        lax.fori_loop(0, n_small, wait_small, 0)


def _dispatch_call(h2, route, locoff, upper, ops, ntot, tail, ntail, *, n_tiles, p_rows):
    grid_spec = pltpu.PrefetchScalarGridSpec(
        num_scalar_prefetch=6,
        grid=(n_tiles // TILES_PER_STEP,),
        in_specs=[
            pl.BlockSpec((STEP_TOKENS, D_MODEL), lambda i, *_: (i, 0)),
            pl.BlockSpec((SUBLANES, STEP_TOKENS), lambda i, *_: (0, i)),
            pl.BlockSpec((TILES_PER_STEP, N_EXPERTS, 1), lambda i, *_: (i, 0, 0)),
            pl.BlockSpec((TOKEN_TILE, TOKEN_TILE), lambda i, *_: (0, 0)),
        ],
        out_specs=pl.BlockSpec(memory_space=pl.ANY),
        scratch_shapes=[
            pltpu.VMEM((2, TILES_PER_STEP, SLOT_ROWS, XB_COLS), U32),
            pltpu.VMEM((ZERO_ROWS, XB_COLS), U32),
            pltpu.SemaphoreType.DMA((2,)),
            pltpu.SemaphoreType.DMA((2,)),
        ],
    )
    return pl.pallas_call(
        functools.partial(_dispatch_kernel, n_tiles=n_tiles),
        grid_spec=grid_spec,
        out_shape=jax.ShapeDtypeStruct((p_rows, XB_COLS), U32),
        compiler_params=pltpu.CompilerParams(dimension_semantics=("arbitrary",), vmem_limit_bytes=VMEM_LIMIT),
        name="dispatch",
    )(*ops, ntot, tail, ntail, h2, route, locoff, upper)


def _ffn_kernel(blk_e_ref, nused_ref, valid_ref, xb_ref, wg_ref, wu_ref, wd_ref, yb_ref, wgb_ref, wub_ref, wdb_ref):
    i = pl.program_id(0)
    new_expert = (i == 0) | (blk_e_ref[i] != blk_e_ref[jnp.maximum(i - 1, 0)])

    @pl.when(new_expert)
    def _():
        wgb_ref[...] = wg_ref[0].astype(BF16)
        wub_ref[...] = wu_ref[0].astype(BF16)
        wdb_ref[...] = wd_ref[0].astype(BF16)

    def ffn_rows(rs):
        xb = _unpack_bf16_pairs(xb_ref[rs, 0:PACK_COLS])
        gate = lax.bitcast_convert_type(xb_ref[rs, PACK_COLS:PACK_COLS + 1], F32)
        a = _dot(xb, wgb_ref[...])
        u = _dot(xb, wub_ref[...])
        mid = (jax.nn.silu(a) * u).astype(BF16)
        y = _dot(mid, wdb_ref[...]) * gate
        yb_ref[rs, :] = _pack_bf16_pairs(y.astype(BF16).astype(F32))

    valid = valid_ref[i]

    @pl.when(valid == FFN_BLOCK)
    def _():
        ffn_rows(slice(None))

    @pl.when((valid > 0) & (valid < FFN_BLOCK))
    def _():
        def per_chunk(c, _):
            ffn_rows(pl.ds(pl.multiple_of(c * FFN_CHUNK, FFN_CHUNK), FFN_CHUNK))
            return 0

        lax.fori_loop(0, (valid + FFN_CHUNK - 1) // FFN_CHUNK, per_chunk, 0)


def _ffn_call(xb, wg, wu, wd, blk_e, nused, valid, *, n_blocks):
    def row_map(i, blk_e_ref, nused_ref, valid_ref):
        return (jnp.minimum(i, nused_ref[0] - 1), 0)

    def w_map(i, blk_e_ref, nused_ref, valid_ref):
        return (blk_e_ref[i], 0, 0)

    grid_spec = pltpu.PrefetchScalarGridSpec(
        num_scalar_prefetch=3,
        grid=(n_blocks,),
        in_specs=[
            pl.BlockSpec((FFN_BLOCK, XB_COLS), row_map),
            pl.BlockSpec((1, D_MODEL, D_EXPERT), w_map),
            pl.BlockSpec((1, D_MODEL, D_EXPERT), w_map),
            pl.BlockSpec((1, D_EXPERT, D_MODEL), w_map),
        ],
        out_specs=pl.BlockSpec((FFN_BLOCK, PACK_COLS), row_map),
        scratch_shapes=[
            pltpu.VMEM((D_MODEL, D_EXPERT), BF16),
            pltpu.VMEM((D_MODEL, D_EXPERT), BF16),
            pltpu.VMEM((D_EXPERT, D_MODEL), BF16),
        ],
    )
    return pl.pallas_call(
        _ffn_kernel,
        grid_spec=grid_spec,
        out_shape=jax.ShapeDtypeStruct((n_blocks * FFN_BLOCK, PACK_COLS), U32),
        compiler_params=pltpu.CompilerParams(dimension_semantics=("arbitrary",), vmem_limit_bytes=VMEM_LIMIT),
        name="expert_ffn",
    )(blk_e, nused, valid, xb, wg, wu, wd)


def _combine_kernel(nops_ref, oloc_ref, odst_ref, ntot_ref, y1_ref, route_ref, locoff_ref, upper_ref, fg_ref,
                    yb_ref, yp_ref, ys_ref, buf_ref, sem, *, n_tiles):
    i = pl.program_id(0)
    n_steps = pl.num_programs(0)
    slot = i % 2

    def chunk_copy(s, u, loc, src, rows):
        return pltpu.make_async_copy(yb_ref.at[pl.ds(src, rows), :],
                                     buf_ref.at[s, u, pl.ds(loc, rows), :], sem.at[s])

    def issue_step(step, s):
        for u in range(TILES_PER_STEP):
            _copy_ops(step * TILES_PER_STEP + u, n_tiles, nops_ref, oloc_ref, odst_ref,
                      lambda loc, src, rows, u=u: chunk_copy(s, u, loc, src, rows).start())

    @pl.when(i == 0)
    def _():
        buf_ref[...] = jnp.zeros_like(buf_ref)
        issue_step(0, 0)

    @pl.when(i + 1 < n_steps)
    def _():
        issue_step(i + 1, 1 - slot)

    iota_s = lax.broadcasted_iota(jnp.int32, (SLOT_ROWS, TOKEN_TILE), 0)
    perms = []
    for u in range(TILES_PER_STEP):
        p0, p1 = _tile_slots(route_ref, locoff_ref, upper_ref, u)
        perms.append(jnp.where((iota_s == p0) | (iota_s == p1), 1.0, 0.0).astype(BF16))

    for u in range(TILES_PER_STEP):
        n_rows = ntot_ref[i * TILES_PER_STEP + u] * SUBLANES

        @pl.when(n_rows > 0)
        def _():
            pltpu.make_async_copy(yb_ref.at[pl.ds(0, n_rows), :], buf_ref.at[slot, u, pl.ds(0, n_rows), :],
                                  sem.at[slot]).wait()

    for u in range(TILES_PER_STEP):
        rs = slice(u * TOKEN_TILE, (u + 1) * TOKEN_TILE)
        yb = _unpack_bf16_pairs(buf_ref[slot, u])
        moe = lax.dot_general(perms[u], yb, (((0,), (0,)), ((), ())), preferred_element_type=F32)
        y_tile = _rms(y1_ref[rs, :] + moe, fg_ref[...])
        yp_ref[rs, :] = y_tile

    @pl.when(i == n_steps - 1)
    def _():
        ys_ref[...] = y_tile


def _combine_call(y1, route, locoff, upper, fg, yb, ops, ntot, *, n_tiles, n_prompt, n_sample):
    n_steps = n_tiles // TILES_PER_STEP
    assert n_sample == TOKEN_TILE and n_prompt + n_sample == n_steps * STEP_TOKENS
    grid_spec = pltpu.PrefetchScalarGridSpec(
        num_scalar_prefetch=4,
        grid=(n_steps,),
        in_specs=[
            pl.BlockSpec((STEP_TOKENS, D_MODEL), lambda i, *_: (i, 0)),
            pl.BlockSpec((SUBLANES, STEP_TOKENS), lambda i, *_: (0, i)),
            pl.BlockSpec((TILES_PER_STEP, N_EXPERTS, 1), lambda i, *_: (i, 0, 0)),
            pl.BlockSpec((TOKEN_TILE, TOKEN_TILE), lambda i, *_: (0, 0)),
            pl.BlockSpec((1, D_MODEL), lambda i, *_: (0, 0)),
            pl.BlockSpec(memory_space=pl.ANY),
        ],
        out_specs=[
            pl.BlockSpec((STEP_TOKENS, D_MODEL), lambda i, *_: (i, 0)),
            pl.BlockSpec((n_sample, D_MODEL), lambda i, *_: (0, 0)),
        ],
        scratch_shapes=[
            pltpu.VMEM((2, TILES_PER_STEP, SLOT_ROWS, PACK_COLS), U32),
            pltpu.SemaphoreType.DMA((2,)),
        ],
    )
    return pl.pallas_call(
        functools.partial(_combine_kernel, n_tiles=n_tiles),
        grid_spec=grid_spec,
        out_shape=[jax.ShapeDtypeStruct((n_prompt, D_MODEL), F32), jax.ShapeDtypeStruct((n_sample, D_MODEL), F32)],
        compiler_params=pltpu.CompilerParams(dimension_semantics=("arbitrary",), vmem_limit_bytes=VMEM_LIMIT),
        name="combine",
    )(*ops, ntot, y1, route, locoff, upper, fg, yb)


def _block_diag(w):
    h, d, _ = w.shape
    eye = jnp.eye(h, dtype=w.dtype)
    return (eye[:, None, :, None] * w[:, :, None, :]).reshape(h * d, h * d)


def _head_blocks(w):
    half = B_HEADS // 2
    return jnp.stack([_block_diag(w[:half]), _block_diag(w[half:])]).astype(BF16)


def _mixer_weights(l, length, reps, norm1_g, w_in, gmlp_ln_g, gmlp_ln_b, gmlp_w_s, gmlp_b_s, conv_w, conv_b,
                   lru_w_a, lru_b_a, lru_w_x, lru_b_x, lru_lambda, w_out, norm2_g,
                   router_group_w, router_group_b, router_expert_w, router_expert_b):
    mask = jnp.tril(jnp.ones((length, length), dtype=bool))
    ws = jnp.where(mask, gmlp_w_s[l][:, :length, :length], 0.0)
    eye = jnp.eye(reps, dtype=ws.dtype)
    wsm = (eye[None, :, None, :, None] * ws[:, None, :, None, :]).reshape(A_GROUPS, reps * length, reps * length)
    bs = jnp.tile(gmlp_b_s[l][:, :length], (1, reps))
    bsx = jnp.repeat(bs.T, A_GROUP_DIM, axis=1)
    wr = jnp.concatenate([router_expert_w[l].T, router_group_w[l].T,
                          jnp.zeros((ROUTER_ROWS - N_EXPERTS - N_GROUPS, D_MODEL), F32)], axis=0)
    br = jnp.concatenate([router_expert_b[l], router_group_b[l],
                          jnp.zeros((ROUTER_ROWS - N_EXPERTS - N_GROUPS,), F32)]).reshape(ROUTER_ROWS, 1)
    return [
        norm1_g[l].reshape(1, D_MODEL), w_in[l].astype(BF16),
        gmlp_ln_g[l].reshape(1, D_A), gmlp_ln_b[l].reshape(1, D_A), wsm.astype(BF16), bsx,
        conv_w[l], conv_b[l].reshape(1, D_B),
        _head_blocks(lru_w_a[l]), lru_b_a[l].reshape(1, D_B),
        _head_blocks(lru_w_x[l]), lru_b_x[l].reshape(1, D_B),
        lru_lambda[l].reshape(1, D_B), w_out[l].astype(BF16), norm2_g[l].reshape(1, D_MODEL),
        wr.astype(BF16), br,
    ]


def _routing_tables(cnt, n_blocks):
    n_tiles = cnt.shape[0]
    seg = (cnt + SUBLANES - 1) // SUBLANES * SUBLANES
    nch = seg // SUBLANES
    tot = jnp.sum(seg, axis=0)
    padded = (tot + FFN_BLOCK - 1) // FFN_BLOCK * FFN_BLOCK
    e_before = jnp.arange(N_EXPERTS)[None, :] < jnp.arange(N_EXPERTS)[:, None]
    t_before = jnp.arange(n_tiles)[None, :] < jnp.arange(n_tiles)[:, None]
    pad_start = jnp.sum(jnp.where(e_before, padded[None, :], 0), axis=1)
    pad_end = pad_start + padded
    doff = pad_start[None, :] + jnp.sum(jnp.where(t_before[:, :, None], seg[None, :, :], 0), axis=1)
    locoff = jnp.sum(jnp.where(e_before[None, :, :], seg[:, None, :], 0), axis=2)
    ntot = jnp.sum(nch, axis=1)
    whole = nch // COPY_CLASSES[0][0]
    done = whole * COPY_CLASSES[0][0] * SUBLANES
    nops, olocs, odsts = [], [], []
    for chunks, cap in COPY_CLASSES:
        if chunks == COPY_CLASSES[0][0]:
            m, loc0, dst0 = whole, locoff, doff
        else:
            m, loc0, dst0 = (nch - whole * COPY_CLASSES[0][0] == chunks).astype(jnp.int32), locoff + done, doff + done
        start = jnp.sum(jnp.where(e_before[None, :, :], m[:, None, :], 0), axis=2)
        s = jnp.arange(cap)
        owner = jnp.sum((start + m)[:, None, :] <= s[None, :, None], axis=2)
        own = owner[:, :, None] == jnp.arange(N_EXPERTS)[None, None, :]
        step = chunks * SUBLANES * (s[None, :, None] - start[:, None, :])
        olocs.append(jnp.sum(jnp.where(own, loc0[:, None, :] + step, 0), axis=2).reshape(-1))
        odsts.append(jnp.sum(jnp.where(own, dst0[:, None, :] + step, 0), axis=2).reshape(-1))
        nops.append(jnp.sum(m, axis=1))
    tail = pad_start + tot
    ntail = ((tot + FFN_CHUNK - 1) // FFN_CHUNK * FFN_CHUNK - tot) // SUBLANES
    blk_end = pad_end // FFN_BLOCK
    nused = blk_end[-1]
    blk = jnp.arange(n_blocks, dtype=jnp.int32)
    blk_e = jnp.minimum(jnp.sum((blk_end[None, :] <= jnp.minimum(blk, nused - 1)[:, None]).astype(jnp.int32), axis=1),
                        N_EXPERTS - 1)
    mine = blk_e[:, None] == jnp.arange(N_EXPERTS)[None, :]
    rows_left = jnp.sum(jnp.where(mine, (pad_start + tot)[None, :], 0), axis=1) - blk * FFN_BLOCK
    valid = jnp.where(blk < nused, jnp.clip(rows_left, 0, FFN_BLOCK), 0)
    i32 = lambda a: a.astype(jnp.int32)
    ops = (i32(jnp.concatenate(nops)), i32(jnp.concatenate(olocs)), i32(jnp.concatenate(odsts)))
    return (ops, i32(ntot), i32(tail), i32(ntail),
            locoff.astype(F32)[:, :, None], i32(blk_e), i32(nused.reshape(1)), i32(valid))


def kernel(x_prompt, x_sample, state_conv, state_rglru, norm1_g, w_in, gmlp_ln_g, gmlp_ln_b, gmlp_w_s, gmlp_b_s,
           conv_w, conv_b, lru_w_a, lru_b_a, lru_w_x, lru_b_x, lru_lambda, w_out, norm2_g,
           router_group_w, router_group_b, router_expert_w, router_expert_b,
           expert_w_gate, expert_w_up, expert_w_down, final_norm_g):
    depth = w_in.shape[0]
    nb, t_len, _ = x_prompt.shape
    db, dt, _ = x_sample.shape
    n_prompt = nb * t_len
    n_sample = db * dt
    assert t_len % PROMPT_TILE == 0 and n_prompt % TOKEN_TILE == 0 and n_sample == TOKEN_TILE
    n_total = n_prompt + n_sample
    assert n_total % STEP_TOKENS == 0
    n_tiles = n_total // TOKEN_TILE
    p_tiles = n_prompt // TOKEN_TILE
    p_rows = 2 * n_total + (SUBLANES - 1) * N_EXPERTS * n_tiles + N_EXPERTS * (FFN_BLOCK - 1)
    n_blocks = -(-p_rows // FFN_BLOCK)
    p_rows = n_blocks * FFN_BLOCK
    upper = jnp.triu(jnp.ones((TOKEN_TILE, TOKEN_TILE), BF16), 1)

    xp, xs = x_prompt, x_sample
    conv_p, h_p, v_s, conv_s, h_s = [], [], [], [], []
    for l in range(depth):
        lw = (norm1_g, w_in, gmlp_ln_g, gmlp_ln_b, gmlp_w_s, gmlp_b_s, conv_w, conv_b, lru_w_a, lru_b_a,
              lru_w_x, lru_b_x, lru_lambda, w_out, norm2_g, router_group_w, router_group_b,
              router_expert_w, router_expert_b)
        wts_p = _mixer_weights(l, GMLP_CHUNK, 1, *lw)
        wts_s = _mixer_weights(l, dt, db, *lw)
        zero_conv = jnp.zeros((nb, CONV_WIDTH - 1, D_B), F32)
        zero_h = jnp.zeros((nb, 1, D_B), F32)
        y1, h2, route, cnt_p, cp, hp = _mixer_call(
            xp, zero_conv, zero_h, wts_p, n_total=n_total, row_offset=0, sb=1, tt=PROMPT_TILE, n_sub=PROMPT_SUB,
            chunk=GMLP_CHUNK, seg_len=PROMPT_SEG, seg_stride=PROMPT_SEG + SUBLANES, chain=True, emit_vn=False)
        y1, h2, route, cnt_s, cs, hs, vs = _mixer_call(
            xs, state_conv[l], state_rglru[l][None], wts_s, n_total=n_total, row_offset=n_prompt, sb=db,
            tt=dt, n_sub=1, chunk=db * dt, seg_len=dt, seg_stride=dt + SUBLANES, chain=False, emit_vn=True,
            alias_in=(y1, h2, route))
        conv_p.append(cp)
        h_p.append(hp[:, 0])
        v_s.append(vs)
        conv_s.append(cs)
        h_s.append(hs[0])

        sub = PROMPT_TILE // TOKEN_TILE
        cnt = jnp.concatenate([
            jnp.swapaxes(cnt_p[:, :, :sub], 1, 2).reshape(p_tiles, N_EXPERTS),
            cnt_s[:, :, 0]], axis=0).astype(jnp.int32)
        ops, ntot, tail, ntail, locoff, blk_e, nused, valid = _routing_tables(cnt, n_blocks)

        xb = _dispatch_call(h2, route, locoff, upper, ops, ntot, tail, ntail, n_tiles=n_tiles, p_rows=p_rows)
        yb = _ffn_call(xb, expert_w_gate[l], expert_w_up[l], expert_w_down[l], blk_e, nused, valid,
                       n_blocks=n_blocks)
        assert l == depth - 1, "deeper stacks need an un-normalised combine between layers"
        fg = final_norm_g.reshape(1, D_MODEL)
        yp, ysm = _combine_call(y1, route, locoff, upper, fg, yb, ops, ntot, n_tiles=n_tiles, n_prompt=n_prompt,
                                n_sample=n_sample)
        xp = yp.reshape(nb, t_len, D_MODEL)
        xs = ysm.reshape(db, dt, D_MODEL)

    return (xp, xs, jnp.stack(conv_p), jnp.stack(h_p), jnp.stack(v_s), jnp.stack(conv_s), jnp.stack(h_s))
```

```python
import functools

import jax
import jax.numpy as jnp
from jax import lax
from jax.experimental import pallas as pl
from jax.experimental.pallas import tpu as pltpu

D_MODEL = 1024
D_A = 512
D_B = 512
A_GROUPS = 4
A_GROUP_DIM = 128
GMLP_CHUNK = 128
B_HEADS = 8
CONV_WIDTH = 4
LRU_C = 8.0
N_GROUPS = 4
EXPERTS_PER_GROUP = 8
N_EXPERTS = 32
D_EXPERT = 512
EPS = 1e-6
TINY = 1e-30

SUBLANES = 8
LANES = 128
ROUTER_ROWS = 40
TOKEN_TILE = 256
PROMPT_TILE = 1024
PROMPT_SUB = 1
MIX_LAG = 6
PROMPT_SEG = PROMPT_TILE // PROMPT_SUB // SUBLANES
FFN_BLOCK = 1024
ZERO_CHUNKS = 8
ZERO_ROWS = ZERO_CHUNKS * SUBLANES
SLOT_ROWS = 2 * TOKEN_TILE + N_EXPERTS * SUBLANES
TILE_CHUNKS = SLOT_ROWS // SUBLANES
COPY_CLASSES = ((4, TILE_CHUNKS // 4), (3, N_EXPERTS), (2, N_EXPERTS), (1, N_EXPERTS))
TILES_PER_STEP = 3
STEP_TOKENS = TILES_PER_STEP * TOKEN_TILE
PACK_COLS = D_MODEL // 2
XB_COLS = PACK_COLS + LANES
U32 = jnp.uint32
HI_MASK = 0xFFFF0000
VMEM_LIMIT = 56 * 1024 * 1024

BF16 = jnp.bfloat16
F32 = jnp.float32


def _rms(x, g):
    return x * lax.rsqrt(jnp.mean(x * x, axis=-1, keepdims=True) + EPS) * g


def _dot(a, b):
    return jnp.dot(a, b, preferred_element_type=F32)


def _sigmoid(x):
    return 1.0 / (1.0 + jnp.exp(-x))


def _pack_bf16_pairs(x):
    bits = lax.bitcast_convert_type(x, U32)
    return (bits[:, PACK_COLS:] & U32(HI_MASK)) | (bits[:, :PACK_COLS] >> 16)


def _unpack_bf16_pairs(w):
    lo = lax.bitcast_convert_type(w << 16, F32).astype(BF16)
    hi = lax.bitcast_convert_type(w & U32(HI_MASK), F32).astype(BF16)
    return jnp.concatenate([lo, hi], axis=1)


def _mixer_kernel(x_ref, hist_ref, h0_ref, n1g_ref, win_ref, lng_ref, lnb_ref, wsm_ref, bsx_ref,
                  cw_ref, cb_ref, wa_ref, ba_ref, wx_ref, bx_ref, lam_ref, wout_ref, n2g_ref,
                  wr_ref, br_ref, *rest, sb, tt, n_sub, chunk, seg_len, seg_stride, chain, emit_vn, aliased):
    if aliased:
        rest = rest[3:]
    if emit_vn:
        (y1_ref, h2_ref, route_ref, cnt_ref, conv_ref, hlast_ref, vn_ref,
         xs_ref, sa_ref, sb_ref, mix_ref, hc_ref) = rest
    else:
        (y1_ref, h2_ref, route_ref, cnt_ref, conv_ref, hlast_ref,
         xs_ref, sa_ref, sb_ref, mix_ref, hc_ref) = rest
        vn_ref = None
    t_idx = pl.program_id(1)

    @pl.when(t_idx == 0)
    def _():
        xs_ref[:, SUBLANES - 3:SUBLANES, :] = hist_ref[...]
        hc_ref[...] = jnp.zeros_like(hc_ref)

    @pl.when(t_idx != 0)
    def _():
        xs_ref[:, SUBLANES - 3:SUBLANES, :] = xs_ref[:, SUBLANES + tt - 3:SUBLANES + tt, :]

    counts = []
    groups = [_mixer_rows(sub, counts, x_ref, h0_ref, n1g_ref, win_ref, lng_ref, lnb_ref, wsm_ref, bsx_ref,
                          cw_ref, cb_ref, wa_ref, ba_ref, wx_ref, bx_ref, lam_ref, wout_ref, n2g_ref, wr_ref,
                          br_ref, y1_ref, h2_ref, route_ref, hlast_ref, vn_ref, xs_ref, sa_ref, sb_ref, mix_ref,
                          hc_ref, sb=sb, tt=tt, n_sub=n_sub, chunk=chunk, seg_len=seg_len,
                          seg_stride=seg_stride, chain=chain)
              for sub in range(n_sub)]
    live = [True] * n_sub
    tick = 0
    while any(live):
        for sub in range(n_sub):
            if live[sub] and tick >= sub * MIX_LAG:
                live[sub] = next(groups[sub], "done") != "done"
        tick += 1
    conv_ref[...] = xs_ref[:, SUBLANES + tt - 3:SUBLANES + tt, :]
    cnt_ref[...] = sum(counts).reshape(1, N_EXPERTS, LANES)


def _mixer_rows(sub, counts, x_ref, h0_ref, n1g_ref, win_ref, lng_ref, lnb_ref, wsm_ref, bsx_ref, cw_ref, cb_ref,
                wa_ref, ba_ref, wx_ref, bx_ref, lam_ref, wout_ref, n2g_ref, wr_ref, br_ref,
                y1_ref, h2_ref, route_ref, hlast_ref, vn_ref, xs_ref, sa_ref, sb_ref, mix_ref, hc_ref,
                *, sb, tt, n_sub, chunk, seg_len, seg_stride, chain):
    tsub = tt // n_sub
    t0 = sub * tsub
    rows = sb * tsub
    r0 = sub * rows
    n_seg = rows // seg_len
    half = D_B // 2

    x = x_ref[:, t0:t0 + tsub, :].reshape(rows, D_MODEL)
    hb = _rms(x, n1g_ref[...]).astype(BF16)
    yield

    v_a = _dot(hb, win_ref[:, D_A:2 * D_A])
    x_b = _dot(hb, win_ref[:, 2 * D_A:2 * D_A + D_B])
    yield
    mixed = []
    for g in range(A_GROUPS):
        cols = slice(g * A_GROUP_DIM, (g + 1) * A_GROUP_DIM)
        vg = v_a[:, cols]
        mu = jnp.mean(vg, axis=-1, keepdims=True)
        dv = vg - mu
        var = jnp.mean(dv * dv, axis=-1, keepdims=True)
        vn = dv * lax.rsqrt(var + EPS) * lng_ref[:, cols] + lnb_ref[:, cols]
        if vn_ref is not None:
            vn_ref[:, t0:t0 + tsub, cols] = vn.reshape(sb, tsub, A_GROUP_DIM)
        vnb = vn.astype(BF16)
        mixed.append([_dot(wsm_ref[g], vnb[c * chunk:(c + 1) * chunk]) for c in range(rows // chunk)])
    yield
    u_a = _dot(hb, win_ref[:, 0:D_A])

    xs_ref[:, SUBLANES + t0:SUBLANES + t0 + tsub, :] = x_b.reshape(sb, tsub, D_B)
    xc = cb_ref[...] + x_b * cw_ref[3:4, :]
    for j in range(1, CONV_WIDTH):
        shifted = xs_ref[:, SUBLANES + t0 - j:SUBLANES + t0 - j + tsub, :].reshape(rows, D_B)
        xc = xc + shifted * cw_ref[3 - j:4 - j, :]
    yield

    xcb = xc.astype(BF16)
    r_lin = jnp.concatenate([_dot(xcb[:, :half], wa_ref[0]), _dot(xcb[:, half:], wa_ref[1])], axis=1)
    i_lin = jnp.concatenate([_dot(xcb[:, :half], wx_ref[0]), _dot(xcb[:, half:], wx_ref[1])], axis=1)

    for g in range(A_GROUPS):
        cols = slice(g * A_GROUP_DIM, (g + 1) * A_GROUP_DIM)
        for c in range(rows // chunk):
            rs = slice(c * chunk, (c + 1) * chunk)
            s = mixed[g][c] + bsx_ref[:, cols]
            mix_ref[r0 + c * chunk:r0 + (c + 1) * chunk, cols] = (u_a[rs, cols] * s).astype(BF16)

    g_b = _dot(hb, win_ref[:, 2 * D_A + D_B:])
    yield
    r = _sigmoid(r_lin + ba_ref[...])
    i_g = _sigmoid(i_lin + bx_ref[...])
    log_a = (-LRU_C * r) * jax.nn.softplus(-lam_ref[...])
    a = jnp.exp(log_a)
    y_gain = 1.0 - a * a
    gain = y_gain * lax.rsqrt(jnp.maximum(y_gain, TINY))
    bterm = gain * i_g * xc

    n_lb = D_B // LANES
    sbase = sub * n_seg * seg_stride
    for j in range(n_seg):
        for k in range(n_lb):
            lc = slice(k * LANES, (k + 1) * LANES)
            dst = slice(sbase + j * seg_stride, sbase + j * seg_stride + seg_len)
            sa_ref[k, dst, :] = a[j * seg_len:(j + 1) * seg_len, lc]
            sb_ref[k, dst, :] = bterm[j * seg_len:(j + 1) * seg_len, lc]
    yield

    y_a = x + _dot(mix_ref[r0:r0 + rows, 0:D_A], wout_ref[0:D_A, :])
    gel = jax.nn.gelu(g_b)
    yield

    for grp in range(n_seg // SUBLANES):
        base = sbase + grp * SUBLANES * seg_stride
        bs = slice(grp * SUBLANES, (grp + 1) * SUBLANES)
        if chain:
            h_init = tuple(jnp.zeros((SUBLANES, LANES), F32) for _ in range(n_lb))
        else:
            h_init = tuple(h0_ref[0, bs, k * LANES:(k + 1) * LANES] for k in range(n_lb))
        a_init = tuple(jnp.ones((SUBLANES, LANES), F32) for _ in range(n_lb))

        def step(i, carry, base=base):
            hs, acs = carry
            idx = pl.ds(base + i, SUBLANES, stride=seg_stride)
            new_h, new_a = [], []
            for k in range(n_lb):
                av = sa_ref[k, idx, :]
                h = av * hs[k] + sb_ref[k, idx, :]
                sb_ref[k, idx, :] = h
                new_h.append(h)
                if chain:
                    ac = av * acs[k]
                    sa_ref[k, idx, :] = ac
                    new_a.append(ac)
                else:
                    new_a.append(acs[k])
            return tuple(new_h), tuple(new_a)

        carry = (h_init, a_init)
        for i in range(seg_len):
            carry = step(i, carry)
        h_end, a_end = carry
        yield

        for k in range(n_lb):
            lc = slice(k * LANES, (k + 1) * LANES)
            mc = slice(D_A + k * LANES, D_A + (k + 1) * LANES)
            if chain:
                h_in = hc_ref[:, lc]
                for j in range(SUBLANES):
                    seg = slice(base + j * seg_stride, base + j * seg_stride + seg_len)
                    rs = slice((grp * SUBLANES + j) * seg_len, (grp * SUBLANES + j + 1) * seg_len)
                    ms = slice(r0 + rs.start, r0 + rs.stop)
                    h_seg = sb_ref[k, seg, :] + sa_ref[k, seg, :] * h_in
                    mix_ref[ms, mc] = (h_seg * gel[rs, lc]).astype(BF16)
                    h_in = h_end[k][j:j + 1] + a_end[k][j:j + 1] * h_in
                hc_ref[:, lc] = h_in
                hlast_ref[0, :, lc] = h_in
            else:
                for j in range(SUBLANES):
                    seg = slice(base + j * seg_stride, base + j * seg_stride + seg_len)
                    rs = slice((grp * SUBLANES + j) * seg_len, (grp * SUBLANES + j + 1) * seg_len)
                    ms = slice(r0 + rs.start, r0 + rs.stop)
                    mix_ref[ms, mc] = (sb_ref[k, seg, :] * gel[rs, lc]).astype(BF16)
                hlast_ref[0, bs, lc] = h_end[k]
        yield

    y1 = y_a + _dot(mix_ref[r0:r0 + rows, D_A:], wout_ref[D_A:, :])
    y1_ref[r0:r0 + rows, :] = y1
    yield
    h2 = _rms(y1, n2g_ref[...]).astype(BF16)
    h2_ref[r0:r0 + rows, :] = h2
    yield

    logits = lax.dot_general(wr_ref[...], h2, (((1,), (1,)), ((), ())), preferred_element_type=F32) + br_ref[...]
    gl = [logits[N_EXPERTS + k:N_EXPERTS + k + 1, :] for k in range(N_GROUPS)]
    gmax = jnp.maximum(jnp.maximum(gl[0], gl[1]), jnp.maximum(gl[2], gl[3]))
    grp_idx = jnp.where(gl[0] == gmax, 0, jnp.where(gl[1] == gmax, 1, jnp.where(gl[2] == gmax, 2, 3)))
    gsum = (jnp.exp(gl[0] - gmax) + jnp.exp(gl[1] - gmax)) + (jnp.exp(gl[2] - gmax) + jnp.exp(gl[3] - gmax))
    p_grp = 1.0 / gsum
    e_in = jnp.where(grp_idx == 0, logits[0:8],
                     jnp.where(grp_idx == 1, logits[8:16], jnp.where(grp_idx == 2, logits[16:24], logits[24:32])))
    iota8 = lax.broadcasted_iota(jnp.int32, (EXPERTS_PER_GROUP, rows), 0)
    m1 = jnp.max(e_in, axis=0, keepdims=True)
    i1 = jnp.min(jnp.where(e_in == m1, iota8, EXPERTS_PER_GROUP), axis=0, keepdims=True)
    e_rest = jnp.where(iota8 == i1, -jnp.inf, e_in)
    m2 = jnp.max(e_rest, axis=0, keepdims=True)
    i2 = jnp.min(jnp.where(e_rest == m2, iota8, EXPERTS_PER_GROUP), axis=0, keepdims=True)
    t2 = jnp.exp(m2 - m1)
    den = 1.0 + t2
    gate0 = p_grp * (1.0 / den)
    gate1 = p_grp * (t2 / den)
    e0 = (grp_idx * EXPERTS_PER_GROUP + i1).astype(F32)
    e1 = (grp_idx * EXPERTS_PER_GROUP + i2).astype(F32)
    route_ref[:, r0:r0 + rows] = jnp.where(iota8 == 0, e0, jnp.where(iota8 == 1, e1, jnp.where(
        iota8 == 2, gate0, jnp.where(iota8 == 3, gate1, 0.0))))

    iota_e = lax.broadcasted_iota(jnp.int32, (N_EXPERTS, rows), 0).astype(F32)
    onehot = (iota_e == e0).astype(F32) + (iota_e == e1).astype(F32)
    lane = lax.broadcasted_iota(jnp.int32, (N_EXPERTS, LANES), 1)
    for s in range(rows // TOKEN_TILE):
        c_s = jnp.sum(onehot[:, s * TOKEN_TILE:(s + 1) * TOKEN_TILE], axis=1, keepdims=True)
        counts.append(jnp.where(lane == r0 // TOKEN_TILE + s, c_s, 0.0))


def _mixer_call(x, hist, h0, wts, *, n_total, row_offset, sb, tt, n_sub, chunk, seg_len, seg_stride, chain,
                emit_vn, alias_in=None):
    nb, t_len, _ = x.shape
    nbb = nb // sb
    ntt = t_len // tt
    rows = sb * tt
    blk0 = row_offset // rows
    n_seg = rows // seg_len

    def full(arr):
        nd = arr.ndim
        return pl.BlockSpec(arr.shape, lambda b, t, _nd=nd: (0,) * _nd)

    in_specs = [
        pl.BlockSpec((sb, tt, D_MODEL), lambda b, t: (b, t, 0)),
        pl.BlockSpec((sb, CONV_WIDTH - 1, D_B), lambda b, t: (b, 0, 0)),
        pl.BlockSpec((1, sb, D_B), lambda b, t: (b, 0, 0)),
    ] + [full(w) for w in wts]
    args = [x, hist, h0] + list(wts)
    io_alias = {}
    if alias_in is not None:
        for k, arr in enumerate(alias_in):
            io_alias[len(args)] = k
            in_specs.append(pl.BlockSpec(memory_space=pl.ANY))
            args.append(arr)

    out_shape = [
        jax.ShapeDtypeStruct((n_total, D_MODEL), F32),
        jax.ShapeDtypeStruct((n_total, D_MODEL), BF16),
        jax.ShapeDtypeStruct((SUBLANES, n_total), F32),
        jax.ShapeDtypeStruct((nbb * ntt, N_EXPERTS, LANES), F32),
        jax.ShapeDtypeStruct((nb, CONV_WIDTH - 1, D_B), F32),
        jax.ShapeDtypeStruct((nbb, sb, D_B), F32),
    ]
    out_specs = [
        pl.BlockSpec((rows, D_MODEL), lambda b, t: (blk0 + b * ntt + t, 0)),
        pl.BlockSpec((rows, D_MODEL), lambda b, t: (blk0 + b * ntt + t, 0)),
        pl.BlockSpec((SUBLANES, rows), lambda b, t: (0, blk0 + b * ntt + t)),
        pl.BlockSpec((1, N_EXPERTS, LANES), lambda b, t: (b * ntt + t, 0, 0)),
        pl.BlockSpec((sb, CONV_WIDTH - 1, D_B), lambda b, t: (b, 0, 0)),
        pl.BlockSpec((1, sb, D_B), lambda b, t: (b, 0, 0)),
    ]
    if emit_vn:
        out_shape.append(jax.ShapeDtypeStruct((nb, t_len, D_A), F32))
        out_specs.append(pl.BlockSpec((sb, tt, D_A), lambda b, t: (b, t, 0)))

    assert n_sub == 1 or sb == 1
    kern = functools.partial(_mixer_kernel, sb=sb, tt=tt, n_sub=n_sub, chunk=chunk, seg_len=seg_len,
                             seg_stride=seg_stride,
                             chain=chain, emit_vn=emit_vn, aliased=alias_in is not None)
    return pl.pallas_call(
        kern,
        grid=(nbb, ntt),
        in_specs=in_specs,
        out_specs=out_specs,
        out_shape=out_shape,
        scratch_shapes=[
            pltpu.VMEM((sb, SUBLANES + tt, D_B), F32),
            pltpu.VMEM((D_B // LANES, n_seg * seg_stride, LANES), F32),
            pltpu.VMEM((D_B // LANES, n_seg * seg_stride, LANES), F32),
            pltpu.VMEM((rows, D_MODEL), BF16),
            pltpu.VMEM((1, D_B), F32),
        ],
        input_output_aliases=io_alias,
        compiler_params=pltpu.CompilerParams(dimension_semantics=("arbitrary", "arbitrary"),
                                             vmem_limit_bytes=VMEM_LIMIT),
        name="mixer_chain" if chain else "mixer_step",
    )(*args)


def _tile_slots(route_ref, locoff_ref, upper_ref, u):
    lanes = slice(u * TOKEN_TILE, (u + 1) * TOKEN_TILE)
    e0 = route_ref[0:1, lanes]
    e1 = route_ref[1:2, lanes]
    iota_e = lax.broadcasted_iota(jnp.int32, (N_EXPERTS, TOKEN_TILE), 0).astype(F32)
    oh0 = (iota_e == e0).astype(F32)
    oh1 = (iota_e == e1).astype(F32)
    c0 = _dot(oh0.astype(BF16), upper_ref[...])
    c1 = _dot(oh1.astype(BF16), upper_ref[...])
    cnt0 = jnp.sum(oh0, axis=1, keepdims=True)
    base0 = locoff_ref[u]
    base1 = base0 + cnt0
    p0 = jnp.sum(oh0 * (base0 + c0), axis=0, keepdims=True)
    p1 = jnp.sum(oh1 * (base1 + c1), axis=0, keepdims=True)
    return p0.astype(jnp.int32), p1.astype(jnp.int32)


def _copy_ops(tile, n_tiles, nops_ref, oloc_ref, odst_ref, make_copy):
    base = 0
    for k, (chunks, cap) in enumerate(COPY_CLASSES):
        def per_op(s, _, chunks=chunks, cap=cap, base=base):
            idx = base + tile * cap + s
            make_copy(pl.multiple_of(oloc_ref[idx], SUBLANES), pl.multiple_of(odst_ref[idx], SUBLANES),
                      chunks * SUBLANES)
            return 0

        lax.fori_loop(0, nops_ref[k * n_tiles + tile], per_op, 0)
        base += n_tiles * cap


def _dispatch_kernel(nops_ref, oloc_ref, odst_ref, ntot_ref, tail_ref, ntail_ref,
                     h2_ref, route_ref, locoff_ref, upper_ref, xb_ref, buf_ref, zero_ref, sem, zsem, *, n_tiles):
    i = pl.program_id(0)
    n_steps = pl.num_programs(0)
    slot = i % 2

    def chunk_copy(s, u, loc, dst, rows=SUBLANES):
        return pltpu.make_async_copy(buf_ref.at[s, u, pl.ds(loc, rows), :],
                                     xb_ref.at[pl.ds(dst, rows), :], sem.at[s])

    def wait_step(step, s):
        for u in range(TILES_PER_STEP):
            n_rows = ntot_ref[step * TILES_PER_STEP + u] * SUBLANES

            @pl.when(n_rows > 0)
            def _():
                pltpu.make_async_copy(buf_ref.at[s, u, pl.ds(0, n_rows), :], xb_ref.at[pl.ds(0, n_rows), :],
                                      sem.at[s]).wait()

    @pl.when(i >= 2)
    def _():
        wait_step(i - 2, slot)

    iota_s = lax.broadcasted_iota(jnp.int32, (SLOT_ROWS, TOKEN_TILE), 0)
    for u in range(TILES_PER_STEP):
        lanes = slice(u * TOKEN_TILE, (u + 1) * TOKEN_TILE)
        p0, p1 = _tile_slots(route_ref, locoff_ref, upper_ref, u)
        hit0 = iota_s == p0
        hit1 = iota_s == p1
        perm = jnp.where(hit0 | hit1, 1.0, 0.0).astype(BF16)
        sorted_rows = _dot(perm, h2_ref[u * TOKEN_TILE:(u + 1) * TOKEN_TILE, :])
        buf_ref[slot, u, :, 0:PACK_COLS] = _pack_bf16_pairs(sorted_rows)
        gates = jnp.where(hit0, route_ref[2:3, lanes], 0.0) + jnp.where(hit1, route_ref[3:4, lanes], 0.0)
        gcol = jnp.sum(gates, axis=1, keepdims=True)
        buf_ref[slot, u, :, PACK_COLS:XB_COLS] = lax.bitcast_convert_type(
            jnp.broadcast_to(gcol, (SLOT_ROWS, LANES)), U32)

    for u in range(TILES_PER_STEP):
        _copy_ops(i * TILES_PER_STEP + u, n_tiles, nops_ref, oloc_ref, odst_ref,
                  lambda loc, dst, rows, u=u: chunk_copy(slot, u, loc, dst, rows).start())

    @pl.when(i == n_steps - 1)
    def _():
        zero_ref[...] = jnp.zeros_like(zero_ref)

        def zero_copy(dst, rows, k):
            return pltpu.make_async_copy(zero_ref.at[pl.ds(0, rows), :], xb_ref.at[pl.ds(dst, rows), :], zsem.at[k])

        def per_expert(e, tot):
            n_big, n_small = tot
            n = ntail_ref[e]
            d0 = tail_ref[e]
            nb = n // ZERO_CHUNKS
            ns = n - nb * ZERO_CHUNKS

            def big(c, _):
                zero_copy(pl.multiple_of(d0 + c * ZERO_ROWS, SUBLANES), ZERO_ROWS, 0).start()
                return 0

            def small(c, _):
                zero_copy(pl.multiple_of(d0 + nb * ZERO_ROWS + c * SUBLANES, SUBLANES), SUBLANES, 1).start()
                return 0

            lax.fori_loop(0, nb, big, 0)
            lax.fori_loop(0, ns, small, 0)
            return n_big + nb, n_small + ns

        n_big, n_small = lax.fori_loop(0, N_EXPERTS, per_expert, (0, 0))

        @pl.when(i >= 1)
        def _():
            wait_step(i - 1, 1 - slot)

        wait_step(i, slot)

        def wait_big(c, _):
            zero_copy(0, ZERO_ROWS, 0).wait()
            return 0

        def wait_small(c, _):
            zero_copy(0, SUBLANES, 1).wait()
            return 0

        lax.fori_loop(0, n_big, wait_big, 0)
        lax.fori_loop(0, n_small, wait_small, 0)


def _dispatch_call(h2, route, locoff, upper, ops, ntot, tail, ntail, *, n_tiles, p_rows):
    grid_spec = pltpu.PrefetchScalarGridSpec(
        num_scalar_prefetch=6,
        grid=(n_tiles // TILES_PER_STEP,),
        in_specs=[
            pl.BlockSpec((STEP_TOKENS, D_MODEL), lambda i, *_: (i, 0)),
            pl.BlockSpec((SUBLANES, STEP_TOKENS), lambda i, *_: (0, i)),
            pl.BlockSpec((TILES_PER_STEP, N_EXPERTS, 1), lambda i, *_: (i, 0, 0)),
            pl.BlockSpec((TOKEN_TILE, TOKEN_TILE), lambda i, *_: (0, 0)),
        ],
        out_specs=pl.BlockSpec(memory_space=pl.ANY),
        scratch_shapes=[
            pltpu.VMEM((2, TILES_PER_STEP, SLOT_ROWS, XB_COLS), U32),
            pltpu.VMEM((ZERO_ROWS, XB_COLS), U32),
            pltpu.SemaphoreType.DMA((2,)),
            pltpu.SemaphoreType.DMA((2,)),
        ],
    )
    return pl.pallas_call(
        functools.partial(_dispatch_kernel, n_tiles=n_tiles),
        grid_spec=grid_spec,
        out_shape=jax.ShapeDtypeStruct((p_rows, XB_COLS), U32),
        compiler_params=pltpu.CompilerParams(dimension_semantics=("arbitrary",), vmem_limit_bytes=VMEM_LIMIT),
        name="dispatch",
    )(*ops, ntot, tail, ntail, h2, route, locoff, upper)


def _ffn_kernel(blk_e_ref, nused_ref, xb_ref, wg_ref, wu_ref, wd_ref, yb_ref, wgb_ref, wub_ref, wdb_ref):
    i = pl.program_id(0)
    new_expert = (i == 0) | (blk_e_ref[i] != blk_e_ref[jnp.maximum(i - 1, 0)])

    @pl.when(new_expert)
    def _():
        wgb_ref[...] = wg_ref[0].astype(BF16)
        wub_ref[...] = wu_ref[0].astype(BF16)
        wdb_ref[...] = wd_ref[0].astype(BF16)

    @pl.when(i < nused_ref[0])
    def _():
        xb = _unpack_bf16_pairs(xb_ref[:, 0:PACK_COLS])
        gate = lax.bitcast_convert_type(xb_ref[:, PACK_COLS:PACK_COLS + 1], F32)
        a = _dot(xb, wgb_ref[...])
        u = _dot(xb, wub_ref[...])
        mid = (a * _sigmoid(a) * u).astype(BF16)
        y = _dot(mid, wdb_ref[...]) * gate
        yb_ref[...] = _pack_bf16_pairs(y.astype(BF16).astype(F32))


def _ffn_call(xb, wg, wu, wd, blk_e, nused, *, n_blocks):
    def row_map(i, blk_e_ref, nused_ref):
        return (jnp.minimum(i, nused_ref[0] - 1), 0)

    def w_map(i, blk_e_ref, nused_ref):
        return (blk_e_ref[i], 0, 0)

    grid_spec = pltpu.PrefetchScalarGridSpec(
        num_scalar_prefetch=2,
        grid=(n_blocks,),
        in_specs=[
            pl.BlockSpec((FFN_BLOCK, XB_COLS), row_map),
            pl.BlockSpec((1, D_MODEL, D_EXPERT), w_map),
            pl.BlockSpec((1, D_MODEL, D_EXPERT), w_map),
            pl.BlockSpec((1, D_EXPERT, D_MODEL), w_map),
        ],
        out_specs=pl.BlockSpec((FFN_BLOCK, PACK_COLS), row_map),
        scratch_shapes=[
            pltpu.VMEM((D_MODEL, D_EXPERT), BF16),
            pltpu.VMEM((D_MODEL, D_EXPERT), BF16),
            pltpu.VMEM((D_EXPERT, D_MODEL), BF16),
        ],
    )
    return pl.pallas_call(
        _ffn_kernel,
        grid_spec=grid_spec,
        out_shape=jax.ShapeDtypeStruct((n_blocks * FFN_BLOCK, PACK_COLS), U32),
        compiler_params=pltpu.CompilerParams(dimension_semantics=("arbitrary",), vmem_limit_bytes=VMEM_LIMIT),
        name="expert_ffn",
    )(blk_e, nused, xb, wg, wu, wd)


def _combine_kernel(nops_ref, oloc_ref, odst_ref, ntot_ref, y1_ref, route_ref, locoff_ref, upper_ref, fg_ref,
                    yb_ref, yp_ref, ys_ref, buf_ref, sem, *, n_tiles):
    i = pl.program_id(0)
    n_steps = pl.num_programs(0)
    slot = i % 2

    def chunk_copy(s, u, loc, src, rows):
        return pltpu.make_async_copy(yb_ref.at[pl.ds(src, rows), :],
                                     buf_ref.at[s, u, pl.ds(loc, rows), :], sem.at[s])

    def issue_step(step, s):
        for u in range(TILES_PER_STEP):
            _copy_ops(step * TILES_PER_STEP + u, n_tiles, nops_ref, oloc_ref, odst_ref,
                      lambda loc, src, rows, u=u: chunk_copy(s, u, loc, src, rows).start())

    @pl.when(i == 0)
    def _():
        buf_ref[...] = jnp.zeros_like(buf_ref)
        issue_step(0, 0)

    @pl.when(i + 1 < n_steps)
    def _():
        issue_step(i + 1, 1 - slot)

    iota_s = lax.broadcasted_iota(jnp.int32, (SLOT_ROWS, TOKEN_TILE), 0)
    perms = []
    for u in range(TILES_PER_STEP):
        p0, p1 = _tile_slots(route_ref, locoff_ref, upper_ref, u)
        perms.append(jnp.where((iota_s == p0) | (iota_s == p1), 1.0, 0.0).astype(BF16))

    for u in range(TILES_PER_STEP):
        n_rows = ntot_ref[i * TILES_PER_STEP + u] * SUBLANES

        @pl.when(n_rows > 0)
        def _():
            pltpu.make_async_copy(yb_ref.at[pl.ds(0, n_rows), :], buf_ref.at[slot, u, pl.ds(0, n_rows), :],
                                  sem.at[slot]).wait()

    for u in range(TILES_PER_STEP):
        rs = slice(u * TOKEN_TILE, (u + 1) * TOKEN_TILE)
        yb = _unpack_bf16_pairs(buf_ref[slot, u])
        moe = lax.dot_general(perms[u], yb, (((0,), (0,)), ((), ())), preferred_element_type=F32)
        y_tile = _rms(y1_ref[rs, :] + moe, fg_ref[...])
        yp_ref[rs, :] = y_tile

    @pl.when(i == n_steps - 1)
    def _():
        ys_ref[...] = y_tile


def _combine_call(y1, route, locoff, upper, fg, yb, ops, ntot, *, n_tiles, n_prompt, n_sample):
    n_steps = n_tiles // TILES_PER_STEP
    assert n_sample == TOKEN_TILE and n_prompt + n_sample == n_steps * STEP_TOKENS
    grid_spec = pltpu.PrefetchScalarGridSpec(
        num_scalar_prefetch=4,
        grid=(n_steps,),
        in_specs=[
            pl.BlockSpec((STEP_TOKENS, D_MODEL), lambda i, *_: (i, 0)),
            pl.BlockSpec((SUBLANES, STEP_TOKENS), lambda i, *_: (0, i)),
            pl.BlockSpec((TILES_PER_STEP, N_EXPERTS, 1), lambda i, *_: (i, 0, 0)),
            pl.BlockSpec((TOKEN_TILE, TOKEN_TILE), lambda i, *_: (0, 0)),
            pl.BlockSpec((1, D_MODEL), lambda i, *_: (0, 0)),
            pl.BlockSpec(memory_space=pl.ANY),
        ],
        out_specs=[
            pl.BlockSpec((STEP_TOKENS, D_MODEL), lambda i, *_: (i, 0)),
            pl.BlockSpec((n_sample, D_MODEL), lambda i, *_: (0, 0)),
        ],
        scratch_shapes=[
            pltpu.VMEM((2, TILES_PER_STEP, SLOT_ROWS, PACK_COLS), U32),
            pltpu.SemaphoreType.DMA((2,)),
        ],
    )
    return pl.pallas_call(
        functools.partial(_combine_kernel, n_tiles=n_tiles),
        grid_spec=grid_spec,
        out_shape=[jax.ShapeDtypeStruct((n_prompt, D_MODEL), F32), jax.ShapeDtypeStruct((n_sample, D_MODEL), F32)],
        compiler_params=pltpu.CompilerParams(dimension_semantics=("arbitrary",), vmem_limit_bytes=VMEM_LIMIT),
        name="combine",
    )(*ops, ntot, y1, route, locoff, upper, fg, yb)


def _block_diag(w):
    h, d, _ = w.shape
    eye = jnp.eye(h, dtype=w.dtype)
    return (eye[:, None, :, None] * w[:, :, None, :]).reshape(h * d, h * d)


def _head_blocks(w):
    half = B_HEADS // 2
    return jnp.stack([_block_diag(w[:half]), _block_diag(w[half:])]).astype(BF16)


def _mixer_weights(l, length, reps, norm1_g, w_in, gmlp_ln_g, gmlp_ln_b, gmlp_w_s, gmlp_b_s, conv_w, conv_b,
                   lru_w_a, lru_b_a, lru_w_x, lru_b_x, lru_lambda, w_out, norm2_g,
                   router_group_w, router_group_b, router_expert_w, router_expert_b):
    mask = jnp.tril(jnp.ones((length, length), dtype=bool))
    ws = jnp.where(mask, gmlp_w_s[l][:, :length, :length], 0.0)
    eye = jnp.eye(reps, dtype=ws.dtype)
    wsm = (eye[None, :, None, :, None] * ws[:, None, :, None, :]).reshape(A_GROUPS, reps * length, reps * length)
    bs = jnp.tile(gmlp_b_s[l][:, :length], (1, reps))
    bsx = jnp.repeat(bs.T, A_GROUP_DIM, axis=1)
    wr = jnp.concatenate([router_expert_w[l].T, router_group_w[l].T,
                          jnp.zeros((ROUTER_ROWS - N_EXPERTS - N_GROUPS, D_MODEL), F32)], axis=0)
    br = jnp.concatenate([router_expert_b[l], router_group_b[l],
                          jnp.zeros((ROUTER_ROWS - N_EXPERTS - N_GROUPS,), F32)]).reshape(ROUTER_ROWS, 1)
    return [
        norm1_g[l].reshape(1, D_MODEL), w_in[l].astype(BF16),
        gmlp_ln_g[l].reshape(1, D_A), gmlp_ln_b[l].reshape(1, D_A), wsm.astype(BF16), bsx,
        conv_w[l], conv_b[l].reshape(1, D_B),
        _head_blocks(lru_w_a[l]), lru_b_a[l].reshape(1, D_B),
        _head_blocks(lru_w_x[l]), lru_b_x[l].reshape(1, D_B),
        lru_lambda[l].reshape(1, D_B), w_out[l].astype(BF16), norm2_g[l].reshape(1, D_MODEL),
        wr.astype(BF16), br,
    ]


def _routing_tables(cnt, n_blocks):
    n_tiles = cnt.shape[0]
    seg = (cnt + SUBLANES - 1) // SUBLANES * SUBLANES
    nch = seg // SUBLANES
    tot = jnp.sum(seg, axis=0)
    padded = (tot + FFN_BLOCK - 1) // FFN_BLOCK * FFN_BLOCK
    e_before = jnp.arange(N_EXPERTS)[None, :] < jnp.arange(N_EXPERTS)[:, None]
    t_before = jnp.arange(n_tiles)[None, :] < jnp.arange(n_tiles)[:, None]
    pad_start = jnp.sum(jnp.where(e_before, padded[None, :], 0), axis=1)
    pad_end = pad_start + padded
    doff = pad_start[None, :] + jnp.sum(jnp.where(t_before[:, :, None], seg[None, :, :], 0), axis=1)
    locoff = jnp.sum(jnp.where(e_before[None, :, :], seg[:, None, :], 0), axis=2)
    ntot = jnp.sum(nch, axis=1)
    whole = nch // COPY_CLASSES[0][0]
    done = whole * COPY_CLASSES[0][0] * SUBLANES
    nops, olocs, odsts = [], [], []
    for chunks, cap in COPY_CLASSES:
        if chunks == COPY_CLASSES[0][0]:
            m, loc0, dst0 = whole, locoff, doff
        else:
            m, loc0, dst0 = (nch - whole * COPY_CLASSES[0][0] == chunks).astype(jnp.int32), locoff + done, doff + done
        start = jnp.sum(jnp.where(e_before[None, :, :], m[:, None, :], 0), axis=2)
        s = jnp.arange(cap)
        owner = jnp.sum((start + m)[:, None, :] <= s[None, :, None], axis=2)
        own = owner[:, :, None] == jnp.arange(N_EXPERTS)[None, None, :]
        step = chunks * SUBLANES * (s[None, :, None] - start[:, None, :])
        olocs.append(jnp.sum(jnp.where(own, loc0[:, None, :] + step, 0), axis=2).reshape(-1))
        odsts.append(jnp.sum(jnp.where(own, dst0[:, None, :] + step, 0), axis=2).reshape(-1))
        nops.append(jnp.sum(m, axis=1))
    tail = pad_start + tot
    ntail = (padded - tot) // SUBLANES
    blk_end = pad_end // FFN_BLOCK
    nused = blk_end[-1]
    blk = jnp.minimum(jnp.arange(n_blocks, dtype=jnp.int32), nused - 1)
    blk_e = jnp.minimum(jnp.sum((blk_end[None, :] <= blk[:, None]).astype(jnp.int32), axis=1), N_EXPERTS - 1)
    i32 = lambda a: a.astype(jnp.int32)
    ops = (i32(jnp.concatenate(nops)), i32(jnp.concatenate(olocs)), i32(jnp.concatenate(odsts)))
    return (ops, i32(ntot), i32(tail), i32(ntail),
            locoff.astype(F32)[:, :, None], i32(blk_e), i32(nused.reshape(1)))


def kernel(x_prompt, x_sample, state_conv, state_rglru, norm1_g, w_in, gmlp_ln_g, gmlp_ln_b, gmlp_w_s, gmlp_b_s,
           conv_w, conv_b, lru_w_a, lru_b_a, lru_w_x, lru_b_x, lru_lambda, w_out, norm2_g,
           router_group_w, router_group_b, router_expert_w, router_expert_b,
           expert_w_gate, expert_w_up, expert_w_down, final_norm_g):
    depth = w_in.shape[0]
    nb, t_len, _ = x_prompt.shape
    db, dt, _ = x_sample.shape
    n_prompt = nb * t_len
    n_sample = db * dt
    assert t_len % PROMPT_TILE == 0 and n_prompt % TOKEN_TILE == 0 and n_sample == TOKEN_TILE
    n_total = n_prompt + n_sample
    assert n_total % STEP_TOKENS == 0
    n_tiles = n_total // TOKEN_TILE
    p_tiles = n_prompt // TOKEN_TILE
    p_rows = 2 * n_total + (SUBLANES - 1) * N_EXPERTS * n_tiles + N_EXPERTS * (FFN_BLOCK - 1)
    n_blocks = -(-p_rows // FFN_BLOCK)
    p_rows = n_blocks * FFN_BLOCK
    upper = jnp.triu(jnp.ones((TOKEN_TILE, TOKEN_TILE), BF16), 1)

    xp, xs = x_prompt, x_sample
    conv_p, h_p, v_s, conv_s, h_s = [], [], [], [], []
    for l in range(depth):
        lw = (norm1_g, w_in, gmlp_ln_g, gmlp_ln_b, gmlp_w_s, gmlp_b_s, conv_w, conv_b, lru_w_a, lru_b_a,
              lru_w_x, lru_b_x, lru_lambda, w_out, norm2_g, router_group_w, router_group_b,
              router_expert_w, router_expert_b)
        wts_p = _mixer_weights(l, GMLP_CHUNK, 1, *lw)
        wts_s = _mixer_weights(l, dt, db, *lw)
        zero_conv = jnp.zeros((nb, CONV_WIDTH - 1, D_B), F32)
        zero_h = jnp.zeros((nb, 1, D_B), F32)
        y1, h2, route, cnt_p, cp, hp = _mixer_call(
            xp, zero_conv, zero_h, wts_p, n_total=n_total, row_offset=0, sb=1, tt=PROMPT_TILE, n_sub=PROMPT_SUB,
            chunk=GMLP_CHUNK, seg_len=PROMPT_SEG, seg_stride=PROMPT_SEG + SUBLANES, chain=True, emit_vn=False)
        y1, h2, route, cnt_s, cs, hs, vs = _mixer_call(
            xs, state_conv[l], state_rglru[l][None], wts_s, n_total=n_total, row_offset=n_prompt, sb=db,
            tt=dt, n_sub=1, chunk=db * dt, seg_len=dt, seg_stride=dt + SUBLANES, chain=False, emit_vn=True,
            alias_in=(y1, h2, route))
        conv_p.append(cp)
        h_p.append(hp[:, 0])
        v_s.append(vs)
        conv_s.append(cs)
        h_s.append(hs[0])

        sub = PROMPT_TILE // TOKEN_TILE
        cnt = jnp.concatenate([
            jnp.swapaxes(cnt_p[:, :, :sub], 1, 2).reshape(p_tiles, N_EXPERTS),
            cnt_s[:, :, 0]], axis=0).astype(jnp.int32)
        ops, ntot, tail, ntail, locoff, blk_e, nused = _routing_tables(cnt, n_blocks)

        xb = _dispatch_call(h2, route, locoff, upper, ops, ntot, tail, ntail, n_tiles=n_tiles, p_rows=p_rows)
        yb = _ffn_call(xb, expert_w_gate[l], expert_w_up[l], expert_w_down[l], blk_e, nused, n_blocks=n_blocks)
        assert l == depth - 1, "deeper stacks need an un-normalised combine between layers"
        fg = final_norm_g.reshape(1, D_MODEL)
        yp, ysm = _combine_call(y1, route, locoff, upper, fg, yb, ops, ntot, n_tiles=n_tiles, n_prompt=n_prompt,
                                n_sample=n_sample)
        xp = yp.reshape(nb, t_len, D_MODEL)
        xs = ysm.reshape(db, dt, D_MODEL)

    return (xp, xs, jnp.stack(conv_p), jnp.stack(h_p), jnp.stack(v_s), jnp.stack(conv_s), jnp.stack(h_s))
```

```python
import functools

import jax
import jax.numpy as jnp
from jax import lax
from jax.experimental import pallas as pl
from jax.experimental.pallas import tpu as pltpu

D_MODEL = 1024
D_A = 512
D_B = 512
A_GROUPS = 4
A_GROUP_DIM = 128
GMLP_CHUNK = 128
B_HEADS = 8
CONV_WIDTH = 4
LRU_C = 8.0
N_GROUPS = 4
EXPERTS_PER_GROUP = 8
N_EXPERTS = 32
D_EXPERT = 512
EPS = 1e-6
TINY = 1e-30

SUBLANES = 8
LANES = 128
ROUTER_ROWS = 40
TOKEN_TILE = 256
PROMPT_TILE = 1024
PROMPT_SUB = 1
MIX_LAG = 6
PROMPT_SEG = PROMPT_TILE // PROMPT_SUB // SUBLANES
FFN_UNIT = 512
FFN_BLOCK = 2 * FFN_UNIT
ZERO_CHUNKS = 8
ZERO_ROWS = ZERO_CHUNKS * SUBLANES
SLOT_ROWS = 2 * TOKEN_TILE + N_EXPERTS * SUBLANES
TILE_CHUNKS = SLOT_ROWS // SUBLANES
COPY_CLASSES = ((4, TILE_CHUNKS // 4), (3, N_EXPERTS), (2, N_EXPERTS), (1, N_EXPERTS))
TILES_PER_STEP = 3
STEP_TOKENS = TILES_PER_STEP * TOKEN_TILE
PACK_COLS = D_MODEL // 2
XB_COLS = PACK_COLS + LANES
U32 = jnp.uint32
HI_MASK = 0xFFFF0000
VMEM_LIMIT = 56 * 1024 * 1024

BF16 = jnp.bfloat16
F32 = jnp.float32


def _rms(x, g):
    return x * lax.rsqrt(jnp.mean(x * x, axis=-1, keepdims=True) + EPS) * g


def _dot(a, b):
    return jnp.dot(a, b, preferred_element_type=F32)


def _sigmoid(x):
    return 1.0 / (1.0 + jnp.exp(-x))


def _pack_bf16_pairs(x):
    bits = lax.bitcast_convert_type(x, U32)
    return (bits[:, PACK_COLS:] & U32(HI_MASK)) | (bits[:, :PACK_COLS] >> 16)


def _unpack_bf16_pairs(w):
    lo = lax.bitcast_convert_type(w << 16, F32).astype(BF16)
    hi = lax.bitcast_convert_type(w & U32(HI_MASK), F32).astype(BF16)
    return jnp.concatenate([lo, hi], axis=1)


def _mixer_kernel(x_ref, hist_ref, h0_ref, n1g_ref, win_ref, lng_ref, lnb_ref, wsm_ref, bsx_ref,
                  cw_ref, cb_ref, wa_ref, ba_ref, wx_ref, bx_ref, lam_ref, wout_ref, n2g_ref,
                  wr_ref, br_ref, *rest, sb, tt, n_sub, chunk, seg_len, seg_stride, chain, emit_vn, aliased):
    if aliased:
        rest = rest[3:]
    if emit_vn:
        (y1_ref, h2_ref, route_ref, cnt_ref, conv_ref, hlast_ref, vn_ref,
         xs_ref, sa_ref, sb_ref, mix_ref, hc_ref) = rest
    else:
        (y1_ref, h2_ref, route_ref, cnt_ref, conv_ref, hlast_ref,
         xs_ref, sa_ref, sb_ref, mix_ref, hc_ref) = rest
        vn_ref = None
    t_idx = pl.program_id(1)

    @pl.when(t_idx == 0)
    def _():
        xs_ref[:, SUBLANES - 3:SUBLANES, :] = hist_ref[...]
        hc_ref[...] = jnp.zeros_like(hc_ref)

    @pl.when(t_idx != 0)
    def _():
        xs_ref[:, SUBLANES - 3:SUBLANES, :] = xs_ref[:, SUBLANES + tt - 3:SUBLANES + tt, :]

    counts = []
    groups = [_mixer_rows(sub, counts, x_ref, h0_ref, n1g_ref, win_ref, lng_ref, lnb_ref, wsm_ref, bsx_ref,
                          cw_ref, cb_ref, wa_ref, ba_ref, wx_ref, bx_ref, lam_ref, wout_ref, n2g_ref, wr_ref,
                          br_ref, y1_ref, h2_ref, route_ref, hlast_ref, vn_ref, xs_ref, sa_ref, sb_ref, mix_ref,
                          hc_ref, sb=sb, tt=tt, n_sub=n_sub, chunk=chunk, seg_len=seg_len,
                          seg_stride=seg_stride, chain=chain)
              for sub in range(n_sub)]
    live = [True] * n_sub
    tick = 0
    while any(live):
        for sub in range(n_sub):
            if live[sub] and tick >= sub * MIX_LAG:
                live[sub] = next(groups[sub], "done") != "done"
        tick += 1
    conv_ref[...] = xs_ref[:, SUBLANES + tt - 3:SUBLANES + tt, :]
    cnt_ref[...] = sum(counts).reshape(1, N_EXPERTS, LANES)


def _mixer_rows(sub, counts, x_ref, h0_ref, n1g_ref, win_ref, lng_ref, lnb_ref, wsm_ref, bsx_ref, cw_ref, cb_ref,
                wa_ref, ba_ref, wx_ref, bx_ref, lam_ref, wout_ref, n2g_ref, wr_ref, br_ref,
                y1_ref, h2_ref, route_ref, hlast_ref, vn_ref, xs_ref, sa_ref, sb_ref, mix_ref, hc_ref,
                *, sb, tt, n_sub, chunk, seg_len, seg_stride, chain):
    tsub = tt // n_sub
    t0 = sub * tsub
    rows = sb * tsub
    r0 = sub * rows
    n_seg = rows // seg_len
    half = D_B // 2

    x = x_ref[:, t0:t0 + tsub, :].reshape(rows, D_MODEL)
    hb = _rms(x, n1g_ref[...]).astype(BF16)
    yield

    v_a = _dot(hb, win_ref[:, D_A:2 * D_A])
    x_b = _dot(hb, win_ref[:, 2 * D_A:2 * D_A + D_B])
    yield
    mixed = []
    for g in range(A_GROUPS):
        cols = slice(g * A_GROUP_DIM, (g + 1) * A_GROUP_DIM)
        vg = v_a[:, cols]
        mu = jnp.mean(vg, axis=-1, keepdims=True)
        dv = vg - mu
        var = jnp.mean(dv * dv, axis=-1, keepdims=True)
        vn = dv * lax.rsqrt(var + EPS) * lng_ref[:, cols] + lnb_ref[:, cols]
        if vn_ref is not None:
            vn_ref[:, t0:t0 + tsub, cols] = vn.reshape(sb, tsub, A_GROUP_DIM)
        vnb = vn.astype(BF16)
        mixed.append([_dot(wsm_ref[g], vnb[c * chunk:(c + 1) * chunk]) for c in range(rows // chunk)])
    yield
    u_a = _dot(hb, win_ref[:, 0:D_A])

    xs_ref[:, SUBLANES + t0:SUBLANES + t0 + tsub, :] = x_b.reshape(sb, tsub, D_B)
    xc = cb_ref[...] + x_b * cw_ref[3:4, :]
    for j in range(1, CONV_WIDTH):
        shifted = xs_ref[:, SUBLANES + t0 - j:SUBLANES + t0 - j + tsub, :].reshape(rows, D_B)
        xc = xc + shifted * cw_ref[3 - j:4 - j, :]
    yield

    xcb = xc.astype(BF16)
    r_lin = jnp.concatenate([_dot(xcb[:, :half], wa_ref[0]), _dot(xcb[:, half:], wa_ref[1])], axis=1)
    i_lin = jnp.concatenate([_dot(xcb[:, :half], wx_ref[0]), _dot(xcb[:, half:], wx_ref[1])], axis=1)

    for g in range(A_GROUPS):
        cols = slice(g * A_GROUP_DIM, (g + 1) * A_GROUP_DIM)
        for c in range(rows // chunk):
            rs = slice(c * chunk, (c + 1) * chunk)
            s = mixed[g][c] + bsx_ref[:, cols]
            mix_ref[r0 + c * chunk:r0 + (c + 1) * chunk, cols] = (u_a[rs, cols] * s).astype(BF16)

    g_b = _dot(hb, win_ref[:, 2 * D_A + D_B:])
    yield
    r = _sigmoid(r_lin + ba_ref[...])
    i_g = _sigmoid(i_lin + bx_ref[...])
    log_a = (-LRU_C * r) * jax.nn.softplus(-lam_ref[...])
    a = jnp.exp(log_a)
    y_gain = 1.0 - a * a
    gain = y_gain * lax.rsqrt(jnp.maximum(y_gain, TINY))
    bterm = gain * i_g * xc

    n_lb = D_B // LANES
    sbase = sub * n_seg * seg_stride
    for j in range(n_seg):
        for k in range(n_lb):
            lc = slice(k * LANES, (k + 1) * LANES)
            dst = slice(sbase + j * seg_stride, sbase + j * seg_stride + seg_len)
            sa_ref[k, dst, :] = a[j * seg_len:(j + 1) * seg_len, lc]
            sb_ref[k, dst, :] = bterm[j * seg_len:(j + 1) * seg_len, lc]
    yield

    y_a = x + _dot(mix_ref[r0:r0 + rows, 0:D_A], wout_ref[0:D_A, :])
    gel = jax.nn.gelu(g_b)
    yield

    for grp in range(n_seg // SUBLANES):
        base = sbase + grp * SUBLANES * seg_stride
        bs = slice(grp * SUBLANES, (grp + 1) * SUBLANES)
        if chain:
            h_init = tuple(jnp.zeros((SUBLANES, LANES), F32) for _ in range(n_lb))
        else:
            h_init = tuple(h0_ref[0, bs, k * LANES:(k + 1) * LANES] for k in range(n_lb))
        a_init = tuple(jnp.ones((SUBLANES, LANES), F32) for _ in range(n_lb))

        def step(i, carry, base=base):
            hs, acs = carry
            idx = pl.ds(base + i, SUBLANES, stride=seg_stride)
            new_h, new_a = [], []
            for k in range(n_lb):
                av = sa_ref[k, idx, :]
                h = av * hs[k] + sb_ref[k, idx, :]
                sb_ref[k, idx, :] = h
                new_h.append(h)
                if chain:
                    ac = av * acs[k]
                    sa_ref[k, idx, :] = ac
                    new_a.append(ac)
                else:
                    new_a.append(acs[k])
            return tuple(new_h), tuple(new_a)

        carry = (h_init, a_init)
        for i in range(seg_len):
            carry = step(i, carry)
        h_end, a_end = carry
        yield

        for k in range(n_lb):
            lc = slice(k * LANES, (k + 1) * LANES)
            mc = slice(D_A + k * LANES, D_A + (k + 1) * LANES)
            if chain:
                h_in = hc_ref[:, lc]
                for j in range(SUBLANES):
                    seg = slice(base + j * seg_stride, base + j * seg_stride + seg_len)
                    rs = slice((grp * SUBLANES + j) * seg_len, (grp * SUBLANES + j + 1) * seg_len)
                    ms = slice(r0 + rs.start, r0 + rs.stop)
                    h_seg = sb_ref[k, seg, :] + sa_ref[k, seg, :] * h_in
                    mix_ref[ms, mc] = (h_seg * gel[rs, lc]).astype(BF16)
                    h_in = h_end[k][j:j + 1] + a_end[k][j:j + 1] * h_in
                hc_ref[:, lc] = h_in
                hlast_ref[0, :, lc] = h_in
            else:
                for j in range(SUBLANES):
                    seg = slice(base + j * seg_stride, base + j * seg_stride + seg_len)
                    rs = slice((grp * SUBLANES + j) * seg_len, (grp * SUBLANES + j + 1) * seg_len)
                    ms = slice(r0 + rs.start, r0 + rs.stop)
                    mix_ref[ms, mc] = (sb_ref[k, seg, :] * gel[rs, lc]).astype(BF16)
                hlast_ref[0, bs, lc] = h_end[k]
        yield

    y1 = y_a + _dot(mix_ref[r0:r0 + rows, D_A:], wout_ref[D_A:, :])
    y1_ref[r0:r0 + rows, :] = y1
    yield
    h2 = _rms(y1, n2g_ref[...]).astype(BF16)
    h2_ref[r0:r0 + rows, :] = h2
    yield

    logits = lax.dot_general(wr_ref[...], h2, (((1,), (1,)), ((), ())), preferred_element_type=F32) + br_ref[...]
    gl = [logits[N_EXPERTS + k:N_EXPERTS + k + 1, :] for k in range(N_GROUPS)]
    gmax = jnp.maximum(jnp.maximum(gl[0], gl[1]), jnp.maximum(gl[2], gl[3]))
    grp_idx = jnp.where(gl[0] == gmax, 0, jnp.where(gl[1] == gmax, 1, jnp.where(gl[2] == gmax, 2, 3)))
    gsum = (jnp.exp(gl[0] - gmax) + jnp.exp(gl[1] - gmax)) + (jnp.exp(gl[2] - gmax) + jnp.exp(gl[3] - gmax))
    p_grp = 1.0 / gsum
    e_in = jnp.where(grp_idx == 0, logits[0:8],
                     jnp.where(grp_idx == 1, logits[8:16], jnp.where(grp_idx == 2, logits[16:24], logits[24:32])))
    iota8 = lax.broadcasted_iota(jnp.int32, (EXPERTS_PER_GROUP, rows), 0)
    m1 = jnp.max(e_in, axis=0, keepdims=True)
    i1 = jnp.min(jnp.where(e_in == m1, iota8, EXPERTS_PER_GROUP), axis=0, keepdims=True)
    e_rest = jnp.where(iota8 == i1, -jnp.inf, e_in)
    m2 = jnp.max(e_rest, axis=0, keepdims=True)
    i2 = jnp.min(jnp.where(e_rest == m2, iota8, EXPERTS_PER_GROUP), axis=0, keepdims=True)
    t2 = jnp.exp(m2 - m1)
    den = 1.0 + t2
    gate0 = p_grp * (1.0 / den)
    gate1 = p_grp * (t2 / den)
    e0 = (grp_idx * EXPERTS_PER_GROUP + i1).astype(F32)
    e1 = (grp_idx * EXPERTS_PER_GROUP + i2).astype(F32)
    route_ref[:, r0:r0 + rows] = jnp.where(iota8 == 0, e0, jnp.where(iota8 == 1, e1, jnp.where(
        iota8 == 2, gate0, jnp.where(iota8 == 3, gate1, 0.0))))

    iota_e = lax.broadcasted_iota(jnp.int32, (N_EXPERTS, rows), 0).astype(F32)
    onehot = (iota_e == e0).astype(F32) + (iota_e == e1).astype(F32)
    lane = lax.broadcasted_iota(jnp.int32, (N_EXPERTS, LANES), 1)
    for s in range(rows // TOKEN_TILE):
        c_s = jnp.sum(onehot[:, s * TOKEN_TILE:(s + 1) * TOKEN_TILE], axis=1, keepdims=True)
        counts.append(jnp.where(lane == r0 // TOKEN_TILE + s, c_s, 0.0))


def _mixer_call(x, hist, h0, wts, *, n_total, row_offset, sb, tt, n_sub, chunk, seg_len, seg_stride, chain,
                emit_vn, alias_in=None):
    nb, t_len, _ = x.shape
    nbb = nb // sb
    ntt = t_len // tt
    rows = sb * tt
    blk0 = row_offset // rows
    n_seg = rows // seg_len

    def full(arr):
        nd = arr.ndim
        return pl.BlockSpec(arr.shape, lambda b, t, _nd=nd: (0,) * _nd)

    in_specs = [
        pl.BlockSpec((sb, tt, D_MODEL), lambda b, t: (b, t, 0)),
        pl.BlockSpec((sb, CONV_WIDTH - 1, D_B), lambda b, t: (b, 0, 0)),
        pl.BlockSpec((1, sb, D_B), lambda b, t: (b, 0, 0)),
    ] + [full(w) for w in wts]
    args = [x, hist, h0] + list(wts)
    io_alias = {}
    if alias_in is not None:
        for k, arr in enumerate(alias_in):
            io_alias[len(args)] = k
            in_specs.append(pl.BlockSpec(memory_space=pl.ANY))
            args.append(arr)

    out_shape = [
        jax.ShapeDtypeStruct((n_total, D_MODEL), F32),
        jax.ShapeDtypeStruct((n_total, D_MODEL), BF16),
        jax.ShapeDtypeStruct((SUBLANES, n_total), F32),
        jax.ShapeDtypeStruct((nbb * ntt, N_EXPERTS, LANES), F32),
        jax.ShapeDtypeStruct((nb, CONV_WIDTH - 1, D_B), F32),
        jax.ShapeDtypeStruct((nbb, sb, D_B), F32),
    ]
    out_specs = [
        pl.BlockSpec((rows, D_MODEL), lambda b, t: (blk0 + b * ntt + t, 0)),
        pl.BlockSpec((rows, D_MODEL), lambda b, t: (blk0 + b * ntt + t, 0)),
        pl.BlockSpec((SUBLANES, rows), lambda b, t: (0, blk0 + b * ntt + t)),
        pl.BlockSpec((1, N_EXPERTS, LANES), lambda b, t: (b * ntt + t, 0, 0)),
        pl.BlockSpec((sb, CONV_WIDTH - 1, D_B), lambda b, t: (b, 0, 0)),
        pl.BlockSpec((1, sb, D_B), lambda b, t: (b, 0, 0)),
    ]
    if emit_vn:
        out_shape.append(jax.ShapeDtypeStruct((nb, t_len, D_A), F32))
        out_specs.append(pl.BlockSpec((sb, tt, D_A), lambda b, t: (b, t, 0)))

    assert n_sub == 1 or sb == 1
    kern = functools.partial(_mixer_kernel, sb=sb, tt=tt, n_sub=n_sub, chunk=chunk, seg_len=seg_len,
                             seg_stride=seg_stride,
                             chain=chain, emit_vn=emit_vn, aliased=alias_in is not None)
    return pl.pallas_call(
        kern,
        grid=(nbb, ntt),
        in_specs=in_specs,
        out_specs=out_specs,
        out_shape=out_shape,
        scratch_shapes=[
            pltpu.VMEM((sb, SUBLANES + tt, D_B), F32),
            pltpu.VMEM((D_B // LANES, n_seg * seg_stride, LANES), F32),
            pltpu.VMEM((D_B // LANES, n_seg * seg_stride, LANES), F32),
            pltpu.VMEM((rows, D_MODEL), BF16),
            pltpu.VMEM((1, D_B), F32),
        ],
        input_output_aliases=io_alias,
        compiler_params=pltpu.CompilerParams(dimension_semantics=("arbitrary", "arbitrary"),
                                             vmem_limit_bytes=VMEM_LIMIT),
        name="mixer_chain" if chain else "mixer_step",
    )(*args)


def _tile_slots(route_ref, locoff_ref, upper_ref, u):
    lanes = slice(u * TOKEN_TILE, (u + 1) * TOKEN_TILE)
    e0 = route_ref[0:1, lanes]
    e1 = route_ref[1:2, lanes]
    iota_e = lax.broadcasted_iota(jnp.int32, (N_EXPERTS, TOKEN_TILE), 0).astype(F32)
    oh0 = (iota_e == e0).astype(F32)
    oh1 = (iota_e == e1).astype(F32)
    c0 = _dot(oh0.astype(BF16), upper_ref[...])
    c1 = _dot(oh1.astype(BF16), upper_ref[...])
    cnt0 = jnp.sum(oh0, axis=1, keepdims=True)
    base0 = locoff_ref[u]
    base1 = base0 + cnt0
    p0 = jnp.sum(oh0 * (base0 + c0), axis=0, keepdims=True)
    p1 = jnp.sum(oh1 * (base1 + c1), axis=0, keepdims=True)
    return p0.astype(jnp.int32), p1.astype(jnp.int32)


def _copy_ops(tile, n_tiles, nops_ref, oloc_ref, odst_ref, make_copy):
    base = 0
    for k, (chunks, cap) in enumerate(COPY_CLASSES):
        def per_op(s, _, chunks=chunks, cap=cap, base=base):
            idx = base + tile * cap + s
            make_copy(pl.multiple_of(oloc_ref[idx], SUBLANES), pl.multiple_of(odst_ref[idx], SUBLANES),
                      chunks * SUBLANES)
            return 0

        lax.fori_loop(0, nops_ref[k * n_tiles + tile], per_op, 0)
        base += n_tiles * cap


def _dispatch_kernel(nops_ref, oloc_ref, odst_ref, ntot_ref, tail_ref, ntail_ref,
                     h2_ref, route_ref, locoff_ref, upper_ref, xb_ref, buf_ref, zero_ref, sem, zsem, *, n_tiles):
    i = pl.program_id(0)
    n_steps = pl.num_programs(0)
    slot = i % 2

    def chunk_copy(s, u, loc, dst, rows=SUBLANES):
        return pltpu.make_async_copy(buf_ref.at[s, u, pl.ds(loc, rows), :],
                                     xb_ref.at[pl.ds(dst, rows), :], sem.at[s])

    def wait_step(step, s):
        for u in range(TILES_PER_STEP):
            n_rows = ntot_ref[step * TILES_PER_STEP + u] * SUBLANES

            @pl.when(n_rows > 0)
            def _():
                pltpu.make_async_copy(buf_ref.at[s, u, pl.ds(0, n_rows), :], xb_ref.at[pl.ds(0, n_rows), :],
                                      sem.at[s]).wait()

    @pl.when(i >= 2)
    def _():
        wait_step(i - 2, slot)

    iota_s = lax.broadcasted_iota(jnp.int32, (SLOT_ROWS, TOKEN_TILE), 0)
    for u in range(TILES_PER_STEP):
        lanes = slice(u * TOKEN_TILE, (u + 1) * TOKEN_TILE)
        p0, p1 = _tile_slots(route_ref, locoff_ref, upper_ref, u)
        hit0 = iota_s == p0
        hit1 = iota_s == p1
        perm = jnp.where(hit0 | hit1, 1.0, 0.0).astype(BF16)
        sorted_rows = _dot(perm, h2_ref[u * TOKEN_TILE:(u + 1) * TOKEN_TILE, :])
        buf_ref[slot, u, :, 0:PACK_COLS] = _pack_bf16_pairs(sorted_rows)
        gates = jnp.where(hit0, route_ref[2:3, lanes], 0.0) + jnp.where(hit1, route_ref[3:4, lanes], 0.0)
        gcol = jnp.sum(gates, axis=1, keepdims=True)
        buf_ref[slot, u, :, PACK_COLS:XB_COLS] = lax.bitcast_convert_type(
            jnp.broadcast_to(gcol, (SLOT_ROWS, LANES)), U32)

    for u in range(TILES_PER_STEP):
        _copy_ops(i * TILES_PER_STEP + u, n_tiles, nops_ref, oloc_ref, odst_ref,
                  lambda loc, dst, rows, u=u: chunk_copy(slot, u, loc, dst, rows).start())

    @pl.when(i == n_steps - 1)
    def _():
        zero_ref[...] = jnp.zeros_like(zero_ref)

        def zero_copy(dst, rows, k):
            return pltpu.make_async_copy(zero_ref.at[pl.ds(0, rows), :], xb_ref.at[pl.ds(dst, rows), :], zsem.at[k])

        def per_expert(e, tot):
            n_big, n_small = tot
            n = ntail_ref[e]
            d0 = tail_ref[e]
            nb = n // ZERO_CHUNKS
            ns = n - nb * ZERO_CHUNKS

            def big(c, _):
                zero_copy(pl.multiple_of(d0 + c * ZERO_ROWS, SUBLANES), ZERO_ROWS, 0).start()
                return 0

            def small(c, _):
                zero_copy(pl.multiple_of(d0 + nb * ZERO_ROWS + c * SUBLANES, SUBLANES), SUBLANES, 1).start()
                return 0

            lax.fori_loop(0, nb, big, 0)
            lax.fori_loop(0, ns, small, 0)
            return n_big + nb, n_small + ns

        n_big, n_small = lax.fori_loop(0, N_EXPERTS, per_expert, (0, 0))

        @pl.when(i >= 1)
        def _():
            wait_step(i - 1, 1 - slot)

        wait_step(i, slot)

        def wait_big(c, _):
            zero_copy(0, ZERO_ROWS, 0).wait()
            return 0

        def wait_small(c, _):
            zero_copy(0, SUBLANES, 1).wait()
            return 0

        lax.fori_loop(0, n_big, wait_big, 0)
        lax.fori_loop(0, n_small, wait_small, 0)


def _dispatch_call(h2, route, locoff, upper, ops, ntot, tail, ntail, *, n_tiles, p_rows):
    grid_spec = pltpu.PrefetchScalarGridSpec(
        num_scalar_prefetch=6,
        grid=(n_tiles // TILES_PER_STEP,),
        in_specs=[
            pl.BlockSpec((STEP_TOKENS, D_MODEL), lambda i, *_: (i, 0)),
            pl.BlockSpec((SUBLANES, STEP_TOKENS), lambda i, *_: (0, i)),
            pl.BlockSpec((TILES_PER_STEP, N_EXPERTS, 1), lambda i, *_: (i, 0, 0)),
            pl.BlockSpec((TOKEN_TILE, TOKEN_TILE), lambda i, *_: (0, 0)),
        ],
        out_specs=pl.BlockSpec(memory_space=pl.ANY),
        scratch_shapes=[
            pltpu.VMEM((2, TILES_PER_STEP, SLOT_ROWS, XB_COLS), U32),
            pltpu.VMEM((ZERO_ROWS, XB_COLS), U32),
            pltpu.SemaphoreType.DMA((2,)),
            pltpu.SemaphoreType.DMA((2,)),
        ],
    )
    return pl.pallas_call(
        functools.partial(_dispatch_kernel, n_tiles=n_tiles),
        grid_spec=grid_spec,
        out_shape=jax.ShapeDtypeStruct((p_rows, XB_COLS), U32),
        compiler_params=pltpu.CompilerParams(dimension_semantics=("arbitrary",), vmem_limit_bytes=VMEM_LIMIT),
        name="dispatch",
    )(*ops, ntot, tail, ntail, h2, route, locoff, upper)


def _ffn_kernel(nitems_ref, istart_ref, iunits_ref, iexp_ref, ifirst_ref, iwslot_ref, inext_ref,
                xb_ref, wg_ref, wu_ref, wd_ref, yb_ref,
                xin_ref, yout_ref, wgf_ref, wuf_ref, wdf_ref, wgb_ref, wub_ref, wdb_ref, sem_in, sem_out, sem_w):
    n_items = nitems_ref[0]
    sizes = (FFN_BLOCK, FFN_UNIT)

    def rows_in(t, s, rows):
        start = pl.multiple_of(istart_ref[t], FFN_UNIT)
        return pltpu.make_async_copy(xb_ref.at[pl.ds(start, rows), :], xin_ref.at[s, pl.ds(0, rows), :], sem_in.at[s])

    def rows_out(t, s, rows):
        start = pl.multiple_of(istart_ref[t], FFN_UNIT)
        return pltpu.make_async_copy(yout_ref.at[s, pl.ds(0, rows), :], yb_ref.at[pl.ds(start, rows), :],
                                     sem_out.at[s])

    def start_by_size(make, t, s):
        for rows in sizes:
            @pl.when(iunits_ref[t] * FFN_UNIT == rows)
            def _():
                make(t, s, rows).start()

    def weight_copies(e, ws):
        return (pltpu.make_async_copy(wg_ref.at[e], wgf_ref.at[ws], sem_w.at[ws]),
                pltpu.make_async_copy(wu_ref.at[e], wuf_ref.at[ws], sem_w.at[ws]),
                pltpu.make_async_copy(wd_ref.at[e], wdf_ref.at[ws], sem_w.at[ws]))

    def compute(s, rows):
        xb = _unpack_bf16_pairs(xin_ref[s, 0:rows, 0:PACK_COLS])
        gate = lax.bitcast_convert_type(xin_ref[s, 0:rows, PACK_COLS:PACK_COLS + 1], F32)
        a = _dot(xb, wgb_ref[...])
        u = _dot(xb, wub_ref[...])
        mid = (a * _sigmoid(a) * u).astype(BF16)
        y = _dot(mid, wdb_ref[...]) * gate
        yout_ref[s, 0:rows, :] = _pack_bf16_pairs(y.astype(BF16).astype(F32))

    @pl.when(n_items > 0)
    def _():
        for c in weight_copies(iexp_ref[0], 0):
            c.start()
        start_by_size(rows_in, 0, 0)

    def per_item(t, _):
        s = t % 2

        @pl.when(t + 1 < n_items)
        def _():
            start_by_size(rows_in, t + 1, 1 - s)

        @pl.when(ifirst_ref[t] == 1)
        def _():
            ws = iwslot_ref[t]
            for c in weight_copies(0, ws):
                c.wait()

            @pl.when(inext_ref[t] >= 0)
            def _():
                for c in weight_copies(inext_ref[t], 1 - ws):
                    c.start()

            wgb_ref[...] = wgf_ref[ws].astype(BF16)
            wub_ref[...] = wuf_ref[ws].astype(BF16)
            wdb_ref[...] = wdf_ref[ws].astype(BF16)

        rows_in(t, s, iunits_ref[t] * FFN_UNIT).wait()

        @pl.when(t >= 2)
        def _():
            rows_out(t - 2, s, iunits_ref[t - 2] * FFN_UNIT).wait()

        for rows in sizes:
            @pl.when(iunits_ref[t] * FFN_UNIT == rows)
            def _():
                compute(s, rows)

        start_by_size(rows_out, t, s)
        return 0

    lax.fori_loop(0, n_items, per_item, 0)

    for back in (2, 1):
        @pl.when(n_items >= back)
        def _():
            t = n_items - back
            rows_out(t, t % 2, iunits_ref[t] * FFN_UNIT).wait()


def _ffn_call(xb, wg, wu, wd, items, *, p_rows):
    any_spec = pl.BlockSpec(memory_space=pl.ANY)
    grid_spec = pltpu.PrefetchScalarGridSpec(
        num_scalar_prefetch=len(items),
        grid=(1,),
        in_specs=[any_spec, any_spec, any_spec, any_spec],
        out_specs=any_spec,
        scratch_shapes=[
            pltpu.VMEM((2, FFN_BLOCK, XB_COLS), U32),
            pltpu.VMEM((2, FFN_BLOCK, PACK_COLS), U32),
            pltpu.VMEM((2, D_MODEL, D_EXPERT), F32),
            pltpu.VMEM((2, D_MODEL, D_EXPERT), F32),
            pltpu.VMEM((2, D_EXPERT, D_MODEL), F32),
            pltpu.VMEM((D_MODEL, D_EXPERT), BF16),
            pltpu.VMEM((D_MODEL, D_EXPERT), BF16),
            pltpu.VMEM((D_EXPERT, D_MODEL), BF16),
            pltpu.SemaphoreType.DMA((2,)),
            pltpu.SemaphoreType.DMA((2,)),
            pltpu.SemaphoreType.DMA((2,)),
        ],
    )
    return pl.pallas_call(
        _ffn_kernel,
        grid_spec=grid_spec,
        out_shape=jax.ShapeDtypeStruct((p_rows, PACK_COLS), U32),
        compiler_params=pltpu.CompilerParams(dimension_semantics=("arbitrary",), vmem_limit_bytes=VMEM_LIMIT),
        name="expert_ffn",
    )(*items, xb, wg, wu, wd)


def _combine_kernel(nops_ref, oloc_ref, odst_ref, ntot_ref, y1_ref, route_ref, locoff_ref, upper_ref, fg_ref,
                    yb_ref, yp_ref, ys_ref, buf_ref, sem, *, n_tiles):
    i = pl.program_id(0)
    n_steps = pl.num_programs(0)
    slot = i % 2

    def chunk_copy(s, u, loc, src, rows):
        return pltpu.make_async_copy(yb_ref.at[pl.ds(src, rows), :],
                                     buf_ref.at[s, u, pl.ds(loc, rows), :], sem.at[s])

    def issue_step(step, s):
        for u in range(TILES_PER_STEP):
            _copy_ops(step * TILES_PER_STEP + u, n_tiles, nops_ref, oloc_ref, odst_ref,
                      lambda loc, src, rows, u=u: chunk_copy(s, u, loc, src, rows).start())

    @pl.when(i == 0)
    def _():
        buf_ref[...] = jnp.zeros_like(buf_ref)
        issue_step(0, 0)

    @pl.when(i + 1 < n_steps)
    def _():
        issue_step(i + 1, 1 - slot)

    iota_s = lax.broadcasted_iota(jnp.int32, (SLOT_ROWS, TOKEN_TILE), 0)
    perms = []
    for u in range(TILES_PER_STEP):
        p0, p1 = _tile_slots(route_ref, locoff_ref, upper_ref, u)
        perms.append(jnp.where((iota_s == p0) | (iota_s == p1), 1.0, 0.0).astype(BF16))

    for u in range(TILES_PER_STEP):
        n_rows = ntot_ref[i * TILES_PER_STEP + u] * SUBLANES

        @pl.when(n_rows > 0)
        def _():
            pltpu.make_async_copy(yb_ref.at[pl.ds(0, n_rows), :], buf_ref.at[slot, u, pl.ds(0, n_rows), :],
                                  sem.at[slot]).wait()

    for u in range(TILES_PER_STEP):
        rs = slice(u * TOKEN_TILE, (u + 1) * TOKEN_TILE)
        yb = _unpack_bf16_pairs(buf_ref[slot, u])
        moe = lax.dot_general(perms[u], yb, (((0,), (0,)), ((), ())), preferred_element_type=F32)
        y_tile = _rms(y1_ref[rs, :] + moe, fg_ref[...])
        yp_ref[rs, :] = y_tile

    @pl.when(i == n_steps - 1)
    def _():
        ys_ref[...] = y_tile


def _combine_call(y1, route, locoff, upper, fg, yb, ops, ntot, *, n_tiles, n_prompt, n_sample):
    n_steps = n_tiles // TILES_PER_STEP
    assert n_sample == TOKEN_TILE and n_prompt + n_sample == n_steps * STEP_TOKENS
    grid_spec = pltpu.PrefetchScalarGridSpec(
        num_scalar_prefetch=4,
        grid=(n_steps,),
        in_specs=[
            pl.BlockSpec((STEP_TOKENS, D_MODEL), lambda i, *_: (i, 0)),
            pl.BlockSpec((SUBLANES, STEP_TOKENS), lambda i, *_: (0, i)),
            pl.BlockSpec((TILES_PER_STEP, N_EXPERTS, 1), lambda i, *_: (i, 0, 0)),
            pl.BlockSpec((TOKEN_TILE, TOKEN_TILE), lambda i, *_: (0, 0)),
            pl.BlockSpec((1, D_MODEL), lambda i, *_: (0, 0)),
            pl.BlockSpec(memory_space=pl.ANY),
        ],
        out_specs=[
            pl.BlockSpec((STEP_TOKENS, D_MODEL), lambda i, *_: (i, 0)),
            pl.BlockSpec((n_sample, D_MODEL), lambda i, *_: (0, 0)),
        ],
        scratch_shapes=[
            pltpu.VMEM((2, TILES_PER_STEP, SLOT_ROWS, PACK_COLS), U32),
            pltpu.SemaphoreType.DMA((2,)),
        ],
    )
    return pl.pallas_call(
        functools.partial(_combine_kernel, n_tiles=n_tiles),
        grid_spec=grid_spec,
        out_shape=[jax.ShapeDtypeStruct((n_prompt, D_MODEL), F32), jax.ShapeDtypeStruct((n_sample, D_MODEL), F32)],
        compiler_params=pltpu.CompilerParams(dimension_semantics=("arbitrary",), vmem_limit_bytes=VMEM_LIMIT),
        name="combine",
    )(*ops, ntot, y1, route, locoff, upper, fg, yb)


def _block_diag(w):
    h, d, _ = w.shape
    eye = jnp.eye(h, dtype=w.dtype)
    return (eye[:, None, :, None] * w[:, :, None, :]).reshape(h * d, h * d)


def _head_blocks(w):
    half = B_HEADS // 2
    return jnp.stack([_block_diag(w[:half]), _block_diag(w[half:])]).astype(BF16)


def _mixer_weights(l, length, reps, norm1_g, w_in, gmlp_ln_g, gmlp_ln_b, gmlp_w_s, gmlp_b_s, conv_w, conv_b,
                   lru_w_a, lru_b_a, lru_w_x, lru_b_x, lru_lambda, w_out, norm2_g,
                   router_group_w, router_group_b, router_expert_w, router_expert_b):
    mask = jnp.tril(jnp.ones((length, length), dtype=bool))
    ws = jnp.where(mask, gmlp_w_s[l][:, :length, :length], 0.0)
    eye = jnp.eye(reps, dtype=ws.dtype)
    wsm = (eye[None, :, None, :, None] * ws[:, None, :, None, :]).reshape(A_GROUPS, reps * length, reps * length)
    bs = jnp.tile(gmlp_b_s[l][:, :length], (1, reps))
    bsx = jnp.repeat(bs.T, A_GROUP_DIM, axis=1)
    wr = jnp.concatenate([router_expert_w[l].T, router_group_w[l].T,
                          jnp.zeros((ROUTER_ROWS - N_EXPERTS - N_GROUPS, D_MODEL), F32)], axis=0)
    br = jnp.concatenate([router_expert_b[l], router_group_b[l],
                          jnp.zeros((ROUTER_ROWS - N_EXPERTS - N_GROUPS,), F32)]).reshape(ROUTER_ROWS, 1)
    return [
        norm1_g[l].reshape(1, D_MODEL), w_in[l].astype(BF16),
        gmlp_ln_g[l].reshape(1, D_A), gmlp_ln_b[l].reshape(1, D_A), wsm.astype(BF16), bsx,
        conv_w[l], conv_b[l].reshape(1, D_B),
        _head_blocks(lru_w_a[l]), lru_b_a[l].reshape(1, D_B),
        _head_blocks(lru_w_x[l]), lru_b_x[l].reshape(1, D_B),
        lru_lambda[l].reshape(1, D_B), w_out[l].astype(BF16), norm2_g[l].reshape(1, D_MODEL),
        wr.astype(BF16), br,
    ]


def _routing_tables(cnt, n_items_max):
    n_tiles = cnt.shape[0]
    seg = (cnt + SUBLANES - 1) // SUBLANES * SUBLANES
    nch = seg // SUBLANES
    tot = jnp.sum(seg, axis=0)
    padded = (tot + FFN_UNIT - 1) // FFN_UNIT * FFN_UNIT
    e_before = jnp.arange(N_EXPERTS)[None, :] < jnp.arange(N_EXPERTS)[:, None]
    t_before = jnp.arange(n_tiles)[None, :] < jnp.arange(n_tiles)[:, None]
    pad_start = jnp.sum(jnp.where(e_before, padded[None, :], 0), axis=1)
    pad_end = pad_start + padded
    doff = pad_start[None, :] + jnp.sum(jnp.where(t_before[:, :, None], seg[None, :, :], 0), axis=1)
    locoff = jnp.sum(jnp.where(e_before[None, :, :], seg[:, None, :], 0), axis=2)
    ntot = jnp.sum(nch, axis=1)
    whole = nch // COPY_CLASSES[0][0]
    done = whole * COPY_CLASSES[0][0] * SUBLANES
    nops, olocs, odsts = [], [], []
    for chunks, cap in COPY_CLASSES:
        if chunks == COPY_CLASSES[0][0]:
            m, loc0, dst0 = whole, locoff, doff
        else:
            m, loc0, dst0 = (nch - whole * COPY_CLASSES[0][0] == chunks).astype(jnp.int32), locoff + done, doff + done
        start = jnp.sum(jnp.where(e_before[None, :, :], m[:, None, :], 0), axis=2)
        s = jnp.arange(cap)
        owner = jnp.sum((start + m)[:, None, :] <= s[None, :, None], axis=2)
        own = owner[:, :, None] == jnp.arange(N_EXPERTS)[None, None, :]
        step = chunks * SUBLANES * (s[None, :, None] - start[:, None, :])
        olocs.append(jnp.sum(jnp.where(own, loc0[:, None, :] + step, 0), axis=2).reshape(-1))
        odsts.append(jnp.sum(jnp.where(own, dst0[:, None, :] + step, 0), axis=2).reshape(-1))
        nops.append(jnp.sum(m, axis=1))
    tail = pad_start + tot
    ntail = (padded - tot) // SUBLANES
    units = padded // FFN_UNIT
    n_it = (units + 1) // 2
    it_start = jnp.sum(jnp.where(e_before, n_it[None, :], 0), axis=1)
    t = jnp.arange(n_items_max)
    e_ids = jnp.arange(N_EXPERTS)
    owner = jnp.minimum(jnp.sum((it_start + n_it)[None, :] <= t[:, None], axis=1), N_EXPERTS - 1)
    own = owner[:, None] == e_ids[None, :]
    pick = lambda v: jnp.sum(jnp.where(own, v[None, :], 0), axis=1)
    j = t - pick(it_start)
    istart = pick(pad_start) + j * FFN_BLOCK
    iunits = jnp.clip(pick(units) - 2 * j, 1, 2)
    ordinal = jnp.sum(jnp.where(e_before, (n_it > 0).astype(jnp.int32)[None, :], 0), axis=1)
    later = (e_ids[None, :] > e_ids[:, None]) & (n_it > 0)[None, :]
    nxt = jnp.min(jnp.where(later, e_ids[None, :], N_EXPERTS), axis=1)
    nxt = jnp.where(nxt == N_EXPERTS, -1, nxt)
    i32 = lambda a: a.astype(jnp.int32)
    items = (i32(jnp.sum(n_it).reshape(1)), i32(istart), i32(iunits), i32(owner), i32(j == 0),
             i32(pick(ordinal) % 2), i32(pick(nxt)))
    ops = (i32(jnp.concatenate(nops)), i32(jnp.concatenate(olocs)), i32(jnp.concatenate(odsts)))
    return ops, i32(ntot), i32(tail), i32(ntail), locoff.astype(F32)[:, :, None], items


def kernel(x_prompt, x_sample, state_conv, state_rglru, norm1_g, w_in, gmlp_ln_g, gmlp_ln_b, gmlp_w_s, gmlp_b_s,
           conv_w, conv_b, lru_w_a, lru_b_a, lru_w_x, lru_b_x, lru_lambda, w_out, norm2_g,
           router_group_w, router_group_b, router_expert_w, router_expert_b,
           expert_w_gate, expert_w_up, expert_w_down, final_norm_g):
    depth = w_in.shape[0]
    nb, t_len, _ = x_prompt.shape
    db, dt, _ = x_sample.shape
    n_prompt = nb * t_len
    n_sample = db * dt
    assert t_len % PROMPT_TILE == 0 and n_prompt % TOKEN_TILE == 0 and n_sample == TOKEN_TILE
    n_total = n_prompt + n_sample
    assert n_total % STEP_TOKENS == 0
    n_tiles = n_total // TOKEN_TILE
    p_tiles = n_prompt // TOKEN_TILE
    p_rows = 2 * n_total + (SUBLANES - 1) * N_EXPERTS * n_tiles + N_EXPERTS * (FFN_UNIT - 1)
    p_rows = -(-p_rows // FFN_UNIT) * FFN_UNIT
    n_items_max = p_rows // FFN_BLOCK + N_EXPERTS
    upper = jnp.triu(jnp.ones((TOKEN_TILE, TOKEN_TILE), BF16), 1)

    xp, xs = x_prompt, x_sample
    conv_p, h_p, v_s, conv_s, h_s = [], [], [], [], []
    for l in range(depth):
        lw = (norm1_g, w_in, gmlp_ln_g, gmlp_ln_b, gmlp_w_s, gmlp_b_s, conv_w, conv_b, lru_w_a, lru_b_a,
              lru_w_x, lru_b_x, lru_lambda, w_out, norm2_g, router_group_w, router_group_b,
              router_expert_w, router_expert_b)
        wts_p = _mixer_weights(l, GMLP_CHUNK, 1, *lw)
        wts_s = _mixer_weights(l, dt, db, *lw)
        zero_conv = jnp.zeros((nb, CONV_WIDTH - 1, D_B), F32)
        zero_h = jnp.zeros((nb, 1, D_B), F32)
        y1, h2, route, cnt_p, cp, hp = _mixer_call(
            xp, zero_conv, zero_h, wts_p, n_total=n_total, row_offset=0, sb=1, tt=PROMPT_TILE, n_sub=PROMPT_SUB,
            chunk=GMLP_CHUNK, seg_len=PROMPT_SEG, seg_stride=PROMPT_SEG + SUBLANES, chain=True, emit_vn=False)
        y1, h2, route, cnt_s, cs, hs, vs = _mixer_call(
            xs, state_conv[l], state_rglru[l][None], wts_s, n_total=n_total, row_offset=n_prompt, sb=db,
            tt=dt, n_sub=1, chunk=db * dt, seg_len=dt, seg_stride=dt + SUBLANES, chain=False, emit_vn=True,
            alias_in=(y1, h2, route))
        conv_p.append(cp)
        h_p.append(hp[:, 0])
        v_s.append(vs)
        conv_s.append(cs)
        h_s.append(hs[0])

        sub = PROMPT_TILE // TOKEN_TILE
        cnt = jnp.concatenate([
            jnp.swapaxes(cnt_p[:, :, :sub], 1, 2).reshape(p_tiles, N_EXPERTS),
            cnt_s[:, :, 0]], axis=0).astype(jnp.int32)
        ops, ntot, tail, ntail, locoff, items = _routing_tables(cnt, n_items_max)

        xb = _dispatch_call(h2, route, locoff, upper, ops, ntot, tail, ntail, n_tiles=n_tiles, p_rows=p_rows)
        yb = _ffn_call(xb, expert_w_gate[l], expert_w_up[l], expert_w_down[l], items, p_rows=p_rows)
        assert l == depth - 1, "deeper stacks need an un-normalised combine between layers"
        fg = final_norm_g.reshape(1, D_MODEL)
        yp, ysm = _combine_call(y1, route, locoff, upper, fg, yb, ops, ntot, n_tiles=n_tiles, n_prompt=n_prompt,
                                n_sample=n_sample)
        xp = yp.reshape(nb, t_len, D_MODEL)
        xs = ysm.reshape(db, dt, D_MODEL)

    return (xp, xs, jnp.stack(conv_p), jnp.stack(h_p), jnp.stack(v_s), jnp.stack(conv_s), jnp.stack(h_s))
```

```python
import functools

import jax
import jax.numpy as jnp
from jax import lax
from jax.experimental import pallas as pl
from jax.experimental.pallas import tpu as pltpu

D_MODEL = 1024
D_A = 512
D_B = 512
A_GROUPS = 4
A_GROUP_DIM = 128
GMLP_CHUNK = 128
B_HEADS = 8
CONV_WIDTH = 4
LRU_C = 8.0
N_GROUPS = 4
EXPERTS_PER_GROUP = 8
N_EXPERTS = 32
D_EXPERT = 512
EPS = 1e-6
TINY = 1e-30

SUBLANES = 8
LANES = 128
ROUTER_ROWS = 40
TOKEN_TILE = 256
PROMPT_TILE = 1024
PROMPT_SEG = PROMPT_TILE // SUBLANES
FFN_UNIT = 256
FFN_ITEM_UNITS = 4
FFN_BLOCK = FFN_ITEM_UNITS * FFN_UNIT
ZERO_CHUNKS = 8
ZERO_ROWS = ZERO_CHUNKS * SUBLANES
SLOT_ROWS = 2 * TOKEN_TILE + N_EXPERTS * SUBLANES
TILE_CHUNKS = SLOT_ROWS // SUBLANES
COPY_CLASSES = ((4, TILE_CHUNKS // 4), (3, N_EXPERTS), (2, N_EXPERTS), (1, N_EXPERTS))
TILES_PER_STEP = 3
STEP_TOKENS = TILES_PER_STEP * TOKEN_TILE
PACK_COLS = D_MODEL // 2
XB_COLS = PACK_COLS + LANES
U32 = jnp.uint32
HI_MASK = 0xFFFF0000
VMEM_LIMIT = 56 * 1024 * 1024

BF16 = jnp.bfloat16
F32 = jnp.float32


def _rms(x, g):
    return x * lax.rsqrt(jnp.mean(x * x, axis=-1, keepdims=True) + EPS) * g


def _dot(a, b):
    return jnp.dot(a, b, preferred_element_type=F32)


def _sigmoid(x):
    return 1.0 / (1.0 + jnp.exp(-x))


def _pack_bf16_pairs(x):
    bits = lax.bitcast_convert_type(x, U32)
    return (bits[:, PACK_COLS:] & U32(HI_MASK)) | (bits[:, :PACK_COLS] >> 16)


def _unpack_bf16_pairs(w):
    lo = lax.bitcast_convert_type(w << 16, F32).astype(BF16)
    hi = lax.bitcast_convert_type(w & U32(HI_MASK), F32).astype(BF16)
    return jnp.concatenate([lo, hi], axis=1)


def _mixer_kernel(x_ref, hist_ref, h0_ref, n1g_ref, win_ref, lng_ref, lnb_ref, wsm_ref, bsx_ref,
                  cw_ref, cb_ref, wa_ref, ba_ref, wx_ref, bx_ref, lam_ref, wout_ref, n2g_ref,
                  wr_ref, br_ref, *rest, sb, tt, chunk, seg_len, seg_stride, chain, emit_vn, aliased):
    if aliased:
        rest = rest[3:]
    if emit_vn:
        (y1_ref, h2_ref, route_ref, cnt_ref, conv_ref, hlast_ref, vn_ref,
         xs_ref, sa_ref, sb_ref, mix_ref, hc_ref) = rest
    else:
        (y1_ref, h2_ref, route_ref, cnt_ref, conv_ref, hlast_ref,
         xs_ref, sa_ref, sb_ref, mix_ref, hc_ref) = rest
        vn_ref = None
    t_idx = pl.program_id(1)

    @pl.when(t_idx == 0)
    def _():
        xs_ref[:, SUBLANES - 3:SUBLANES, :] = hist_ref[...]
        hc_ref[...] = jnp.zeros_like(hc_ref)

    @pl.when(t_idx != 0)
    def _():
        xs_ref[:, SUBLANES - 3:SUBLANES, :] = xs_ref[:, SUBLANES + tt - 3:SUBLANES + tt, :]

    rows = sb * tt
    n_seg = rows // seg_len
    half = D_B // 2

    x = x_ref[...].reshape(rows, D_MODEL)
    hb = _rms(x, n1g_ref[...]).astype(BF16)

    v_a = _dot(hb, win_ref[:, D_A:2 * D_A])
    x_b = _dot(hb, win_ref[:, 2 * D_A:2 * D_A + D_B])
    mixed = []
    for g in range(A_GROUPS):
        cols = slice(g * A_GROUP_DIM, (g + 1) * A_GROUP_DIM)
        vg = v_a[:, cols]
        mu = jnp.mean(vg, axis=-1, keepdims=True)
        dv = vg - mu
        var = jnp.mean(dv * dv, axis=-1, keepdims=True)
        vn = dv * lax.rsqrt(var + EPS) * lng_ref[:, cols] + lnb_ref[:, cols]
        if vn_ref is not None:
            vn_ref[:, :, cols] = vn.reshape(sb, tt, A_GROUP_DIM)
        vnb = vn.astype(BF16)
        mixed.append([_dot(wsm_ref[g], vnb[c * chunk:(c + 1) * chunk]) for c in range(rows // chunk)])
    u_a = _dot(hb, win_ref[:, 0:D_A])

    xs_ref[:, SUBLANES:SUBLANES + tt, :] = x_b.reshape(sb, tt, D_B)
    xc = cb_ref[...] + x_b * cw_ref[3:4, :]
    for j in range(1, CONV_WIDTH):
        shifted = xs_ref[:, SUBLANES - j:SUBLANES - j + tt, :].reshape(rows, D_B)
        xc = xc + shifted * cw_ref[3 - j:4 - j, :]

    xcb = xc.astype(BF16)
    r_lin = jnp.concatenate([_dot(xcb[:, :half], wa_ref[0]), _dot(xcb[:, half:], wa_ref[1])], axis=1)
    i_lin = jnp.concatenate([_dot(xcb[:, :half], wx_ref[0]), _dot(xcb[:, half:], wx_ref[1])], axis=1)

    for g in range(A_GROUPS):
        cols = slice(g * A_GROUP_DIM, (g + 1) * A_GROUP_DIM)
        for c in range(rows // chunk):
            rs = slice(c * chunk, (c + 1) * chunk)
            s = mixed[g][c] + bsx_ref[:, cols]
            mix_ref[rs, cols] = (u_a[rs, cols] * s).astype(BF16)

    g_b = _dot(hb, win_ref[:, 2 * D_A + D_B:])
    r = _sigmoid(r_lin + ba_ref[...])
    i_g = _sigmoid(i_lin + bx_ref[...])
    log_a = (-LRU_C * r) * jax.nn.softplus(-lam_ref[...])
    a = jnp.exp(log_a)
    y_gain = 1.0 - a * a
    gain = y_gain * lax.rsqrt(jnp.maximum(y_gain, TINY))
    bterm = gain * i_g * xc

    n_lb = D_B // LANES
    for j in range(n_seg):
        for k in range(n_lb):
            lc = slice(k * LANES, (k + 1) * LANES)
            dst = slice(j * seg_stride, j * seg_stride + seg_len)
            sa_ref[k, dst, :] = a[j * seg_len:(j + 1) * seg_len, lc]
            sb_ref[k, dst, :] = bterm[j * seg_len:(j + 1) * seg_len, lc]

    y_a = x + _dot(mix_ref[:, 0:D_A], wout_ref[0:D_A, :])
    gel = jax.nn.gelu(g_b)

    for grp in range(n_seg // SUBLANES):
        base = grp * SUBLANES * seg_stride
        bs = slice(grp * SUBLANES, (grp + 1) * SUBLANES)
        if chain:
            h_init = tuple(jnp.zeros((SUBLANES, LANES), F32) for _ in range(n_lb))
        else:
            h_init = tuple(h0_ref[0, bs, k * LANES:(k + 1) * LANES] for k in range(n_lb))
        a_init = tuple(jnp.ones((SUBLANES, LANES), F32) for _ in range(n_lb))

        def step(i, carry, base=base):
            hs, acs = carry
            idx = pl.ds(base + i, SUBLANES, stride=seg_stride)
            new_h, new_a = [], []
            for k in range(n_lb):
                av = sa_ref[k, idx, :]
                h = av * hs[k] + sb_ref[k, idx, :]
                sb_ref[k, idx, :] = h
                new_h.append(h)
                if chain:
                    ac = av * acs[k]
                    sa_ref[k, idx, :] = ac
                    new_a.append(ac)
                else:
                    new_a.append(acs[k])
            return tuple(new_h), tuple(new_a)

        carry = (h_init, a_init)
        for i in range(seg_len):
            carry = step(i, carry)
        h_end, a_end = carry

        for k in range(n_lb):
            lc = slice(k * LANES, (k + 1) * LANES)
            mc = slice(D_A + k * LANES, D_A + (k + 1) * LANES)
            if chain:
                h_in = hc_ref[:, lc]
                for j in range(SUBLANES):
                    seg = slice(base + j * seg_stride, base + j * seg_stride + seg_len)
                    rs = slice((grp * SUBLANES + j) * seg_len, (grp * SUBLANES + j + 1) * seg_len)
                    h_seg = sb_ref[k, seg, :] + sa_ref[k, seg, :] * h_in
                    mix_ref[rs, mc] = (h_seg * gel[rs, lc]).astype(BF16)
                    h_in = h_end[k][j:j + 1] + a_end[k][j:j + 1] * h_in
                hc_ref[:, lc] = h_in
                hlast_ref[0, :, lc] = h_in
            else:
                for j in range(SUBLANES):
                    seg = slice(base + j * seg_stride, base + j * seg_stride + seg_len)
                    rs = slice((grp * SUBLANES + j) * seg_len, (grp * SUBLANES + j + 1) * seg_len)
                    mix_ref[rs, mc] = (sb_ref[k, seg, :] * gel[rs, lc]).astype(BF16)
                hlast_ref[0, bs, lc] = h_end[k]

    y1 = y_a + _dot(mix_ref[:, D_A:], wout_ref[D_A:, :])
    y1_ref[...] = y1
    h2 = _rms(y1, n2g_ref[...]).astype(BF16)
    h2_ref[...] = h2

    logits = lax.dot_general(wr_ref[...], h2, (((1,), (1,)), ((), ())), preferred_element_type=F32) + br_ref[...]
    gl = [logits[N_EXPERTS + k:N_EXPERTS + k + 1, :] for k in range(N_GROUPS)]
    gmax = jnp.maximum(jnp.maximum(gl[0], gl[1]), jnp.maximum(gl[2], gl[3]))
    grp_idx = jnp.where(gl[0] == gmax, 0, jnp.where(gl[1] == gmax, 1, jnp.where(gl[2] == gmax, 2, 3)))
    gsum = (jnp.exp(gl[0] - gmax) + jnp.exp(gl[1] - gmax)) + (jnp.exp(gl[2] - gmax) + jnp.exp(gl[3] - gmax))
    p_grp = 1.0 / gsum
    e_in = jnp.where(grp_idx == 0, logits[0:8],
                     jnp.where(grp_idx == 1, logits[8:16], jnp.where(grp_idx == 2, logits[16:24], logits[24:32])))
    iota8 = lax.broadcasted_iota(jnp.int32, (EXPERTS_PER_GROUP, rows), 0)
    m1 = jnp.max(e_in, axis=0, keepdims=True)
    i1 = jnp.min(jnp.where(e_in == m1, iota8, EXPERTS_PER_GROUP), axis=0, keepdims=True)
    e_rest = jnp.where(iota8 == i1, -jnp.inf, e_in)
    m2 = jnp.max(e_rest, axis=0, keepdims=True)
    i2 = jnp.min(jnp.where(e_rest == m2, iota8, EXPERTS_PER_GROUP), axis=0, keepdims=True)
    t2 = jnp.exp(m2 - m1)
    den = 1.0 + t2
    gate0 = p_grp * (1.0 / den)
    gate1 = p_grp * (t2 / den)
    e0 = (grp_idx * EXPERTS_PER_GROUP + i1).astype(F32)
    e1 = (grp_idx * EXPERTS_PER_GROUP + i2).astype(F32)
    route_ref[...] = jnp.where(iota8 == 0, e0, jnp.where(iota8 == 1, e1, jnp.where(
        iota8 == 2, gate0, jnp.where(iota8 == 3, gate1, 0.0))))

    iota_e = lax.broadcasted_iota(jnp.int32, (N_EXPERTS, rows), 0).astype(F32)
    onehot = (iota_e == e0).astype(F32) + (iota_e == e1).astype(F32)
    lane = lax.broadcasted_iota(jnp.int32, (N_EXPERTS, LANES), 1)
    cnt = jnp.zeros((N_EXPERTS, LANES), F32)
    for s in range(rows // TOKEN_TILE):
        c_s = jnp.sum(onehot[:, s * TOKEN_TILE:(s + 1) * TOKEN_TILE], axis=1, keepdims=True)
        cnt = cnt + jnp.where(lane == s, c_s, 0.0)
    cnt_ref[...] = cnt.reshape(1, N_EXPERTS, LANES)
    conv_ref[...] = xs_ref[:, SUBLANES + tt - 3:SUBLANES + tt, :]


def _mixer_call(x, hist, h0, wts, *, n_total, row_offset, sb, tt, chunk, seg_len, seg_stride, chain,
                emit_vn, alias_in=None):
    nb, t_len, _ = x.shape
    nbb = nb // sb
    ntt = t_len // tt
    rows = sb * tt
    blk0 = row_offset // rows
    n_seg = rows // seg_len

    def full(arr):
        nd = arr.ndim
        return pl.BlockSpec(arr.shape, lambda b, t, _nd=nd: (0,) * _nd)

    in_specs = [
        pl.BlockSpec((sb, tt, D_MODEL), lambda b, t: (b, t, 0)),
        pl.BlockSpec((sb, CONV_WIDTH - 1, D_B), lambda b, t: (b, 0, 0)),
        pl.BlockSpec((1, sb, D_B), lambda b, t: (b, 0, 0)),
    ] + [full(w) for w in wts]
    args = [x, hist, h0] + list(wts)
    io_alias = {}
    if alias_in is not None:
        for k, arr in enumerate(alias_in):
            io_alias[len(args)] = k
            in_specs.append(pl.BlockSpec(memory_space=pl.ANY))
            args.append(arr)

    out_shape = [
        jax.ShapeDtypeStruct((n_total, D_MODEL), F32),
        jax.ShapeDtypeStruct((n_total, D_MODEL), BF16),
        jax.ShapeDtypeStruct((SUBLANES, n_total), F32),
        jax.ShapeDtypeStruct((nbb * ntt, N_EXPERTS, LANES), F32),
        jax.ShapeDtypeStruct((nb, CONV_WIDTH - 1, D_B), F32),
        jax.ShapeDtypeStruct((nbb, sb, D_B), F32),
    ]
    out_specs = [
        pl.BlockSpec((rows, D_MODEL), lambda b, t: (blk0 + b * ntt + t, 0)),
        pl.BlockSpec((rows, D_MODEL), lambda b, t: (blk0 + b * ntt + t, 0)),
        pl.BlockSpec((SUBLANES, rows), lambda b, t: (0, blk0 + b * ntt + t)),
        pl.BlockSpec((1, N_EXPERTS, LANES), lambda b, t: (b * ntt + t, 0, 0)),
        pl.BlockSpec((sb, CONV_WIDTH - 1, D_B), lambda b, t: (b, 0, 0)),
        pl.BlockSpec((1, sb, D_B), lambda b, t: (b, 0, 0)),
    ]
    if emit_vn:
        out_shape.append(jax.ShapeDtypeStruct((nb, t_len, D_A), F32))
        out_specs.append(pl.BlockSpec((sb, tt, D_A), lambda b, t: (b, t, 0)))

    kern = functools.partial(_mixer_kernel, sb=sb, tt=tt, chunk=chunk, seg_len=seg_len, seg_stride=seg_stride,
                             chain=chain, emit_vn=emit_vn, aliased=alias_in is not None)
    return pl.pallas_call(
        kern,
        grid=(nbb, ntt),
        in_specs=in_specs,
        out_specs=out_specs,
        out_shape=out_shape,
        scratch_shapes=[
            pltpu.VMEM((sb, SUBLANES + tt, D_B), F32),
            pltpu.VMEM((D_B // LANES, n_seg * seg_stride, LANES), F32),
            pltpu.VMEM((D_B // LANES, n_seg * seg_stride, LANES), F32),
            pltpu.VMEM((rows, D_MODEL), BF16),
            pltpu.VMEM((1, D_B), F32),
        ],
        input_output_aliases=io_alias,
        compiler_params=pltpu.CompilerParams(dimension_semantics=("arbitrary", "arbitrary"),
                                             vmem_limit_bytes=VMEM_LIMIT),
        name="mixer_chain" if chain else "mixer_step",
    )(*args)


def _tile_slots(route_ref, locoff_ref, upper_ref, u):
    lanes = slice(u * TOKEN_TILE, (u + 1) * TOKEN_TILE)
    e0 = route_ref[0:1, lanes]
    e1 = route_ref[1:2, lanes]
    iota_e = lax.broadcasted_iota(jnp.int32, (N_EXPERTS, TOKEN_TILE), 0).astype(F32)
    oh0 = (iota_e == e0).astype(F32)
    oh1 = (iota_e == e1).astype(F32)
    c0 = _dot(oh0.astype(BF16), upper_ref[...])
    c1 = _dot(oh1.astype(BF16), upper_ref[...])
    cnt0 = jnp.sum(oh0, axis=1, keepdims=True)
    base0 = locoff_ref[u]
    base1 = base0 + cnt0
    p0 = jnp.sum(oh0 * (base0 + c0), axis=0, keepdims=True)
    p1 = jnp.sum(oh1 * (base1 + c1), axis=0, keepdims=True)
    return p0.astype(jnp.int32), p1.astype(jnp.int32)


def _copy_ops(tile, n_tiles, nops_ref, oloc_ref, odst_ref, make_copy):
    base = 0
    for k, (chunks, cap) in enumerate(COPY_CLASSES):
        def per_op(s, _, chunks=chunks, cap=cap, base=base):
            idx = base + tile * cap + s
            make_copy(pl.multiple_of(oloc_ref[idx], SUBLANES), pl.multiple_of(odst_ref[idx], SUBLANES),
                      chunks * SUBLANES)
            return 0

        lax.fori_loop(0, nops_ref[k * n_tiles + tile], per_op, 0)
        base += n_tiles * cap


def _dispatch_kernel(nops_ref, oloc_ref, odst_ref, ntot_ref, tail_ref, ntail_ref,
                     h2_ref, route_ref, locoff_ref, upper_ref, xb_ref, buf_ref, zero_ref, sem, zsem, *, n_tiles):
    i = pl.program_id(0)
    n_steps = pl.num_programs(0)
    slot = i % 2

    def chunk_copy(s, u, loc, dst, rows=SUBLANES):
        return pltpu.make_async_copy(buf_ref.at[s, u, pl.ds(loc, rows), :],
                                     xb_ref.at[pl.ds(dst, rows), :], sem.at[s])

    def wait_step(step, s):
        for u in range(TILES_PER_STEP):
            n_rows = ntot_ref[step * TILES_PER_STEP + u] * SUBLANES

            @pl.when(n_rows > 0)
            def _():
                pltpu.make_async_copy(buf_ref.at[s, u, pl.ds(0, n_rows), :], xb_ref.at[pl.ds(0, n_rows), :],
                                      sem.at[s]).wait()

    @pl.when(i >= 2)
    def _():
        wait_step(i - 2, slot)

    iota_s = lax.broadcasted_iota(jnp.int32, (SLOT_ROWS, TOKEN_TILE), 0)
    for u in range(TILES_PER_STEP):
        lanes = slice(u * TOKEN_TILE, (u + 1) * TOKEN_TILE)
        p0, p1 = _tile_slots(route_ref, locoff_ref, upper_ref, u)
        hit0 = iota_s == p0
        hit1 = iota_s == p1
        perm = jnp.where(hit0 | hit1, 1.0, 0.0).astype(BF16)
        sorted_rows = _dot(perm, h2_ref[u * TOKEN_TILE:(u + 1) * TOKEN_TILE, :])
        buf_ref[slot, u, :, 0:PACK_COLS] = _pack_bf16_pairs(sorted_rows)
        gates = jnp.where(hit0, route_ref[2:3, lanes], 0.0) + jnp.where(hit1, route_ref[3:4, lanes], 0.0)
        gcol = jnp.sum(gates, axis=1, keepdims=True)
        buf_ref[slot, u, :, PACK_COLS:XB_COLS] = lax.bitcast_convert_type(
            jnp.broadcast_to(gcol, (SLOT_ROWS, LANES)), U32)

    for u in range(TILES_PER_STEP):
        _copy_ops(i * TILES_PER_STEP + u, n_tiles, nops_ref, oloc_ref, odst_ref,
                  lambda loc, dst, rows, u=u: chunk_copy(slot, u, loc, dst, rows).start())

    @pl.when(i == n_steps - 1)
    def _():
        zero_ref[...] = jnp.zeros_like(zero_ref)

        def zero_copy(dst, rows, k):
            return pltpu.make_async_copy(zero_ref.at[pl.ds(0, rows), :], xb_ref.at[pl.ds(dst, rows), :], zsem.at[k])

        def per_expert(e, tot):
            n_big, n_small = tot
            n = ntail_ref[e]
            d0 = tail_ref[e]
            nb = n // ZERO_CHUNKS
            ns = n - nb * ZERO_CHUNKS

            def big(c, _):
                zero_copy(pl.multiple_of(d0 + c * ZERO_ROWS, SUBLANES), ZERO_ROWS, 0).start()
                return 0

            def small(c, _):
                zero_copy(pl.multiple_of(d0 + nb * ZERO_ROWS + c * SUBLANES, SUBLANES), SUBLANES, 1).start()
                return 0

            lax.fori_loop(0, nb, big, 0)
            lax.fori_loop(0, ns, small, 0)
            return n_big + nb, n_small + ns

        n_big, n_small = lax.fori_loop(0, N_EXPERTS, per_expert, (0, 0))

        @pl.when(i >= 1)
        def _():
            wait_step(i - 1, 1 - slot)

        wait_step(i, slot)

        def wait_big(c, _):
            zero_copy(0, ZERO_ROWS, 0).wait()
            return 0

        def wait_small(c, _):
            zero_copy(0, SUBLANES, 1).wait()
            return 0

        lax.fori_loop(0, n_big, wait_big, 0)
        lax.fori_loop(0, n_small, wait_small, 0)


def _dispatch_call(h2, route, locoff, upper, ops, ntot, tail, ntail, *, n_tiles, p_rows):
    grid_spec = pltpu.PrefetchScalarGridSpec(
        num_scalar_prefetch=6,
        grid=(n_tiles // TILES_PER_STEP,),
        in_specs=[
            pl.BlockSpec((STEP_TOKENS, D_MODEL), lambda i, *_: (i, 0)),
            pl.BlockSpec((SUBLANES, STEP_TOKENS), lambda i, *_: (0, i)),
            pl.BlockSpec((TILES_PER_STEP, N_EXPERTS, 1), lambda i, *_: (i, 0, 0)),
            pl.BlockSpec((TOKEN_TILE, TOKEN_TILE), lambda i, *_: (0, 0)),
        ],
        out_specs=pl.BlockSpec(memory_space=pl.ANY),
        scratch_shapes=[
            pltpu.VMEM((2, TILES_PER_STEP, SLOT_ROWS, XB_COLS), U32),
            pltpu.VMEM((ZERO_ROWS, XB_COLS), U32),
            pltpu.SemaphoreType.DMA((2,)),
            pltpu.SemaphoreType.DMA((2,)),
        ],
    )
    return pl.pallas_call(
        functools.partial(_dispatch_kernel, n_tiles=n_tiles),
        grid_spec=grid_spec,
        out_shape=jax.ShapeDtypeStruct((p_rows, XB_COLS), U32),
        compiler_params=pltpu.CompilerParams(dimension_semantics=("arbitrary",), vmem_limit_bytes=VMEM_LIMIT),
        name="dispatch",
    )(*ops, ntot, tail, ntail, h2, route, locoff, upper)


def _ffn_kernel(nitems_ref, istart_ref, iunits_ref, iexp_ref, ifirst_ref, iwslot_ref, inext_ref,
                xb_ref, wg_ref, wu_ref, wd_ref, yb_ref,
                xin_ref, yout_ref, wgf_ref, wuf_ref, wdf_ref, wgb_ref, wub_ref, wdb_ref, sem_in, sem_out, sem_w):
    n_items = nitems_ref[0]
    sizes = tuple(k * FFN_UNIT for k in range(FFN_ITEM_UNITS, 0, -1))

    def rows_in(t, s, rows):
        start = pl.multiple_of(istart_ref[t], FFN_UNIT)
        return pltpu.make_async_copy(xb_ref.at[pl.ds(start, rows), :], xin_ref.at[s, pl.ds(0, rows), :], sem_in.at[s])

    def rows_out(t, s, rows):
        start = pl.multiple_of(istart_ref[t], FFN_UNIT)
        return pltpu.make_async_copy(yout_ref.at[s, pl.ds(0, rows), :], yb_ref.at[pl.ds(start, rows), :],
                                     sem_out.at[s])

    def start_by_size(make, t, s):
        for rows in sizes:
            @pl.when(iunits_ref[t] * FFN_UNIT == rows)
            def _():
                make(t, s, rows).start()

    def weight_copies(e, ws):
        return (pltpu.make_async_copy(wg_ref.at[e], wgf_ref.at[ws], sem_w.at[ws]),
                pltpu.make_async_copy(wu_ref.at[e], wuf_ref.at[ws], sem_w.at[ws]),
                pltpu.make_async_copy(wd_ref.at[e], wdf_ref.at[ws], sem_w.at[ws]))

    def compute(s, rows):
        xb = _unpack_bf16_pairs(xin_ref[s, 0:rows, 0:PACK_COLS])
        gate = lax.bitcast_convert_type(xin_ref[s, 0:rows, PACK_COLS:PACK_COLS + 1], F32)
        a = _dot(xb, wgb_ref[...])
        u = _dot(xb, wub_ref[...])
        mid = (a * _sigmoid(a) * u).astype(BF16)
        y = _dot(mid, wdb_ref[...]) * gate
        yout_ref[s, 0:rows, :] = _pack_bf16_pairs(y.astype(BF16).astype(F32))

    @pl.when(n_items > 0)
    def _():
        for c in weight_copies(iexp_ref[0], 0):
            c.start()
        start_by_size(rows_in, 0, 0)

    def per_item(t, _):
        s = t % 2

        @pl.when(t + 1 < n_items)
        def _():
            start_by_size(rows_in, t + 1, 1 - s)

        @pl.when(ifirst_ref[t] == 1)
        def _():
            ws = iwslot_ref[t]
            for c in weight_copies(0, ws):
                c.wait()

            @pl.when(inext_ref[t] >= 0)
            def _():
                for c in weight_copies(inext_ref[t], 1 - ws):
                    c.start()

            wgb_ref[...] = wgf_ref[ws].astype(BF16)
            wub_ref[...] = wuf_ref[ws].astype(BF16)
            wdb_ref[...] = wdf_ref[ws].astype(BF16)

        rows_in(t, s, iunits_ref[t] * FFN_UNIT).wait()

        @pl.when(t >= 2)
        def _():
            rows_out(t - 2, s, iunits_ref[t - 2] * FFN_UNIT).wait()

        for rows in sizes:
            @pl.when(iunits_ref[t] * FFN_UNIT == rows)
            def _():
                compute(s, rows)

        start_by_size(rows_out, t, s)
        return 0

    lax.fori_loop(0, n_items, per_item, 0)

    for back in (2, 1):
        @pl.when(n_items >= back)
        def _():
            t = n_items - back
            rows_out(t, t % 2, iunits_ref[t] * FFN_UNIT).wait()


def _ffn_call(xb, wg, wu, wd, items, *, p_rows):
    any_spec = pl.BlockSpec(memory_space=pl.ANY)
    grid_spec = pltpu.PrefetchScalarGridSpec(
        num_scalar_prefetch=len(items),
        grid=(1,),
        in_specs=[any_spec, any_spec, any_spec, any_spec],
        out_specs=any_spec,
        scratch_shapes=[
            pltpu.VMEM((2, FFN_BLOCK, XB_COLS), U32),
            pltpu.VMEM((2, FFN_BLOCK, PACK_COLS), U32),
            pltpu.VMEM((2, D_MODEL, D_EXPERT), F32),
            pltpu.VMEM((2, D_MODEL, D_EXPERT), F32),
            pltpu.VMEM((2, D_EXPERT, D_MODEL), F32),
            pltpu.VMEM((D_MODEL, D_EXPERT), BF16),
            pltpu.VMEM((D_MODEL, D_EXPERT), BF16),
            pltpu.VMEM((D_EXPERT, D_MODEL), BF16),
            pltpu.SemaphoreType.DMA((2,)),
            pltpu.SemaphoreType.DMA((2,)),
            pltpu.SemaphoreType.DMA((2,)),
        ],
    )
    return pl.pallas_call(
        _ffn_kernel,
        grid_spec=grid_spec,
        out_shape=jax.ShapeDtypeStruct((p_rows, PACK_COLS), U32),
        compiler_params=pltpu.CompilerParams(dimension_semantics=("arbitrary",), vmem_limit_bytes=VMEM_LIMIT),
        name="expert_ffn",
    )(*items, xb, wg, wu, wd)


def _combine_kernel(nops_ref, oloc_ref, odst_ref, ntot_ref, y1_ref, route_ref, locoff_ref, upper_ref, fg_ref,
                    yb_ref, yp_ref, ys_ref, buf_ref, sem, *, n_tiles):
    i = pl.program_id(0)
    n_steps = pl.num_programs(0)
    slot = i % 2

    def chunk_copy(s, u, loc, src, rows):
        return pltpu.make_async_copy(yb_ref.at[pl.ds(src, rows), :],
                                     buf_ref.at[s, u, pl.ds(loc, rows), :], sem.at[s])

    def issue_step(step, s):
        for u in range(TILES_PER_STEP):
            _copy_ops(step * TILES_PER_STEP + u, n_tiles, nops_ref, oloc_ref, odst_ref,
                      lambda loc, src, rows, u=u: chunk_copy(s, u, loc, src, rows).start())

    @pl.when(i == 0)
    def _():
        buf_ref[...] = jnp.zeros_like(buf_ref)
        issue_step(0, 0)

    @pl.when(i + 1 < n_steps)
    def _():
        issue_step(i + 1, 1 - slot)

    iota_s = lax.broadcasted_iota(jnp.int32, (SLOT_ROWS, TOKEN_TILE), 0)
    perms = []
    for u in range(TILES_PER_STEP):
        p0, p1 = _tile_slots(route_ref, locoff_ref, upper_ref, u)
        perms.append(jnp.where((iota_s == p0) | (iota_s == p1), 1.0, 0.0).astype(BF16))

    for u in range(TILES_PER_STEP):
        n_rows = ntot_ref[i * TILES_PER_STEP + u] * SUBLANES

        @pl.when(n_rows > 0)
        def _():
            pltpu.make_async_copy(yb_ref.at[pl.ds(0, n_rows), :], buf_ref.at[slot, u, pl.ds(0, n_rows), :],
                                  sem.at[slot]).wait()

    for u in range(TILES_PER_STEP):
        rs = slice(u * TOKEN_TILE, (u + 1) * TOKEN_TILE)
        yb = _unpack_bf16_pairs(buf_ref[slot, u])
        moe = lax.dot_general(perms[u], yb, (((0,), (0,)), ((), ())), preferred_element_type=F32)
        y_tile = _rms(y1_ref[rs, :] + moe, fg_ref[...])
        yp_ref[rs, :] = y_tile

    @pl.when(i == n_steps - 1)
    def _():
        ys_ref[...] = y_tile


def _combine_call(y1, route, locoff, upper, fg, yb, ops, ntot, *, n_tiles, n_prompt, n_sample):
    n_steps = n_tiles // TILES_PER_STEP
    assert n_sample == TOKEN_TILE and n_prompt + n_sample == n_steps * STEP_TOKENS
    grid_spec = pltpu.PrefetchScalarGridSpec(
        num_scalar_prefetch=4,
        grid=(n_steps,),
        in_specs=[
            pl.BlockSpec((STEP_TOKENS, D_MODEL), lambda i, *_: (i, 0)),
            pl.BlockSpec((SUBLANES, STEP_TOKENS), lambda i, *_: (0, i)),
            pl.BlockSpec((TILES_PER_STEP, N_EXPERTS, 1), lambda i, *_: (i, 0, 0)),
            pl.BlockSpec((TOKEN_TILE, TOKEN_TILE), lambda i, *_: (0, 0)),
            pl.BlockSpec((1, D_MODEL), lambda i, *_: (0, 0)),
            pl.BlockSpec(memory_space=pl.ANY),
        ],
        out_specs=[
            pl.BlockSpec((STEP_TOKENS, D_MODEL), lambda i, *_: (i, 0)),
            pl.BlockSpec((n_sample, D_MODEL), lambda i, *_: (0, 0)),
        ],
        scratch_shapes=[
            pltpu.VMEM((2, TILES_PER_STEP, SLOT_ROWS, PACK_COLS), U32),
            pltpu.SemaphoreType.DMA((2,)),
        ],
    )
    return pl.pallas_call(
        functools.partial(_combine_kernel, n_tiles=n_tiles),
        grid_spec=grid_spec,
        out_shape=[jax.ShapeDtypeStruct((n_prompt, D_MODEL), F32), jax.ShapeDtypeStruct((n_sample, D_MODEL), F32)],
        compiler_params=pltpu.CompilerParams(dimension_semantics=("arbitrary",), vmem_limit_bytes=VMEM_LIMIT),
        name="combine",
    )(*ops, ntot, y1, route, locoff, upper, fg, yb)


def _block_diag(w):
    h, d, _ = w.shape
    eye = jnp.eye(h, dtype=w.dtype)
    return (eye[:, None, :, None] * w[:, :, None, :]).reshape(h * d, h * d)


def _head_blocks(w):
    half = B_HEADS // 2
    return jnp.stack([_block_diag(w[:half]), _block_diag(w[half:])]).astype(BF16)


def _mixer_weights(l, length, reps, norm1_g, w_in, gmlp_ln_g, gmlp_ln_b, gmlp_w_s, gmlp_b_s, conv_w, conv_b,
                   lru_w_a, lru_b_a, lru_w_x, lru_b_x, lru_lambda, w_out, norm2_g,
                   router_group_w, router_group_b, router_expert_w, router_expert_b):
    mask = jnp.tril(jnp.ones((length, length), dtype=bool))
    ws = jnp.where(mask, gmlp_w_s[l][:, :length, :length], 0.0)
    eye = jnp.eye(reps, dtype=ws.dtype)
    wsm = (eye[None, :, None, :, None] * ws[:, None, :, None, :]).reshape(A_GROUPS, reps * length, reps * length)
    bs = jnp.tile(gmlp_b_s[l][:, :length], (1, reps))
    bsx = jnp.repeat(bs.T, A_GROUP_DIM, axis=1)
    wr = jnp.concatenate([router_expert_w[l].T, router_group_w[l].T,
                          jnp.zeros((ROUTER_ROWS - N_EXPERTS - N_GROUPS, D_MODEL), F32)], axis=0)
    br = jnp.concatenate([router_expert_b[l], router_group_b[l],
                          jnp.zeros((ROUTER_ROWS - N_EXPERTS - N_GROUPS,), F32)]).reshape(ROUTER_ROWS, 1)
    return [
        norm1_g[l].reshape(1, D_MODEL), w_in[l].astype(BF16),
        gmlp_ln_g[l].reshape(1, D_A), gmlp_ln_b[l].reshape(1, D_A), wsm.astype(BF16), bsx,
        conv_w[l], conv_b[l].reshape(1, D_B),
        _head_blocks(lru_w_a[l]), lru_b_a[l].reshape(1, D_B),
        _head_blocks(lru_w_x[l]), lru_b_x[l].reshape(1, D_B),
        lru_lambda[l].reshape(1, D_B), w_out[l].astype(BF16), norm2_g[l].reshape(1, D_MODEL),
        wr.astype(BF16), br,
    ]


def _routing_tables(cnt, n_items_max):
    n_tiles = cnt.shape[0]
    seg = (cnt + SUBLANES - 1) // SUBLANES * SUBLANES
    nch = seg // SUBLANES
    tot = jnp.sum(seg, axis=0)
    padded = (tot + FFN_UNIT - 1) // FFN_UNIT * FFN_UNIT
    e_before = jnp.arange(N_EXPERTS)[None, :] < jnp.arange(N_EXPERTS)[:, None]
    t_before = jnp.arange(n_tiles)[None, :] < jnp.arange(n_tiles)[:, None]
    pad_start = jnp.sum(jnp.where(e_before, padded[None, :], 0), axis=1)
    pad_end = pad_start + padded
    doff = pad_start[None, :] + jnp.sum(jnp.where(t_before[:, :, None], seg[None, :, :], 0), axis=1)
    locoff = jnp.sum(jnp.where(e_before[None, :, :], seg[:, None, :], 0), axis=2)
    ntot = jnp.sum(nch, axis=1)
    whole = nch // COPY_CLASSES[0][0]
    done = whole * COPY_CLASSES[0][0] * SUBLANES
    nops, olocs, odsts = [], [], []
    for chunks, cap in COPY_CLASSES:
        if chunks == COPY_CLASSES[0][0]:
            m, loc0, dst0 = whole, locoff, doff
        else:
            m, loc0, dst0 = (nch - whole * COPY_CLASSES[0][0] == chunks).astype(jnp.int32), locoff + done, doff + done
        start = jnp.sum(jnp.where(e_before[None, :, :], m[:, None, :], 0), axis=2)
        s = jnp.arange(cap)
        owner = jnp.sum((start + m)[:, None, :] <= s[None, :, None], axis=2)
        own = owner[:, :, None] == jnp.arange(N_EXPERTS)[None, None, :]
        step = chunks * SUBLANES * (s[None, :, None] - start[:, None, :])
        olocs.append(jnp.sum(jnp.where(own, loc0[:, None, :] + step, 0), axis=2).reshape(-1))
        odsts.append(jnp.sum(jnp.where(own, dst0[:, None, :] + step, 0), axis=2).reshape(-1))
        nops.append(jnp.sum(m, axis=1))
    tail = pad_start + tot
    ntail = (padded - tot) // SUBLANES
    units = padded // FFN_UNIT
    n_it = (units + FFN_ITEM_UNITS - 1) // FFN_ITEM_UNITS
    it_start = jnp.sum(jnp.where(e_before, n_it[None, :], 0), axis=1)
    t = jnp.arange(n_items_max)
    e_ids = jnp.arange(N_EXPERTS)
    owner = jnp.minimum(jnp.sum((it_start + n_it)[None, :] <= t[:, None], axis=1), N_EXPERTS - 1)
    own = owner[:, None] == e_ids[None, :]
    pick = lambda v: jnp.sum(jnp.where(own, v[None, :], 0), axis=1)
    j = t - pick(it_start)
    istart = pick(pad_start) + j * FFN_BLOCK
    iunits = jnp.clip(pick(units) - FFN_ITEM_UNITS * j, 1, FFN_ITEM_UNITS)
    ordinal = jnp.sum(jnp.where(e_before, (n_it > 0).astype(jnp.int32)[None, :], 0), axis=1)
    later = (e_ids[None, :] > e_ids[:, None]) & (n_it > 0)[None, :]
    nxt = jnp.min(jnp.where(later, e_ids[None, :], N_EXPERTS), axis=1)
    nxt = jnp.where(nxt == N_EXPERTS, -1, nxt)
    i32 = lambda a: a.astype(jnp.int32)
    items = (i32(jnp.sum(n_it).reshape(1)), i32(istart), i32(iunits), i32(owner), i32(j == 0),
             i32(pick(ordinal) % 2), i32(pick(nxt)))
    ops = (i32(jnp.concatenate(nops)), i32(jnp.concatenate(olocs)), i32(jnp.concatenate(odsts)))
    return ops, i32(ntot), i32(tail), i32(ntail), locoff.astype(F32)[:, :, None], items


def kernel(x_prompt, x_sample, state_conv, state_rglru, norm1_g, w_in, gmlp_ln_g, gmlp_ln_b, gmlp_w_s, gmlp_b_s,
           conv_w, conv_b, lru_w_a, lru_b_a, lru_w_x, lru_b_x, lru_lambda, w_out, norm2_g,
           router_group_w, router_group_b, router_expert_w, router_expert_b,
           expert_w_gate, expert_w_up, expert_w_down, final_norm_g):
    depth = w_in.shape[0]
    nb, t_len, _ = x_prompt.shape
    db, dt, _ = x_sample.shape
    n_prompt = nb * t_len
    n_sample = db * dt
    assert t_len % PROMPT_TILE == 0 and n_prompt % TOKEN_TILE == 0 and n_sample == TOKEN_TILE
    n_total = n_prompt + n_sample
    assert n_total % STEP_TOKENS == 0
    n_tiles = n_total // TOKEN_TILE
    p_tiles = n_prompt // TOKEN_TILE
    p_rows = 2 * n_total + (SUBLANES - 1) * N_EXPERTS * n_tiles + N_EXPERTS * (FFN_UNIT - 1)
    p_rows = -(-p_rows // FFN_UNIT) * FFN_UNIT
    n_items_max = p_rows // FFN_BLOCK + N_EXPERTS
    upper = jnp.triu(jnp.ones((TOKEN_TILE, TOKEN_TILE), BF16), 1)

    xp, xs = x_prompt, x_sample
    conv_p, h_p, v_s, conv_s, h_s = [], [], [], [], []
    for l in range(depth):
        lw = (norm1_g, w_in, gmlp_ln_g, gmlp_ln_b, gmlp_w_s, gmlp_b_s, conv_w, conv_b, lru_w_a, lru_b_a,
              lru_w_x, lru_b_x, lru_lambda, w_out, norm2_g, router_group_w, router_group_b,
              router_expert_w, router_expert_b)
        wts_p = _mixer_weights(l, GMLP_CHUNK, 1, *lw)
        wts_s = _mixer_weights(l, dt, db, *lw)
        zero_conv = jnp.zeros((nb, CONV_WIDTH - 1, D_B), F32)
        zero_h = jnp.zeros((nb, 1, D_B), F32)
        y1, h2, route, cnt_p, cp, hp = _mixer_call(
            xp, zero_conv, zero_h, wts_p, n_total=n_total, row_offset=0, sb=1, tt=PROMPT_TILE, chunk=GMLP_CHUNK, seg_len=PROMPT_SEG, seg_stride=PROMPT_SEG + SUBLANES, chain=True, emit_vn=False)
        y1, h2, route, cnt_s, cs, hs, vs = _mixer_call(
            xs, state_conv[l], state_rglru[l][None], wts_s, n_total=n_total, row_offset=n_prompt, sb=db,
            tt=dt, chunk=db * dt, seg_len=dt, seg_stride=dt + SUBLANES, chain=False, emit_vn=True,
            alias_in=(y1, h2, route))
        conv_p.append(cp)
        h_p.append(hp[:, 0])
        v_s.append(vs)
        conv_s.append(cs)
        h_s.append(hs[0])

        sub = PROMPT_TILE // TOKEN_TILE
        cnt = jnp.concatenate([
            jnp.swapaxes(cnt_p[:, :, :sub], 1, 2).reshape(p_tiles, N_EXPERTS),
            cnt_s[:, :, 0]], axis=0).astype(jnp.int32)
        ops, ntot, tail, ntail, locoff, items = _routing_tables(cnt, n_items_max)

        xb = _dispatch_call(h2, route, locoff, upper, ops, ntot, tail, ntail, n_tiles=n_tiles, p_rows=p_rows)
        yb = _ffn_call(xb, expert_w_gate[l], expert_w_up[l], expert_w_down[l], items, p_rows=p_rows)
        assert l == depth - 1, "deeper stacks need an un-normalised combine between layers"
        fg = final_norm_g.reshape(1, D_MODEL)
        yp, ysm = _combine_call(y1, route, locoff, upper, fg, yb, ops, ntot, n_tiles=n_tiles, n_prompt=n_prompt,
                                n_sample=n_sample)
        xp = yp.reshape(nb, t_len, D_MODEL)
        xs = ysm.reshape(db, dt, D_MODEL)

    return (xp, xs, jnp.stack(conv_p), jnp.stack(h_p), jnp.stack(v_s), jnp.stack(conv_s), jnp.stack(h_s))
```

```python
import functools

import jax
import jax.numpy as jnp
from jax import lax
from jax.experimental import pallas as pl
from jax.experimental.pallas import tpu as pltpu

D_MODEL = 1024
D_A = 512
D_B = 512
A_GROUPS = 4
A_GROUP_DIM = 128
GMLP_CHUNK = 128
B_HEADS = 8
CONV_WIDTH = 4
LRU_C = 8.0
N_GROUPS = 4
EXPERTS_PER_GROUP = 8
N_EXPERTS = 32
D_EXPERT = 512
EPS = 1e-6
TINY = 1e-30

SUBLANES = 8
LANES = 128
ROUTER_ROWS = 40
TOKEN_TILE = 256
PROMPT_TILE = 1024
PROMPT_SEG = PROMPT_TILE // SUBLANES
FFN_UNIT = 256
FFN_ITEM_UNITS = 4
FFN_BLOCK = FFN_ITEM_UNITS * FFN_UNIT
ZERO_CHUNKS = 8
ZERO_ROWS = ZERO_CHUNKS * SUBLANES
SLOT_ROWS = 2 * TOKEN_TILE + N_EXPERTS * SUBLANES
TILE_CHUNKS = SLOT_ROWS // SUBLANES
COPY_CLASSES = ((4, TILE_CHUNKS // 4), (3, N_EXPERTS), (2, N_EXPERTS), (1, N_EXPERTS))
TILES_PER_STEP = 3
STEP_TOKENS = TILES_PER_STEP * TOKEN_TILE
PACK_COLS = D_MODEL // 2
XB_COLS = PACK_COLS + LANES
U32 = jnp.uint32
HI_MASK = 0xFFFF0000
VMEM_LIMIT = 56 * 1024 * 1024

BF16 = jnp.bfloat16
F32 = jnp.float32


def _rms(x, g):
    return x * lax.rsqrt(jnp.mean(x * x, axis=-1, keepdims=True) + EPS) * g


def _dot(a, b):
    return jnp.dot(a, b, preferred_element_type=F32)


def _sigmoid(x):
    return 1.0 / (1.0 + jnp.exp(-x))


def _pack_bf16_pairs(x):
    bits = lax.bitcast_convert_type(x, U32)
    return (bits[:, PACK_COLS:] & U32(HI_MASK)) | (bits[:, :PACK_COLS] >> 16)


def _unpack_bf16_pairs(w):
    lo = lax.bitcast_convert_type(w << 16, F32).astype(BF16)
    hi = lax.bitcast_convert_type(w & U32(HI_MASK), F32).astype(BF16)
    return jnp.concatenate([lo, hi], axis=1)


def _mixer_kernel(x_ref, hist_ref, h0_ref, n1g_ref, win_ref, lng_ref, lnb_ref, wsm_ref, bsx_ref,
                  cw_ref, cb_ref, wa_ref, ba_ref, wx_ref, bx_ref, lam_ref, wout_ref, n2g_ref,
                  wr_ref, br_ref, *rest, sb, tt, chunk, seg_len, seg_stride, chain, emit_vn, aliased):
    if aliased:
        rest = rest[3:]
    if emit_vn:
        (y1_ref, h2_ref, route_ref, cnt_ref, conv_ref, hlast_ref, vn_ref,
         xs_ref, sa_ref, sb_ref, mix_ref, hc_ref) = rest
    else:
        (y1_ref, h2_ref, route_ref, cnt_ref, conv_ref, hlast_ref,
         xs_ref, sa_ref, sb_ref, mix_ref, hc_ref) = rest
        vn_ref = None
    t_idx = pl.program_id(1)

    @pl.when(t_idx == 0)
    def _():
        xs_ref[:, SUBLANES - 3:SUBLANES, :] = hist_ref[...]
        hc_ref[...] = jnp.zeros_like(hc_ref)

    @pl.when(t_idx != 0)
    def _():
        xs_ref[:, SUBLANES - 3:SUBLANES, :] = xs_ref[:, SUBLANES + tt - 3:SUBLANES + tt, :]

    rows = sb * tt
    n_seg = rows // seg_len
    half = D_B // 2

    x = x_ref[...].reshape(rows, D_MODEL)
    hb = _rms(x, n1g_ref[...]).astype(BF16)

    v_a = _dot(hb, win_ref[:, D_A:2 * D_A])
    x_b = _dot(hb, win_ref[:, 2 * D_A:2 * D_A + D_B])
    mixed = []
    for g in range(A_GROUPS):
        cols = slice(g * A_GROUP_DIM, (g + 1) * A_GROUP_DIM)
        vg = v_a[:, cols]
        mu = jnp.mean(vg, axis=-1, keepdims=True)
        dv = vg - mu
        var = jnp.mean(dv * dv, axis=-1, keepdims=True)
        vn = dv * lax.rsqrt(var + EPS) * lng_ref[:, cols] + lnb_ref[:, cols]
        if vn_ref is not None:
            vn_ref[:, :, cols] = vn.reshape(sb, tt, A_GROUP_DIM)
        vnb = vn.astype(BF16)
        mixed.append([_dot(wsm_ref[g], vnb[c * chunk:(c + 1) * chunk]) for c in range(rows // chunk)])
    u_a = _dot(hb, win_ref[:, 0:D_A])

    xs_ref[:, SUBLANES:SUBLANES + tt, :] = x_b.reshape(sb, tt, D_B)
    xc = cb_ref[...] + x_b * cw_ref[3:4, :]
    for j in range(1, CONV_WIDTH):
        shifted = xs_ref[:, SUBLANES - j:SUBLANES - j + tt, :].reshape(rows, D_B)
        xc = xc + shifted * cw_ref[3 - j:4 - j, :]

    xcb = xc.astype(BF16)
    r_lin = jnp.concatenate([_dot(xcb[:, :half], wa_ref[0]), _dot(xcb[:, half:], wa_ref[1])], axis=1)
    i_lin = jnp.concatenate([_dot(xcb[:, :half], wx_ref[0]), _dot(xcb[:, half:], wx_ref[1])], axis=1)

    for g in range(A_GROUPS):
        cols = slice(g * A_GROUP_DIM, (g + 1) * A_GROUP_DIM)
        for c in range(rows // chunk):
            rs = slice(c * chunk, (c + 1) * chunk)
            s = mixed[g][c] + bsx_ref[:, cols]
            mix_ref[rs, cols] = (u_a[rs, cols] * s).astype(BF16)

    g_b = _dot(hb, win_ref[:, 2 * D_A + D_B:])
    r = _sigmoid(r_lin + ba_ref[...])
    i_g = _sigmoid(i_lin + bx_ref[...])
    log_a = (-LRU_C * r) * jax.nn.softplus(-lam_ref[...])
    a = jnp.exp(log_a)
    y_gain = 1.0 - a * a
    gain = y_gain * lax.rsqrt(jnp.maximum(y_gain, TINY))
    bterm = gain * i_g * xc

    n_lb = D_B // LANES
    for j in range(n_seg):
        for k in range(n_lb):
            lc = slice(k * LANES, (k + 1) * LANES)
            dst = slice(j * seg_stride, j * seg_stride + seg_len)
            sa_ref[k, dst, :] = a[j * seg_len:(j + 1) * seg_len, lc]
            sb_ref[k, dst, :] = bterm[j * seg_len:(j + 1) * seg_len, lc]

    y_a = x + _dot(mix_ref[:, 0:D_A], wout_ref[0:D_A, :])
    gel = jax.nn.gelu(g_b)

    for grp in range(n_seg // SUBLANES):
        base = grp * SUBLANES * seg_stride
        bs = slice(grp * SUBLANES, (grp + 1) * SUBLANES)
        if chain:
            h_init = tuple(jnp.zeros((SUBLANES, LANES), F32) for _ in range(n_lb))
        else:
            h_init = tuple(h0_ref[0, bs, k * LANES:(k + 1) * LANES] for k in range(n_lb))
        a_init = tuple(jnp.ones((SUBLANES, LANES), F32) for _ in range(n_lb))

        def step(i, carry, base=base):
            hs, acs = carry
            idx = pl.ds(base + i, SUBLANES, stride=seg_stride)
            new_h, new_a = [], []
            for k in range(n_lb):
                av = sa_ref[k, idx, :]
                h = av * hs[k] + sb_ref[k, idx, :]
                sb_ref[k, idx, :] = h
                new_h.append(h)
                if chain:
                    ac = av * acs[k]
                    sa_ref[k, idx, :] = ac
                    new_a.append(ac)
                else:
                    new_a.append(acs[k])
            return tuple(new_h), tuple(new_a)

        carry = (h_init, a_init)
        for i in range(seg_len):
            carry = step(i, carry)
        h_end, a_end = carry

        for k in range(n_lb):
            lc = slice(k * LANES, (k + 1) * LANES)
            mc = slice(D_A + k * LANES, D_A + (k + 1) * LANES)
            if chain:
                h_in = hc_ref[:, lc]
                for j in range(SUBLANES):
                    seg = slice(base + j * seg_stride, base + j * seg_stride + seg_len)
                    rs = slice((grp * SUBLANES + j) * seg_len, (grp * SUBLANES + j + 1) * seg_len)
                    h_seg = sb_ref[k, seg, :] + sa_ref[k, seg, :] * h_in
                    mix_ref[rs, mc] = (h_seg * gel[rs, lc]).astype(BF16)
                    h_in = h_end[k][j:j + 1] + a_end[k][j:j + 1] * h_in
                hc_ref[:, lc] = h_in
                hlast_ref[0, :, lc] = h_in
            else:
                for j in range(SUBLANES):
                    seg = slice(base + j * seg_stride, base + j * seg_stride + seg_len)
                    rs = slice((grp * SUBLANES + j) * seg_len, (grp * SUBLANES + j + 1) * seg_len)
                    mix_ref[rs, mc] = (sb_ref[k, seg, :] * gel[rs, lc]).astype(BF16)
                hlast_ref[0, bs, lc] = h_end[k]

    y1 = y_a + _dot(mix_ref[:, D_A:], wout_ref[D_A:, :])
    y1_ref[...] = y1
    h2 = _rms(y1, n2g_ref[...]).astype(BF16)
    h2_ref[...] = h2

    logits = lax.dot_general(wr_ref[...], h2, (((1,), (1,)), ((), ())), preferred_element_type=F32) + br_ref[...]
    gl = [logits[N_EXPERTS + k:N_EXPERTS + k + 1, :] for k in range(N_GROUPS)]
    gmax = jnp.maximum(jnp.maximum(gl[0], gl[1]), jnp.maximum(gl[2], gl[3]))
    grp_idx = jnp.where(gl[0] == gmax, 0, jnp.where(gl[1] == gmax, 1, jnp.where(gl[2] == gmax, 2, 3)))
    gsum = (jnp.exp(gl[0] - gmax) + jnp.exp(gl[1] - gmax)) + (jnp.exp(gl[2] - gmax) + jnp.exp(gl[3] - gmax))
    p_grp = 1.0 / gsum
    e_in = jnp.where(grp_idx == 0, logits[0:8],
                     jnp.where(grp_idx == 1, logits[8:16], jnp.where(grp_idx == 2, logits[16:24], logits[24:32])))
    iota8 = lax.broadcasted_iota(jnp.int32, (EXPERTS_PER_GROUP, rows), 0)
    m1 = jnp.max(e_in, axis=0, keepdims=True)
    i1 = jnp.min(jnp.where(e_in == m1, iota8, EXPERTS_PER_GROUP), axis=0, keepdims=True)
    e_rest = jnp.where(iota8 == i1, -jnp.inf, e_in)
    m2 = jnp.max(e_rest, axis=0, keepdims=True)
    i2 = jnp.min(jnp.where(e_rest == m2, iota8, EXPERTS_PER_GROUP), axis=0, keepdims=True)
    t2 = jnp.exp(m2 - m1)
    den = 1.0 + t2
    gate0 = p_grp * (1.0 / den)
    gate1 = p_grp * (t2 / den)
    e0 = (grp_idx * EXPERTS_PER_GROUP + i1).astype(F32)
    e1 = (grp_idx * EXPERTS_PER_GROUP + i2).astype(F32)
    route_ref[...] = jnp.where(iota8 == 0, e0, jnp.where(iota8 == 1, e1, jnp.where(
        iota8 == 2, gate0, jnp.where(iota8 == 3, gate1, 0.0))))

    iota_e = lax.broadcasted_iota(jnp.int32, (N_EXPERTS, rows), 0).astype(F32)
    onehot = (iota_e == e0).astype(F32) + (iota_e == e1).astype(F32)
    lane = lax.broadcasted_iota(jnp.int32, (N_EXPERTS, LANES), 1)
    cnt = jnp.zeros((N_EXPERTS, LANES), F32)
    for s in range(rows // TOKEN_TILE):
        c_s = jnp.sum(onehot[:, s * TOKEN_TILE:(s + 1) * TOKEN_TILE], axis=1, keepdims=True)
        cnt = cnt + jnp.where(lane == s, c_s, 0.0)
    cnt_ref[...] = cnt.reshape(1, N_EXPERTS, LANES)
    conv_ref[...] = xs_ref[:, SUBLANES + tt - 3:SUBLANES + tt, :]


def _mixer_call(x, hist, h0, wts, *, n_total, row_offset, sb, tt, chunk, seg_len, seg_stride, chain,
                emit_vn, alias_in=None):
    nb, t_len, _ = x.shape
    nbb = nb // sb
    ntt = t_len // tt
    rows = sb * tt
    blk0 = row_offset // rows
    n_seg = rows // seg_len

    def full(arr):
        nd = arr.ndim
        return pl.BlockSpec(arr.shape, lambda b, t, _nd=nd: (0,) * _nd)

    in_specs = [
        pl.BlockSpec((sb, tt, D_MODEL), lambda b, t: (b, t, 0)),
        pl.BlockSpec((sb, CONV_WIDTH - 1, D_B), lambda b, t: (b, 0, 0)),
        pl.BlockSpec((1, sb, D_B), lambda b, t: (b, 0, 0)),
    ] + [full(w) for w in wts]
    args = [x, hist, h0] + list(wts)
    io_alias = {}
    if alias_in is not None:
        for k, arr in enumerate(alias_in):
            io_alias[len(args)] = k
            in_specs.append(pl.BlockSpec(memory_space=pl.ANY))
            args.append(arr)

    out_shape = [
        jax.ShapeDtypeStruct((n_total, D_MODEL), F32),
        jax.ShapeDtypeStruct((n_total, D_MODEL), BF16),
        jax.ShapeDtypeStruct((SUBLANES, n_total), F32),
        jax.ShapeDtypeStruct((nbb * ntt, N_EXPERTS, LANES), F32),
        jax.ShapeDtypeStruct((nb, CONV_WIDTH - 1, D_B), F32),
        jax.ShapeDtypeStruct((nbb, sb, D_B), F32),
    ]
    out_specs = [
        pl.BlockSpec((rows, D_MODEL), lambda b, t: (blk0 + b * ntt + t, 0)),
        pl.BlockSpec((rows, D_MODEL), lambda b, t: (blk0 + b * ntt + t, 0)),
        pl.BlockSpec((SUBLANES, rows), lambda b, t: (0, blk0 + b * ntt + t)),
        pl.BlockSpec((1, N_EXPERTS, LANES), lambda b, t: (b * ntt + t, 0, 0)),
        pl.BlockSpec((sb, CONV_WIDTH - 1, D_B), lambda b, t: (b, 0, 0)),
        pl.BlockSpec((1, sb, D_B), lambda b, t: (b, 0, 0)),
    ]
    if emit_vn:
        out_shape.append(jax.ShapeDtypeStruct((nb, t_len, D_A), F32))
        out_specs.append(pl.BlockSpec((sb, tt, D_A), lambda b, t: (b, t, 0)))

    kern = functools.partial(_mixer_kernel, sb=sb, tt=tt, chunk=chunk, seg_len=seg_len, seg_stride=seg_stride,
                             chain=chain, emit_vn=emit_vn, aliased=alias_in is not None)
    return pl.pallas_call(
        kern,
        grid=(nbb, ntt),
        in_specs=in_specs,
        out_specs=out_specs,
        out_shape=out_shape,
        scratch_shapes=[
            pltpu.VMEM((sb, SUBLANES + tt, D_B), F32),
            pltpu.VMEM((D_B // LANES, n_seg * seg_stride, LANES), F32),
            pltpu.VMEM((D_B // LANES, n_seg * seg_stride, LANES), F32),
            pltpu.VMEM((rows, D_MODEL), BF16),
            pltpu.VMEM((1, D_B), F32),
        ],
        input_output_aliases=io_alias,
        compiler_params=pltpu.CompilerParams(dimension_semantics=("arbitrary", "arbitrary"),
                                             vmem_limit_bytes=VMEM_LIMIT),
        name="mixer_chain" if chain else "mixer_step",
    )(*args)


def _tile_slots(route_ref, locoff_ref, upper_ref, u):
    lanes = slice(u * TOKEN_TILE, (u + 1) * TOKEN_TILE)
    e0 = route_ref[0:1, lanes]
    e1 = route_ref[1:2, lanes]
    iota_e = lax.broadcasted_iota(jnp.int32, (N_EXPERTS, TOKEN_TILE), 0).astype(F32)
    oh0 = (iota_e == e0).astype(F32)
    oh1 = (iota_e == e1).astype(F32)
    c0 = _dot(oh0.astype(BF16), upper_ref[...])
    c1 = _dot(oh1.astype(BF16), upper_ref[...])
    cnt0 = jnp.sum(oh0, axis=1, keepdims=True)
    base0 = locoff_ref[u]
    base1 = base0 + cnt0
    p0 = jnp.sum(oh0 * (base0 + c0), axis=0, keepdims=True)
    p1 = jnp.sum(oh1 * (base1 + c1), axis=0, keepdims=True)
    return p0.astype(jnp.int32), p1.astype(jnp.int32)


def _copy_ops(tile, n_tiles, nops_ref, oloc_ref, odst_ref, make_copy):
    base = 0
    for k, (chunks, cap) in enumerate(COPY_CLASSES):
        n = nops_ref[k * n_tiles + tile]

        def per_pair(i, _, chunks=chunks, cap=cap, base=base, n=n):
            idx = base + tile * cap + 2 * i
            make_copy(pl.multiple_of(oloc_ref[idx], SUBLANES), pl.multiple_of(odst_ref[idx], SUBLANES),
                      chunks * SUBLANES)

            @pl.when(2 * i + 1 < n)
            def _():
                make_copy(pl.multiple_of(oloc_ref[idx + 1], SUBLANES), pl.multiple_of(odst_ref[idx + 1], SUBLANES),
                          chunks * SUBLANES)
            return 0

        assert cap % 2 == 0
        lax.fori_loop(0, (n + 1) // 2, per_pair, 0)
        base += n_tiles * cap


def _dispatch_kernel(nops_ref, oloc_ref, odst_ref, ntot_ref, tail_ref, ntail_ref,
                     h2_ref, route_ref, locoff_ref, upper_ref, xb_ref, buf_ref, zero_ref, sem, zsem, *, n_tiles):
    i = pl.program_id(0)
    n_steps = pl.num_programs(0)
    slot = i % 2

    def chunk_copy(s, u, loc, dst, rows=SUBLANES):
        return pltpu.make_async_copy(buf_ref.at[s, u, pl.ds(loc, rows), :],
                                     xb_ref.at[pl.ds(dst, rows), :], sem.at[s])

    def wait_step(step, s):
        for u in range(TILES_PER_STEP):
            n_rows = ntot_ref[step * TILES_PER_STEP + u] * SUBLANES

            @pl.when(n_rows > 0)
            def _():
                pltpu.make_async_copy(buf_ref.at[s, u, pl.ds(0, n_rows), :], xb_ref.at[pl.ds(0, n_rows), :],
                                      sem.at[s]).wait()

    @pl.when(i >= 2)
    def _():
        wait_step(i - 2, slot)

    iota_s = lax.broadcasted_iota(jnp.int32, (SLOT_ROWS, TOKEN_TILE), 0)
    for u in range(TILES_PER_STEP):
        lanes = slice(u * TOKEN_TILE, (u + 1) * TOKEN_TILE)
        p0, p1 = _tile_slots(route_ref, locoff_ref, upper_ref, u)
        hit0 = iota_s == p0
        hit1 = iota_s == p1
        perm = jnp.where(hit0 | hit1, 1.0, 0.0).astype(BF16)
        sorted_rows = _dot(perm, h2_ref[u * TOKEN_TILE:(u + 1) * TOKEN_TILE, :])
        buf_ref[slot, u, :, 0:PACK_COLS] = _pack_bf16_pairs(sorted_rows)
        gates = jnp.where(hit0, route_ref[2:3, lanes], 0.0) + jnp.where(hit1, route_ref[3:4, lanes], 0.0)
        gcol = jnp.sum(gates, axis=1, keepdims=True)
        buf_ref[slot, u, :, PACK_COLS:XB_COLS] = lax.bitcast_convert_type(
            jnp.broadcast_to(gcol, (SLOT_ROWS, LANES)), U32)

    for u in range(TILES_PER_STEP):
        _copy_ops(i * TILES_PER_STEP + u, n_tiles, nops_ref, oloc_ref, odst_ref,
                  lambda loc, dst, rows, u=u: chunk_copy(slot, u, loc, dst, rows).start())

    @pl.when(i == n_steps - 1)
    def _():
        zero_ref[...] = jnp.zeros_like(zero_ref)

        def zero_copy(dst, rows, k):
            return pltpu.make_async_copy(zero_ref.at[pl.ds(0, rows), :], xb_ref.at[pl.ds(dst, rows), :], zsem.at[k])

        def per_expert(e, tot):
            n_big, n_small = tot
            n = ntail_ref[e]
            d0 = tail_ref[e]
            nb = n // ZERO_CHUNKS
            ns = n - nb * ZERO_CHUNKS

            def big(c, _):
                zero_copy(pl.multiple_of(d0 + c * ZERO_ROWS, SUBLANES), ZERO_ROWS, 0).start()
                return 0

            def small(c, _):
                zero_copy(pl.multiple_of(d0 + nb * ZERO_ROWS + c * SUBLANES, SUBLANES), SUBLANES, 1).start()
                return 0

            lax.fori_loop(0, nb, big, 0)
            lax.fori_loop(0, ns, small, 0)
            return n_big + nb, n_small + ns

        n_big, n_small = lax.fori_loop(0, N_EXPERTS, per_expert, (0, 0))

        @pl.when(i >= 1)
        def _():
            wait_step(i - 1, 1 - slot)

        wait_step(i, slot)

        def wait_big(c, _):
            zero_copy(0, ZERO_ROWS, 0).wait()
            return 0

        def wait_small(c, _):
            zero_copy(0, SUBLANES, 1).wait()
            return 0

        lax.fori_loop(0, n_big, wait_big, 0)
        lax.fori_loop(0, n_small, wait_small, 0)


def _dispatch_call(h2, route, locoff, upper, ops, ntot, tail, ntail, *, n_tiles, p_rows):
    grid_spec = pltpu.PrefetchScalarGridSpec(
        num_scalar_prefetch=6,
        grid=(n_tiles // TILES_PER_STEP,),
        in_specs=[
            pl.BlockSpec((STEP_TOKENS, D_MODEL), lambda i, *_: (i, 0)),
            pl.BlockSpec((SUBLANES, STEP_TOKENS), lambda i, *_: (0, i)),
            pl.BlockSpec((TILES_PER_STEP, N_EXPERTS, 1), lambda i, *_: (i, 0, 0)),
            pl.BlockSpec((TOKEN_TILE, TOKEN_TILE), lambda i, *_: (0, 0)),
        ],
        out_specs=pl.BlockSpec(memory_space=pl.ANY),
        scratch_shapes=[
            pltpu.VMEM((2, TILES_PER_STEP, SLOT_ROWS, XB_COLS), U32),
            pltpu.VMEM((ZERO_ROWS, XB_COLS), U32),
            pltpu.SemaphoreType.DMA((2,)),
            pltpu.SemaphoreType.DMA((2,)),
        ],
    )
    return pl.pallas_call(
        functools.partial(_dispatch_kernel, n_tiles=n_tiles),
        grid_spec=grid_spec,
        out_shape=jax.ShapeDtypeStruct((p_rows, XB_COLS), U32),
        compiler_params=pltpu.CompilerParams(dimension_semantics=("arbitrary",), vmem_limit_bytes=VMEM_LIMIT),
        name="dispatch",
    )(*ops, ntot, tail, ntail, h2, route, locoff, upper)


def _ffn_kernel(nitems_ref, istart_ref, iunits_ref, iexp_ref, ifirst_ref, iwslot_ref, inext_ref,
                xb_ref, wg_ref, wu_ref, wd_ref, yb_ref,
                xin_ref, yout_ref, wgf_ref, wuf_ref, wdf_ref, wgb_ref, wub_ref, wdb_ref, sem_in, sem_out, sem_w):
    n_items = nitems_ref[0]
    sizes = tuple(k * FFN_UNIT for k in range(FFN_ITEM_UNITS, 0, -1))

    def rows_in(t, s, rows):
        start = pl.multiple_of(istart_ref[t], FFN_UNIT)
        return pltpu.make_async_copy(xb_ref.at[pl.ds(start, rows), :], xin_ref.at[s, pl.ds(0, rows), :], sem_in.at[s])

    def rows_out(t, s, rows):
        start = pl.multiple_of(istart_ref[t], FFN_UNIT)
        return pltpu.make_async_copy(yout_ref.at[s, pl.ds(0, rows), :], yb_ref.at[pl.ds(start, rows), :],
                                     sem_out.at[s])

    def start_by_size(make, t, s):
        for rows in sizes:
            @pl.when(iunits_ref[t] * FFN_UNIT == rows)
            def _():
                make(t, s, rows).start()

    def weight_copies(e, ws):
        return (pltpu.make_async_copy(wg_ref.at[e], wgf_ref.at[ws], sem_w.at[ws]),
                pltpu.make_async_copy(wu_ref.at[e], wuf_ref.at[ws], sem_w.at[ws]),
                pltpu.make_async_copy(wd_ref.at[e], wdf_ref.at[ws], sem_w.at[ws]))

    def compute(s, rows):
        xb = _unpack_bf16_pairs(xin_ref[s, 0:rows, 0:PACK_COLS])
        gate = lax.bitcast_convert_type(xin_ref[s, 0:rows, PACK_COLS:PACK_COLS + 1], F32)
        a = _dot(xb, wgb_ref[...])
        u = _dot(xb, wub_ref[...])
        mid = (a * _sigmoid(a) * u).astype(BF16)
        y = _dot(mid, wdb_ref[...]) * gate
        yout_ref[s, 0:rows, :] = _pack_bf16_pairs(y.astype(BF16).astype(F32))

    @pl.when(n_items > 0)
    def _():
        for c in weight_copies(iexp_ref[0], 0):
            c.start()
        start_by_size(rows_in, 0, 0)

    def per_item(t, _):
        s = t % 2

        @pl.when(t + 1 < n_items)
        def _():
            start_by_size(rows_in, t + 1, 1 - s)

        @pl.when(ifirst_ref[t] == 1)
        def _():
            ws = iwslot_ref[t]
            for c in weight_copies(0, ws):
                c.wait()

            @pl.when(inext_ref[t] >= 0)
            def _():
                for c in weight_copies(inext_ref[t], 1 - ws):
                    c.start()

            wgb_ref[...] = wgf_ref[ws].astype(BF16)
            wub_ref[...] = wuf_ref[ws].astype(BF16)
            wdb_ref[...] = wdf_ref[ws].astype(BF16)

        rows_in(t, s, iunits_ref[t] * FFN_UNIT).wait()

        @pl.when(t >= 2)
        def _():
            rows_out(t - 2, s, iunits_ref[t - 2] * FFN_UNIT).wait()

        for rows in sizes:
            @pl.when(iunits_ref[t] * FFN_UNIT == rows)
            def _():
                compute(s, rows)

        start_by_size(rows_out, t, s)
        return 0

    lax.fori_loop(0, n_items, per_item, 0)

    for back in (2, 1):
        @pl.when(n_items >= back)
        def _():
            t = n_items - back
            rows_out(t, t % 2, iunits_ref[t] * FFN_UNIT).wait()


def _ffn_call(xb, wg, wu, wd, items, *, p_rows):
    any_spec = pl.BlockSpec(memory_space=pl.ANY)
    grid_spec = pltpu.PrefetchScalarGridSpec(
        num_scalar_prefetch=len(items),
        grid=(1,),
        in_specs=[any_spec, any_spec, any_spec, any_spec],
        out_specs=any_spec,
        scratch_shapes=[
            pltpu.VMEM((2, FFN_BLOCK, XB_COLS), U32),
            pltpu.VMEM((2, FFN_BLOCK, PACK_COLS), U32),
            pltpu.VMEM((2, D_MODEL, D_EXPERT), F32),
            pltpu.VMEM((2, D_MODEL, D_EXPERT), F32),
            pltpu.VMEM((2, D_EXPERT, D_MODEL), F32),
            pltpu.VMEM((D_MODEL, D_EXPERT), BF16),
            pltpu.VMEM((D_MODEL, D_EXPERT), BF16),
            pltpu.VMEM((D_EXPERT, D_MODEL), BF16),
            pltpu.SemaphoreType.DMA((2,)),
            pltpu.SemaphoreType.DMA((2,)),
            pltpu.SemaphoreType.DMA((2,)),
        ],
    )
    return pl.pallas_call(
        _ffn_kernel,
        grid_spec=grid_spec,
        out_shape=jax.ShapeDtypeStruct((p_rows, PACK_COLS), U32),
        compiler_params=pltpu.CompilerParams(dimension_semantics=("arbitrary",), vmem_limit_bytes=VMEM_LIMIT),
        name="expert_ffn",
    )(*items, xb, wg, wu, wd)


def _combine_kernel(nops_ref, oloc_ref, odst_ref, ntot_ref, y1_ref, route_ref, locoff_ref, upper_ref, fg_ref,
                    yb_ref, yp_ref, ys_ref, buf_ref, sem, *, n_tiles):
    i = pl.program_id(0)
    n_steps = pl.num_programs(0)
    slot = i % 2

    def chunk_copy(s, u, loc, src, rows):
        return pltpu.make_async_copy(yb_ref.at[pl.ds(src, rows), :],
                                     buf_ref.at[s, u, pl.ds(loc, rows), :], sem.at[s])

    def issue_step(step, s):
        for u in range(TILES_PER_STEP):
            _copy_ops(step * TILES_PER_STEP + u, n_tiles, nops_ref, oloc_ref, odst_ref,
                      lambda loc, src, rows, u=u: chunk_copy(s, u, loc, src, rows).start())

    @pl.when(i == 0)
    def _():
        buf_ref[...] = jnp.zeros_like(buf_ref)
        issue_step(0, 0)

    @pl.when(i + 1 < n_steps)
    def _():
        issue_step(i + 1, 1 - slot)

    iota_s = lax.broadcasted_iota(jnp.int32, (SLOT_ROWS, TOKEN_TILE), 0)
    perms = []
    for u in range(TILES_PER_STEP):
        p0, p1 = _tile_slots(route_ref, locoff_ref, upper_ref, u)
        perms.append(jnp.where((iota_s == p0) | (iota_s == p1), 1.0, 0.0).astype(BF16))

    for u in range(TILES_PER_STEP):
        n_rows = ntot_ref[i * TILES_PER_STEP + u] * SUBLANES

        @pl.when(n_rows > 0)
        def _():
            pltpu.make_async_copy(yb_ref.at[pl.ds(0, n_rows), :], buf_ref.at[slot, u, pl.ds(0, n_rows), :],
                                  sem.at[slot]).wait()

    for u in range(TILES_PER_STEP):
        rs = slice(u * TOKEN_TILE, (u + 1) * TOKEN_TILE)
        yb = _unpack_bf16_pairs(buf_ref[slot, u])
        moe = lax.dot_general(perms[u], yb, (((0,), (0,)), ((), ())), preferred_element_type=F32)
        y_tile = _rms(y1_ref[rs, :] + moe, fg_ref[...])
        yp_ref[rs, :] = y_tile

    @pl.when(i == n_steps - 1)
    def _():
        ys_ref[...] = y_tile


def _combine_call(y1, route, locoff, upper, fg, yb, ops, ntot, *, n_tiles, n_prompt, n_sample):
    n_steps = n_tiles // TILES_PER_STEP
    assert n_sample == TOKEN_TILE and n_prompt + n_sample == n_steps * STEP_TOKENS
    grid_spec = pltpu.PrefetchScalarGridSpec(
        num_scalar_prefetch=4,
        grid=(n_steps,),
        in_specs=[
            pl.BlockSpec((STEP_TOKENS, D_MODEL), lambda i, *_: (i, 0)),
            pl.BlockSpec((SUBLANES, STEP_TOKENS), lambda i, *_: (0, i)),
            pl.BlockSpec((TILES_PER_STEP, N_EXPERTS, 1), lambda i, *_: (i, 0, 0)),
            pl.BlockSpec((TOKEN_TILE, TOKEN_TILE), lambda i, *_: (0, 0)),
            pl.BlockSpec((1, D_MODEL), lambda i, *_: (0, 0)),
            pl.BlockSpec(memory_space=pl.ANY),
        ],
        out_specs=[
            pl.BlockSpec((STEP_TOKENS, D_MODEL), lambda i, *_: (i, 0)),
            pl.BlockSpec((n_sample, D_MODEL), lambda i, *_: (0, 0)),
        ],
        scratch_shapes=[
            pltpu.VMEM((2, TILES_PER_STEP, SLOT_ROWS, PACK_COLS), U32),
            pltpu.SemaphoreType.DMA((2,)),
        ],
    )
    return pl.pallas_call(
        functools.partial(_combine_kernel, n_tiles=n_tiles),
        grid_spec=grid_spec,
        out_shape=[jax.ShapeDtypeStruct((n_prompt, D_MODEL), F32), jax.ShapeDtypeStruct((n_sample, D_MODEL), F32)],
        compiler_params=pltpu.CompilerParams(dimension_semantics=("arbitrary",), vmem_limit_bytes=VMEM_LIMIT),
        name="combine",
    )(*ops, ntot, y1, route, locoff, upper, fg, yb)


def _block_diag(w):
    h, d, _ = w.shape
    eye = jnp.eye(h, dtype=w.dtype)
    return (eye[:, None, :, None] * w[:, :, None, :]).reshape(h * d, h * d)


def _head_blocks(w):
    half = B_HEADS // 2
    return jnp.stack([_block_diag(w[:half]), _block_diag(w[half:])]).astype(BF16)


def _mixer_weights(l, length, reps, norm1_g, w_in, gmlp_ln_g, gmlp_ln_b, gmlp_w_s, gmlp_b_s, conv_w, conv_b,
                   lru_w_a, lru_b_a, lru_w_x, lru_b_x, lru_lambda, w_out, norm2_g,
                   router_group_w, router_group_b, router_expert_w, router_expert_b):
    mask = jnp.tril(jnp.ones((length, length), dtype=bool))
    ws = jnp.where(mask, gmlp_w_s[l][:, :length, :length], 0.0)
    eye = jnp.eye(reps, dtype=ws.dtype)
    wsm = (eye[None, :, None, :, None] * ws[:, None, :, None, :]).reshape(A_GROUPS, reps * length, reps * length)
    bs = jnp.tile(gmlp_b_s[l][:, :length], (1, reps))
    bsx = jnp.repeat(bs.T, A_GROUP_DIM, axis=1)
    wr = jnp.concatenate([router_expert_w[l].T, router_group_w[l].T,
                          jnp.zeros((ROUTER_ROWS - N_EXPERTS - N_GROUPS, D_MODEL), F32)], axis=0)
    br = jnp.concatenate([router_expert_b[l], router_group_b[l],
                          jnp.zeros((ROUTER_ROWS - N_EXPERTS - N_GROUPS,), F32)]).reshape(ROUTER_ROWS, 1)
    return [
        norm1_g[l].reshape(1, D_MODEL), w_in[l].astype(BF16),
        gmlp_ln_g[l].reshape(1, D_A), gmlp_ln_b[l].reshape(1, D_A), wsm.astype(BF16), bsx,
        conv_w[l], conv_b[l].reshape(1, D_B),
        _head_blocks(lru_w_a[l]), lru_b_a[l].reshape(1, D_B),
        _head_blocks(lru_w_x[l]), lru_b_x[l].reshape(1, D_B),
        lru_lambda[l].reshape(1, D_B), w_out[l].astype(BF16), norm2_g[l].reshape(1, D_MODEL),
        wr.astype(BF16), br,
    ]


def _routing_tables(cnt, n_items_max):
    n_tiles = cnt.shape[0]
    seg = (cnt + SUBLANES - 1) // SUBLANES * SUBLANES
    nch = seg // SUBLANES
    tot = jnp.sum(seg, axis=0)
    padded = (tot + FFN_UNIT - 1) // FFN_UNIT * FFN_UNIT
    e_before = jnp.arange(N_EXPERTS)[None, :] < jnp.arange(N_EXPERTS)[:, None]
    t_before = jnp.arange(n_tiles)[None, :] < jnp.arange(n_tiles)[:, None]
    pad_start = jnp.sum(jnp.where(e_before, padded[None, :], 0), axis=1)
    pad_end = pad_start + padded
    doff = pad_start[None, :] + jnp.sum(jnp.where(t_before[:, :, None], seg[None, :, :], 0), axis=1)
    locoff = jnp.sum(jnp.where(e_before[None, :, :], seg[:, None, :], 0), axis=2)
    ntot = jnp.sum(nch, axis=1)
    whole = nch // COPY_CLASSES[0][0]
    done = whole * COPY_CLASSES[0][0] * SUBLANES
    nops, olocs, odsts = [], [], []
    for chunks, cap in COPY_CLASSES:
        if chunks == COPY_CLASSES[0][0]:
            m, loc0, dst0 = whole, locoff, doff
        else:
            m, loc0, dst0 = (nch - whole * COPY_CLASSES[0][0] == chunks).astype(jnp.int32), locoff + done, doff + done
        start = jnp.sum(jnp.where(e_before[None, :, :], m[:, None, :], 0), axis=2)
        s = jnp.arange(cap)
        owner = jnp.sum((start + m)[:, None, :] <= s[None, :, None], axis=2)
        own = owner[:, :, None] == jnp.arange(N_EXPERTS)[None, None, :]
        step = chunks * SUBLANES * (s[None, :, None] - start[:, None, :])
        olocs.append(jnp.sum(jnp.where(own, loc0[:, None, :] + step, 0), axis=2).reshape(-1))
        odsts.append(jnp.sum(jnp.where(own, dst0[:, None, :] + step, 0), axis=2).reshape(-1))
        nops.append(jnp.sum(m, axis=1))
    tail = pad_start + tot
    ntail = (padded - tot) // SUBLANES
    units = padded // FFN_UNIT
    n_it = (units + FFN_ITEM_UNITS - 1) // FFN_ITEM_UNITS
    it_start = jnp.sum(jnp.where(e_before, n_it[None, :], 0), axis=1)
    t = jnp.arange(n_items_max)
    e_ids = jnp.arange(N_EXPERTS)
    owner = jnp.minimum(jnp.sum((it_start + n_it)[None, :] <= t[:, None], axis=1), N_EXPERTS - 1)
    own = owner[:, None] == e_ids[None, :]
    pick = lambda v: jnp.sum(jnp.where(own, v[None, :], 0), axis=1)
    j = t - pick(it_start)
    istart = pick(pad_start) + j * FFN_BLOCK
    iunits = jnp.clip(pick(units) - FFN_ITEM_UNITS * j, 1, FFN_ITEM_UNITS)
    ordinal = jnp.sum(jnp.where(e_before, (n_it > 0).astype(jnp.int32)[None, :], 0), axis=1)
    later = (e_ids[None, :] > e_ids[:, None]) & (n_it > 0)[None, :]
    nxt = jnp.min(jnp.where(later, e_ids[None, :], N_EXPERTS), axis=1)
    nxt = jnp.where(nxt == N_EXPERTS, -1, nxt)
    i32 = lambda a: a.astype(jnp.int32)
    items = (i32(jnp.sum(n_it).reshape(1)), i32(istart), i32(iunits), i32(owner), i32(j == 0),
             i32(pick(ordinal) % 2), i32(pick(nxt)))
    ops = (i32(jnp.concatenate(nops)), i32(jnp.concatenate(olocs)), i32(jnp.concatenate(odsts)))
    return ops, i32(ntot), i32(tail), i32(ntail), locoff.astype(F32)[:, :, None], items


def kernel(x_prompt, x_sample, state_conv, state_rglru, norm1_g, w_in, gmlp_ln_g, gmlp_ln_b, gmlp_w_s, gmlp_b_s,
           conv_w, conv_b, lru_w_a, lru_b_a, lru_w_x, lru_b_x, lru_lambda, w_out, norm2_g,
           router_group_w, router_group_b, router_expert_w, router_expert_b,
           expert_w_gate, expert_w_up, expert_w_down, final_norm_g):
    depth = w_in.shape[0]
    nb, t_len, _ = x_prompt.shape
    db, dt, _ = x_sample.shape
    n_prompt = nb * t_len
    n_sample = db * dt
    assert t_len % PROMPT_TILE == 0 and n_prompt % TOKEN_TILE == 0 and n_sample == TOKEN_TILE
    n_total = n_prompt + n_sample
    assert n_total % STEP_TOKENS == 0
    n_tiles = n_total // TOKEN_TILE
    p_tiles = n_prompt // TOKEN_TILE
    p_rows = 2 * n_total + (SUBLANES - 1) * N_EXPERTS * n_tiles + N_EXPERTS * (FFN_UNIT - 1)
    p_rows = -(-p_rows // FFN_UNIT) * FFN_UNIT
    n_items_max = p_rows // FFN_BLOCK + N_EXPERTS
    upper = jnp.triu(jnp.ones((TOKEN_TILE, TOKEN_TILE), BF16), 1)

    xp, xs = x_prompt, x_sample
    conv_p, h_p, v_s, conv_s, h_s = [], [], [], [], []
    for l in range(depth):
        lw = (norm1_g, w_in, gmlp_ln_g, gmlp_ln_b, gmlp_w_s, gmlp_b_s, conv_w, conv_b, lru_w_a, lru_b_a,
              lru_w_x, lru_b_x, lru_lambda, w_out, norm2_g, router_group_w, router_group_b,
              router_expert_w, router_expert_b)
        wts_p = _mixer_weights(l, GMLP_CHUNK, 1, *lw)
        wts_s = _mixer_weights(l, dt, db, *lw)
        zero_conv = jnp.zeros((nb, CONV_WIDTH - 1, D_B), F32)
        zero_h = jnp.zeros((nb, 1, D_B), F32)
        y1, h2, route, cnt_p, cp, hp = _mixer_call(
            xp, zero_conv, zero_h, wts_p, n_total=n_total, row_offset=0, sb=1, tt=PROMPT_TILE, chunk=GMLP_CHUNK, seg_len=PROMPT_SEG, seg_stride=PROMPT_SEG + SUBLANES, chain=True, emit_vn=False)
        y1, h2, route, cnt_s, cs, hs, vs = _mixer_call(
            xs, state_conv[l], state_rglru[l][None], wts_s, n_total=n_total, row_offset=n_prompt, sb=db,
            tt=dt, chunk=db * dt, seg_len=dt, seg_stride=dt + SUBLANES, chain=False, emit_vn=True,
            alias_in=(y1, h2, route))
        conv_p.append(cp)
        h_p.append(hp[:, 0])
        v_s.append(vs)
        conv_s.append(cs)
        h_s.append(hs[0])

        sub = PROMPT_TILE // TOKEN_TILE
        cnt = jnp.concatenate([
            jnp.swapaxes(cnt_p[:, :, :sub], 1, 2).reshape(p_tiles, N_EXPERTS),
            cnt_s[:, :, 0]], axis=0).astype(jnp.int32)
        ops, ntot, tail, ntail, locoff, items = _routing_tables(cnt, n_items_max)

        xb = _dispatch_call(h2, route, locoff, upper, ops, ntot, tail, ntail, n_tiles=n_tiles, p_rows=p_rows)
        yb = _ffn_call(xb, expert_w_gate[l], expert_w_up[l], expert_w_down[l], items, p_rows=p_rows)
        assert l == depth - 1, "deeper stacks need an un-normalised combine between layers"
        fg = final_norm_g.reshape(1, D_MODEL)
        yp, ysm = _combine_call(y1, route, locoff, upper, fg, yb, ops, ntot, n_tiles=n_tiles, n_prompt=n_prompt,
                                n_sample=n_sample)
        xp = yp.reshape(nb, t_len, D_MODEL)
        xs = ysm.reshape(db, dt, D_MODEL)

    return (xp, xs, jnp.stack(conv_p), jnp.stack(h_p), jnp.stack(v_s), jnp.stack(conv_s), jnp.stack(h_s))
```

```python
import functools

import jax
import jax.numpy as jnp
from jax import lax
from jax.experimental import pallas as pl
from jax.experimental.pallas import tpu as pltpu

D_MODEL = 1024
D_A = 512
D_B = 512
A_GROUPS = 4
A_GROUP_DIM = 128
GMLP_CHUNK = 128
B_HEADS = 8
CONV_WIDTH = 4
LRU_C = 8.0
N_GROUPS = 4
EXPERTS_PER_GROUP = 8
N_EXPERTS = 32
D_EXPERT = 512
EPS = 1e-6
TINY = 1e-30

SUBLANES = 8
LANES = 128
ROUTER_ROWS = 40
TOKEN_TILE = 256
PROMPT_TILE = 1024
PROMPT_SEG = PROMPT_TILE // SUBLANES
FFN_UNIT = 256
FFN_ITEM_UNITS = 4
FFN_BLOCK = FFN_ITEM_UNITS * FFN_UNIT
ZERO_CHUNKS = 8
ZERO_ROWS = ZERO_CHUNKS * SUBLANES
SLOT_ROWS = 2 * TOKEN_TILE + N_EXPERTS * SUBLANES
TILE_CHUNKS = SLOT_ROWS // SUBLANES
COPY_CLASSES = ((4, TILE_CHUNKS // 4), (3, N_EXPERTS), (2, N_EXPERTS), (1, N_EXPERTS))
TILES_PER_STEP = 3
STEP_TOKENS = TILES_PER_STEP * TOKEN_TILE
PACK_COLS = D_MODEL // 2
XB_COLS = PACK_COLS + LANES
U32 = jnp.uint32
HI_MASK = 0xFFFF0000
VMEM_LIMIT = 56 * 1024 * 1024

BF16 = jnp.bfloat16
F32 = jnp.float32


def _rms(x, g):
    return x * lax.rsqrt(jnp.mean(x * x, axis=-1, keepdims=True) + EPS) * g


def _dot(a, b):
    return jnp.dot(a, b, preferred_element_type=F32)


def _sigmoid(x):
    return 1.0 / (1.0 + jnp.exp(-x))


def _pack_bf16_pairs(x):
    bits = lax.bitcast_convert_type(x, U32)
    return (bits[:, PACK_COLS:] & U32(HI_MASK)) | (bits[:, :PACK_COLS] >> 16)


def _unpack_bf16_pairs(w):
    lo = lax.bitcast_convert_type(w << 16, F32).astype(BF16)
    hi = lax.bitcast_convert_type(w & U32(HI_MASK), F32).astype(BF16)
    return jnp.concatenate([lo, hi], axis=1)


def _mixer_kernel(x_ref, hist_ref, h0_ref, n1g_ref, win_ref, lng_ref, lnb_ref, wsm_ref, bsx_ref,
                  cw_ref, cb_ref, wa_ref, ba_ref, wx_ref, bx_ref, lam_ref, wout_ref, n2g_ref,
                  wr_ref, br_ref, *rest, sb, tt, chunk, seg_len, seg_stride, chain, emit_vn, aliased):
    if aliased:
        rest = rest[3:]
    if emit_vn:
        (y1_ref, h2_ref, route_ref, cnt_ref, conv_ref, hlast_ref, vn_ref,
         xs_ref, sa_ref, sb_ref, mix_ref, hc_ref) = rest
    else:
        (y1_ref, h2_ref, route_ref, cnt_ref, conv_ref, hlast_ref,
         xs_ref, sa_ref, sb_ref, mix_ref, hc_ref) = rest
        vn_ref = None
    t_idx = pl.program_id(1)

    @pl.when(t_idx == 0)
    def _():
        xs_ref[:, SUBLANES - 3:SUBLANES, :] = hist_ref[...]
        hc_ref[...] = jnp.zeros_like(hc_ref)

    @pl.when(t_idx != 0)
    def _():
        xs_ref[:, SUBLANES - 3:SUBLANES, :] = xs_ref[:, SUBLANES + tt - 3:SUBLANES + tt, :]

    rows = sb * tt
    n_seg = rows // seg_len
    half = D_B // 2

    x = x_ref[...].reshape(rows, D_MODEL)
    hb = _rms(x, n1g_ref[...]).astype(BF16)

    v_a = _dot(hb, win_ref[:, D_A:2 * D_A])
    x_b = _dot(hb, win_ref[:, 2 * D_A:2 * D_A + D_B])
    mixed = []
    for g in range(A_GROUPS):
        cols = slice(g * A_GROUP_DIM, (g + 1) * A_GROUP_DIM)
        vg = v_a[:, cols]
        mu = jnp.mean(vg, axis=-1, keepdims=True)
        dv = vg - mu
        var = jnp.mean(dv * dv, axis=-1, keepdims=True)
        vn = dv * lax.rsqrt(var + EPS) * lng_ref[:, cols] + lnb_ref[:, cols]
        if vn_ref is not None:
            vn_ref[:, :, cols] = vn.reshape(sb, tt, A_GROUP_DIM)
        vnb = vn.astype(BF16)
        mixed.append([_dot(wsm_ref[g], vnb[c * chunk:(c + 1) * chunk]) for c in range(rows // chunk)])
    u_a = _dot(hb, win_ref[:, 0:D_A])

    xs_ref[:, SUBLANES:SUBLANES + tt, :] = x_b.reshape(sb, tt, D_B)
    xc = cb_ref[...] + x_b * cw_ref[3:4, :]
    for j in range(1, CONV_WIDTH):
        shifted = xs_ref[:, SUBLANES - j:SUBLANES - j + tt, :].reshape(rows, D_B)
        xc = xc + shifted * cw_ref[3 - j:4 - j, :]

    xcb = xc.astype(BF16)
    r_lin = jnp.concatenate([_dot(xcb[:, :half], wa_ref[0]), _dot(xcb[:, half:], wa_ref[1])], axis=1)
    i_lin = jnp.concatenate([_dot(xcb[:, :half], wx_ref[0]), _dot(xcb[:, half:], wx_ref[1])], axis=1)

    for g in range(A_GROUPS):
        cols = slice(g * A_GROUP_DIM, (g + 1) * A_GROUP_DIM)
        for c in range(rows // chunk):
            rs = slice(c * chunk, (c + 1) * chunk)
            s = mixed[g][c] + bsx_ref[:, cols]
            mix_ref[rs, cols] = (u_a[rs, cols] * s).astype(BF16)

    g_b = _dot(hb, win_ref[:, 2 * D_A + D_B:])
    r = _sigmoid(r_lin + ba_ref[...])
    i_g = _sigmoid(i_lin + bx_ref[...])
    log_a = (-LRU_C * r) * jax.nn.softplus(-lam_ref[...])
    a = jnp.exp(log_a)
    y_gain = 1.0 - a * a
    gain = y_gain * lax.rsqrt(jnp.maximum(y_gain, TINY))
    bterm = gain * i_g * xc

    n_lb = D_B // LANES
    for j in range(n_seg):
        for k in range(n_lb):
            lc = slice(k * LANES, (k + 1) * LANES)
            dst = slice(j * seg_stride, j * seg_stride + seg_len)
            sa_ref[k, dst, :] = a[j * seg_len:(j + 1) * seg_len, lc]
            sb_ref[k, dst, :] = bterm[j * seg_len:(j + 1) * seg_len, lc]

    y_a = x + _dot(mix_ref[:, 0:D_A], wout_ref[0:D_A, :])
    gel = jax.nn.gelu(g_b)

    for grp in range(n_seg // SUBLANES):
        base = grp * SUBLANES * seg_stride
        bs = slice(grp * SUBLANES, (grp + 1) * SUBLANES)
        if chain:
            h_init = tuple(jnp.zeros((SUBLANES, LANES), F32) for _ in range(n_lb))
        else:
            h_init = tuple(h0_ref[0, bs, k * LANES:(k + 1) * LANES] for k in range(n_lb))
        a_init = tuple(jnp.ones((SUBLANES, LANES), F32) for _ in range(n_lb))

        def step(i, carry, base=base):
            hs, acs = carry
            idx = pl.ds(base + i, SUBLANES, stride=seg_stride)
            new_h, new_a = [], []
            for k in range(n_lb):
                av = sa_ref[k, idx, :]
                h = av * hs[k] + sb_ref[k, idx, :]
                sb_ref[k, idx, :] = h
                new_h.append(h)
                if chain:
                    ac = av * acs[k]
                    sa_ref[k, idx, :] = ac
                    new_a.append(ac)
                else:
                    new_a.append(acs[k])
            return tuple(new_h), tuple(new_a)

        carry = (h_init, a_init)
        for i in range(seg_len):
            carry = step(i, carry)
        h_end, a_end = carry

        for k in range(n_lb):
            lc = slice(k * LANES, (k + 1) * LANES)
            mc = slice(D_A + k * LANES, D_A + (k + 1) * LANES)
            if chain:
                h_in = hc_ref[:, lc]
                for j in range(SUBLANES):
                    seg = slice(base + j * seg_stride, base + j * seg_stride + seg_len)
                    rs = slice((grp * SUBLANES + j) * seg_len, (grp * SUBLANES + j + 1) * seg_len)
                    h_seg = sb_ref[k, seg, :] + sa_ref[k, seg, :] * h_in
                    mix_ref[rs, mc] = (h_seg * gel[rs, lc]).astype(BF16)
                    h_in = h_end[k][j:j + 1] + a_end[k][j:j + 1] * h_in
                hc_ref[:, lc] = h_in
                hlast_ref[0, :, lc] = h_in
            else:
                for j in range(SUBLANES):
                    seg = slice(base + j * seg_stride, base + j * seg_stride + seg_len)
                    rs = slice((grp * SUBLANES + j) * seg_len, (grp * SUBLANES + j + 1) * seg_len)
                    mix_ref[rs, mc] = (sb_ref[k, seg, :] * gel[rs, lc]).astype(BF16)
                hlast_ref[0, bs, lc] = h_end[k]

    y1 = y_a + _dot(mix_ref[:, D_A:], wout_ref[D_A:, :])
    y1_ref[...] = y1
    h2 = _rms(y1, n2g_ref[...]).astype(BF16)
    h2_ref[...] = h2

    logits = lax.dot_general(wr_ref[...], h2, (((1,), (1,)), ((), ())), preferred_element_type=F32) + br_ref[...]
    gl = [logits[N_EXPERTS + k:N_EXPERTS + k + 1, :] for k in range(N_GROUPS)]
    gmax = jnp.maximum(jnp.maximum(gl[0], gl[1]), jnp.maximum(gl[2], gl[3]))
    grp_idx = jnp.where(gl[0] == gmax, 0, jnp.where(gl[1] == gmax, 1, jnp.where(gl[2] == gmax, 2, 3)))
    gsum = (jnp.exp(gl[0] - gmax) + jnp.exp(gl[1] - gmax)) + (jnp.exp(gl[2] - gmax) + jnp.exp(gl[3] - gmax))
    p_grp = 1.0 / gsum
    e_in = jnp.where(grp_idx == 0, logits[0:8],
                     jnp.where(grp_idx == 1, logits[8:16], jnp.where(grp_idx == 2, logits[16:24], logits[24:32])))
    iota8 = lax.broadcasted_iota(jnp.int32, (EXPERTS_PER_GROUP, rows), 0)
    m1 = jnp.max(e_in, axis=0, keepdims=True)
    i1 = jnp.min(jnp.where(e_in == m1, iota8, EXPERTS_PER_GROUP), axis=0, keepdims=True)
    e_rest = jnp.where(iota8 == i1, -jnp.inf, e_in)
    m2 = jnp.max(e_rest, axis=0, keepdims=True)
    i2 = jnp.min(jnp.where(e_rest == m2, iota8, EXPERTS_PER_GROUP), axis=0, keepdims=True)
    t2 = jnp.exp(m2 - m1)
    den = 1.0 + t2
    gate0 = p_grp * (1.0 / den)
    gate1 = p_grp * (t2 / den)
    e0 = (grp_idx * EXPERTS_PER_GROUP + i1).astype(F32)
    e1 = (grp_idx * EXPERTS_PER_GROUP + i2).astype(F32)
    route_ref[...] = jnp.where(iota8 == 0, e0, jnp.where(iota8 == 1, e1, jnp.where(
        iota8 == 2, gate0, jnp.where(iota8 == 3, gate1, 0.0))))

    iota_e = lax.broadcasted_iota(jnp.int32, (N_EXPERTS, rows), 0).astype(F32)
    onehot = (iota_e == e0).astype(F32) + (iota_e == e1).astype(F32)
    lane = lax.broadcasted_iota(jnp.int32, (N_EXPERTS, LANES), 1)
    cnt = jnp.zeros((N_EXPERTS, LANES), F32)
    for s in range(rows // TOKEN_TILE):
        c_s = jnp.sum(onehot[:, s * TOKEN_TILE:(s + 1) * TOKEN_TILE], axis=1, keepdims=True)
        cnt = cnt + jnp.where(lane == s, c_s, 0.0)
    cnt_ref[...] = cnt.reshape(1, N_EXPERTS, LANES)
    conv_ref[...] = xs_ref[:, SUBLANES + tt - 3:SUBLANES + tt, :]


def _mixer_call(x, hist, h0, wts, *, n_total, row_offset, sb, tt, chunk, seg_len, seg_stride, chain,
                emit_vn, alias_in=None):
    nb, t_len, _ = x.shape
    nbb = nb // sb
    ntt = t_len // tt
    rows = sb * tt
    blk0 = row_offset // rows
    n_seg = rows // seg_len

    def full(arr):
        nd = arr.ndim
        return pl.BlockSpec(arr.shape, lambda b, t, _nd=nd: (0,) * _nd)

    in_specs = [
        pl.BlockSpec((sb, tt, D_MODEL), lambda b, t: (b, t, 0)),
        pl.BlockSpec((sb, CONV_WIDTH - 1, D_B), lambda b, t: (b, 0, 0)),
        pl.BlockSpec((1, sb, D_B), lambda b, t: (b, 0, 0)),
    ] + [full(w) for w in wts]
    args = [x, hist, h0] + list(wts)
    io_alias = {}
    if alias_in is not None:
        for k, arr in enumerate(alias_in):
            io_alias[len(args)] = k
            in_specs.append(pl.BlockSpec(memory_space=pl.ANY))
            args.append(arr)

    out_shape = [
        jax.ShapeDtypeStruct((n_total, D_MODEL), F32),
        jax.ShapeDtypeStruct((n_total, D_MODEL), BF16),
        jax.ShapeDtypeStruct((SUBLANES, n_total), F32),
        jax.ShapeDtypeStruct((nbb * ntt, N_EXPERTS, LANES), F32),
        jax.ShapeDtypeStruct((nb, CONV_WIDTH - 1, D_B), F32),
        jax.ShapeDtypeStruct((nbb, sb, D_B), F32),
    ]
    out_specs = [
        pl.BlockSpec((rows, D_MODEL), lambda b, t: (blk0 + b * ntt + t, 0)),
        pl.BlockSpec((rows, D_MODEL), lambda b, t: (blk0 + b * ntt + t, 0)),
        pl.BlockSpec((SUBLANES, rows), lambda b, t: (0, blk0 + b * ntt + t)),
        pl.BlockSpec((1, N_EXPERTS, LANES), lambda b, t: (b * ntt + t, 0, 0)),
        pl.BlockSpec((sb, CONV_WIDTH - 1, D_B), lambda b, t: (b, 0, 0)),
        pl.BlockSpec((1, sb, D_B), lambda b, t: (b, 0, 0)),
    ]
    if emit_vn:
        out_shape.append(jax.ShapeDtypeStruct((nb, t_len, D_A), F32))
        out_specs.append(pl.BlockSpec((sb, tt, D_A), lambda b, t: (b, t, 0)))

    kern = functools.partial(_mixer_kernel, sb=sb, tt=tt, chunk=chunk, seg_len=seg_len, seg_stride=seg_stride,
                             chain=chain, emit_vn=emit_vn, aliased=alias_in is not None)
    return pl.pallas_call(
        kern,
        grid=(nbb, ntt),
        in_specs=in_specs,
        out_specs=out_specs,
        out_shape=out_shape,
        scratch_shapes=[
            pltpu.VMEM((sb, SUBLANES + tt, D_B), F32),
            pltpu.VMEM((D_B // LANES, n_seg * seg_stride, LANES), F32),
            pltpu.VMEM((D_B // LANES, n_seg * seg_stride, LANES), F32),
            pltpu.VMEM((rows, D_MODEL), BF16),
            pltpu.VMEM((1, D_B), F32),
        ],
        input_output_aliases=io_alias,
        compiler_params=pltpu.CompilerParams(dimension_semantics=("arbitrary", "arbitrary"),
                                             vmem_limit_bytes=VMEM_LIMIT),
        name="mixer_chain" if chain else "mixer_step",
    )(*args)


def _tile_slots(route_ref, locoff_ref, upper_ref, u):
    lanes = slice(u * TOKEN_TILE, (u + 1) * TOKEN_TILE)
    e0 = route_ref[0:1, lanes]
    e1 = route_ref[1:2, lanes]
    iota_e = lax.broadcasted_iota(jnp.int32, (N_EXPERTS, TOKEN_TILE), 0).astype(F32)
    oh0 = (iota_e == e0).astype(F32)
    oh1 = (iota_e == e1).astype(F32)
    c0 = _dot(oh0.astype(BF16), upper_ref[...])
    c1 = _dot(oh1.astype(BF16), upper_ref[...])
    cnt0 = jnp.sum(oh0, axis=1, keepdims=True)
    base0 = locoff_ref[u]
    base1 = base0 + cnt0
    p0 = jnp.sum(oh0 * (base0 + c0), axis=0, keepdims=True)
    p1 = jnp.sum(oh1 * (base1 + c1), axis=0, keepdims=True)
    return p0.astype(jnp.int32), p1.astype(jnp.int32)


def _copy_ops(tile, n_tiles, nops_ref, oloc_ref, odst_ref, make_copy):
    base = 0
    for k, (chunks, cap) in enumerate(COPY_CLASSES):
        n = nops_ref[k * n_tiles + tile]

        def per_pair(i, _, chunks=chunks, cap=cap, base=base, n=n):
            idx = base + tile * cap + 2 * i
            make_copy(pl.multiple_of(oloc_ref[idx], SUBLANES), pl.multiple_of(odst_ref[idx], SUBLANES),
                      chunks * SUBLANES)

            @pl.when(2 * i + 1 < n)
            def _():
                make_copy(pl.multiple_of(oloc_ref[idx + 1], SUBLANES), pl.multiple_of(odst_ref[idx + 1], SUBLANES),
                          chunks * SUBLANES)
            return 0

        assert cap % 2 == 0
        lax.fori_loop(0, (n + 1) // 2, per_pair, 0)
        base += n_tiles * cap


def _dispatch_kernel(nops_ref, oloc_ref, odst_ref, ntot_ref, tail_ref, ntail_ref,
                     h2_ref, route_ref, locoff_ref, upper_ref, xb_ref, slots_ref, buf_ref, zero_ref, sem, zsem,
                     *, n_tiles):
    i = pl.program_id(0)
    n_steps = pl.num_programs(0)
    slot = i % 2

    def chunk_copy(s, u, loc, dst, rows=SUBLANES):
        return pltpu.make_async_copy(buf_ref.at[s, u, pl.ds(loc, rows), :],
                                     xb_ref.at[pl.ds(dst, rows), :], sem.at[s])

    def wait_step(step, s):
        for u in range(TILES_PER_STEP):
            n_rows = ntot_ref[step * TILES_PER_STEP + u] * SUBLANES

            @pl.when(n_rows > 0)
            def _():
                pltpu.make_async_copy(buf_ref.at[s, u, pl.ds(0, n_rows), :], xb_ref.at[pl.ds(0, n_rows), :],
                                      sem.at[s]).wait()

    @pl.when(i >= 2)
    def _():
        wait_step(i - 2, slot)

    iota_s = lax.broadcasted_iota(jnp.int32, (SLOT_ROWS, TOKEN_TILE), 0)
    iota8 = lax.broadcasted_iota(jnp.int32, (SUBLANES, TOKEN_TILE), 0)
    for u in range(TILES_PER_STEP):
        lanes = slice(u * TOKEN_TILE, (u + 1) * TOKEN_TILE)
        p0, p1 = _tile_slots(route_ref, locoff_ref, upper_ref, u)
        slots_ref[:, lanes] = jnp.where(iota8 == 0, p0, jnp.where(iota8 == 1, p1, 0))
        hit0 = iota_s == p0
        hit1 = iota_s == p1
        perm = jnp.where(hit0 | hit1, 1.0, 0.0).astype(BF16)
        sorted_rows = _dot(perm, h2_ref[u * TOKEN_TILE:(u + 1) * TOKEN_TILE, :])
        buf_ref[slot, u, :, 0:PACK_COLS] = _pack_bf16_pairs(sorted_rows)
        gates = jnp.where(hit0, route_ref[2:3, lanes], 0.0) + jnp.where(hit1, route_ref[3:4, lanes], 0.0)
        gcol = jnp.sum(gates, axis=1, keepdims=True)
        buf_ref[slot, u, :, PACK_COLS:XB_COLS] = lax.bitcast_convert_type(
            jnp.broadcast_to(gcol, (SLOT_ROWS, LANES)), U32)

    for u in range(TILES_PER_STEP):
        _copy_ops(i * TILES_PER_STEP + u, n_tiles, nops_ref, oloc_ref, odst_ref,
                  lambda loc, dst, rows, u=u: chunk_copy(slot, u, loc, dst, rows).start())

    @pl.when(i == n_steps - 1)
    def _():
        zero_ref[...] = jnp.zeros_like(zero_ref)

        def zero_copy(dst, rows, k):
            return pltpu.make_async_copy(zero_ref.at[pl.ds(0, rows), :], xb_ref.at[pl.ds(dst, rows), :], zsem.at[k])

        def per_expert(e, tot):
            n_big, n_small = tot
            n = ntail_ref[e]
            d0 = tail_ref[e]
            nb = n // ZERO_CHUNKS
            ns = n - nb * ZERO_CHUNKS

            def big(c, _):
                zero_copy(pl.multiple_of(d0 + c * ZERO_ROWS, SUBLANES), ZERO_ROWS, 0).start()
                return 0

            def small(c, _):
                zero_copy(pl.multiple_of(d0 + nb * ZERO_ROWS + c * SUBLANES, SUBLANES), SUBLANES, 1).start()
                return 0

            lax.fori_loop(0, nb, big, 0)
            lax.fori_loop(0, ns, small, 0)
            return n_big + nb, n_small + ns

        n_big, n_small = lax.fori_loop(0, N_EXPERTS, per_expert, (0, 0))

        @pl.when(i >= 1)
        def _():
            wait_step(i - 1, 1 - slot)

        wait_step(i, slot)

        def wait_big(c, _):
            zero_copy(0, ZERO_ROWS, 0).wait()
            return 0

        def wait_small(c, _):
            zero_copy(0, SUBLANES, 1).wait()
            return 0

        lax.fori_loop(0, n_big, wait_big, 0)
        lax.fori_loop(0, n_small, wait_small, 0)


def _dispatch_call(h2, route, locoff, upper, ops, ntot, tail, ntail, *, n_tiles, p_rows):
    grid_spec = pltpu.PrefetchScalarGridSpec(
        num_scalar_prefetch=6,
        grid=(n_tiles // TILES_PER_STEP,),
        in_specs=[
            pl.BlockSpec((STEP_TOKENS, D_MODEL), lambda i, *_: (i, 0)),
            pl.BlockSpec((SUBLANES, STEP_TOKENS), lambda i, *_: (0, i)),
            pl.BlockSpec((TILES_PER_STEP, N_EXPERTS, 1), lambda i, *_: (i, 0, 0)),
            pl.BlockSpec((TOKEN_TILE, TOKEN_TILE), lambda i, *_: (0, 0)),
        ],
        out_specs=[pl.BlockSpec(memory_space=pl.ANY),
                   pl.BlockSpec((SUBLANES, STEP_TOKENS), lambda i, *_: (0, i))],
        scratch_shapes=[
            pltpu.VMEM((2, TILES_PER_STEP, SLOT_ROWS, XB_COLS), U32),
            pltpu.VMEM((ZERO_ROWS, XB_COLS), U32),
            pltpu.SemaphoreType.DMA((2,)),
            pltpu.SemaphoreType.DMA((2,)),
        ],
    )
    return pl.pallas_call(
        functools.partial(_dispatch_kernel, n_tiles=n_tiles),
        grid_spec=grid_spec,
        out_shape=[jax.ShapeDtypeStruct((p_rows, XB_COLS), U32),
                   jax.ShapeDtypeStruct((SUBLANES, n_tiles * TOKEN_TILE), jnp.int32)],
        compiler_params=pltpu.CompilerParams(dimension_semantics=("arbitrary",), vmem_limit_bytes=VMEM_LIMIT),
        name="dispatch",
    )(*ops, ntot, tail, ntail, h2, route, locoff, upper)


def _ffn_kernel(nitems_ref, istart_ref, iunits_ref, iexp_ref, ifirst_ref, iwslot_ref, inext_ref,
                xb_ref, wg_ref, wu_ref, wd_ref, yb_ref,
                xin_ref, yout_ref, wgf_ref, wuf_ref, wdf_ref, wgb_ref, wub_ref, wdb_ref, sem_in, sem_out, sem_w):
    n_items = nitems_ref[0]
    sizes = tuple(k * FFN_UNIT for k in range(FFN_ITEM_UNITS, 0, -1))

    def rows_in(t, s, rows):
        start = pl.multiple_of(istart_ref[t], FFN_UNIT)
        return pltpu.make_async_copy(xb_ref.at[pl.ds(start, rows), :], xin_ref.at[s, pl.ds(0, rows), :], sem_in.at[s])

    def rows_out(t, s, rows):
        start = pl.multiple_of(istart_ref[t], FFN_UNIT)
        return pltpu.make_async_copy(yout_ref.at[s, pl.ds(0, rows), :], yb_ref.at[pl.ds(start, rows), :],
                                     sem_out.at[s])

    def start_by_size(make, t, s):
        for rows in sizes:
            @pl.when(iunits_ref[t] * FFN_UNIT == rows)
            def _():
                make(t, s, rows).start()

    def weight_copies(e, ws):
        return (pltpu.make_async_copy(wg_ref.at[e], wgf_ref.at[ws], sem_w.at[ws]),
                pltpu.make_async_copy(wu_ref.at[e], wuf_ref.at[ws], sem_w.at[ws]),
                pltpu.make_async_copy(wd_ref.at[e], wdf_ref.at[ws], sem_w.at[ws]))

    def compute(s, rows):
        xb = _unpack_bf16_pairs(xin_ref[s, 0:rows, 0:PACK_COLS])
        gate = lax.bitcast_convert_type(xin_ref[s, 0:rows, PACK_COLS:PACK_COLS + 1], F32)
        a = _dot(xb, wgb_ref[...])
        u = _dot(xb, wub_ref[...])
        mid = (a * _sigmoid(a) * u).astype(BF16)
        y = _dot(mid, wdb_ref[...]) * gate
        yout_ref[s, 0:rows, :] = _pack_bf16_pairs(y.astype(BF16).astype(F32))

    @pl.when(n_items > 0)
    def _():
        for c in weight_copies(iexp_ref[0], 0):
            c.start()
        start_by_size(rows_in, 0, 0)

    def per_item(t, _):
        s = t % 2

        @pl.when(t + 1 < n_items)
        def _():
            start_by_size(rows_in, t + 1, 1 - s)

        @pl.when(ifirst_ref[t] == 1)
        def _():
            ws = iwslot_ref[t]
            for c in weight_copies(0, ws):
                c.wait()

            @pl.when(inext_ref[t] >= 0)
            def _():
                for c in weight_copies(inext_ref[t], 1 - ws):
                    c.start()

            wgb_ref[...] = wgf_ref[ws].astype(BF16)
            wub_ref[...] = wuf_ref[ws].astype(BF16)
            wdb_ref[...] = wdf_ref[ws].astype(BF16)

        rows_in(t, s, iunits_ref[t] * FFN_UNIT).wait()

        @pl.when(t >= 2)
        def _():
            rows_out(t - 2, s, iunits_ref[t - 2] * FFN_UNIT).wait()

        for rows in sizes:
            @pl.when(iunits_ref[t] * FFN_UNIT == rows)
            def _():
                compute(s, rows)

        start_by_size(rows_out, t, s)
        return 0

    lax.fori_loop(0, n_items, per_item, 0)

    for back in (2, 1):
        @pl.when(n_items >= back)
        def _():
            t = n_items - back
            rows_out(t, t % 2, iunits_ref[t] * FFN_UNIT).wait()


def _ffn_call(xb, wg, wu, wd, items, *, p_rows):
    any_spec = pl.BlockSpec(memory_space=pl.ANY)
    grid_spec = pltpu.PrefetchScalarGridSpec(
        num_scalar_prefetch=len(items),
        grid=(1,),
        in_specs=[any_spec, any_spec, any_spec, any_spec],
        out_specs=any_spec,
        scratch_shapes=[
            pltpu.VMEM((2, FFN_BLOCK, XB_COLS), U32),
            pltpu.VMEM((2, FFN_BLOCK, PACK_COLS), U32),
            pltpu.VMEM((2, D_MODEL, D_EXPERT), F32),
            pltpu.VMEM((2, D_MODEL, D_EXPERT), F32),
            pltpu.VMEM((2, D_EXPERT, D_MODEL), F32),
            pltpu.VMEM((D_MODEL, D_EXPERT), BF16),
            pltpu.VMEM((D_MODEL, D_EXPERT), BF16),
            pltpu.VMEM((D_EXPERT, D_MODEL), BF16),
            pltpu.SemaphoreType.DMA((2,)),
            pltpu.SemaphoreType.DMA((2,)),
            pltpu.SemaphoreType.DMA((2,)),
        ],
    )
    return pl.pallas_call(
        _ffn_kernel,
        grid_spec=grid_spec,
        out_shape=jax.ShapeDtypeStruct((p_rows, PACK_COLS), U32),
        compiler_params=pltpu.CompilerParams(dimension_semantics=("arbitrary",), vmem_limit_bytes=VMEM_LIMIT),
        name="expert_ffn",
    )(*items, xb, wg, wu, wd)


def _combine_kernel(nops_ref, oloc_ref, odst_ref, ntot_ref, y1_ref, slots_ref, fg_ref,
                    yb_ref, yp_ref, ys_ref, buf_ref, sem, *, n_tiles):
    i = pl.program_id(0)
    n_steps = pl.num_programs(0)
    slot = i % 2

    def chunk_copy(s, u, loc, src, rows):
        return pltpu.make_async_copy(yb_ref.at[pl.ds(src, rows), :],
                                     buf_ref.at[s, u, pl.ds(loc, rows), :], sem.at[s])

    def issue_step(step, s):
        for u in range(TILES_PER_STEP):
            _copy_ops(step * TILES_PER_STEP + u, n_tiles, nops_ref, oloc_ref, odst_ref,
                      lambda loc, src, rows, u=u: chunk_copy(s, u, loc, src, rows).start())

    @pl.when(i == 0)
    def _():
        buf_ref[...] = jnp.zeros_like(buf_ref)
        issue_step(0, 0)

    @pl.when(i + 1 < n_steps)
    def _():
        issue_step(i + 1, 1 - slot)

    iota_s = lax.broadcasted_iota(jnp.int32, (SLOT_ROWS, TOKEN_TILE), 0)
    perms = []
    for u in range(TILES_PER_STEP):
        lanes = slice(u * TOKEN_TILE, (u + 1) * TOKEN_TILE)
        p0 = slots_ref[0:1, lanes]
        p1 = slots_ref[1:2, lanes]
        perms.append(jnp.where((iota_s == p0) | (iota_s == p1), 1.0, 0.0).astype(BF16))

    for u in range(TILES_PER_STEP):
        n_rows = ntot_ref[i * TILES_PER_STEP + u] * SUBLANES

        @pl.when(n_rows > 0)
        def _():
            pltpu.make_async_copy(yb_ref.at[pl.ds(0, n_rows), :], buf_ref.at[slot, u, pl.ds(0, n_rows), :],
                                  sem.at[slot]).wait()

    for u in range(TILES_PER_STEP):
        rs = slice(u * TOKEN_TILE, (u + 1) * TOKEN_TILE)
        yb = _unpack_bf16_pairs(buf_ref[slot, u])
        moe = lax.dot_general(perms[u], yb, (((0,), (0,)), ((), ())), preferred_element_type=F32)
        y_tile = _rms(y1_ref[rs, :] + moe, fg_ref[...])
        yp_ref[rs, :] = y_tile

    @pl.when(i == n_steps - 1)
    def _():
        ys_ref[...] = y_tile


def _combine_call(y1, slots, fg, yb, ops, ntot, *, n_tiles, n_prompt, n_sample):
    n_steps = n_tiles // TILES_PER_STEP
    assert n_sample == TOKEN_TILE and n_prompt + n_sample == n_steps * STEP_TOKENS
    grid_spec = pltpu.PrefetchScalarGridSpec(
        num_scalar_prefetch=4,
        grid=(n_steps,),
        in_specs=[
            pl.BlockSpec((STEP_TOKENS, D_MODEL), lambda i, *_: (i, 0)),
            pl.BlockSpec((SUBLANES, STEP_TOKENS), lambda i, *_: (0, i)),
            pl.BlockSpec((1, D_MODEL), lambda i, *_: (0, 0)),
            pl.BlockSpec(memory_space=pl.ANY),
        ],
        out_specs=[
            pl.BlockSpec((STEP_TOKENS, D_MODEL), lambda i, *_: (i, 0)),
            pl.BlockSpec((n_sample, D_MODEL), lambda i, *_: (0, 0)),
        ],
        scratch_shapes=[
            pltpu.VMEM((2, TILES_PER_STEP, SLOT_ROWS, PACK_COLS), U32),
            pltpu.SemaphoreType.DMA((2,)),
        ],
    )
    return pl.pallas_call(
        functools.partial(_combine_kernel, n_tiles=n_tiles),
        grid_spec=grid_spec,
        out_shape=[jax.ShapeDtypeStruct((n_prompt, D_MODEL), F32), jax.ShapeDtypeStruct((n_sample, D_MODEL), F32)],
        compiler_params=pltpu.CompilerParams(dimension_semantics=("arbitrary",), vmem_limit_bytes=VMEM_LIMIT),
        name="combine",
    )(*ops, ntot, y1, slots, fg, yb)


def _block_diag(w):
    h, d, _ = w.shape
    eye = jnp.eye(h, dtype=w.dtype)
    return (eye[:, None, :, None] * w[:, :, None, :]).reshape(h * d, h * d)


def _head_blocks(w):
    half = B_HEADS // 2
    return jnp.stack([_block_diag(w[:half]), _block_diag(w[half:])]).astype(BF16)


def _mixer_weights(l, length, reps, norm1_g, w_in, gmlp_ln_g, gmlp_ln_b, gmlp_w_s, gmlp_b_s, conv_w, conv_b,
                   lru_w_a, lru_b_a, lru_w_x, lru_b_x, lru_lambda, w_out, norm2_g,
                   router_group_w, router_group_b, router_expert_w, router_expert_b):
    mask = jnp.tril(jnp.ones((length, length), dtype=bool))
    ws = jnp.where(mask, gmlp_w_s[l][:, :length, :length], 0.0)
    eye = jnp.eye(reps, dtype=ws.dtype)
    wsm = (eye[None, :, None, :, None] * ws[:, None, :, None, :]).reshape(A_GROUPS, reps * length, reps * length)
    bs = jnp.tile(gmlp_b_s[l][:, :length], (1, reps))
    bsx = jnp.repeat(bs.T, A_GROUP_DIM, axis=1)
    wr = jnp.concatenate([router_expert_w[l].T, router_group_w[l].T,
                          jnp.zeros((ROUTER_ROWS - N_EXPERTS - N_GROUPS, D_MODEL), F32)], axis=0)
    br = jnp.concatenate([router_expert_b[l], router_group_b[l],
                          jnp.zeros((ROUTER_ROWS - N_EXPERTS - N_GROUPS,), F32)]).reshape(ROUTER_ROWS, 1)
    return [
        norm1_g[l].reshape(1, D_MODEL), w_in[l].astype(BF16),
        gmlp_ln_g[l].reshape(1, D_A), gmlp_ln_b[l].reshape(1, D_A), wsm.astype(BF16), bsx,
        conv_w[l], conv_b[l].reshape(1, D_B),
        _head_blocks(lru_w_a[l]), lru_b_a[l].reshape(1, D_B),
        _head_blocks(lru_w_x[l]), lru_b_x[l].reshape(1, D_B),
        lru_lambda[l].reshape(1, D_B), w_out[l].astype(BF16), norm2_g[l].reshape(1, D_MODEL),
        wr.astype(BF16), br,
    ]


def _routing_tables(cnt, n_items_max):
    n_tiles = cnt.shape[0]
    seg = (cnt + SUBLANES - 1) // SUBLANES * SUBLANES
    nch = seg // SUBLANES
    tot = jnp.sum(seg, axis=0)
    padded = (tot + FFN_UNIT - 1) // FFN_UNIT * FFN_UNIT
    e_before = jnp.arange(N_EXPERTS)[None, :] < jnp.arange(N_EXPERTS)[:, None]
    t_before = jnp.arange(n_tiles)[None, :] < jnp.arange(n_tiles)[:, None]
    pad_start = jnp.sum(jnp.where(e_before, padded[None, :], 0), axis=1)
    pad_end = pad_start + padded
    doff = pad_start[None, :] + jnp.sum(jnp.where(t_before[:, :, None], seg[None, :, :], 0), axis=1)
    locoff = jnp.sum(jnp.where(e_before[None, :, :], seg[:, None, :], 0), axis=2)
    ntot = jnp.sum(nch, axis=1)
    whole = nch // COPY_CLASSES[0][0]
    done = whole * COPY_CLASSES[0][0] * SUBLANES
    nops, olocs, odsts = [], [], []
    for chunks, cap in COPY_CLASSES:
        if chunks == COPY_CLASSES[0][0]:
            m, loc0, dst0 = whole, locoff, doff
        else:
            m, loc0, dst0 = (nch - whole * COPY_CLASSES[0][0] == chunks).astype(jnp.int32), locoff + done, doff + done
        start = jnp.sum(jnp.where(e_before[None, :, :], m[:, None, :], 0), axis=2)
        s = jnp.arange(cap)
        owner = jnp.sum((start + m)[:, None, :] <= s[None, :, None], axis=2)
        own = owner[:, :, None] == jnp.arange(N_EXPERTS)[None, None, :]
        step = chunks * SUBLANES * (s[None, :, None] - start[:, None, :])
        olocs.append(jnp.sum(jnp.where(own, loc0[:, None, :] + step, 0), axis=2).reshape(-1))
        odsts.append(jnp.sum(jnp.where(own, dst0[:, None, :] + step, 0), axis=2).reshape(-1))
        nops.append(jnp.sum(m, axis=1))
    tail = pad_start + tot
    ntail = (padded - tot) // SUBLANES
    units = padded // FFN_UNIT
    n_it = (units + FFN_ITEM_UNITS - 1) // FFN_ITEM_UNITS
    it_start = jnp.sum(jnp.where(e_before, n_it[None, :], 0), axis=1)
    t = jnp.arange(n_items_max)
    e_ids = jnp.arange(N_EXPERTS)
    owner = jnp.minimum(jnp.sum((it_start + n_it)[None, :] <= t[:, None], axis=1), N_EXPERTS - 1)
    own = owner[:, None] == e_ids[None, :]
    pick = lambda v: jnp.sum(jnp.where(own, v[None, :], 0), axis=1)
    j = t - pick(it_start)
    istart = pick(pad_start) + j * FFN_BLOCK
    iunits = jnp.clip(pick(units) - FFN_ITEM_UNITS * j, 1, FFN_ITEM_UNITS)
    ordinal = jnp.sum(jnp.where(e_before, (n_it > 0).astype(jnp.int32)[None, :], 0), axis=1)
    later = (e_ids[None, :] > e_ids[:, None]) & (n_it > 0)[None, :]
    nxt = jnp.min(jnp.where(later, e_ids[None, :], N_EXPERTS), axis=1)
    nxt = jnp.where(nxt == N_EXPERTS, -1, nxt)
    i32 = lambda a: a.astype(jnp.int32)
    items = (i32(jnp.sum(n_it).reshape(1)), i32(istart), i32(iunits), i32(owner), i32(j == 0),
             i32(pick(ordinal) % 2), i32(pick(nxt)))
    ops = (i32(jnp.concatenate(nops)), i32(jnp.concatenate(olocs)), i32(jnp.concatenate(odsts)))
    return ops, i32(ntot), i32(tail), i32(ntail), locoff.astype(F32)[:, :, None], items


def kernel(x_prompt, x_sample, state_conv, state_rglru, norm1_g, w_in, gmlp_ln_g, gmlp_ln_b, gmlp_w_s, gmlp_b_s,
           conv_w, conv_b, lru_w_a, lru_b_a, lru_w_x, lru_b_x, lru_lambda, w_out, norm2_g,
           router_group_w, router_group_b, router_expert_w, router_expert_b,
           expert_w_gate, expert_w_up, expert_w_down, final_norm_g):
    depth = w_in.shape[0]
    nb, t_len, _ = x_prompt.shape
    db, dt, _ = x_sample.shape
    n_prompt = nb * t_len
    n_sample = db * dt
    assert t_len % PROMPT_TILE == 0 and n_prompt % TOKEN_TILE == 0 and n_sample == TOKEN_TILE
    n_total = n_prompt + n_sample
    assert n_total % STEP_TOKENS == 0
    n_tiles = n_total // TOKEN_TILE
    p_tiles = n_prompt // TOKEN_TILE
    p_rows = 2 * n_total + (SUBLANES - 1) * N_EXPERTS * n_tiles + N_EXPERTS * (FFN_UNIT - 1)
    p_rows = -(-p_rows // FFN_UNIT) * FFN_UNIT
    n_items_max = p_rows // FFN_BLOCK + N_EXPERTS
    upper = jnp.triu(jnp.ones((TOKEN_TILE, TOKEN_TILE), BF16), 1)

    xp, xs = x_prompt, x_sample
    conv_p, h_p, v_s, conv_s, h_s = [], [], [], [], []
    for l in range(depth):
        lw = (norm1_g, w_in, gmlp_ln_g, gmlp_ln_b, gmlp_w_s, gmlp_b_s, conv_w, conv_b, lru_w_a, lru_b_a,
              lru_w_x, lru_b_x, lru_lambda, w_out, norm2_g, router_group_w, router_group_b,
              router_expert_w, router_expert_b)
        wts_p = _mixer_weights(l, GMLP_CHUNK, 1, *lw)
        wts_s = _mixer_weights(l, dt, db, *lw)
        zero_conv = jnp.zeros((nb, CONV_WIDTH - 1, D_B), F32)
        zero_h = jnp.zeros((nb, 1, D_B), F32)
        y1, h2, route, cnt_p, cp, hp = _mixer_call(
            xp, zero_conv, zero_h, wts_p, n_total=n_total, row_offset=0, sb=1, tt=PROMPT_TILE, chunk=GMLP_CHUNK, seg_len=PROMPT_SEG, seg_stride=PROMPT_SEG + SUBLANES, chain=True, emit_vn=False)
        y1, h2, route, cnt_s, cs, hs, vs = _mixer_call(
            xs, state_conv[l], state_rglru[l][None], wts_s, n_total=n_total, row_offset=n_prompt, sb=db,
            tt=dt, chunk=db * dt, seg_len=dt, seg_stride=dt + SUBLANES, chain=False, emit_vn=True,
            alias_in=(y1, h2, route))
        conv_p.append(cp)
        h_p.append(hp[:, 0])
        v_s.append(vs)
        conv_s.append(cs)
        h_s.append(hs[0])

        sub = PROMPT_TILE // TOKEN_TILE
        cnt = jnp.concatenate([
            jnp.swapaxes(cnt_p[:, :, :sub], 1, 2).reshape(p_tiles, N_EXPERTS),
            cnt_s[:, :, 0]], axis=0).astype(jnp.int32)
        ops, ntot, tail, ntail, locoff, items = _routing_tables(cnt, n_items_max)

        xb, slots = _dispatch_call(h2, route, locoff, upper, ops, ntot, tail, ntail, n_tiles=n_tiles,
                                   p_rows=p_rows)
        yb = _ffn_call(xb, expert_w_gate[l], expert_w_up[l], expert_w_down[l], items, p_rows=p_rows)
        assert l == depth - 1, "deeper stacks need an un-normalised combine between layers"
        fg = final_norm_g.reshape(1, D_MODEL)
        yp, ysm = _combine_call(y1, slots, fg, yb, ops, ntot, n_tiles=n_tiles, n_prompt=n_prompt,
                                n_sample=n_sample)
        xp = yp.reshape(nb, t_len, D_MODEL)
        xs = ysm.reshape(db, dt, D_MODEL)

    return (xp, xs, jnp.stack(conv_p), jnp.stack(h_p), jnp.stack(v_s), jnp.stack(conv_s), jnp.stack(h_s))
```

```python
import functools

import jax
import jax.numpy as jnp
from jax import lax
from jax.experimental import pallas as pl
from jax.experimental.pallas import tpu as pltpu

D_MODEL = 1024
D_A = 512
D_B = 512
A_GROUPS = 4
A_GROUP_DIM = 128
GMLP_CHUNK = 128
B_HEADS = 8
CONV_WIDTH = 4
LRU_C = 8.0
N_GROUPS = 4
EXPERTS_PER_GROUP = 8
N_EXPERTS = 32
D_EXPERT = 512
EPS = 1e-6
TINY = 1e-30

SUBLANES = 8
LANES = 128
ROUTER_ROWS = 40
TOKEN_TILE = 256
PROMPT_TILE = 1024
PROMPT_SEG = PROMPT_TILE // SUBLANES
FFN_UNIT = 256
FFN_ITEM_UNITS = 4
FFN_BLOCK = FFN_ITEM_UNITS * FFN_UNIT
ZERO_CHUNKS = 8
ZERO_ROWS = ZERO_CHUNKS * SUBLANES
SLOT_ROWS = 2 * TOKEN_TILE + N_EXPERTS * SUBLANES
TILE_CHUNKS = SLOT_ROWS // SUBLANES
COPY_CLASSES = ((4, TILE_CHUNKS // 4), (3, N_EXPERTS), (2, N_EXPERTS), (1, N_EXPERTS))
TILES_PER_STEP = 3
STEP_TOKENS = TILES_PER_STEP * TOKEN_TILE
PACK_COLS = D_MODEL // 2
XB_COLS = PACK_COLS + LANES
U32 = jnp.uint32
HI_MASK = 0xFFFF0000
VMEM_BYTES_V7X = 64 * 1024 * 1024
VMEM_LIMIT = VMEM_BYTES_V7X - 8 * 1024 * 1024

BF16 = jnp.bfloat16
F32 = jnp.float32


def _rms(x, g):
    return x * lax.rsqrt(jnp.mean(x * x, axis=-1, keepdims=True) + EPS) * g


def _dot(a, b):
    return jnp.dot(a, b, preferred_element_type=F32)


def _sigmoid(x):
    return 1.0 / (1.0 + jnp.exp(-x))


def _pack_bf16_pairs(x):
    bits = lax.bitcast_convert_type(x, U32)
    return (bits[:, PACK_COLS:] & U32(HI_MASK)) | (bits[:, :PACK_COLS] >> 16)


def _unpack_bf16_pairs(w):
    lo = lax.bitcast_convert_type(w << 16, F32).astype(BF16)
    hi = lax.bitcast_convert_type(w & U32(HI_MASK), F32).astype(BF16)
    return jnp.concatenate([lo, hi], axis=1)


def _mixer_kernel(x_ref, hist_ref, h0_ref, n1g_ref, win_ref, lng_ref, lnb_ref, wsm_ref, bsx_ref,
                  cw_ref, cb_ref, wa_ref, ba_ref, wx_ref, bx_ref, lam_ref, wout_ref, n2g_ref,
                  wr_ref, br_ref, *rest, sb, tt, chunk, seg_len, seg_stride, chain, emit_vn, aliased):
    if aliased:
        rest = rest[3:]
    if emit_vn:
        (y1_ref, h2_ref, route_ref, cnt_ref, conv_ref, hlast_ref, vn_ref,
         xs_ref, sa_ref, sb_ref, mix_ref, hc_ref) = rest
    else:
        (y1_ref, h2_ref, route_ref, cnt_ref, conv_ref, hlast_ref,
         xs_ref, sa_ref, sb_ref, mix_ref, hc_ref) = rest
        vn_ref = None
    t_idx = pl.program_id(1)

    @pl.when(t_idx == 0)
    def _():
        xs_ref[:, SUBLANES - 3:SUBLANES, :] = hist_ref[...]
        hc_ref[...] = jnp.zeros_like(hc_ref)

    @pl.when(t_idx != 0)
    def _():
        xs_ref[:, SUBLANES - 3:SUBLANES, :] = xs_ref[:, SUBLANES + tt - 3:SUBLANES + tt, :]

    rows = sb * tt
    n_seg = rows // seg_len
    half = D_B // 2

    x = x_ref[...].reshape(rows, D_MODEL)
    hb = _rms(x, n1g_ref[...]).astype(BF16)

    v_a = _dot(hb, win_ref[:, D_A:2 * D_A])
    x_b = _dot(hb, win_ref[:, 2 * D_A:2 * D_A + D_B])
    mixed = []
    for g in range(A_GROUPS):
        cols = slice(g * A_GROUP_DIM, (g + 1) * A_GROUP_DIM)
        vg = v_a[:, cols]
        mu = jnp.mean(vg, axis=-1, keepdims=True)
        dv = vg - mu
        var = jnp.mean(dv * dv, axis=-1, keepdims=True)
        vn = dv * lax.rsqrt(var + EPS) * lng_ref[:, cols] + lnb_ref[:, cols]
        if vn_ref is not None:
            vn_ref[:, :, cols] = vn.reshape(sb, tt, A_GROUP_DIM)
        vnb = vn.astype(BF16)
        mixed.append([_dot(wsm_ref[g], vnb[c * chunk:(c + 1) * chunk]) for c in range(rows // chunk)])
    u_a = _dot(hb, win_ref[:, 0:D_A])

    xs_ref[:, SUBLANES:SUBLANES + tt, :] = x_b.reshape(sb, tt, D_B)
    xc = cb_ref[...] + x_b * cw_ref[3:4, :]
    for j in range(1, CONV_WIDTH):
        shifted = xs_ref[:, SUBLANES - j:SUBLANES - j + tt, :].reshape(rows, D_B)
        xc = xc + shifted * cw_ref[3 - j:4 - j, :]

    xcb = xc.astype(BF16)
    r_lin = jnp.concatenate([_dot(xcb[:, :half], wa_ref[0]), _dot(xcb[:, half:], wa_ref[1])], axis=1)
    i_lin = jnp.concatenate([_dot(xcb[:, :half], wx_ref[0]), _dot(xcb[:, half:], wx_ref[1])], axis=1)

    for g in range(A_GROUPS):
        cols = slice(g * A_GROUP_DIM, (g + 1) * A_GROUP_DIM)
        for c in range(rows // chunk):
            rs = slice(c * chunk, (c + 1) * chunk)
            s = mixed[g][c] + bsx_ref[:, cols]
            mix_ref[rs, cols] = (u_a[rs, cols] * s).astype(BF16)

    g_b = _dot(hb, win_ref[:, 2 * D_A + D_B:])
    r = _sigmoid(r_lin + ba_ref[...])
    i_g = _sigmoid(i_lin + bx_ref[...])
    log_a = (-LRU_C * r) * jax.nn.softplus(-lam_ref[...])
    a = jnp.exp(log_a)
    y_gain = 1.0 - a * a
    gain = y_gain * lax.rsqrt(jnp.maximum(y_gain, TINY))
    bterm = gain * i_g * xc

    n_lb = D_B // LANES
    for j in range(n_seg):
        for k in range(n_lb):
            lc = slice(k * LANES, (k + 1) * LANES)
            dst = slice(j * seg_stride, j * seg_stride + seg_len)
            sa_ref[k, dst, :] = a[j * seg_len:(j + 1) * seg_len, lc]
            sb_ref[k, dst, :] = bterm[j * seg_len:(j + 1) * seg_len, lc]

    y_a = x + _dot(mix_ref[:, 0:D_A], wout_ref[0:D_A, :])
    gel = jax.nn.gelu(g_b)

    for grp in range(n_seg // SUBLANES):
        base = grp * SUBLANES * seg_stride
        bs = slice(grp * SUBLANES, (grp + 1) * SUBLANES)
        if chain:
            h_init = tuple(jnp.zeros((SUBLANES, LANES), F32) for _ in range(n_lb))
        else:
            h_init = tuple(h0_ref[0, bs, k * LANES:(k + 1) * LANES] for k in range(n_lb))
        a_init = tuple(jnp.ones((SUBLANES, LANES), F32) for _ in range(n_lb))

        def step(i, carry, base=base):
            hs, acs = carry
            idx = pl.ds(base + i, SUBLANES, stride=seg_stride)
            new_h, new_a = [], []
            for k in range(n_lb):
                av = sa_ref[k, idx, :]
                h = av * hs[k] + sb_ref[k, idx, :]
                sb_ref[k, idx, :] = h
                new_h.append(h)
                if chain:
                    ac = av * acs[k]
                    sa_ref[k, idx, :] = ac
                    new_a.append(ac)
                else:
                    new_a.append(acs[k])
            return tuple(new_h), tuple(new_a)

        carry = (h_init, a_init)
        for i in range(seg_len):
            carry = step(i, carry)
        h_end, a_end = carry

        for k in range(n_lb):
            lc = slice(k * LANES, (k + 1) * LANES)
            mc = slice(D_A + k * LANES, D_A + (k + 1) * LANES)
            if chain:
                h_in = hc_ref[:, lc]
                for j in range(SUBLANES):
                    seg = slice(base + j * seg_stride, base + j * seg_stride + seg_len)
                    rs = slice((grp * SUBLANES + j) * seg_len, (grp * SUBLANES + j + 1) * seg_len)
                    h_seg = sb_ref[k, seg, :] + sa_ref[k, seg, :] * h_in
                    mix_ref[rs, mc] = (h_seg * gel[rs, lc]).astype(BF16)
                    h_in = h_end[k][j:j + 1] + a_end[k][j:j + 1] * h_in
                hc_ref[:, lc] = h_in
                hlast_ref[0, :, lc] = h_in
            else:
                for j in range(SUBLANES):
                    seg = slice(base + j * seg_stride, base + j * seg_stride + seg_len)
                    rs = slice((grp * SUBLANES + j) * seg_len, (grp * SUBLANES + j + 1) * seg_len)
                    mix_ref[rs, mc] = (sb_ref[k, seg, :] * gel[rs, lc]).astype(BF16)
                hlast_ref[0, bs, lc] = h_end[k]

    y1 = y_a + _dot(mix_ref[:, D_A:], wout_ref[D_A:, :])
    y1_ref[...] = y1
    h2 = _rms(y1, n2g_ref[...]).astype(BF16)
    h2_ref[...] = h2

    logits = lax.dot_general(wr_ref[...], h2, (((1,), (1,)), ((), ())), preferred_element_type=F32) + br_ref[...]
    gl = [logits[N_EXPERTS + k:N_EXPERTS + k + 1, :] for k in range(N_GROUPS)]
    gmax = jnp.maximum(jnp.maximum(gl[0], gl[1]), jnp.maximum(gl[2], gl[3]))
    grp_idx = jnp.where(gl[0] == gmax, 0, jnp.where(gl[1] == gmax, 1, jnp.where(gl[2] == gmax, 2, 3)))
    gsum = (jnp.exp(gl[0] - gmax) + jnp.exp(gl[1] - gmax)) + (jnp.exp(gl[2] - gmax) + jnp.exp(gl[3] - gmax))
    p_grp = 1.0 / gsum
    e_in = jnp.where(grp_idx == 0, logits[0:8],
                     jnp.where(grp_idx == 1, logits[8:16], jnp.where(grp_idx == 2, logits[16:24], logits[24:32])))
    iota8 = lax.broadcasted_iota(jnp.int32, (EXPERTS_PER_GROUP, rows), 0)
    m1 = jnp.max(e_in, axis=0, keepdims=True)
    i1 = jnp.min(jnp.where(e_in == m1, iota8, EXPERTS_PER_GROUP), axis=0, keepdims=True)
    e_rest = jnp.where(iota8 == i1, -jnp.inf, e_in)
    m2 = jnp.max(e_rest, axis=0, keepdims=True)
    i2 = jnp.min(jnp.where(e_rest == m2, iota8, EXPERTS_PER_GROUP), axis=0, keepdims=True)
    t2 = jnp.exp(m2 - m1)
    den = 1.0 + t2
    gate0 = p_grp * (1.0 / den)
    gate1 = p_grp * (t2 / den)
    e0 = (grp_idx * EXPERTS_PER_GROUP + i1).astype(F32)
    e1 = (grp_idx * EXPERTS_PER_GROUP + i2).astype(F32)
    route_ref[...] = jnp.where(iota8 == 0, e0, jnp.where(iota8 == 1, e1, jnp.where(
        iota8 == 2, gate0, jnp.where(iota8 == 3, gate1, 0.0))))

    iota_e = lax.broadcasted_iota(jnp.int32, (N_EXPERTS, rows), 0).astype(F32)
    onehot = (iota_e == e0).astype(F32) + (iota_e == e1).astype(F32)
    lane = lax.broadcasted_iota(jnp.int32, (N_EXPERTS, LANES), 1)
    cnt = jnp.zeros((N_EXPERTS, LANES), F32)
    for s in range(rows // TOKEN_TILE):
        c_s = jnp.sum(onehot[:, s * TOKEN_TILE:(s + 1) * TOKEN_TILE], axis=1, keepdims=True)
        cnt = cnt + jnp.where(lane == s, c_s, 0.0)
    cnt_ref[...] = cnt.reshape(1, N_EXPERTS, LANES)
    conv_ref[...] = xs_ref[:, SUBLANES + tt - 3:SUBLANES + tt, :]


def _mixer_call(x, hist, h0, wts, *, n_total, row_offset, sb, tt, chunk, seg_len, seg_stride, chain,
                emit_vn, alias_in=None):
    nb, t_len, _ = x.shape
    nbb = nb // sb
    ntt = t_len // tt
    rows = sb * tt
    blk0 = row_offset // rows
    n_seg = rows // seg_len

    def full(arr):
        nd = arr.ndim
        return pl.BlockSpec(arr.shape, lambda b, t, _nd=nd: (0,) * _nd)

    in_specs = [
        pl.BlockSpec((sb, tt, D_MODEL), lambda b, t: (b, t, 0)),
        pl.BlockSpec((sb, CONV_WIDTH - 1, D_B), lambda b, t: (b, 0, 0)),
        pl.BlockSpec((1, sb, D_B), lambda b, t: (b, 0, 0)),
    ] + [full(w) for w in wts]
    args = [x, hist, h0] + list(wts)
    io_alias = {}
    if alias_in is not None:
        for k, arr in enumerate(alias_in):
            io_alias[len(args)] = k
            in_specs.append(pl.BlockSpec(memory_space=pl.ANY))
            args.append(arr)

    out_shape = [
        jax.ShapeDtypeStruct((n_total, D_MODEL), F32),
        jax.ShapeDtypeStruct((n_total, D_MODEL), BF16),
        jax.ShapeDtypeStruct((SUBLANES, n_total), F32),
        jax.ShapeDtypeStruct((nbb * ntt, N_EXPERTS, LANES), F32),
        jax.ShapeDtypeStruct((nb, CONV_WIDTH - 1, D_B), F32),
        jax.ShapeDtypeStruct((nbb, sb, D_B), F32),
    ]
    out_specs = [
        pl.BlockSpec((rows, D_MODEL), lambda b, t: (blk0 + b * ntt + t, 0)),
        pl.BlockSpec((rows, D_MODEL), lambda b, t: (blk0 + b * ntt + t, 0)),
        pl.BlockSpec((SUBLANES, rows), lambda b, t: (0, blk0 + b * ntt + t)),
        pl.BlockSpec((1, N_EXPERTS, LANES), lambda b, t: (b * ntt + t, 0, 0)),
        pl.BlockSpec((sb, CONV_WIDTH - 1, D_B), lambda b, t: (b, 0, 0)),
        pl.BlockSpec((1, sb, D_B), lambda b, t: (b, 0, 0)),
    ]
    if emit_vn:
        out_shape.append(jax.ShapeDtypeStruct((nb, t_len, D_A), F32))
        out_specs.append(pl.BlockSpec((sb, tt, D_A), lambda b, t: (b, t, 0)))

    kern = functools.partial(_mixer_kernel, sb=sb, tt=tt, chunk=chunk, seg_len=seg_len, seg_stride=seg_stride,
                             chain=chain, emit_vn=emit_vn, aliased=alias_in is not None)
    return pl.pallas_call(
        kern,
        grid=(nbb, ntt),
        in_specs=in_specs,
        out_specs=out_specs,
        out_shape=out_shape,
        scratch_shapes=[
            pltpu.VMEM((sb, SUBLANES + tt, D_B), F32),
            pltpu.VMEM((D_B // LANES, n_seg * seg_stride, LANES), F32),
            pltpu.VMEM((D_B // LANES, n_seg * seg_stride, LANES), F32),
            pltpu.VMEM((rows, D_MODEL), BF16),
            pltpu.VMEM((1, D_B), F32),
        ],
        input_output_aliases=io_alias,
        compiler_params=pltpu.CompilerParams(dimension_semantics=("arbitrary", "arbitrary"),
                                             vmem_limit_bytes=VMEM_LIMIT),
        name="mixer_chain" if chain else "mixer_step",
    )(*args)


def _tile_slots(route_ref, locoff_ref, upper_ref, u):
    lanes = slice(u * TOKEN_TILE, (u + 1) * TOKEN_TILE)
    e0 = route_ref[0:1, lanes]
    e1 = route_ref[1:2, lanes]
    iota_e = lax.broadcasted_iota(jnp.int32, (N_EXPERTS, TOKEN_TILE), 0).astype(F32)
    oh0 = (iota_e == e0).astype(F32)
    oh1 = (iota_e == e1).astype(F32)
    c0 = _dot(oh0.astype(BF16), upper_ref[...])
    c1 = _dot(oh1.astype(BF16), upper_ref[...])
    cnt0 = jnp.sum(oh0, axis=1, keepdims=True)
    base0 = locoff_ref[u]
    base1 = base0 + cnt0
    p0 = jnp.sum(oh0 * (base0 + c0), axis=0, keepdims=True)
    p1 = jnp.sum(oh1 * (base1 + c1), axis=0, keepdims=True)
    return p0.astype(jnp.int32), p1.astype(jnp.int32)


def _copy_ops(tile, n_tiles, nops_ref, oloc_ref, odst_ref, make_copy):
    base = 0
    for k, (chunks, cap) in enumerate(COPY_CLASSES):
        n = nops_ref[k * n_tiles + tile]

        def per_pair(i, _, chunks=chunks, cap=cap, base=base, n=n):
            idx = base + tile * cap + 2 * i
            make_copy(pl.multiple_of(oloc_ref[idx], SUBLANES), pl.multiple_of(odst_ref[idx], SUBLANES),
                      chunks * SUBLANES)

            @pl.when(2 * i + 1 < n)
            def _():
                make_copy(pl.multiple_of(oloc_ref[idx + 1], SUBLANES), pl.multiple_of(odst_ref[idx + 1], SUBLANES),
                          chunks * SUBLANES)
            return 0

        assert cap % 2 == 0
        lax.fori_loop(0, (n + 1) // 2, per_pair, 0)
        base += n_tiles * cap


def _dispatch_kernel(nops_ref, oloc_ref, odst_ref, ntot_ref, tail_ref, ntail_ref,
                     h2_ref, route_ref, locoff_ref, upper_ref, xb_ref, slots_ref, buf_ref, zero_ref, sem, zsem,
                     *, n_tiles):
    i = pl.program_id(0)
    n_steps = pl.num_programs(0)
    slot = i % 2

    def chunk_copy(s, u, loc, dst, rows=SUBLANES):
        return pltpu.make_async_copy(buf_ref.at[s, u, pl.ds(loc, rows), :],
                                     xb_ref.at[pl.ds(dst, rows), :], sem.at[s])

    def wait_step(step, s):
        for u in range(TILES_PER_STEP):
            n_rows = ntot_ref[step * TILES_PER_STEP + u] * SUBLANES

            @pl.when(n_rows > 0)
            def _():
                pltpu.make_async_copy(buf_ref.at[s, u, pl.ds(0, n_rows), :], xb_ref.at[pl.ds(0, n_rows), :],
                                      sem.at[s]).wait()

    @pl.when(i >= 2)
    def _():
        wait_step(i - 2, slot)

    iota_s = lax.broadcasted_iota(jnp.int32, (SLOT_ROWS, TOKEN_TILE), 0)
    iota8 = lax.broadcasted_iota(jnp.int32, (SUBLANES, TOKEN_TILE), 0)
    for u in range(TILES_PER_STEP):
        lanes = slice(u * TOKEN_TILE, (u + 1) * TOKEN_TILE)
        p0, p1 = _tile_slots(route_ref, locoff_ref, upper_ref, u)
        slots_ref[:, lanes] = jnp.where(iota8 == 0, p0, jnp.where(iota8 == 1, p1, 0))
        hit0 = iota_s == p0
        hit1 = iota_s == p1
        perm = jnp.where(hit0 | hit1, 1.0, 0.0).astype(BF16)
        sorted_rows = _dot(perm, h2_ref[u * TOKEN_TILE:(u + 1) * TOKEN_TILE, :])
        buf_ref[slot, u, :, 0:PACK_COLS] = _pack_bf16_pairs(sorted_rows)
        gates = jnp.where(hit0, route_ref[2:3, lanes], 0.0) + jnp.where(hit1, route_ref[3:4, lanes], 0.0)
        gcol = jnp.sum(gates, axis=1, keepdims=True)
        buf_ref[slot, u, :, PACK_COLS:XB_COLS] = lax.bitcast_convert_type(
            jnp.broadcast_to(gcol, (SLOT_ROWS, LANES)), U32)

    for u in range(TILES_PER_STEP):
        _copy_ops(i * TILES_PER_STEP + u, n_tiles, nops_ref, oloc_ref, odst_ref,
                  lambda loc, dst, rows, u=u: chunk_copy(slot, u, loc, dst, rows).start())

    @pl.when(i == n_steps - 1)
    def _():
        zero_ref[...] = jnp.zeros_like(zero_ref)

        def zero_copy(dst, rows, k):
            return pltpu.make_async_copy(zero_ref.at[pl.ds(0, rows), :], xb_ref.at[pl.ds(dst, rows), :], zsem.at[k])

        def per_expert(e, tot):
            n_big, n_small = tot
            n = ntail_ref[e]
            d0 = tail_ref[e]
            nb = n // ZERO_CHUNKS
            ns = n - nb * ZERO_CHUNKS

            def big(c, _):
                zero_copy(pl.multiple_of(d0 + c * ZERO_ROWS, SUBLANES), ZERO_ROWS, 0).start()
                return 0

            def small(c, _):
                zero_copy(pl.multiple_of(d0 + nb * ZERO_ROWS + c * SUBLANES, SUBLANES), SUBLANES, 1).start()
                return 0

            lax.fori_loop(0, nb, big, 0)
            lax.fori_loop(0, ns, small, 0)
            return n_big + nb, n_small + ns

        n_big, n_small = lax.fori_loop(0, N_EXPERTS, per_expert, (0, 0))

        @pl.when(i >= 1)
        def _():
            wait_step(i - 1, 1 - slot)

        wait_step(i, slot)

        def wait_big(c, _):
            zero_copy(0, ZERO_ROWS, 0).wait()
            return 0

        def wait_small(c, _):
            zero_copy(0, SUBLANES, 1).wait()
            return 0

        lax.fori_loop(0, n_big, wait_big, 0)
        lax.fori_loop(0, n_small, wait_small, 0)


def _dispatch_call(h2, route, locoff, upper, ops, ntot, tail, ntail, *, n_tiles, p_rows):
    grid_spec = pltpu.PrefetchScalarGridSpec(
        num_scalar_prefetch=6,
        grid=(n_tiles // TILES_PER_STEP,),
        in_specs=[
            pl.BlockSpec((STEP_TOKENS, D_MODEL), lambda i, *_: (i, 0)),
            pl.BlockSpec((SUBLANES, STEP_TOKENS), lambda i, *_: (0, i)),
            pl.BlockSpec((TILES_PER_STEP, N_EXPERTS, 1), lambda i, *_: (i, 0, 0)),
            pl.BlockSpec((TOKEN_TILE, TOKEN_TILE), lambda i, *_: (0, 0)),
        ],
        out_specs=[pl.BlockSpec(memory_space=pl.ANY),
                   pl.BlockSpec((SUBLANES, STEP_TOKENS), lambda i, *_: (0, i))],
        scratch_shapes=[
            pltpu.VMEM((2, TILES_PER_STEP, SLOT_ROWS, XB_COLS), U32),
            pltpu.VMEM((ZERO_ROWS, XB_COLS), U32),
            pltpu.SemaphoreType.DMA((2,)),
            pltpu.SemaphoreType.DMA((2,)),
        ],
    )
    return pl.pallas_call(
        functools.partial(_dispatch_kernel, n_tiles=n_tiles),
        grid_spec=grid_spec,
        out_shape=[jax.ShapeDtypeStruct((p_rows, XB_COLS), U32),
                   jax.ShapeDtypeStruct((SUBLANES, n_tiles * TOKEN_TILE), jnp.int32)],
        compiler_params=pltpu.CompilerParams(dimension_semantics=("arbitrary",), vmem_limit_bytes=VMEM_LIMIT),
        name="dispatch",
    )(*ops, ntot, tail, ntail, h2, route, locoff, upper)


def _ffn_kernel(nitems_ref, istart_ref, iunits_ref, iexp_ref, ifirst_ref, iwslot_ref, inext_ref,
                xb_ref, wg_ref, wu_ref, wd_ref, yb_ref,
                xin_ref, yout_ref, wgf_ref, wuf_ref, wdf_ref, wgb_ref, wub_ref, wdb_ref, sem_in, sem_out, sem_w):
    n_items = nitems_ref[0]
    sizes = tuple(k * FFN_UNIT for k in range(FFN_ITEM_UNITS, 0, -1))

    def rows_in(t, s, rows):
        start = pl.multiple_of(istart_ref[t], FFN_UNIT)
        return pltpu.make_async_copy(xb_ref.at[pl.ds(start, rows), :], xin_ref.at[s, pl.ds(0, rows), :], sem_in.at[s])

    def rows_out(t, s, rows):
        start = pl.multiple_of(istart_ref[t], FFN_UNIT)
        return pltpu.make_async_copy(yout_ref.at[s, pl.ds(0, rows), :], yb_ref.at[pl.ds(start, rows), :],
                                     sem_out.at[s])

    def start_by_size(make, t, s):
        for rows in sizes:
            @pl.when(iunits_ref[t] * FFN_UNIT == rows)
            def _():
                make(t, s, rows).start()

    def weight_copies(e, ws):
        return (pltpu.make_async_copy(wg_ref.at[e], wgf_ref.at[ws], sem_w.at[ws]),
                pltpu.make_async_copy(wu_ref.at[e], wuf_ref.at[ws], sem_w.at[ws]),
                pltpu.make_async_copy(wd_ref.at[e], wdf_ref.at[ws], sem_w.at[ws]))

    def compute(s, rows):
        xb = _unpack_bf16_pairs(xin_ref[s, 0:rows, 0:PACK_COLS])
        gate = lax.bitcast_convert_type(xin_ref[s, 0:rows, PACK_COLS:PACK_COLS + 1], F32)
        a = _dot(xb, wgb_ref[...])
        u = _dot(xb, wub_ref[...])
        mid = (a * _sigmoid(a) * u).astype(BF16)
        y = _dot(mid, wdb_ref[...]) * gate
        yout_ref[s, 0:rows, :] = _pack_bf16_pairs(y.astype(BF16).astype(F32))

    @pl.when(n_items > 0)
    def _():
        for c in weight_copies(iexp_ref[0], 0):
            c.start()
        start_by_size(rows_in, 0, 0)

    def per_item(t, _):
        s = t % 2

        @pl.when(t + 1 < n_items)
        def _():
            start_by_size(rows_in, t + 1, 1 - s)

        @pl.when(ifirst_ref[t] == 1)
        def _():
            ws = iwslot_ref[t]
            for c in weight_copies(0, ws):
                c.wait()

            @pl.when(inext_ref[t] >= 0)
            def _():
                for c in weight_copies(inext_ref[t], 1 - ws):
                    c.start()

            wgb_ref[...] = wgf_ref[ws].astype(BF16)
            wub_ref[...] = wuf_ref[ws].astype(BF16)
            wdb_ref[...] = wdf_ref[ws].astype(BF16)

        rows_in(t, s, iunits_ref[t] * FFN_UNIT).wait()

        @pl.when(t >= 2)
        def _():
            rows_out(t - 2, s, iunits_ref[t - 2] * FFN_UNIT).wait()

        for rows in sizes:
            @pl.when(iunits_ref[t] * FFN_UNIT == rows)
            def _():
                compute(s, rows)

        start_by_size(rows_out, t, s)
        return 0

    lax.fori_loop(0, n_items, per_item, 0)

    for back in (2, 1):
        @pl.when(n_items >= back)
        def _():
            t = n_items - back
            rows_out(t, t % 2, iunits_ref[t] * FFN_UNIT).wait()


def _ffn_call(xb, wg, wu, wd, items, *, p_rows):
    any_spec = pl.BlockSpec(memory_space=pl.ANY)
    grid_spec = pltpu.PrefetchScalarGridSpec(
        num_scalar_prefetch=len(items),
        grid=(1,),
        in_specs=[any_spec, any_spec, any_spec, any_spec],
        out_specs=any_spec,
        scratch_shapes=[
            pltpu.VMEM((2, FFN_BLOCK, XB_COLS), U32),
            pltpu.VMEM((2, FFN_BLOCK, PACK_COLS), U32),
            pltpu.VMEM((2, D_MODEL, D_EXPERT), F32),
            pltpu.VMEM((2, D_MODEL, D_EXPERT), F32),
            pltpu.VMEM((2, D_EXPERT, D_MODEL), F32),
            pltpu.VMEM((D_MODEL, D_EXPERT), BF16),
            pltpu.VMEM((D_MODEL, D_EXPERT), BF16),
            pltpu.VMEM((D_EXPERT, D_MODEL), BF16),
            pltpu.SemaphoreType.DMA((2,)),
            pltpu.SemaphoreType.DMA((2,)),
            pltpu.SemaphoreType.DMA((2,)),
        ],
    )
    return pl.pallas_call(
        _ffn_kernel,
        grid_spec=grid_spec,
        out_shape=jax.ShapeDtypeStruct((p_rows, PACK_COLS), U32),
        compiler_params=pltpu.CompilerParams(dimension_semantics=("arbitrary",), vmem_limit_bytes=VMEM_LIMIT),
        name="expert_ffn",
    )(*items, xb, wg, wu, wd)


def _combine_kernel(nops_ref, oloc_ref, odst_ref, ntot_ref, y1_ref, slots_ref, fg_ref,
                    yb_ref, yp_ref, ys_ref, buf_ref, sem, *, n_tiles):
    i = pl.program_id(0)
    n_steps = pl.num_programs(0)
    slot = i % 2

    def chunk_copy(s, u, loc, src, rows):
        return pltpu.make_async_copy(yb_ref.at[pl.ds(src, rows), :],
                                     buf_ref.at[s, u, pl.ds(loc, rows), :], sem.at[s])

    def issue_step(step, s):
        for u in range(TILES_PER_STEP):
            _copy_ops(step * TILES_PER_STEP + u, n_tiles, nops_ref, oloc_ref, odst_ref,
                      lambda loc, src, rows, u=u: chunk_copy(s, u, loc, src, rows).start())

    @pl.when(i == 0)
    def _():
        buf_ref[...] = jnp.zeros_like(buf_ref)
        issue_step(0, 0)

    @pl.when(i + 1 < n_steps)
    def _():
        issue_step(i + 1, 1 - slot)

    iota_s = lax.broadcasted_iota(jnp.int32, (SLOT_ROWS, TOKEN_TILE), 0)
    perms = []
    for u in range(TILES_PER_STEP):
        lanes = slice(u * TOKEN_TILE, (u + 1) * TOKEN_TILE)
        p0 = slots_ref[0:1, lanes]
        p1 = slots_ref[1:2, lanes]
        perms.append(jnp.where((iota_s == p0) | (iota_s == p1), 1.0, 0.0).astype(BF16))

    for u in range(TILES_PER_STEP):
        n_rows = ntot_ref[i * TILES_PER_STEP + u] * SUBLANES

        @pl.when(n_rows > 0)
        def _():
            pltpu.make_async_copy(yb_ref.at[pl.ds(0, n_rows), :], buf_ref.at[slot, u, pl.ds(0, n_rows), :],
                                  sem.at[slot]).wait()

    for u in range(TILES_PER_STEP):
        rs = slice(u * TOKEN_TILE, (u + 1) * TOKEN_TILE)
        yb = _unpack_bf16_pairs(buf_ref[slot, u])
        moe = lax.dot_general(perms[u], yb, (((0,), (0,)), ((), ())), preferred_element_type=F32)
        y_tile = _rms(y1_ref[rs, :] + moe, fg_ref[...])
        yp_ref[rs, :] = y_tile

    @pl.when(i == n_steps - 1)
    def _():
        ys_ref[...] = y_tile


def _combine_call(y1, slots, fg, yb, ops, ntot, *, n_tiles, n_prompt, n_sample):
    n_steps = n_tiles // TILES_PER_STEP
    assert n_sample == TOKEN_TILE and n_prompt + n_sample == n_steps * STEP_TOKENS
    grid_spec = pltpu.PrefetchScalarGridSpec(
        num_scalar_prefetch=4,
        grid=(n_steps,),
        in_specs=[
            pl.BlockSpec((STEP_TOKENS, D_MODEL), lambda i, *_: (i, 0)),
            pl.BlockSpec((SUBLANES, STEP_TOKENS), lambda i, *_: (0, i)),
            pl.BlockSpec((1, D_MODEL), lambda i, *_: (0, 0)),
            pl.BlockSpec(memory_space=pl.ANY),
        ],
        out_specs=[
            pl.BlockSpec((STEP_TOKENS, D_MODEL), lambda i, *_: (i, 0)),
            pl.BlockSpec((n_sample, D_MODEL), lambda i, *_: (0, 0)),
        ],
        scratch_shapes=[
            pltpu.VMEM((2, TILES_PER_STEP, SLOT_ROWS, PACK_COLS), U32),
            pltpu.SemaphoreType.DMA((2,)),
        ],
    )
    return pl.pallas_call(
        functools.partial(_combine_kernel, n_tiles=n_tiles),
        grid_spec=grid_spec,
        out_shape=[jax.ShapeDtypeStruct((n_prompt, D_MODEL), F32), jax.ShapeDtypeStruct((n_sample, D_MODEL), F32)],
        compiler_params=pltpu.CompilerParams(dimension_semantics=("arbitrary",), vmem_limit_bytes=VMEM_LIMIT),
        name="combine",
    )(*ops, ntot, y1, slots, fg, yb)


def _block_diag(w):
    h, d, _ = w.shape
    eye = jnp.eye(h, dtype=w.dtype)
    return (eye[:, None, :, None] * w[:, :, None, :]).reshape(h * d, h * d)


def _head_blocks(w):
    half = B_HEADS // 2
    return jnp.stack([_block_diag(w[:half]), _block_diag(w[half:])]).astype(BF16)


def _mixer_weights(l, length, reps, norm1_g, w_in, gmlp_ln_g, gmlp_ln_b, gmlp_w_s, gmlp_b_s, conv_w, conv_b,
                   lru_w_a, lru_b_a, lru_w_x, lru_b_x, lru_lambda, w_out, norm2_g,
                   router_group_w, router_group_b, router_expert_w, router_expert_b):
    mask = jnp.tril(jnp.ones((length, length), dtype=bool))
    ws = jnp.where(mask, gmlp_w_s[l][:, :length, :length], 0.0)
    eye = jnp.eye(reps, dtype=ws.dtype)
    wsm = (eye[None, :, None, :, None] * ws[:, None, :, None, :]).reshape(A_GROUPS, reps * length, reps * length)
    bs = jnp.tile(gmlp_b_s[l][:, :length], (1, reps))
    bsx = jnp.repeat(bs.T, A_GROUP_DIM, axis=1)
    wr = jnp.concatenate([router_expert_w[l].T, router_group_w[l].T,
                          jnp.zeros((ROUTER_ROWS - N_EXPERTS - N_GROUPS, D_MODEL), F32)], axis=0)
    br = jnp.concatenate([router_expert_b[l], router_group_b[l],
                          jnp.zeros((ROUTER_ROWS - N_EXPERTS - N_GROUPS,), F32)]).reshape(ROUTER_ROWS, 1)
    return [
        norm1_g[l].reshape(1, D_MODEL), w_in[l].astype(BF16),
        gmlp_ln_g[l].reshape(1, D_A), gmlp_ln_b[l].reshape(1, D_A), wsm.astype(BF16), bsx,
        conv_w[l], conv_b[l].reshape(1, D_B),
        _head_blocks(lru_w_a[l]), lru_b_a[l].reshape(1, D_B),
        _head_blocks(lru_w_x[l]), lru_b_x[l].reshape(1, D_B),
        lru_lambda[l].reshape(1, D_B), w_out[l].astype(BF16), norm2_g[l].reshape(1, D_MODEL),
        wr.astype(BF16), br,
    ]


def _routing_tables(cnt, n_items_max):
    n_tiles = cnt.shape[0]
    seg = (cnt + SUBLANES - 1) // SUBLANES * SUBLANES
    nch = seg // SUBLANES
    tot = jnp.sum(seg, axis=0)
    padded = (tot + FFN_UNIT - 1) // FFN_UNIT * FFN_UNIT
    e_before = jnp.arange(N_EXPERTS)[None, :] < jnp.arange(N_EXPERTS)[:, None]
    t_before = jnp.arange(n_tiles)[None, :] < jnp.arange(n_tiles)[:, None]
    pad_start = jnp.sum(jnp.where(e_before, padded[None, :], 0), axis=1)
    doff = pad_start[None, :] + jnp.sum(jnp.where(t_before[:, :, None], seg[None, :, :], 0), axis=1)
    locoff = jnp.sum(jnp.where(e_before[None, :, :], seg[:, None, :], 0), axis=2)
    ntot = jnp.sum(nch, axis=1)
    whole = nch // COPY_CLASSES[0][0]
    done = whole * COPY_CLASSES[0][0] * SUBLANES
    nops, olocs, odsts = [], [], []
    for chunks, cap in COPY_CLASSES:
        if chunks == COPY_CLASSES[0][0]:
            m, loc0, dst0 = whole, locoff, doff
        else:
            m, loc0, dst0 = (nch - whole * COPY_CLASSES[0][0] == chunks).astype(jnp.int32), locoff + done, doff + done
        start = jnp.sum(jnp.where(e_before[None, :, :], m[:, None, :], 0), axis=2)
        s = jnp.arange(cap)
        owner = jnp.sum((start + m)[:, None, :] <= s[None, :, None], axis=2)
        own = owner[:, :, None] == jnp.arange(N_EXPERTS)[None, None, :]
        step = chunks * SUBLANES * (s[None, :, None] - start[:, None, :])
        olocs.append(jnp.sum(jnp.where(own, loc0[:, None, :] + step, 0), axis=2).reshape(-1))
        odsts.append(jnp.sum(jnp.where(own, dst0[:, None, :] + step, 0), axis=2).reshape(-1))
        nops.append(jnp.sum(m, axis=1))
    tail = pad_start + tot
    ntail = (padded - tot) // SUBLANES
    units = padded // FFN_UNIT
    n_it = (units + FFN_ITEM_UNITS - 1) // FFN_ITEM_UNITS
    it_start = jnp.sum(jnp.where(e_before, n_it[None, :], 0), axis=1)
    t = jnp.arange(n_items_max)
    e_ids = jnp.arange(N_EXPERTS)
    owner = jnp.minimum(jnp.sum((it_start + n_it)[None, :] <= t[:, None], axis=1), N_EXPERTS - 1)
    own = owner[:, None] == e_ids[None, :]
    pick = lambda v: jnp.sum(jnp.where(own, v[None, :], 0), axis=1)
    j = t - pick(it_start)
    istart = pick(pad_start) + j * FFN_BLOCK
    iunits = jnp.clip(pick(units) - FFN_ITEM_UNITS * j, 1, FFN_ITEM_UNITS)
    ordinal = jnp.sum(jnp.where(e_before, (n_it > 0).astype(jnp.int32)[None, :], 0), axis=1)
    later = (e_ids[None, :] > e_ids[:, None]) & (n_it > 0)[None, :]
    nxt = jnp.min(jnp.where(later, e_ids[None, :], N_EXPERTS), axis=1)
    nxt = jnp.where(nxt == N_EXPERTS, -1, nxt)
    i32 = lambda a: a.astype(jnp.int32)
    items = (i32(jnp.sum(n_it).reshape(1)), i32(istart), i32(iunits), i32(owner), i32(j == 0),
             i32(pick(ordinal) % 2), i32(pick(nxt)))
    ops = (i32(jnp.concatenate(nops)), i32(jnp.concatenate(olocs)), i32(jnp.concatenate(odsts)))
    return ops, i32(ntot), i32(tail), i32(ntail), locoff.astype(F32)[:, :, None], items


def kernel(x_prompt, x_sample, state_conv, state_rglru, norm1_g, w_in, gmlp_ln_g, gmlp_ln_b, gmlp_w_s, gmlp_b_s,
           conv_w, conv_b, lru_w_a, lru_b_a, lru_w_x, lru_b_x, lru_lambda, w_out, norm2_g,
           router_group_w, router_group_b, router_expert_w, router_expert_b,
           expert_w_gate, expert_w_up, expert_w_down, final_norm_g):
    depth = w_in.shape[0]
    nb, t_len, _ = x_prompt.shape
    db, dt, _ = x_sample.shape
    n_prompt = nb * t_len
    n_sample = db * dt
    assert t_len % PROMPT_TILE == 0 and n_prompt % TOKEN_TILE == 0 and n_sample == TOKEN_TILE
    n_total = n_prompt + n_sample
    assert n_total % STEP_TOKENS == 0
    n_tiles = n_total // TOKEN_TILE
    p_tiles = n_prompt // TOKEN_TILE
    p_rows = 2 * n_total + (SUBLANES - 1) * N_EXPERTS * n_tiles + N_EXPERTS * (FFN_UNIT - 1)
    p_rows = -(-p_rows // FFN_UNIT) * FFN_UNIT
    n_items_max = p_rows // FFN_BLOCK + N_EXPERTS
    upper = jnp.triu(jnp.ones((TOKEN_TILE, TOKEN_TILE), BF16), 1)

    xp, xs = x_prompt, x_sample
    conv_p, h_p, v_s, conv_s, h_s = [], [], [], [], []
    for l in range(depth):
        lw = (norm1_g, w_in, gmlp_ln_g, gmlp_ln_b, gmlp_w_s, gmlp_b_s, conv_w, conv_b, lru_w_a, lru_b_a,
              lru_w_x, lru_b_x, lru_lambda, w_out, norm2_g, router_group_w, router_group_b,
              router_expert_w, router_expert_b)
        wts_p = _mixer_weights(l, GMLP_CHUNK, 1, *lw)
        wts_s = _mixer_weights(l, dt, db, *lw)
        zero_conv = jnp.zeros((nb, CONV_WIDTH - 1, D_B), F32)
        zero_h = jnp.zeros((nb, 1, D_B), F32)
        y1, h2, route, cnt_p, cp, hp = _mixer_call(
            xp, zero_conv, zero_h, wts_p, n_total=n_total, row_offset=0, sb=1, tt=PROMPT_TILE, chunk=GMLP_CHUNK, seg_len=PROMPT_SEG, seg_stride=PROMPT_SEG + SUBLANES, chain=True, emit_vn=False)
        y1, h2, route, cnt_s, cs, hs, vs = _mixer_call(
            xs, state_conv[l], state_rglru[l][None], wts_s, n_total=n_total, row_offset=n_prompt, sb=db,
            tt=dt, chunk=db * dt, seg_len=dt, seg_stride=dt + SUBLANES, chain=False, emit_vn=True,
            alias_in=(y1, h2, route))
        conv_p.append(cp)
        h_p.append(hp[:, 0])
        v_s.append(vs)
        conv_s.append(cs)
        h_s.append(hs[0])

        sub = PROMPT_TILE // TOKEN_TILE
        cnt = jnp.concatenate([
            jnp.swapaxes(cnt_p[:, :, :sub], 1, 2).reshape(p_tiles, N_EXPERTS),
            cnt_s[:, :, 0]], axis=0).astype(jnp.int32)
        ops, ntot, tail, ntail, locoff, items = _routing_tables(cnt, n_items_max)

        xb, slots = _dispatch_call(h2, route, locoff, upper, ops, ntot, tail, ntail, n_tiles=n_tiles,
                                   p_rows=p_rows)
        yb = _ffn_call(xb, expert_w_gate[l], expert_w_up[l], expert_w_down[l], items, p_rows=p_rows)
        assert l == depth - 1, "deeper stacks need an un-normalised combine between layers"
        fg = final_norm_g.reshape(1, D_MODEL)
        yp, ysm = _combine_call(y1, slots, fg, yb, ops, ntot, n_tiles=n_tiles, n_prompt=n_prompt,
                                n_sample=n_sample)
        xp = yp.reshape(nb, t_len, D_MODEL)
        xs = ysm.reshape(db, dt, D_MODEL)

    return (xp, xs, jnp.stack(conv_p), jnp.stack(h_p), jnp.stack(v_s), jnp.stack(conv_s), jnp.stack(h_s))
```

```python
import functools

import jax
import jax.numpy as jnp
from jax import lax
from jax.experimental import pallas as pl
from jax.experimental.pallas import tpu as pltpu

D_MODEL = 1024
D_A = 512
D_B = 512
A_GROUPS = 4
A_GROUP_DIM = 128
GMLP_CHUNK = 128
B_HEADS = 8
CONV_WIDTH = 4
LRU_C = 8.0
N_GROUPS = 4
EXPERTS_PER_GROUP = 8
N_EXPERTS = 32
D_EXPERT = 512
EPS = 1e-6
TINY = 1e-30

SUBLANES = 8
LANES = 128
ROUTER_ROWS = 40
TOKEN_TILE = 256
PROMPT_TILE = 1024
PROMPT_SEG = PROMPT_TILE // SUBLANES
FFN_UNIT = 256
FFN_ITEM_UNITS = 4
FFN_BLOCK = FFN_ITEM_UNITS * FFN_UNIT
ZERO_CHUNKS = 8
ZERO_ROWS = ZERO_CHUNKS * SUBLANES
SLOT_ROWS = 2 * TOKEN_TILE + N_EXPERTS * SUBLANES
TILE_CHUNKS = SLOT_ROWS // SUBLANES
COPY_CLASSES = ((4, TILE_CHUNKS // 4), (3, N_EXPERTS), (2, N_EXPERTS), (1, N_EXPERTS))
TILES_PER_STEP = 3
STEP_TOKENS = TILES_PER_STEP * TOKEN_TILE
PACK_COLS = D_MODEL // 2
XB_COLS = PACK_COLS + LANES
U32 = jnp.uint32
HI_MASK = 0xFFFF0000
VMEM_BYTES_V7X = 64 * 1024 * 1024
VMEM_LIMIT = VMEM_BYTES_V7X - 8 * 1024 * 1024

BF16 = jnp.bfloat16
F32 = jnp.float32


def _rms(x, g):
    return x * lax.rsqrt(jnp.mean(x * x, axis=-1, keepdims=True) + EPS) * g


def _dot(a, b):
    return jnp.dot(a, b, preferred_element_type=F32)


def _sigmoid(x):
    return 1.0 / (1.0 + jnp.exp(-x))


def _pack_bf16_pairs(x):
    bits = lax.bitcast_convert_type(x, U32)
    return (bits[:, PACK_COLS:] & U32(HI_MASK)) | (bits[:, :PACK_COLS] >> 16)


def _unpack_bf16_pairs(w):
    lo = lax.bitcast_convert_type(w << 16, F32).astype(BF16)
    hi = lax.bitcast_convert_type(w & U32(HI_MASK), F32).astype(BF16)
    return jnp.concatenate([lo, hi], axis=1)


def _mixer_kernel(x_ref, hist_ref, h0_ref, n1g_ref, win_ref, lng_ref, lnb_ref, wsm_ref, bsx_ref,
                  cw_ref, cb_ref, wa_ref, ba_ref, wx_ref, bx_ref, lam_ref, wout_ref, n2g_ref,
                  wr_ref, br_ref, *rest, sb, tt, chunk, seg_len, seg_stride, chain, emit_vn, aliased):
    if aliased:
        rest = rest[3:]
    if emit_vn:
        (y1_ref, h2_ref, route_ref, cnt_ref, conv_ref, hlast_ref, vn_ref,
         xs_ref, sa_ref, sb_ref, mix_ref, hc_ref) = rest
    else:
        (y1_ref, h2_ref, route_ref, cnt_ref, conv_ref, hlast_ref,
         xs_ref, sa_ref, sb_ref, mix_ref, hc_ref) = rest
        vn_ref = None
    t_idx = pl.program_id(1)

    @pl.when(t_idx == 0)
    def _():
        xs_ref[:, SUBLANES - 3:SUBLANES, :] = hist_ref[...]
        hc_ref[...] = jnp.zeros_like(hc_ref)

    @pl.when(t_idx != 0)
    def _():
        xs_ref[:, SUBLANES - 3:SUBLANES, :] = xs_ref[:, SUBLANES + tt - 3:SUBLANES + tt, :]

    rows = sb * tt
    n_seg = rows // seg_len
    half = D_B // 2

    x = x_ref[...].reshape(rows, D_MODEL)
    hb = _rms(x, n1g_ref[...]).astype(BF16)

    v_a = _dot(hb, win_ref[:, D_A:2 * D_A])
    x_b = _dot(hb, win_ref[:, 2 * D_A:2 * D_A + D_B])
    mixed = []
    for g in range(A_GROUPS):
        cols = slice(g * A_GROUP_DIM, (g + 1) * A_GROUP_DIM)
        vg = v_a[:, cols]
        mu = jnp.mean(vg, axis=-1, keepdims=True)
        dv = vg - mu
        var = jnp.mean(dv * dv, axis=-1, keepdims=True)
        vn = dv * lax.rsqrt(var + EPS) * lng_ref[:, cols] + lnb_ref[:, cols]
        if vn_ref is not None:
            vn_ref[:, :, cols] = vn.reshape(sb, tt, A_GROUP_DIM)
        vnb = vn.astype(BF16)
        mixed.append([_dot(wsm_ref[g], vnb[c * chunk:(c + 1) * chunk]) for c in range(rows // chunk)])
    u_a = _dot(hb, win_ref[:, 0:D_A])

    xs_ref[:, SUBLANES:SUBLANES + tt, :] = x_b.reshape(sb, tt, D_B)
    xc = cb_ref[...] + x_b * cw_ref[3:4, :]
    for j in range(1, CONV_WIDTH):
        shifted = xs_ref[:, SUBLANES - j:SUBLANES - j + tt, :].reshape(rows, D_B)
        xc = xc + shifted * cw_ref[3 - j:4 - j, :]

    xcb = xc.astype(BF16)
    r_lin = jnp.concatenate([_dot(xcb[:, :half], wa_ref[0]), _dot(xcb[:, half:], wa_ref[1])], axis=1)
    i_lin = jnp.concatenate([_dot(xcb[:, :half], wx_ref[0]), _dot(xcb[:, half:], wx_ref[1])], axis=1)

    for g in range(A_GROUPS):
        cols = slice(g * A_GROUP_DIM, (g + 1) * A_GROUP_DIM)
        for c in range(rows // chunk):
            rs = slice(c * chunk, (c + 1) * chunk)
            s = mixed[g][c] + bsx_ref[:, cols]
            mix_ref[rs, cols] = (u_a[rs, cols] * s).astype(BF16)

    g_b = _dot(hb, win_ref[:, 2 * D_A + D_B:])
    r = _sigmoid(r_lin + ba_ref[...])
    i_g = _sigmoid(i_lin + bx_ref[...])
    log_a = (-LRU_C * r) * jax.nn.softplus(-lam_ref[...])
    a = jnp.exp(log_a)
    y_gain = 1.0 - a * a
    gain = y_gain * lax.rsqrt(jnp.maximum(y_gain, TINY))
    bterm = gain * i_g * xc

    n_lb = D_B // LANES
    for j in range(n_seg):
        for k in range(n_lb):
            lc = slice(k * LANES, (k + 1) * LANES)
            dst = slice(j * seg_stride, j * seg_stride + seg_len)
            sa_ref[k, dst, :] = a[j * seg_len:(j + 1) * seg_len, lc]
            sb_ref[k, dst, :] = bterm[j * seg_len:(j + 1) * seg_len, lc]

    y_a = x + _dot(mix_ref[:, 0:D_A], wout_ref[0:D_A, :])
    gel = jax.nn.gelu(g_b)

    for grp in range(n_seg // SUBLANES):
        base = grp * SUBLANES * seg_stride
        bs = slice(grp * SUBLANES, (grp + 1) * SUBLANES)
        if chain:
            h_init = tuple(jnp.zeros((SUBLANES, LANES), F32) for _ in range(n_lb))
        else:
            h_init = tuple(h0_ref[0, bs, k * LANES:(k + 1) * LANES] for k in range(n_lb))
        a_init = tuple(jnp.ones((SUBLANES, LANES), F32) for _ in range(n_lb))

        def step(i, carry, base=base):
            hs, acs = carry
            idx = pl.ds(base + i, SUBLANES, stride=seg_stride)
            new_h, new_a = [], []
            for k in range(n_lb):
                av = sa_ref[k, idx, :]
                h = av * hs[k] + sb_ref[k, idx, :]
                sb_ref[k, idx, :] = h
                new_h.append(h)
                if chain:
                    ac = av * acs[k]
                    sa_ref[k, idx, :] = ac
                    new_a.append(ac)
                else:
                    new_a.append(acs[k])
            return tuple(new_h), tuple(new_a)

        carry = (h_init, a_init)
        for i in range(seg_len):
            carry = step(i, carry)
        h_end, a_end = carry

        for k in range(n_lb):
            lc = slice(k * LANES, (k + 1) * LANES)
            mc = slice(D_A + k * LANES, D_A + (k + 1) * LANES)
            if chain:
                h_in = hc_ref[:, lc]
                for j in range(SUBLANES):
                    seg = slice(base + j * seg_stride, base + j * seg_stride + seg_len)
                    rs = slice((grp * SUBLANES + j) * seg_len, (grp * SUBLANES + j + 1) * seg_len)
                    h_seg = sb_ref[k, seg, :] + sa_ref[k, seg, :] * h_in
                    mix_ref[rs, mc] = (h_seg * gel[rs, lc]).astype(BF16)
                    h_in = h_end[k][j:j + 1] + a_end[k][j:j + 1] * h_in
                hc_ref[:, lc] = h_in
                hlast_ref[0, :, lc] = h_in
            else:
                for j in range(SUBLANES):
                    seg = slice(base + j * seg_stride, base + j * seg_stride + seg_len)
                    rs = slice((grp * SUBLANES + j) * seg_len, (grp * SUBLANES + j + 1) * seg_len)
                    mix_ref[rs, mc] = (sb_ref[k, seg, :] * gel[rs, lc]).astype(BF16)
                hlast_ref[0, bs, lc] = h_end[k]

    y1 = y_a + _dot(mix_ref[:, D_A:], wout_ref[D_A:, :])
    y1_ref[...] = y1
    h2 = _rms(y1, n2g_ref[...]).astype(BF16)
    h2_ref[...] = h2

    logits = lax.dot_general(wr_ref[...], h2, (((1,), (1,)), ((), ())), preferred_element_type=F32) + br_ref[...]
    gl = [logits[N_EXPERTS + k:N_EXPERTS + k + 1, :] for k in range(N_GROUPS)]
    gmax = jnp.maximum(jnp.maximum(gl[0], gl[1]), jnp.maximum(gl[2], gl[3]))
    grp_idx = jnp.where(gl[0] == gmax, 0, jnp.where(gl[1] == gmax, 1, jnp.where(gl[2] == gmax, 2, 3)))
    gsum = (jnp.exp(gl[0] - gmax) + jnp.exp(gl[1] - gmax)) + (jnp.exp(gl[2] - gmax) + jnp.exp(gl[3] - gmax))
    p_grp = 1.0 / gsum
    e_in = jnp.where(grp_idx == 0, logits[0:8],
                     jnp.where(grp_idx == 1, logits[8:16], jnp.where(grp_idx == 2, logits[16:24], logits[24:32])))
    iota8 = lax.broadcasted_iota(jnp.int32, (EXPERTS_PER_GROUP, rows), 0)
    m1 = jnp.max(e_in, axis=0, keepdims=True)
    i1 = jnp.min(jnp.where(e_in == m1, iota8, EXPERTS_PER_GROUP), axis=0, keepdims=True)
    e_rest = jnp.where(iota8 == i1, -jnp.inf, e_in)
    m2 = jnp.max(e_rest, axis=0, keepdims=True)
    i2 = jnp.min(jnp.where(e_rest == m2, iota8, EXPERTS_PER_GROUP), axis=0, keepdims=True)
    t2 = jnp.exp(m2 - m1)
    den = 1.0 + t2
    gate0 = p_grp * (1.0 / den)
    gate1 = p_grp * (t2 / den)
    e0 = (grp_idx * EXPERTS_PER_GROUP + i1).astype(F32)
    e1 = (grp_idx * EXPERTS_PER_GROUP + i2).astype(F32)
    route_ref[...] = jnp.where(iota8 == 0, e0, jnp.where(iota8 == 1, e1, jnp.where(
        iota8 == 2, gate0, jnp.where(iota8 == 3, gate1, 0.0))))

    iota_e = lax.broadcasted_iota(jnp.int32, (N_EXPERTS, rows), 0).astype(F32)
    onehot = (iota_e == e0).astype(F32) + (iota_e == e1).astype(F32)
    lane = lax.broadcasted_iota(jnp.int32, (N_EXPERTS, LANES), 1)
    cnt = jnp.zeros((N_EXPERTS, LANES), F32)
    for s in range(rows // TOKEN_TILE):
        c_s = jnp.sum(onehot[:, s * TOKEN_TILE:(s + 1) * TOKEN_TILE], axis=1, keepdims=True)
        cnt = cnt + jnp.where(lane == s, c_s, 0.0)
    cnt_ref[...] = cnt.reshape(1, N_EXPERTS, LANES)
    conv_ref[...] = xs_ref[:, SUBLANES + tt - 3:SUBLANES + tt, :]


def _mixer_call(x, hist, h0, wts, *, n_total, row_offset, sb, tt, chunk, seg_len, seg_stride, chain,
                emit_vn, alias_in=None):
    nb, t_len, _ = x.shape
    nbb = nb // sb
    ntt = t_len // tt
    rows = sb * tt
    blk0 = row_offset // rows
    n_seg = rows // seg_len

    def full(arr):
        nd = arr.ndim
        return pl.BlockSpec(arr.shape, lambda b, t, _nd=nd: (0,) * _nd)

    in_specs = [
        pl.BlockSpec((sb, tt, D_MODEL), lambda b, t: (b, t, 0)),
        pl.BlockSpec((sb, CONV_WIDTH - 1, D_B), lambda b, t: (b, 0, 0)),
        pl.BlockSpec((1, sb, D_B), lambda b, t: (b, 0, 0)),
    ] + [full(w) for w in wts]
    args = [x, hist, h0] + list(wts)
    io_alias = {}
    if alias_in is not None:
        for k, arr in enumerate(alias_in):
            io_alias[len(args)] = k
            in_specs.append(pl.BlockSpec(memory_space=pl.ANY))
            args.append(arr)

    out_shape = [
        jax.ShapeDtypeStruct((n_total, D_MODEL), F32),
        jax.ShapeDtypeStruct((n_total, D_MODEL), BF16),
        jax.ShapeDtypeStruct((SUBLANES, n_total), F32),
        jax.ShapeDtypeStruct((nbb * ntt, N_EXPERTS, LANES), F32),
        jax.ShapeDtypeStruct((nb, CONV_WIDTH - 1, D_B), F32),
        jax.ShapeDtypeStruct((nbb, sb, D_B), F32),
    ]
    out_specs = [
        pl.BlockSpec((rows, D_MODEL), lambda b, t: (blk0 + b * ntt + t, 0)),
        pl.BlockSpec((rows, D_MODEL), lambda b, t: (blk0 + b * ntt + t, 0)),
        pl.BlockSpec((SUBLANES, rows), lambda b, t: (0, blk0 + b * ntt + t)),
        pl.BlockSpec((1, N_EXPERTS, LANES), lambda b, t: (b * ntt + t, 0, 0)),
        pl.BlockSpec((sb, CONV_WIDTH - 1, D_B), lambda b, t: (b, 0, 0)),
        pl.BlockSpec((1, sb, D_B), lambda b, t: (b, 0, 0)),
    ]
    if emit_vn:
        out_shape.append(jax.ShapeDtypeStruct((nb, t_len, D_A), F32))
        out_specs.append(pl.BlockSpec((sb, tt, D_A), lambda b, t: (b, t, 0)))

    kern = functools.partial(_mixer_kernel, sb=sb, tt=tt, chunk=chunk, seg_len=seg_len, seg_stride=seg_stride,
                             chain=chain, emit_vn=emit_vn, aliased=alias_in is not None)
    return pl.pallas_call(
        kern,
        grid=(nbb, ntt),
        in_specs=in_specs,
        out_specs=out_specs,
        out_shape=out_shape,
        scratch_shapes=[
            pltpu.VMEM((sb, SUBLANES + tt, D_B), F32),
            pltpu.VMEM((D_B // LANES, n_seg * seg_stride, LANES), F32),
            pltpu.VMEM((D_B // LANES, n_seg * seg_stride, LANES), F32),
            pltpu.VMEM((rows, D_MODEL), BF16),
            pltpu.VMEM((1, D_B), F32),
        ],
        input_output_aliases=io_alias,
        compiler_params=pltpu.CompilerParams(dimension_semantics=("arbitrary", "arbitrary"),
                                             vmem_limit_bytes=VMEM_LIMIT),
        name="mixer_chain" if chain else "mixer_step",
    )(*args)


def _tile_slots(route_ref, locoff_ref, upper_ref, u):
    lanes = slice(u * TOKEN_TILE, (u + 1) * TOKEN_TILE)
    e0 = route_ref[0:1, lanes]
    e1 = route_ref[1:2, lanes]
    iota_e = lax.broadcasted_iota(jnp.int32, (N_EXPERTS, TOKEN_TILE), 0).astype(F32)
    oh0 = (iota_e == e0).astype(F32)
    oh1 = (iota_e == e1).astype(F32)
    c0 = _dot(oh0.astype(BF16), upper_ref[...])
    c1 = _dot(oh1.astype(BF16), upper_ref[...])
    cnt0 = jnp.sum(oh0, axis=1, keepdims=True)
    base0 = locoff_ref[u]
    base1 = base0 + cnt0
    p0 = jnp.sum(oh0 * (base0 + c0), axis=0, keepdims=True)
    p1 = jnp.sum(oh1 * (base1 + c1), axis=0, keepdims=True)
    return p0.astype(jnp.int32), p1.astype(jnp.int32)


def _copy_ops(tile, n_tiles, nops_ref, oloc_ref, odst_ref, make_copy):
    base = 0
    for k, (chunks, cap) in enumerate(COPY_CLASSES):
        n = nops_ref[k * n_tiles + tile]

        def per_pair(i, _, chunks=chunks, cap=cap, base=base, n=n):
            idx = base + tile * cap + 2 * i
            make_copy(pl.multiple_of(oloc_ref[idx], SUBLANES), pl.multiple_of(odst_ref[idx], SUBLANES),
                      chunks * SUBLANES)

            @pl.when(2 * i + 1 < n)
            def _():
                make_copy(pl.multiple_of(oloc_ref[idx + 1], SUBLANES), pl.multiple_of(odst_ref[idx + 1], SUBLANES),
                          chunks * SUBLANES)
            return 0

        assert cap % 2 == 0
        lax.fori_loop(0, (n + 1) // 2, per_pair, 0)
        base += n_tiles * cap


def _dispatch_kernel(nops_ref, oloc_ref, odst_ref, ntot_ref, tail_ref, ntail_ref,
                     h2_ref, route_ref, locoff_ref, upper_ref, xb_ref, slots_ref, buf_ref, zero_ref, sem, zsem,
                     *, n_tiles):
    i = pl.program_id(0)
    n_steps = pl.num_programs(0)
    slot = i % 2

    def chunk_copy(s, u, loc, dst, rows=SUBLANES):
        return pltpu.make_async_copy(buf_ref.at[s, u, pl.ds(loc, rows), :],
                                     xb_ref.at[pl.ds(dst, rows), :], sem.at[s])

    def wait_step(step, s):
        for u in range(TILES_PER_STEP):
            n_rows = ntot_ref[step * TILES_PER_STEP + u] * SUBLANES

            @pl.when(n_rows > 0)
            def _():
                pltpu.make_async_copy(buf_ref.at[s, u, pl.ds(0, n_rows), :], xb_ref.at[pl.ds(0, n_rows), :],
                                      sem.at[s]).wait()

    @pl.when(i >= 2)
    def _():
        wait_step(i - 2, slot)

    iota_s = lax.broadcasted_iota(jnp.int32, (SLOT_ROWS, TOKEN_TILE), 0)
    iota8 = lax.broadcasted_iota(jnp.int32, (SUBLANES, TOKEN_TILE), 0)
    for u in range(TILES_PER_STEP):
        lanes = slice(u * TOKEN_TILE, (u + 1) * TOKEN_TILE)
        p0, p1 = _tile_slots(route_ref, locoff_ref, upper_ref, u)
        slots_ref[:, lanes] = jnp.where(iota8 == 0, p0, jnp.where(iota8 == 1, p1, 0))
        hit0 = iota_s == p0
        hit1 = iota_s == p1
        perm = jnp.where(hit0 | hit1, 1.0, 0.0).astype(BF16)
        sorted_rows = _dot(perm, h2_ref[u * TOKEN_TILE:(u + 1) * TOKEN_TILE, :])
        buf_ref[slot, u, :, 0:PACK_COLS] = _pack_bf16_pairs(sorted_rows)
        gates = jnp.where(hit0, route_ref[2:3, lanes], 0.0) + jnp.where(hit1, route_ref[3:4, lanes], 0.0)
        gcol = jnp.sum(gates, axis=1, keepdims=True)
        buf_ref[slot, u, :, PACK_COLS:XB_COLS] = lax.bitcast_convert_type(
            jnp.broadcast_to(gcol, (SLOT_ROWS, LANES)), U32)

    for u in range(TILES_PER_STEP):
        _copy_ops(i * TILES_PER_STEP + u, n_tiles, nops_ref, oloc_ref, odst_ref,
                  lambda loc, dst, rows, u=u: chunk_copy(slot, u, loc, dst, rows).start())

    @pl.when(i == n_steps - 1)
    def _():
        zero_ref[...] = jnp.zeros_like(zero_ref)

        def zero_copy(dst, rows, k):
            return pltpu.make_async_copy(zero_ref.at[pl.ds(0, rows), :], xb_ref.at[pl.ds(dst, rows), :], zsem.at[k])

        def per_expert(e, tot):
            n_big, n_small = tot
            n = ntail_ref[e]
            d0 = tail_ref[e]
            nb = n // ZERO_CHUNKS
            ns = n - nb * ZERO_CHUNKS

            def big(c, _):
                zero_copy(pl.multiple_of(d0 + c * ZERO_ROWS, SUBLANES), ZERO_ROWS, 0).start()
                return 0

            def small(c, _):
                zero_copy(pl.multiple_of(d0 + nb * ZERO_ROWS + c * SUBLANES, SUBLANES), SUBLANES, 1).start()
                return 0

            lax.fori_loop(0, nb, big, 0)
            lax.fori_loop(0, ns, small, 0)
            return n_big + nb, n_small + ns

        n_big, n_small = lax.fori_loop(0, N_EXPERTS, per_expert, (0, 0))

        @pl.when(i >= 1)
        def _():
            wait_step(i - 1, 1 - slot)

        wait_step(i, slot)

        def wait_big(c, _):
            zero_copy(0, ZERO_ROWS, 0).wait()
            return 0

        def wait_small(c, _):
            zero_copy(0, SUBLANES, 1).wait()
            return 0

        lax.fori_loop(0, n_big, wait_big, 0)
        lax.fori_loop(0, n_small, wait_small, 0)


def _dispatch_call(h2, route, locoff, upper, ops, ntot, tail, ntail, *, n_tiles, p_rows):
    grid_spec = pltpu.PrefetchScalarGridSpec(
        num_scalar_prefetch=6,
        grid=(n_tiles // TILES_PER_STEP,),
        in_specs=[
            pl.BlockSpec((STEP_TOKENS, D_MODEL), lambda i, *_: (i, 0)),
            pl.BlockSpec((SUBLANES, STEP_TOKENS), lambda i, *_: (0, i)),
            pl.BlockSpec((TILES_PER_STEP, N_EXPERTS, 1), lambda i, *_: (i, 0, 0)),
            pl.BlockSpec((TOKEN_TILE, TOKEN_TILE), lambda i, *_: (0, 0)),
        ],
        out_specs=[pl.BlockSpec(memory_space=pl.ANY),
                   pl.BlockSpec((SUBLANES, STEP_TOKENS), lambda i, *_: (0, i))],
        scratch_shapes=[
            pltpu.VMEM((2, TILES_PER_STEP, SLOT_ROWS, XB_COLS), U32),
            pltpu.VMEM((ZERO_ROWS, XB_COLS), U32),
            pltpu.SemaphoreType.DMA((2,)),
            pltpu.SemaphoreType.DMA((2,)),
        ],
    )
    return pl.pallas_call(
        functools.partial(_dispatch_kernel, n_tiles=n_tiles),
        grid_spec=grid_spec,
        out_shape=[jax.ShapeDtypeStruct((p_rows, XB_COLS), U32),
                   jax.ShapeDtypeStruct((SUBLANES, n_tiles * TOKEN_TILE), jnp.int32)],
        compiler_params=pltpu.CompilerParams(dimension_semantics=("arbitrary",), vmem_limit_bytes=VMEM_LIMIT),
        name="dispatch",
    )(*ops, ntot, tail, ntail, h2, route, locoff, upper)


def _ffn_kernel(nitems_ref, istart_ref, iunits_ref, iexp_ref, ifirst_ref, iwslot_ref, inext_ref,
                xb_ref, wg_ref, wu_ref, wd_ref, yb_ref,
                xin_ref, yout_ref, wgf_ref, wuf_ref, wdf_ref, wgb_ref, wub_ref, wdb_ref, sem_in, sem_out, sem_w):
    n_items = nitems_ref[0]
    sizes = tuple(k * FFN_UNIT for k in range(FFN_ITEM_UNITS, 0, -1))

    def rows_in(t, s, rows):
        start = pl.multiple_of(istart_ref[t], FFN_UNIT)
        return pltpu.make_async_copy(xb_ref.at[pl.ds(start, rows), :], xin_ref.at[s, pl.ds(0, rows), :], sem_in.at[s])

    def rows_out(t, s, rows):
        start = pl.multiple_of(istart_ref[t], FFN_UNIT)
        return pltpu.make_async_copy(yout_ref.at[s, pl.ds(0, rows), :], yb_ref.at[pl.ds(start, rows), :],
                                     sem_out.at[s])

    def start_by_size(make, t, s):
        for rows in sizes:
            @pl.when(iunits_ref[t] * FFN_UNIT == rows)
            def _():
                make(t, s, rows).start()

    def weight_copies(e, ws):
        return (pltpu.make_async_copy(wg_ref.at[e], wgf_ref.at[ws], sem_w.at[ws]),
                pltpu.make_async_copy(wu_ref.at[e], wuf_ref.at[ws], sem_w.at[ws]),
                pltpu.make_async_copy(wd_ref.at[e], wdf_ref.at[ws], sem_w.at[ws]))

    def compute(s, rows):
        xb = _unpack_bf16_pairs(xin_ref[s, 0:rows, 0:PACK_COLS])
        gate = lax.bitcast_convert_type(xin_ref[s, 0:rows, PACK_COLS:PACK_COLS + 1], F32)
        a = _dot(xb, wgb_ref[...])
        u = _dot(xb, wub_ref[...])
        mid = (a * _sigmoid(a) * u).astype(BF16)
        y = _dot(mid, wdb_ref[...]) * gate
        yout_ref[s, 0:rows, :] = _pack_bf16_pairs(y.astype(BF16).astype(F32))

    @pl.when(n_items > 0)
    def _():
        for c in weight_copies(iexp_ref[0], 0):
            c.start()
        start_by_size(rows_in, 0, 0)

    def per_item(t, _):
        s = t % 2

        @pl.when(t + 1 < n_items)
        def _():
            start_by_size(rows_in, t + 1, 1 - s)

        @pl.when(ifirst_ref[t] == 1)
        def _():
            ws = iwslot_ref[t]
            for c in weight_copies(0, ws):
                c.wait()

            @pl.when(inext_ref[t] >= 0)
            def _():
                for c in weight_copies(inext_ref[t], 1 - ws):
                    c.start()

            wgb_ref[...] = wgf_ref[ws].astype(BF16)
            wub_ref[...] = wuf_ref[ws].astype(BF16)
            wdb_ref[...] = wdf_ref[ws].astype(BF16)

        rows_in(t, s, iunits_ref[t] * FFN_UNIT).wait()

        @pl.when(t >= 2)
        def _():
            rows_out(t - 2, s, iunits_ref[t - 2] * FFN_UNIT).wait()

        for rows in sizes:
            @pl.when(iunits_ref[t] * FFN_UNIT == rows)
            def _():
                compute(s, rows)

        start_by_size(rows_out, t, s)
        return 0

    lax.fori_loop(0, n_items, per_item, 0)

    for back in (2, 1):
        @pl.when(n_items >= back)
        def _():
            t = n_items - back
            rows_out(t, t % 2, iunits_ref[t] * FFN_UNIT).wait()


def _ffn_call(xb, wg, wu, wd, items, *, p_rows):
    any_spec = pl.BlockSpec(memory_space=pl.ANY)
    grid_spec = pltpu.PrefetchScalarGridSpec(
        num_scalar_prefetch=len(items),
        grid=(1,),
        in_specs=[any_spec, any_spec, any_spec, any_spec],
        out_specs=any_spec,
        scratch_shapes=[
            pltpu.VMEM((2, FFN_BLOCK, XB_COLS), U32),
            pltpu.VMEM((2, FFN_BLOCK, PACK_COLS), U32),
            pltpu.VMEM((2, D_MODEL, D_EXPERT), F32),
            pltpu.VMEM((2, D_MODEL, D_EXPERT), F32),
            pltpu.VMEM((2, D_EXPERT, D_MODEL), F32),
            pltpu.VMEM((D_MODEL, D_EXPERT), BF16),
            pltpu.VMEM((D_MODEL, D_EXPERT), BF16),
            pltpu.VMEM((D_EXPERT, D_MODEL), BF16),
            pltpu.SemaphoreType.DMA((2,)),
            pltpu.SemaphoreType.DMA((2,)),
            pltpu.SemaphoreType.DMA((2,)),
        ],
    )
    return pl.pallas_call(
        _ffn_kernel,
        grid_spec=grid_spec,
        out_shape=jax.ShapeDtypeStruct((p_rows, PACK_COLS), U32),
        compiler_params=pltpu.CompilerParams(dimension_semantics=("arbitrary",), vmem_limit_bytes=VMEM_LIMIT),
        name="expert_ffn",
    )(*items, xb, wg, wu, wd)


def _combine_kernel(nops_ref, oloc_ref, odst_ref, ntot_ref, y1_ref, slots_ref, fg_ref,
                    yb_ref, yp_ref, ys_ref, buf_ref, sem, *, n_tiles):
    i = pl.program_id(0)
    n_steps = pl.num_programs(0)
    slot = i % 2

    def chunk_copy(s, u, loc, src, rows):
        return pltpu.make_async_copy(yb_ref.at[pl.ds(src, rows), :],
                                     buf_ref.at[s, u, pl.ds(loc, rows), :], sem.at[s])

    def issue_step(step, s):
        for u in range(TILES_PER_STEP):
            _copy_ops(step * TILES_PER_STEP + u, n_tiles, nops_ref, oloc_ref, odst_ref,
                      lambda loc, src, rows, u=u: chunk_copy(s, u, loc, src, rows).start())

    @pl.when(i == 0)
    def _():
        buf_ref[...] = jnp.zeros_like(buf_ref)
        issue_step(0, 0)

    @pl.when(i + 1 < n_steps)
    def _():
        issue_step(i + 1, 1 - slot)

    iota_s = lax.broadcasted_iota(jnp.int32, (SLOT_ROWS, TOKEN_TILE), 0)
    perms = []
    for u in range(TILES_PER_STEP):
        lanes = slice(u * TOKEN_TILE, (u + 1) * TOKEN_TILE)
        p0 = slots_ref[0:1, lanes]
        p1 = slots_ref[1:2, lanes]
        perms.append(jnp.where((iota_s == p0) | (iota_s == p1), 1.0, 0.0).astype(BF16))

    for u in range(TILES_PER_STEP):
        n_rows = ntot_ref[i * TILES_PER_STEP + u] * SUBLANES

        @pl.when(n_rows > 0)
        def _():
            pltpu.make_async_copy(yb_ref.at[pl.ds(0, n_rows), :], buf_ref.at[slot, u, pl.ds(0, n_rows), :],
                                  sem.at[slot]).wait()

    for u in range(TILES_PER_STEP):
        rs = slice(u * TOKEN_TILE, (u + 1) * TOKEN_TILE)
        yb = _unpack_bf16_pairs(buf_ref[slot, u])
        moe = lax.dot_general(perms[u], yb, (((0,), (0,)), ((), ())), preferred_element_type=F32)
        y_tile = _rms(y1_ref[rs, :] + moe, fg_ref[...])
        yp_ref[rs, :] = y_tile

    @pl.when(i == n_steps - 1)
    def _():
        ys_ref[...] = y_tile


def _combine_call(y1, slots, fg, yb, ops, ntot, *, n_tiles, n_prompt, n_sample):
    n_steps = n_tiles // TILES_PER_STEP
    assert n_sample == TOKEN_TILE and n_prompt + n_sample == n_steps * STEP_TOKENS
    grid_spec = pltpu.PrefetchScalarGridSpec(
        num_scalar_prefetch=4,
        grid=(n_steps,),
        in_specs=[
            pl.BlockSpec((STEP_TOKENS, D_MODEL), lambda i, *_: (i, 0)),
            pl.BlockSpec((SUBLANES, STEP_TOKENS), lambda i, *_: (0, i)),
            pl.BlockSpec((1, D_MODEL), lambda i, *_: (0, 0)),
            pl.BlockSpec(memory_space=pl.ANY),
        ],
        out_specs=[
            pl.BlockSpec((STEP_TOKENS, D_MODEL), lambda i, *_: (i, 0)),
            pl.BlockSpec((n_sample, D_MODEL), lambda i, *_: (0, 0)),
        ],
        scratch_shapes=[
            pltpu.VMEM((2, TILES_PER_STEP, SLOT_ROWS, PACK_COLS), U32),
            pltpu.SemaphoreType.DMA((2,)),
        ],
    )
    return pl.pallas_call(
        functools.partial(_combine_kernel, n_tiles=n_tiles),
        grid_spec=grid_spec,
        out_shape=[jax.ShapeDtypeStruct((n_prompt, D_MODEL), F32), jax.ShapeDtypeStruct((n_sample, D_MODEL), F32)],
        compiler_params=pltpu.CompilerParams(dimension_semantics=("arbitrary",), vmem_limit_bytes=VMEM_LIMIT),
        name="combine",
    )(*ops, ntot, y1, slots, fg, yb)


def _block_diag(w):
    h, d, _ = w.shape
    eye = jnp.eye(h, dtype=w.dtype)
    return (eye[:, None, :, None] * w[:, :, None, :]).reshape(h * d, h * d)


def _head_blocks(w):
    half = B_HEADS // 2
    return jnp.stack([_block_diag(w[:half]), _block_diag(w[half:])]).astype(BF16)


def _mixer_weights(l, length, reps, norm1_g, w_in, gmlp_ln_g, gmlp_ln_b, gmlp_w_s, gmlp_b_s, conv_w, conv_b,
                   lru_w_a, lru_b_a, lru_w_x, lru_b_x, lru_lambda, w_out, norm2_g,
                   router_group_w, router_group_b, router_expert_w, router_expert_b):
    mask = jnp.tril(jnp.ones((length, length), dtype=bool))
    ws = jnp.where(mask, gmlp_w_s[l][:, :length, :length], 0.0)
    wsm = jnp.concatenate([jnp.pad(ws, ((0, 0), (0, 0), (j * length, (reps - 1 - j) * length)))
                           for j in range(reps)], axis=1)
    bs = jnp.tile(gmlp_b_s[l][:, :length], (1, reps))
    bsx = jnp.repeat(bs.T, A_GROUP_DIM, axis=1)
    wr = jnp.concatenate([router_expert_w[l].T, router_group_w[l].T,
                          jnp.zeros((ROUTER_ROWS - N_EXPERTS - N_GROUPS, D_MODEL), F32)], axis=0)
    br = jnp.concatenate([router_expert_b[l], router_group_b[l],
                          jnp.zeros((ROUTER_ROWS - N_EXPERTS - N_GROUPS,), F32)]).reshape(ROUTER_ROWS, 1)
    return [
        norm1_g[l].reshape(1, D_MODEL), w_in[l].astype(BF16),
        gmlp_ln_g[l].reshape(1, D_A), gmlp_ln_b[l].reshape(1, D_A), wsm.astype(BF16), bsx,
        conv_w[l], conv_b[l].reshape(1, D_B),
        _head_blocks(lru_w_a[l]), lru_b_a[l].reshape(1, D_B),
        _head_blocks(lru_w_x[l]), lru_b_x[l].reshape(1, D_B),
        lru_lambda[l].reshape(1, D_B), w_out[l].astype(BF16), norm2_g[l].reshape(1, D_MODEL),
        wr.astype(BF16), br,
    ]


def _routing_tables(cnt, n_items_max):
    n_tiles = cnt.shape[0]
    seg = (cnt + SUBLANES - 1) // SUBLANES * SUBLANES
    nch = seg // SUBLANES
    tot = jnp.sum(seg, axis=0)
    padded = (tot + FFN_UNIT - 1) // FFN_UNIT * FFN_UNIT
    e_before = jnp.arange(N_EXPERTS)[None, :] < jnp.arange(N_EXPERTS)[:, None]
    t_before = jnp.arange(n_tiles)[None, :] < jnp.arange(n_tiles)[:, None]
    pad_start = jnp.sum(jnp.where(e_before, padded[None, :], 0), axis=1)
    doff = pad_start[None, :] + jnp.sum(jnp.where(t_before[:, :, None], seg[None, :, :], 0), axis=1)
    locoff = jnp.sum(jnp.where(e_before[None, :, :], seg[:, None, :], 0), axis=2)
    ntot = jnp.sum(nch, axis=1)
    whole = nch // COPY_CLASSES[0][0]
    done = whole * COPY_CLASSES[0][0] * SUBLANES
    nops, olocs, odsts = [], [], []
    for chunks, cap in COPY_CLASSES:
        if chunks == COPY_CLASSES[0][0]:
            m, loc0, dst0 = whole, locoff, doff
        else:
            m, loc0, dst0 = (nch - whole * COPY_CLASSES[0][0] == chunks).astype(jnp.int32), locoff + done, doff + done
        start = jnp.sum(jnp.where(e_before[None, :, :], m[:, None, :], 0), axis=2)
        s = jnp.arange(cap)
        owner = jnp.sum((start + m)[:, None, :] <= s[None, :, None], axis=2)
        own = owner[:, :, None] == jnp.arange(N_EXPERTS)[None, None, :]
        step = chunks * SUBLANES * (s[None, :, None] - start[:, None, :])
        olocs.append(jnp.sum(jnp.where(own, loc0[:, None, :] + step, 0), axis=2).reshape(-1))
        odsts.append(jnp.sum(jnp.where(own, dst0[:, None, :] + step, 0), axis=2).reshape(-1))
        nops.append(jnp.sum(m, axis=1))
    tail = pad_start + tot
    ntail = (padded - tot) // SUBLANES
    units = padded // FFN_UNIT
    n_it = (units + FFN_ITEM_UNITS - 1) // FFN_ITEM_UNITS
    it_start = jnp.sum(jnp.where(e_before, n_it[None, :], 0), axis=1)
    t = jnp.arange(n_items_max)
    e_ids = jnp.arange(N_EXPERTS)
    owner = jnp.minimum(jnp.sum((it_start + n_it)[None, :] <= t[:, None], axis=1), N_EXPERTS - 1)
    own = owner[:, None] == e_ids[None, :]
    pick = lambda v: jnp.sum(jnp.where(own, v[None, :], 0), axis=1)
    j = t - pick(it_start)
    istart = pick(pad_start) + j * FFN_BLOCK
    iunits = jnp.clip(pick(units) - FFN_ITEM_UNITS * j, 1, FFN_ITEM_UNITS)
    ordinal = jnp.sum(jnp.where(e_before, (n_it > 0).astype(jnp.int32)[None, :], 0), axis=1)
    later = (e_ids[None, :] > e_ids[:, None]) & (n_it > 0)[None, :]
    nxt = jnp.min(jnp.where(later, e_ids[None, :], N_EXPERTS), axis=1)
    nxt = jnp.where(nxt == N_EXPERTS, -1, nxt)
    i32 = lambda a: a.astype(jnp.int32)
    items = (i32(jnp.sum(n_it).reshape(1)), i32(istart), i32(iunits), i32(owner), i32(j == 0),
             i32(pick(ordinal) % 2), i32(pick(nxt)))
    ops = (i32(jnp.concatenate(nops)), i32(jnp.concatenate(olocs)), i32(jnp.concatenate(odsts)))
    return ops, i32(ntot), i32(tail), i32(ntail), locoff.astype(F32)[:, :, None], items


def kernel(x_prompt, x_sample, state_conv, state_rglru, norm1_g, w_in, gmlp_ln_g, gmlp_ln_b, gmlp_w_s, gmlp_b_s,
           conv_w, conv_b, lru_w_a, lru_b_a, lru_w_x, lru_b_x, lru_lambda, w_out, norm2_g,
           router_group_w, router_group_b, router_expert_w, router_expert_b,
           expert_w_gate, expert_w_up, expert_w_down, final_norm_g):
    depth = w_in.shape[0]
    nb, t_len, _ = x_prompt.shape
    db, dt, _ = x_sample.shape
    n_prompt = nb * t_len
    n_sample = db * dt
    assert t_len % PROMPT_TILE == 0 and n_prompt % TOKEN_TILE == 0 and n_sample == TOKEN_TILE
    n_total = n_prompt + n_sample
    assert n_total % STEP_TOKENS == 0
    n_tiles = n_total // TOKEN_TILE
    p_tiles = n_prompt // TOKEN_TILE
    p_rows = 2 * n_total + (SUBLANES - 1) * N_EXPERTS * n_tiles + N_EXPERTS * (FFN_UNIT - 1)
    p_rows = -(-p_rows // FFN_UNIT) * FFN_UNIT
    n_items_max = p_rows // FFN_BLOCK + N_EXPERTS
    upper = jnp.triu(jnp.ones((TOKEN_TILE, TOKEN_TILE), BF16), 1)

    xp, xs = x_prompt, x_sample
    conv_p, h_p, v_s, conv_s, h_s = [], [], [], [], []
    for l in range(depth):
        lw = (norm1_g, w_in, gmlp_ln_g, gmlp_ln_b, gmlp_w_s, gmlp_b_s, conv_w, conv_b, lru_w_a, lru_b_a,
              lru_w_x, lru_b_x, lru_lambda, w_out, norm2_g, router_group_w, router_group_b,
              router_expert_w, router_expert_b)
        wts_p = _mixer_weights(l, GMLP_CHUNK, 1, *lw)
        wts_s = _mixer_weights(l, dt, db, *lw)
        zero_conv = jnp.zeros((nb, CONV_WIDTH - 1, D_B), F32)
        zero_h = jnp.zeros((nb, 1, D_B), F32)
        y1, h2, route, cnt_p, cp, hp = _mixer_call(
            xp, zero_conv, zero_h, wts_p, n_total=n_total, row_offset=0, sb=1, tt=PROMPT_TILE, chunk=GMLP_CHUNK, seg_len=PROMPT_SEG, seg_stride=PROMPT_SEG + SUBLANES, chain=True, emit_vn=False)
        y1, h2, route, cnt_s, cs, hs, vs = _mixer_call(
            xs, state_conv[l], state_rglru[l][None], wts_s, n_total=n_total, row_offset=n_prompt, sb=db,
            tt=dt, chunk=db * dt, seg_len=dt, seg_stride=dt + SUBLANES, chain=False, emit_vn=True,
            alias_in=(y1, h2, route))
        conv_p.append(cp)
        h_p.append(hp[:, 0])
        v_s.append(vs)
        conv_s.append(cs)
        h_s.append(hs[0])

        sub = PROMPT_TILE // TOKEN_TILE
        cnt = jnp.concatenate([
            jnp.swapaxes(cnt_p[:, :, :sub], 1, 2).reshape(p_tiles, N_EXPERTS),
            cnt_s[:, :, 0]], axis=0).astype(jnp.int32)
        ops, ntot, tail, ntail, locoff, items = _routing_tables(cnt, n_items_max)

        xb, slots = _dispatch_call(h2, route, locoff, upper, ops, ntot, tail, ntail, n_tiles=n_tiles,
                                   p_rows=p_rows)
        yb = _ffn_call(xb, expert_w_gate[l], expert_w_up[l], expert_w_down[l], items, p_rows=p_rows)
        assert l == depth - 1, "deeper stacks need an un-normalised combine between layers"
        fg = final_norm_g.reshape(1, D_MODEL)
        yp, ysm = _combine_call(y1, slots, fg, yb, ops, ntot, n_tiles=n_tiles, n_prompt=n_prompt,
                                n_sample=n_sample)
        xp = yp.reshape(nb, t_len, D_MODEL)
        xs = ysm.reshape(db, dt, D_MODEL)

    return (xp, xs, jnp.stack(conv_p), jnp.stack(h_p), jnp.stack(v_s), jnp.stack(conv_s), jnp.stack(h_s))
```

```python
import functools

import jax
import jax.numpy as jnp
from jax import lax
from jax.experimental import pallas as pl
from jax.experimental.pallas import tpu as pltpu

D_MODEL = 1024
D_A = 512
D_B = 512
A_GROUPS = 4
A_GROUP_DIM = 128
GMLP_CHUNK = 128
B_HEADS = 8
CONV_WIDTH = 4
LRU_C = 8.0
N_GROUPS = 4
EXPERTS_PER_GROUP = 8
N_EXPERTS = 32
D_EXPERT = 512
EPS = 1e-6
TINY = 1e-30

SUBLANES = 8
LANES = 128
ROUTER_ROWS = 40
TOKEN_TILE = 256
PROMPT_TILE = 1024
PROMPT_SEG = PROMPT_TILE // SUBLANES
FFN_UNIT = 256
FFN_ITEM_UNITS = 4
FFN_BLOCK = FFN_ITEM_UNITS * FFN_UNIT
ZERO_CHUNKS = 8
ZERO_ROWS = ZERO_CHUNKS * SUBLANES
SLOT_ROWS = 2 * TOKEN_TILE + N_EXPERTS * SUBLANES
TILE_CHUNKS = SLOT_ROWS // SUBLANES
COPY_SLOTS = N_EXPERTS
COPY_CLASSES = ((4, COPY_SLOTS), (3, COPY_SLOTS), (2, COPY_SLOTS), (1, COPY_SLOTS))
assert TILE_CHUNKS // 4 <= COPY_SLOTS
TILES_PER_STEP = 3
STEP_TOKENS = TILES_PER_STEP * TOKEN_TILE
PACK_COLS = D_MODEL // 2
XB_COLS = PACK_COLS + LANES
U32 = jnp.uint32
HI_MASK = 0xFFFF0000
VMEM_BYTES_V7X = 64 * 1024 * 1024
VMEM_LIMIT = VMEM_BYTES_V7X - 8 * 1024 * 1024

BF16 = jnp.bfloat16
F32 = jnp.float32


def _rms(x, g):
    return x * lax.rsqrt(jnp.mean(x * x, axis=-1, keepdims=True) + EPS) * g


def _dot(a, b):
    return jnp.dot(a, b, preferred_element_type=F32)


def _sigmoid(x):
    return 1.0 / (1.0 + jnp.exp(-x))


def _pack_bf16_pairs(x):
    bits = lax.bitcast_convert_type(x, U32)
    return (bits[:, PACK_COLS:] & U32(HI_MASK)) | (bits[:, :PACK_COLS] >> 16)


def _unpack_bf16_pairs(w):
    lo = lax.bitcast_convert_type(w << 16, F32).astype(BF16)
    hi = lax.bitcast_convert_type(w & U32(HI_MASK), F32).astype(BF16)
    return jnp.concatenate([lo, hi], axis=1)


def _mixer_kernel(x_ref, hist_ref, h0_ref, n1g_ref, win_ref, lng_ref, lnb_ref, wsm_ref, bsx_ref,
                  cw_ref, cb_ref, wa_ref, ba_ref, wx_ref, bx_ref, lam_ref, wout_ref, n2g_ref,
                  wr_ref, br_ref, *rest, sb, tt, chunk, seg_len, seg_stride, chain, emit_vn, aliased):
    if aliased:
        rest = rest[3:]
    if emit_vn:
        (y1_ref, h2_ref, route_ref, cnt_ref, conv_ref, hlast_ref, vn_ref,
         xs_ref, sa_ref, sb_ref, mix_ref, hc_ref) = rest
    else:
        (y1_ref, h2_ref, route_ref, cnt_ref, conv_ref, hlast_ref,
         xs_ref, sa_ref, sb_ref, mix_ref, hc_ref) = rest
        vn_ref = None
    t_idx = pl.program_id(1)

    @pl.when(t_idx == 0)
    def _():
        xs_ref[:, SUBLANES - 3:SUBLANES, :] = hist_ref[...]
        hc_ref[...] = jnp.zeros_like(hc_ref)

    @pl.when(t_idx != 0)
    def _():
        xs_ref[:, SUBLANES - 3:SUBLANES, :] = xs_ref[:, SUBLANES + tt - 3:SUBLANES + tt, :]

    rows = sb * tt
    n_seg = rows // seg_len
    half = D_B // 2

    x = x_ref[...].reshape(rows, D_MODEL)
    hb = _rms(x, n1g_ref[...]).astype(BF16)

    v_a = _dot(hb, win_ref[:, D_A:2 * D_A])
    x_b = _dot(hb, win_ref[:, 2 * D_A:2 * D_A + D_B])
    mixed = []
    for g in range(A_GROUPS):
        cols = slice(g * A_GROUP_DIM, (g + 1) * A_GROUP_DIM)
        vg = v_a[:, cols]
        mu = jnp.mean(vg, axis=-1, keepdims=True)
        dv = vg - mu
        var = jnp.mean(dv * dv, axis=-1, keepdims=True)
        vn = dv * lax.rsqrt(var + EPS) * lng_ref[:, cols] + lnb_ref[:, cols]
        if vn_ref is not None:
            vn_ref[:, :, cols] = vn.reshape(sb, tt, A_GROUP_DIM)
        vnb = vn.astype(BF16)
        mixed.append([_dot(wsm_ref[g], vnb[c * chunk:(c + 1) * chunk]) for c in range(rows // chunk)])
    u_a = _dot(hb, win_ref[:, 0:D_A])

    xs_ref[:, SUBLANES:SUBLANES + tt, :] = x_b.reshape(sb, tt, D_B)
    xc = cb_ref[...] + x_b * cw_ref[3:4, :]
    for j in range(1, CONV_WIDTH):
        shifted = xs_ref[:, SUBLANES - j:SUBLANES - j + tt, :].reshape(rows, D_B)
        xc = xc + shifted * cw_ref[3 - j:4 - j, :]

    xcb = xc.astype(BF16)
    r_lin = jnp.concatenate([_dot(xcb[:, :half], wa_ref[0]), _dot(xcb[:, half:], wa_ref[1])], axis=1)
    i_lin = jnp.concatenate([_dot(xcb[:, :half], wx_ref[0]), _dot(xcb[:, half:], wx_ref[1])], axis=1)

    for g in range(A_GROUPS):
        cols = slice(g * A_GROUP_DIM, (g + 1) * A_GROUP_DIM)
        for c in range(rows // chunk):
            rs = slice(c * chunk, (c + 1) * chunk)
            s = mixed[g][c] + bsx_ref[:, cols]
            mix_ref[rs, cols] = (u_a[rs, cols] * s).astype(BF16)

    g_b = _dot(hb, win_ref[:, 2 * D_A + D_B:])
    r = _sigmoid(r_lin + ba_ref[...])
    i_g = _sigmoid(i_lin + bx_ref[...])
    log_a = (-LRU_C * r) * jax.nn.softplus(-lam_ref[...])
    a = jnp.exp(log_a)
    y_gain = 1.0 - a * a
    gain = y_gain * lax.rsqrt(jnp.maximum(y_gain, TINY))
    bterm = gain * i_g * xc

    n_lb = D_B // LANES
    for j in range(n_seg):
        for k in range(n_lb):
            lc = slice(k * LANES, (k + 1) * LANES)
            dst = slice(j * seg_stride, j * seg_stride + seg_len)
            sa_ref[k, dst, :] = a[j * seg_len:(j + 1) * seg_len, lc]
            sb_ref[k, dst, :] = bterm[j * seg_len:(j + 1) * seg_len, lc]

    y_a = x + _dot(mix_ref[:, 0:D_A], wout_ref[0:D_A, :])
    gel = jax.nn.gelu(g_b)

    for grp in range(n_seg // SUBLANES):
        base = grp * SUBLANES * seg_stride
        bs = slice(grp * SUBLANES, (grp + 1) * SUBLANES)
        if chain:
            h_init = tuple(jnp.zeros((SUBLANES, LANES), F32) for _ in range(n_lb))
        else:
            h_init = tuple(h0_ref[0, bs, k * LANES:(k + 1) * LANES] for k in range(n_lb))
        a_init = tuple(jnp.ones((SUBLANES, LANES), F32) for _ in range(n_lb))

        def step(i, carry, base=base):
            hs, acs = carry
            idx = pl.ds(base + i, SUBLANES, stride=seg_stride)
            new_h, new_a = [], []
            for k in range(n_lb):
                av = sa_ref[k, idx, :]
                h = av * hs[k] + sb_ref[k, idx, :]
                sb_ref[k, idx, :] = h
                new_h.append(h)
                if chain:
                    ac = av * acs[k]
                    sa_ref[k, idx, :] = ac
                    new_a.append(ac)
                else:
                    new_a.append(acs[k])
            return tuple(new_h), tuple(new_a)

        carry = (h_init, a_init)
        for i in range(seg_len):
            carry = step(i, carry)
        h_end, a_end = carry

        for k in range(n_lb):
            lc = slice(k * LANES, (k + 1) * LANES)
            mc = slice(D_A + k * LANES, D_A + (k + 1) * LANES)
            if chain:
                h_in = hc_ref[:, lc]
                for j in range(SUBLANES):
                    seg = slice(base + j * seg_stride, base + j * seg_stride + seg_len)
                    rs = slice((grp * SUBLANES + j) * seg_len, (grp * SUBLANES + j + 1) * seg_len)
                    h_seg = sb_ref[k, seg, :] + sa_ref[k, seg, :] * h_in
                    mix_ref[rs, mc] = (h_seg * gel[rs, lc]).astype(BF16)
                    h_in = h_end[k][j:j + 1] + a_end[k][j:j + 1] * h_in
                hc_ref[:, lc] = h_in
                hlast_ref[0, :, lc] = h_in
            else:
                for j in range(SUBLANES):
                    seg = slice(base + j * seg_stride, base + j * seg_stride + seg_len)
                    rs = slice((grp * SUBLANES + j) * seg_len, (grp * SUBLANES + j + 1) * seg_len)
                    mix_ref[rs, mc] = (sb_ref[k, seg, :] * gel[rs, lc]).astype(BF16)
                hlast_ref[0, bs, lc] = h_end[k]

    y1 = y_a + _dot(mix_ref[:, D_A:], wout_ref[D_A:, :])
    y1_ref[...] = y1
    h2 = _rms(y1, n2g_ref[...]).astype(BF16)
    h2_ref[...] = h2

    logits = lax.dot_general(wr_ref[...], h2, (((1,), (1,)), ((), ())), preferred_element_type=F32) + br_ref[...]
    gl = [logits[N_EXPERTS + k:N_EXPERTS + k + 1, :] for k in range(N_GROUPS)]
    gmax = jnp.maximum(jnp.maximum(gl[0], gl[1]), jnp.maximum(gl[2], gl[3]))
    grp_idx = jnp.where(gl[0] == gmax, 0, jnp.where(gl[1] == gmax, 1, jnp.where(gl[2] == gmax, 2, 3)))
    gsum = (jnp.exp(gl[0] - gmax) + jnp.exp(gl[1] - gmax)) + (jnp.exp(gl[2] - gmax) + jnp.exp(gl[3] - gmax))
    p_grp = 1.0 / gsum
    e_in = jnp.where(grp_idx == 0, logits[0:8],
                     jnp.where(grp_idx == 1, logits[8:16], jnp.where(grp_idx == 2, logits[16:24], logits[24:32])))
    iota8 = lax.broadcasted_iota(jnp.int32, (EXPERTS_PER_GROUP, rows), 0)
    m1 = jnp.max(e_in, axis=0, keepdims=True)
    i1 = jnp.min(jnp.where(e_in == m1, iota8, EXPERTS_PER_GROUP), axis=0, keepdims=True)
    e_rest = jnp.where(iota8 == i1, -jnp.inf, e_in)
    m2 = jnp.max(e_rest, axis=0, keepdims=True)
    i2 = jnp.min(jnp.where(e_rest == m2, iota8, EXPERTS_PER_GROUP), axis=0, keepdims=True)
    t2 = jnp.exp(m2 - m1)
    den = 1.0 + t2
    gate0 = p_grp * (1.0 / den)
    gate1 = p_grp * (t2 / den)
    e0 = (grp_idx * EXPERTS_PER_GROUP + i1).astype(F32)
    e1 = (grp_idx * EXPERTS_PER_GROUP + i2).astype(F32)
    route_ref[...] = jnp.where(iota8 == 0, e0, jnp.where(iota8 == 1, e1, jnp.where(
        iota8 == 2, gate0, jnp.where(iota8 == 3, gate1, 0.0))))

    iota_e = lax.broadcasted_iota(jnp.int32, (N_EXPERTS, rows), 0).astype(F32)
    onehot = (iota_e == e0).astype(F32) + (iota_e == e1).astype(F32)
    lane = lax.broadcasted_iota(jnp.int32, (N_EXPERTS, LANES), 1)
    cnt = jnp.zeros((N_EXPERTS, LANES), F32)
    for s in range(rows // TOKEN_TILE):
        c_s = jnp.sum(onehot[:, s * TOKEN_TILE:(s + 1) * TOKEN_TILE], axis=1, keepdims=True)
        cnt = cnt + jnp.where(lane == s, c_s, 0.0)
    cnt_ref[...] = cnt.reshape(1, N_EXPERTS, LANES)
    conv_ref[...] = xs_ref[:, SUBLANES + tt - 3:SUBLANES + tt, :]


def _mixer_call(x, hist, h0, wts, *, n_total, row_offset, sb, tt, chunk, seg_len, seg_stride, chain,
                emit_vn, alias_in=None):
    nb, t_len, _ = x.shape
    nbb = nb // sb
    ntt = t_len // tt
    rows = sb * tt
    blk0 = row_offset // rows
    n_seg = rows // seg_len

    def full(arr):
        nd = arr.ndim
        return pl.BlockSpec(arr.shape, lambda b, t, _nd=nd: (0,) * _nd)

    in_specs = [
        pl.BlockSpec((sb, tt, D_MODEL), lambda b, t: (b, t, 0)),
        pl.BlockSpec((sb, CONV_WIDTH - 1, D_B), lambda b, t: (b, 0, 0)),
        pl.BlockSpec((1, sb, D_B), lambda b, t: (b, 0, 0)),
    ] + [full(w) for w in wts]
    args = [x, hist, h0] + list(wts)
    io_alias = {}
    if alias_in is not None:
        for k, arr in enumerate(alias_in):
            io_alias[len(args)] = k
            in_specs.append(pl.BlockSpec(memory_space=pl.ANY))
            args.append(arr)

    out_shape = [
        jax.ShapeDtypeStruct((n_total, D_MODEL), F32),
        jax.ShapeDtypeStruct((n_total, D_MODEL), BF16),
        jax.ShapeDtypeStruct((SUBLANES, n_total), F32),
        jax.ShapeDtypeStruct((nbb * ntt, N_EXPERTS, LANES), F32),
        jax.ShapeDtypeStruct((nb, CONV_WIDTH - 1, D_B), F32),
        jax.ShapeDtypeStruct((nbb, sb, D_B), F32),
    ]
    out_specs = [
        pl.BlockSpec((rows, D_MODEL), lambda b, t: (blk0 + b * ntt + t, 0)),
        pl.BlockSpec((rows, D_MODEL), lambda b, t: (blk0 + b * ntt + t, 0)),
        pl.BlockSpec((SUBLANES, rows), lambda b, t: (0, blk0 + b * ntt + t)),
        pl.BlockSpec((1, N_EXPERTS, LANES), lambda b, t: (b * ntt + t, 0, 0)),
        pl.BlockSpec((sb, CONV_WIDTH - 1, D_B), lambda b, t: (b, 0, 0)),
        pl.BlockSpec((1, sb, D_B), lambda b, t: (b, 0, 0)),
    ]
    if emit_vn:
        out_shape.append(jax.ShapeDtypeStruct((nb, t_len, D_A), F32))
        out_specs.append(pl.BlockSpec((sb, tt, D_A), lambda b, t: (b, t, 0)))

    kern = functools.partial(_mixer_kernel, sb=sb, tt=tt, chunk=chunk, seg_len=seg_len, seg_stride=seg_stride,
                             chain=chain, emit_vn=emit_vn, aliased=alias_in is not None)
    return pl.pallas_call(
        kern,
        grid=(nbb, ntt),
        in_specs=in_specs,
        out_specs=out_specs,
        out_shape=out_shape,
        scratch_shapes=[
            pltpu.VMEM((sb, SUBLANES + tt, D_B), F32),
            pltpu.VMEM((D_B // LANES, n_seg * seg_stride, LANES), F32),
            pltpu.VMEM((D_B // LANES, n_seg * seg_stride, LANES), F32),
            pltpu.VMEM((rows, D_MODEL), BF16),
            pltpu.VMEM((1, D_B), F32),
        ],
        input_output_aliases=io_alias,
        compiler_params=pltpu.CompilerParams(dimension_semantics=("arbitrary", "arbitrary"),
                                             vmem_limit_bytes=VMEM_LIMIT),
        name="mixer_chain" if chain else "mixer_step",
    )(*args)


def _tile_slots(route_ref, locoff_ref, upper_ref, u):
    lanes = slice(u * TOKEN_TILE, (u + 1) * TOKEN_TILE)
    e0 = route_ref[0:1, lanes]
    e1 = route_ref[1:2, lanes]
    iota_e = lax.broadcasted_iota(jnp.int32, (N_EXPERTS, TOKEN_TILE), 0).astype(F32)
    oh0 = (iota_e == e0).astype(F32)
    oh1 = (iota_e == e1).astype(F32)
    c0 = _dot(oh0.astype(BF16), upper_ref[...])
    c1 = _dot(oh1.astype(BF16), upper_ref[...])
    cnt0 = jnp.sum(oh0, axis=1, keepdims=True)
    base0 = locoff_ref[u]
    base1 = base0 + cnt0
    p0 = jnp.sum(oh0 * (base0 + c0), axis=0, keepdims=True)
    p1 = jnp.sum(oh1 * (base1 + c1), axis=0, keepdims=True)
    return p0.astype(jnp.int32), p1.astype(jnp.int32)


def _copy_ops(tile, n_tiles, nops_ref, oloc_ref, odst_ref, make_copy):
    base = 0
    for k, (chunks, cap) in enumerate(COPY_CLASSES):
        n = nops_ref[k * n_tiles + tile]

        def per_pair(i, _, chunks=chunks, cap=cap, base=base, n=n):
            idx = base + tile * cap + 2 * i
            make_copy(pl.multiple_of(oloc_ref[idx], SUBLANES), pl.multiple_of(odst_ref[idx], SUBLANES),
                      chunks * SUBLANES)

            @pl.when(2 * i + 1 < n)
            def _():
                make_copy(pl.multiple_of(oloc_ref[idx + 1], SUBLANES), pl.multiple_of(odst_ref[idx + 1], SUBLANES),
                          chunks * SUBLANES)
            return 0

        assert cap % 2 == 0
        lax.fori_loop(0, (n + 1) // 2, per_pair, 0)
        base += n_tiles * cap


def _dispatch_kernel(nops_ref, oloc_ref, odst_ref, ntot_ref, tail_ref, ntail_ref,
                     h2_ref, route_ref, locoff_ref, upper_ref, xb_ref, slots_ref, buf_ref, zero_ref, sem, zsem,
                     *, n_tiles):
    i = pl.program_id(0)
    n_steps = pl.num_programs(0)
    slot = i % 2

    def chunk_copy(s, u, loc, dst, rows=SUBLANES):
        return pltpu.make_async_copy(buf_ref.at[s, u, pl.ds(loc, rows), :],
                                     xb_ref.at[pl.ds(dst, rows), :], sem.at[s])

    def wait_step(step, s):
        for u in range(TILES_PER_STEP):
            n_rows = ntot_ref[step * TILES_PER_STEP + u] * SUBLANES

            @pl.when(n_rows > 0)
            def _():
                pltpu.make_async_copy(buf_ref.at[s, u, pl.ds(0, n_rows), :], xb_ref.at[pl.ds(0, n_rows), :],
                                      sem.at[s]).wait()

    @pl.when(i >= 2)
    def _():
        wait_step(i - 2, slot)

    iota_s = lax.broadcasted_iota(jnp.int32, (SLOT_ROWS, TOKEN_TILE), 0)
    iota8 = lax.broadcasted_iota(jnp.int32, (SUBLANES, TOKEN_TILE), 0)
    for u in range(TILES_PER_STEP):
        lanes = slice(u * TOKEN_TILE, (u + 1) * TOKEN_TILE)
        p0, p1 = _tile_slots(route_ref, locoff_ref, upper_ref, u)
        slots_ref[:, lanes] = jnp.where(iota8 == 0, p0, jnp.where(iota8 == 1, p1, 0))
        hit0 = iota_s == p0
        hit1 = iota_s == p1
        perm = jnp.where(hit0 | hit1, 1.0, 0.0).astype(BF16)
        sorted_rows = _dot(perm, h2_ref[u * TOKEN_TILE:(u + 1) * TOKEN_TILE, :])
        buf_ref[slot, u, :, 0:PACK_COLS] = _pack_bf16_pairs(sorted_rows)
        gates = jnp.where(hit0, route_ref[2:3, lanes], 0.0) + jnp.where(hit1, route_ref[3:4, lanes], 0.0)
        gcol = jnp.sum(gates, axis=1, keepdims=True)
        buf_ref[slot, u, :, PACK_COLS:XB_COLS] = lax.bitcast_convert_type(
            jnp.broadcast_to(gcol, (SLOT_ROWS, LANES)), U32)

    for u in range(TILES_PER_STEP):
        _copy_ops(i * TILES_PER_STEP + u, n_tiles, nops_ref, oloc_ref, odst_ref,
                  lambda loc, dst, rows, u=u: chunk_copy(slot, u, loc, dst, rows).start())

    @pl.when(i == n_steps - 1)
    def _():
        zero_ref[...] = jnp.zeros_like(zero_ref)

        def zero_copy(dst, rows, k):
            return pltpu.make_async_copy(zero_ref.at[pl.ds(0, rows), :], xb_ref.at[pl.ds(dst, rows), :], zsem.at[k])

        def per_expert(e, tot):
            n_big, n_small = tot
            n = ntail_ref[e]
            d0 = tail_ref[e]
            nb = n // ZERO_CHUNKS
            ns = n - nb * ZERO_CHUNKS

            def big(c, _):
                zero_copy(pl.multiple_of(d0 + c * ZERO_ROWS, SUBLANES), ZERO_ROWS, 0).start()
                return 0

            def small(c, _):
                zero_copy(pl.multiple_of(d0 + nb * ZERO_ROWS + c * SUBLANES, SUBLANES), SUBLANES, 1).start()
                return 0

            lax.fori_loop(0, nb, big, 0)
            lax.fori_loop(0, ns, small, 0)
            return n_big + nb, n_small + ns

        n_big, n_small = lax.fori_loop(0, N_EXPERTS, per_expert, (0, 0))

        @pl.when(i >= 1)
        def _():
            wait_step(i - 1, 1 - slot)

        wait_step(i, slot)

        def wait_big(c, _):
            zero_copy(0, ZERO_ROWS, 0).wait()
            return 0

        def wait_small(c, _):
            zero_copy(0, SUBLANES, 1).wait()
            return 0

        lax.fori_loop(0, n_big, wait_big, 0)
        lax.fori_loop(0, n_small, wait_small, 0)


def _dispatch_call(h2, route, locoff, upper, ops, ntot, tail, ntail, *, n_tiles, p_rows):
    grid_spec = pltpu.PrefetchScalarGridSpec(
        num_scalar_prefetch=6,
        grid=(n_tiles // TILES_PER_STEP,),
        in_specs=[
            pl.BlockSpec((STEP_TOKENS, D_MODEL), lambda i, *_: (i, 0)),
            pl.BlockSpec((SUBLANES, STEP_TOKENS), lambda i, *_: (0, i)),
            pl.BlockSpec((TILES_PER_STEP, N_EXPERTS, 1), lambda i, *_: (i, 0, 0)),
            pl.BlockSpec((TOKEN_TILE, TOKEN_TILE), lambda i, *_: (0, 0)),
        ],
        out_specs=[pl.BlockSpec(memory_space=pl.ANY),
                   pl.BlockSpec((SUBLANES, STEP_TOKENS), lambda i, *_: (0, i))],
        scratch_shapes=[
            pltpu.VMEM((2, TILES_PER_STEP, SLOT_ROWS, XB_COLS), U32),
            pltpu.VMEM((ZERO_ROWS, XB_COLS), U32),
            pltpu.SemaphoreType.DMA((2,)),
            pltpu.SemaphoreType.DMA((2,)),
        ],
    )
    return pl.pallas_call(
        functools.partial(_dispatch_kernel, n_tiles=n_tiles),
        grid_spec=grid_spec,
        out_shape=[jax.ShapeDtypeStruct((p_rows, XB_COLS), U32),
                   jax.ShapeDtypeStruct((SUBLANES, n_tiles * TOKEN_TILE), jnp.int32)],
        compiler_params=pltpu.CompilerParams(dimension_semantics=("arbitrary",), vmem_limit_bytes=VMEM_LIMIT),
        name="dispatch",
    )(*ops, ntot, tail, ntail, h2, route, locoff, upper)


def _ffn_kernel(nitems_ref, istart_ref, iunits_ref, iexp_ref, ifirst_ref, iwslot_ref, inext_ref,
                xb_ref, wg_ref, wu_ref, wd_ref, yb_ref,
                xin_ref, yout_ref, wgf_ref, wuf_ref, wdf_ref, wgb_ref, wub_ref, wdb_ref, sem_in, sem_out, sem_w):
    n_items = nitems_ref[0]
    sizes = tuple(k * FFN_UNIT for k in range(FFN_ITEM_UNITS, 0, -1))

    def rows_in(t, s, rows):
        start = pl.multiple_of(istart_ref[t], FFN_UNIT)
        return pltpu.make_async_copy(xb_ref.at[pl.ds(start, rows), :], xin_ref.at[s, pl.ds(0, rows), :], sem_in.at[s])

    def rows_out(t, s, rows):
        start = pl.multiple_of(istart_ref[t], FFN_UNIT)
        return pltpu.make_async_copy(yout_ref.at[s, pl.ds(0, rows), :], yb_ref.at[pl.ds(start, rows), :],
                                     sem_out.at[s])

    def start_by_size(make, t, s):
        for rows in sizes:
            @pl.when(iunits_ref[t] * FFN_UNIT == rows)
            def _():
                make(t, s, rows).start()

    def weight_copies(e, ws):
        return (pltpu.make_async_copy(wg_ref.at[e], wgf_ref.at[ws], sem_w.at[ws]),
                pltpu.make_async_copy(wu_ref.at[e], wuf_ref.at[ws], sem_w.at[ws]),
                pltpu.make_async_copy(wd_ref.at[e], wdf_ref.at[ws], sem_w.at[ws]))

    def compute(s, rows):
        xb = _unpack_bf16_pairs(xin_ref[s, 0:rows, 0:PACK_COLS])
        gate = lax.bitcast_convert_type(xin_ref[s, 0:rows, PACK_COLS:PACK_COLS + 1], F32)
        a = _dot(xb, wgb_ref[...])
        u = _dot(xb, wub_ref[...])
        mid = (a * _sigmoid(a) * u).astype(BF16)
        y = _dot(mid, wdb_ref[...]) * gate
        yout_ref[s, 0:rows, :] = _pack_bf16_pairs(y.astype(BF16).astype(F32))

    @pl.when(n_items > 0)
    def _():
        for c in weight_copies(iexp_ref[0], 0):
            c.start()
        start_by_size(rows_in, 0, 0)

    def per_item(t, _):
        s = t % 2

        @pl.when(t + 1 < n_items)
        def _():
            start_by_size(rows_in, t + 1, 1 - s)

        @pl.when(ifirst_ref[t] == 1)
        def _():
            ws = iwslot_ref[t]
            for c in weight_copies(0, ws):
                c.wait()

            @pl.when(inext_ref[t] >= 0)
            def _():
                for c in weight_copies(inext_ref[t], 1 - ws):
                    c.start()

            wgb_ref[...] = wgf_ref[ws].astype(BF16)
            wub_ref[...] = wuf_ref[ws].astype(BF16)
            wdb_ref[...] = wdf_ref[ws].astype(BF16)

        rows_in(t, s, iunits_ref[t] * FFN_UNIT).wait()

        @pl.when(t >= 2)
        def _():
            rows_out(t - 2, s, iunits_ref[t - 2] * FFN_UNIT).wait()

        for rows in sizes:
            @pl.when(iunits_ref[t] * FFN_UNIT == rows)
            def _():
                compute(s, rows)

        start_by_size(rows_out, t, s)
        return 0

    lax.fori_loop(0, n_items, per_item, 0)

    for back in (2, 1):
        @pl.when(n_items >= back)
        def _():
            t = n_items - back
            rows_out(t, t % 2, iunits_ref[t] * FFN_UNIT).wait()


def _ffn_call(xb, wg, wu, wd, items, *, p_rows):
    any_spec = pl.BlockSpec(memory_space=pl.ANY)
    grid_spec = pltpu.PrefetchScalarGridSpec(
        num_scalar_prefetch=len(items),
        grid=(1,),
        in_specs=[any_spec, any_spec, any_spec, any_spec],
        out_specs=any_spec,
        scratch_shapes=[
            pltpu.VMEM((2, FFN_BLOCK, XB_COLS), U32),
            pltpu.VMEM((2, FFN_BLOCK, PACK_COLS), U32),
            pltpu.VMEM((2, D_MODEL, D_EXPERT), F32),
            pltpu.VMEM((2, D_MODEL, D_EXPERT), F32),
            pltpu.VMEM((2, D_EXPERT, D_MODEL), F32),
            pltpu.VMEM((D_MODEL, D_EXPERT), BF16),
            pltpu.VMEM((D_MODEL, D_EXPERT), BF16),
            pltpu.VMEM((D_EXPERT, D_MODEL), BF16),
            pltpu.SemaphoreType.DMA((2,)),
            pltpu.SemaphoreType.DMA((2,)),
            pltpu.SemaphoreType.DMA((2,)),
        ],
    )
    return pl.pallas_call(
        _ffn_kernel,
        grid_spec=grid_spec,
        out_shape=jax.ShapeDtypeStruct((p_rows, PACK_COLS), U32),
        compiler_params=pltpu.CompilerParams(dimension_semantics=("arbitrary",), vmem_limit_bytes=VMEM_LIMIT),
        name="expert_ffn",
    )(*items, xb, wg, wu, wd)


def _combine_kernel(nops_ref, oloc_ref, odst_ref, ntot_ref, y1_ref, slots_ref, fg_ref,
                    yb_ref, yp_ref, ys_ref, buf_ref, sem, *, n_tiles):
    i = pl.program_id(0)
    n_steps = pl.num_programs(0)
    slot = i % 2

    def chunk_copy(s, u, loc, src, rows):
        return pltpu.make_async_copy(yb_ref.at[pl.ds(src, rows), :],
                                     buf_ref.at[s, u, pl.ds(loc, rows), :], sem.at[s])

    def issue_step(step, s):
        for u in range(TILES_PER_STEP):
            _copy_ops(step * TILES_PER_STEP + u, n_tiles, nops_ref, oloc_ref, odst_ref,
                      lambda loc, src, rows, u=u: chunk_copy(s, u, loc, src, rows).start())

    @pl.when(i == 0)
    def _():
        buf_ref[...] = jnp.zeros_like(buf_ref)
        issue_step(0, 0)

    @pl.when(i + 1 < n_steps)
    def _():
        issue_step(i + 1, 1 - slot)

    iota_s = lax.broadcasted_iota(jnp.int32, (SLOT_ROWS, TOKEN_TILE), 0)
    perms = []
    for u in range(TILES_PER_STEP):
        lanes = slice(u * TOKEN_TILE, (u + 1) * TOKEN_TILE)
        p0 = slots_ref[0:1, lanes]
        p1 = slots_ref[1:2, lanes]
        perms.append(jnp.where((iota_s == p0) | (iota_s == p1), 1.0, 0.0).astype(BF16))

    for u in range(TILES_PER_STEP):
        n_rows = ntot_ref[i * TILES_PER_STEP + u] * SUBLANES

        @pl.when(n_rows > 0)
        def _():
            pltpu.make_async_copy(yb_ref.at[pl.ds(0, n_rows), :], buf_ref.at[slot, u, pl.ds(0, n_rows), :],
                                  sem.at[slot]).wait()

    for u in range(TILES_PER_STEP):
        rs = slice(u * TOKEN_TILE, (u + 1) * TOKEN_TILE)
        yb = _unpack_bf16_pairs(buf_ref[slot, u])
        moe = lax.dot_general(perms[u], yb, (((0,), (0,)), ((), ())), preferred_element_type=F32)
        y_tile = _rms(y1_ref[rs, :] + moe, fg_ref[...])
        yp_ref[rs, :] = y_tile

    @pl.when(i == n_steps - 1)
    def _():
        ys_ref[...] = y_tile


def _combine_call(y1, slots, fg, yb, ops, ntot, *, n_tiles, n_prompt, n_sample):
    n_steps = n_tiles // TILES_PER_STEP
    assert n_sample == TOKEN_TILE and n_prompt + n_sample == n_steps * STEP_TOKENS
    grid_spec = pltpu.PrefetchScalarGridSpec(
        num_scalar_prefetch=4,
        grid=(n_steps,),
        in_specs=[
            pl.BlockSpec((STEP_TOKENS, D_MODEL), lambda i, *_: (i, 0)),
            pl.BlockSpec((SUBLANES, STEP_TOKENS), lambda i, *_: (0, i)),
            pl.BlockSpec((1, D_MODEL), lambda i, *_: (0, 0)),
            pl.BlockSpec(memory_space=pl.ANY),
        ],
        out_specs=[
            pl.BlockSpec((STEP_TOKENS, D_MODEL), lambda i, *_: (i, 0)),
            pl.BlockSpec((n_sample, D_MODEL), lambda i, *_: (0, 0)),
        ],
        scratch_shapes=[
            pltpu.VMEM((2, TILES_PER_STEP, SLOT_ROWS, PACK_COLS), U32),
            pltpu.SemaphoreType.DMA((2,)),
        ],
    )
    return pl.pallas_call(
        functools.partial(_combine_kernel, n_tiles=n_tiles),
        grid_spec=grid_spec,
        out_shape=[jax.ShapeDtypeStruct((n_prompt, D_MODEL), F32), jax.ShapeDtypeStruct((n_sample, D_MODEL), F32)],
        compiler_params=pltpu.CompilerParams(dimension_semantics=("arbitrary",), vmem_limit_bytes=VMEM_LIMIT),
        name="combine",
    )(*ops, ntot, y1, slots, fg, yb)


def _block_diag(w):
    h, d, _ = w.shape
    eye = jnp.eye(h, dtype=w.dtype)
    return (eye[:, None, :, None] * w[:, :, None, :]).reshape(h * d, h * d)


def _head_blocks(w):
    half = B_HEADS // 2
    return jnp.stack([_block_diag(w[:half]), _block_diag(w[half:])]).astype(BF16)


def _mixer_weights(l, length, reps, norm1_g, w_in, gmlp_ln_g, gmlp_ln_b, gmlp_w_s, gmlp_b_s, conv_w, conv_b,
                   lru_w_a, lru_b_a, lru_w_x, lru_b_x, lru_lambda, w_out, norm2_g,
                   router_group_w, router_group_b, router_expert_w, router_expert_b):
    mask = jnp.tril(jnp.ones((length, length), dtype=bool))
    ws = jnp.where(mask, gmlp_w_s[l][:, :length, :length], 0.0)
    wsm = jnp.concatenate([jnp.pad(ws, ((0, 0), (0, 0), (j * length, (reps - 1 - j) * length)))
                           for j in range(reps)], axis=1)
    bs = jnp.tile(gmlp_b_s[l][:, :length], (1, reps))
    bsx = jnp.repeat(bs.T, A_GROUP_DIM, axis=1)
    wr = jnp.concatenate([router_expert_w[l].T, router_group_w[l].T,
                          jnp.zeros((ROUTER_ROWS - N_EXPERTS - N_GROUPS, D_MODEL), F32)], axis=0)
    br = jnp.concatenate([router_expert_b[l], router_group_b[l],
                          jnp.zeros((ROUTER_ROWS - N_EXPERTS - N_GROUPS,), F32)]).reshape(ROUTER_ROWS, 1)
    return [
        norm1_g[l].reshape(1, D_MODEL), w_in[l].astype(BF16),
        gmlp_ln_g[l].reshape(1, D_A), gmlp_ln_b[l].reshape(1, D_A), wsm.astype(BF16), bsx,
        conv_w[l], conv_b[l].reshape(1, D_B),
        _head_blocks(lru_w_a[l]), lru_b_a[l].reshape(1, D_B),
        _head_blocks(lru_w_x[l]), lru_b_x[l].reshape(1, D_B),
        lru_lambda[l].reshape(1, D_B), w_out[l].astype(BF16), norm2_g[l].reshape(1, D_MODEL),
        wr.astype(BF16), br,
    ]


def _routing_tables(cnt, n_items_max):
    n_tiles = cnt.shape[0]
    seg = (cnt + SUBLANES - 1) // SUBLANES * SUBLANES
    nch = seg // SUBLANES
    tot = jnp.sum(seg, axis=0)
    padded = (tot + FFN_UNIT - 1) // FFN_UNIT * FFN_UNIT
    e_before = jnp.arange(N_EXPERTS)[None, :] < jnp.arange(N_EXPERTS)[:, None]
    t_before = jnp.arange(n_tiles)[None, :] < jnp.arange(n_tiles)[:, None]
    pad_start = jnp.sum(jnp.where(e_before, padded[None, :], 0), axis=1)
    doff = pad_start[None, :] + jnp.sum(jnp.where(t_before[:, :, None], seg[None, :, :], 0), axis=1)
    locoff = jnp.sum(jnp.where(e_before[None, :, :], seg[:, None, :], 0), axis=2)
    ntot = jnp.sum(nch, axis=1)
    big = COPY_CLASSES[0][0]
    sizes = jnp.array([chunks for chunks, _ in COPY_CLASSES], jnp.int32)
    whole = nch // big
    done = whole * big * SUBLANES
    rest = nch - whole * big
    is_big = (sizes == big)[:, None, None]
    m = jnp.where(is_big, whole[None], (rest[None] == sizes[:, None, None]).astype(jnp.int32))
    loc0 = jnp.where(is_big, locoff[None], (locoff + done)[None])
    dst0 = jnp.where(is_big, doff[None], (doff + done)[None])
    start = jnp.sum(jnp.where(e_before[None, None], m[:, :, None, :], 0), axis=3)
    s = jnp.arange(COPY_SLOTS)
    owner = jnp.sum((start + m)[:, :, None, :] <= s[None, None, :, None], axis=3)
    own = owner[..., None] == jnp.arange(N_EXPERTS)
    step = (sizes * SUBLANES)[:, None, None, None] * (s[None, None, :, None] - start[:, :, None, :])
    olocs = jnp.sum(jnp.where(own, loc0[:, :, None, :] + step, 0), axis=3)
    odsts = jnp.sum(jnp.where(own, dst0[:, :, None, :] + step, 0), axis=3)
    nops = jnp.sum(m, axis=2)
    tail = pad_start + tot
    ntail = (padded - tot) // SUBLANES
    units = padded // FFN_UNIT
    n_it = (units + FFN_ITEM_UNITS - 1) // FFN_ITEM_UNITS
    it_start = jnp.sum(jnp.where(e_before, n_it[None, :], 0), axis=1)
    t = jnp.arange(n_items_max)
    e_ids = jnp.arange(N_EXPERTS)
    owner = jnp.minimum(jnp.sum((it_start + n_it)[None, :] <= t[:, None], axis=1), N_EXPERTS - 1)
    own = owner[:, None] == e_ids[None, :]
    pick = lambda v: jnp.sum(jnp.where(own, v[None, :], 0), axis=1)
    j = t - pick(it_start)
    istart = pick(pad_start) + j * FFN_BLOCK
    iunits = jnp.clip(pick(units) - FFN_ITEM_UNITS * j, 1, FFN_ITEM_UNITS)
    ordinal = jnp.sum(jnp.where(e_before, (n_it > 0).astype(jnp.int32)[None, :], 0), axis=1)
    later = (e_ids[None, :] > e_ids[:, None]) & (n_it > 0)[None, :]
    nxt = jnp.min(jnp.where(later, e_ids[None, :], N_EXPERTS), axis=1)
    nxt = jnp.where(nxt == N_EXPERTS, -1, nxt)
    i32 = lambda a: a.astype(jnp.int32)
    items = (i32(jnp.sum(n_it).reshape(1)), i32(istart), i32(iunits), i32(owner), i32(j == 0),
             i32(pick(ordinal) % 2), i32(pick(nxt)))
    ops = (i32(nops.reshape(-1)), i32(olocs.reshape(-1)), i32(odsts.reshape(-1)))
    return ops, i32(ntot), i32(tail), i32(ntail), locoff.astype(F32)[:, :, None], items


def kernel(x_prompt, x_sample, state_conv, state_rglru, norm1_g, w_in, gmlp_ln_g, gmlp_ln_b, gmlp_w_s, gmlp_b_s,
           conv_w, conv_b, lru_w_a, lru_b_a, lru_w_x, lru_b_x, lru_lambda, w_out, norm2_g,
           router_group_w, router_group_b, router_expert_w, router_expert_b,
           expert_w_gate, expert_w_up, expert_w_down, final_norm_g):
    depth = w_in.shape[0]
    nb, t_len, _ = x_prompt.shape
    db, dt, _ = x_sample.shape
    n_prompt = nb * t_len
    n_sample = db * dt
    assert t_len % PROMPT_TILE == 0 and n_prompt % TOKEN_TILE == 0 and n_sample == TOKEN_TILE
    n_total = n_prompt + n_sample
    assert n_total % STEP_TOKENS == 0
    n_tiles = n_total // TOKEN_TILE
    p_tiles = n_prompt // TOKEN_TILE
    p_rows = 2 * n_total + (SUBLANES - 1) * N_EXPERTS * n_tiles + N_EXPERTS * (FFN_UNIT - 1)
    p_rows = -(-p_rows // FFN_UNIT) * FFN_UNIT
    n_items_max = p_rows // FFN_BLOCK + N_EXPERTS
    upper = jnp.triu(jnp.ones((TOKEN_TILE, TOKEN_TILE), BF16), 1)

    xp, xs = x_prompt, x_sample
    conv_p, h_p, v_s, conv_s, h_s = [], [], [], [], []
    for l in range(depth):
        lw = (norm1_g, w_in, gmlp_ln_g, gmlp_ln_b, gmlp_w_s, gmlp_b_s, conv_w, conv_b, lru_w_a, lru_b_a,
              lru_w_x, lru_b_x, lru_lambda, w_out, norm2_g, router_group_w, router_group_b,
              router_expert_w, router_expert_b)
        wts_p = _mixer_weights(l, GMLP_CHUNK, 1, *lw)
        wts_s = _mixer_weights(l, dt, db, *lw)
        zero_conv = jnp.zeros((nb, CONV_WIDTH - 1, D_B), F32)
        zero_h = jnp.zeros((nb, 1, D_B), F32)
        y1, h2, route, cnt_p, cp, hp = _mixer_call(
            xp, zero_conv, zero_h, wts_p, n_total=n_total, row_offset=0, sb=1, tt=PROMPT_TILE, chunk=GMLP_CHUNK, seg_len=PROMPT_SEG, seg_stride=PROMPT_SEG + SUBLANES, chain=True, emit_vn=False)
        y1, h2, route, cnt_s, cs, hs, vs = _mixer_call(
            xs, state_conv[l], state_rglru[l][None], wts_s, n_total=n_total, row_offset=n_prompt, sb=db,
            tt=dt, chunk=db * dt, seg_len=dt, seg_stride=dt + SUBLANES, chain=False, emit_vn=True,
            alias_in=(y1, h2, route))
        conv_p.append(cp)
        h_p.append(hp[:, 0])
        v_s.append(vs)
        conv_s.append(cs)
        h_s.append(hs[0])

        sub = PROMPT_TILE // TOKEN_TILE
        cnt = jnp.concatenate([
            jnp.swapaxes(cnt_p[:, :, :sub], 1, 2).reshape(p_tiles, N_EXPERTS),
            cnt_s[:, :, 0]], axis=0).astype(jnp.int32)
        ops, ntot, tail, ntail, locoff, items = _routing_tables(cnt, n_items_max)

        xb, slots = _dispatch_call(h2, route, locoff, upper, ops, ntot, tail, ntail, n_tiles=n_tiles,
                                   p_rows=p_rows)
        yb = _ffn_call(xb, expert_w_gate[l], expert_w_up[l], expert_w_down[l], items, p_rows=p_rows)
        assert l == depth - 1, "deeper stacks need an un-normalised combine between layers"
        fg = final_norm_g.reshape(1, D_MODEL)
        yp, ysm = _combine_call(y1, slots, fg, yb, ops, ntot, n_tiles=n_tiles, n_prompt=n_prompt,
                                n_sample=n_sample)
        xp = yp.reshape(nb, t_len, D_MODEL)
        xs = ysm.reshape(db, dt, D_MODEL)

    return (xp, xs, jnp.stack(conv_p), jnp.stack(h_p), jnp.stack(v_s), jnp.stack(conv_s), jnp.stack(h_s))
```

```python
import functools

import jax
import jax.numpy as jnp
from jax import lax
from jax.experimental import pallas as pl
from jax.experimental.pallas import tpu as pltpu

D_MODEL = 1024
D_A = 512
D_B = 512
A_GROUPS = 4
A_GROUP_DIM = 128
GMLP_CHUNK = 128
B_HEADS = 8
CONV_WIDTH = 4
LRU_C = 8.0
N_GROUPS = 4
EXPERTS_PER_GROUP = 8
N_EXPERTS = 32
D_EXPERT = 512
EPS = 1e-6
TINY = 1e-30

SUBLANES = 8
LANES = 128
ROUTER_ROWS = 40
TOKEN_TILE = 256
PROMPT_TILE = 1024
PROMPT_SEG = PROMPT_TILE // SUBLANES
FFN_UNIT = 256
FFN_ITEM_UNITS = 4
FFN_BLOCK = FFN_ITEM_UNITS * FFN_UNIT
ZERO_CHUNKS = 8
ZERO_ROWS = ZERO_CHUNKS * SUBLANES
SLOT_ROWS = 2 * TOKEN_TILE + N_EXPERTS * SUBLANES
SLOT_ROWS_COMMON = 2 * TOKEN_TILE + 9 * 16
TILE_CHUNKS = SLOT_ROWS // SUBLANES
COPY_SLOTS = N_EXPERTS
COPY_CLASSES = ((4, COPY_SLOTS), (3, COPY_SLOTS), (2, COPY_SLOTS), (1, COPY_SLOTS))
assert TILE_CHUNKS // 4 <= COPY_SLOTS
TILES_PER_STEP = 3
STEP_TOKENS = TILES_PER_STEP * TOKEN_TILE
PACK_COLS = D_MODEL // 2
XB_COLS = PACK_COLS + LANES
U32 = jnp.uint32
HI_MASK = 0xFFFF0000
VMEM_BYTES_V7X = 64 * 1024 * 1024
VMEM_LIMIT = VMEM_BYTES_V7X - 8 * 1024 * 1024

BF16 = jnp.bfloat16
F32 = jnp.float32


def _rms(x, g):
    return x * lax.rsqrt(jnp.mean(x * x, axis=-1, keepdims=True) + EPS) * g


def _dot(a, b):
    return jnp.dot(a, b, preferred_element_type=F32)


def _sigmoid(x):
    return 1.0 / (1.0 + jnp.exp(-x))


def _pack_bf16_pairs(x):
    bits = lax.bitcast_convert_type(x, U32)
    return (bits[:, PACK_COLS:] & U32(HI_MASK)) | (bits[:, :PACK_COLS] >> 16)


def _unpack_bf16_pairs(w):
    lo = lax.bitcast_convert_type(w << 16, F32).astype(BF16)
    hi = lax.bitcast_convert_type(w & U32(HI_MASK), F32).astype(BF16)
    return jnp.concatenate([lo, hi], axis=1)


def _mixer_kernel(x_ref, hist_ref, h0_ref, n1g_ref, win_ref, lng_ref, lnb_ref, wsm_ref, bsx_ref,
                  cw_ref, cb_ref, wa_ref, ba_ref, wx_ref, bx_ref, lam_ref, wout_ref, n2g_ref,
                  wr_ref, br_ref, *rest, sb, tt, chunk, seg_len, seg_stride, chain, emit_vn, aliased):
    if aliased:
        rest = rest[3:]
    if emit_vn:
        (y1_ref, h2_ref, route_ref, cnt_ref, conv_ref, hlast_ref, vn_ref,
         xs_ref, sa_ref, sb_ref, mix_ref, hc_ref) = rest
    else:
        (y1_ref, h2_ref, route_ref, cnt_ref, conv_ref, hlast_ref,
         xs_ref, sa_ref, sb_ref, mix_ref, hc_ref) = rest
        vn_ref = None
    t_idx = pl.program_id(1)

    @pl.when(t_idx == 0)
    def _():
        xs_ref[:, SUBLANES - 3:SUBLANES, :] = hist_ref[...]
        hc_ref[...] = jnp.zeros_like(hc_ref)

    @pl.when(t_idx != 0)
    def _():
        xs_ref[:, SUBLANES - 3:SUBLANES, :] = xs_ref[:, SUBLANES + tt - 3:SUBLANES + tt, :]

    rows = sb * tt
    n_seg = rows // seg_len
    half = D_B // 2

    x = x_ref[...].reshape(rows, D_MODEL)
    hb = _rms(x, n1g_ref[...]).astype(BF16)

    v_a = _dot(hb, win_ref[:, D_A:2 * D_A])
    x_b = _dot(hb, win_ref[:, 2 * D_A:2 * D_A + D_B])
    mixed = []
    for g in range(A_GROUPS):
        cols = slice(g * A_GROUP_DIM, (g + 1) * A_GROUP_DIM)
        vg = v_a[:, cols]
        mu = jnp.mean(vg, axis=-1, keepdims=True)
        dv = vg - mu
        var = jnp.mean(dv * dv, axis=-1, keepdims=True)
        vn = dv * lax.rsqrt(var + EPS) * lng_ref[:, cols] + lnb_ref[:, cols]
        if vn_ref is not None:
            vn_ref[:, :, cols] = vn.reshape(sb, tt, A_GROUP_DIM)
        vnb = vn.astype(BF16)
        mixed.append([_dot(wsm_ref[g], vnb[c * chunk:(c + 1) * chunk]) for c in range(rows // chunk)])
    u_a = _dot(hb, win_ref[:, 0:D_A])

    xs_ref[:, SUBLANES:SUBLANES + tt, :] = x_b.reshape(sb, tt, D_B)
    xc = cb_ref[...] + x_b * cw_ref[3:4, :]
    for j in range(1, CONV_WIDTH):
        shifted = xs_ref[:, SUBLANES - j:SUBLANES - j + tt, :].reshape(rows, D_B)
        xc = xc + shifted * cw_ref[3 - j:4 - j, :]

    xcb = xc.astype(BF16)
    r_lin = jnp.concatenate([_dot(xcb[:, :half], wa_ref[0]), _dot(xcb[:, half:], wa_ref[1])], axis=1)
    i_lin = jnp.concatenate([_dot(xcb[:, :half], wx_ref[0]), _dot(xcb[:, half:], wx_ref[1])], axis=1)

    for g in range(A_GROUPS):
        cols = slice(g * A_GROUP_DIM, (g + 1) * A_GROUP_DIM)
        for c in range(rows // chunk):
            rs = slice(c * chunk, (c + 1) * chunk)
            s = mixed[g][c] + bsx_ref[:, cols]
            mix_ref[rs, cols] = (u_a[rs, cols] * s).astype(BF16)

    g_b = _dot(hb, win_ref[:, 2 * D_A + D_B:])
    r = _sigmoid(r_lin + ba_ref[...])
    i_g = _sigmoid(i_lin + bx_ref[...])
    log_a = (-LRU_C * r) * jax.nn.softplus(-lam_ref[...])
    a = jnp.exp(log_a)
    y_gain = 1.0 - a * a
    gain = y_gain * lax.rsqrt(jnp.maximum(y_gain, TINY))
    bterm = gain * i_g * xc

    n_lb = D_B // LANES
    for j in range(n_seg):
        for k in range(n_lb):
            lc = slice(k * LANES, (k + 1) * LANES)
            dst = slice(j * seg_stride, j * seg_stride + seg_len)
            sa_ref[k, dst, :] = a[j * seg_len:(j + 1) * seg_len, lc]
            sb_ref[k, dst, :] = bterm[j * seg_len:(j + 1) * seg_len, lc]

    y_a = x + _dot(mix_ref[:, 0:D_A], wout_ref[0:D_A, :])
    gel = jax.nn.gelu(g_b)

    for grp in range(n_seg // SUBLANES):
        base = grp * SUBLANES * seg_stride
        bs = slice(grp * SUBLANES, (grp + 1) * SUBLANES)
        if chain:
            h_init = tuple(jnp.zeros((SUBLANES, LANES), F32) for _ in range(n_lb))
        else:
            h_init = tuple(h0_ref[0, bs, k * LANES:(k + 1) * LANES] for k in range(n_lb))
        a_init = tuple(jnp.ones((SUBLANES, LANES), F32) for _ in range(n_lb))

        def step(i, carry, base=base):
            hs, acs = carry
            idx = pl.ds(base + i, SUBLANES, stride=seg_stride)
            new_h, new_a = [], []
            for k in range(n_lb):
                av = sa_ref[k, idx, :]
                h = av * hs[k] + sb_ref[k, idx, :]
                sb_ref[k, idx, :] = h
                new_h.append(h)
                if chain:
                    ac = av * acs[k]
                    sa_ref[k, idx, :] = ac
                    new_a.append(ac)
                else:
                    new_a.append(acs[k])
            return tuple(new_h), tuple(new_a)

        carry = (h_init, a_init)
        for i in range(seg_len):
            carry = step(i, carry)
        h_end, a_end = carry

        for k in range(n_lb):
            lc = slice(k * LANES, (k + 1) * LANES)
            mc = slice(D_A + k * LANES, D_A + (k + 1) * LANES)
            if chain:
                h_in = hc_ref[:, lc]
                for j in range(SUBLANES):
                    seg = slice(base + j * seg_stride, base + j * seg_stride + seg_len)
                    rs = slice((grp * SUBLANES + j) * seg_len, (grp * SUBLANES + j + 1) * seg_len)
                    h_seg = sb_ref[k, seg, :] + sa_ref[k, seg, :] * h_in
                    mix_ref[rs, mc] = (h_seg * gel[rs, lc]).astype(BF16)
                    h_in = h_end[k][j:j + 1] + a_end[k][j:j + 1] * h_in
                hc_ref[:, lc] = h_in
                hlast_ref[0, :, lc] = h_in
            else:
                for j in range(SUBLANES):
                    seg = slice(base + j * seg_stride, base + j * seg_stride + seg_len)
                    rs = slice((grp * SUBLANES + j) * seg_len, (grp * SUBLANES + j + 1) * seg_len)
                    mix_ref[rs, mc] = (sb_ref[k, seg, :] * gel[rs, lc]).astype(BF16)
                hlast_ref[0, bs, lc] = h_end[k]

    y1 = y_a + _dot(mix_ref[:, D_A:], wout_ref[D_A:, :])
    y1_ref[...] = y1
    h2 = _rms(y1, n2g_ref[...]).astype(BF16)
    h2_ref[...] = h2

    logits = lax.dot_general(wr_ref[...], h2, (((1,), (1,)), ((), ())), preferred_element_type=F32) + br_ref[...]
    gl = [logits[N_EXPERTS + k:N_EXPERTS + k + 1, :] for k in range(N_GROUPS)]
    gmax = jnp.maximum(jnp.maximum(gl[0], gl[1]), jnp.maximum(gl[2], gl[3]))
    grp_idx = jnp.where(gl[0] == gmax, 0, jnp.where(gl[1] == gmax, 1, jnp.where(gl[2] == gmax, 2, 3)))
    gsum = (jnp.exp(gl[0] - gmax) + jnp.exp(gl[1] - gmax)) + (jnp.exp(gl[2] - gmax) + jnp.exp(gl[3] - gmax))
    p_grp = 1.0 / gsum
    e_in = jnp.where(grp_idx == 0, logits[0:8],
                     jnp.where(grp_idx == 1, logits[8:16], jnp.where(grp_idx == 2, logits[16:24], logits[24:32])))
    iota8 = lax.broadcasted_iota(jnp.int32, (EXPERTS_PER_GROUP, rows), 0)
    m1 = jnp.max(e_in, axis=0, keepdims=True)
    i1 = jnp.min(jnp.where(e_in == m1, iota8, EXPERTS_PER_GROUP), axis=0, keepdims=True)
    e_rest = jnp.where(iota8 == i1, -jnp.inf, e_in)
    m2 = jnp.max(e_rest, axis=0, keepdims=True)
    i2 = jnp.min(jnp.where(e_rest == m2, iota8, EXPERTS_PER_GROUP), axis=0, keepdims=True)
    t2 = jnp.exp(m2 - m1)
    den = 1.0 + t2
    gate0 = p_grp * (1.0 / den)
    gate1 = p_grp * (t2 / den)
    e0 = (grp_idx * EXPERTS_PER_GROUP + i1).astype(F32)
    e1 = (grp_idx * EXPERTS_PER_GROUP + i2).astype(F32)
    route_ref[...] = jnp.where(iota8 == 0, e0, jnp.where(iota8 == 1, e1, jnp.where(
        iota8 == 2, gate0, jnp.where(iota8 == 3, gate1, 0.0))))

    iota_e = lax.broadcasted_iota(jnp.int32, (N_EXPERTS, rows), 0).astype(F32)
    onehot = (iota_e == e0).astype(F32) + (iota_e == e1).astype(F32)
    lane = lax.broadcasted_iota(jnp.int32, (N_EXPERTS, LANES), 1)
    cnt = jnp.zeros((N_EXPERTS, LANES), F32)
    for s in range(rows // TOKEN_TILE):
        c_s = jnp.sum(onehot[:, s * TOKEN_TILE:(s + 1) * TOKEN_TILE], axis=1, keepdims=True)
        cnt = cnt + jnp.where(lane == s, c_s, 0.0)
    cnt_ref[...] = cnt.reshape(1, N_EXPERTS, LANES)
    conv_ref[...] = xs_ref[:, SUBLANES + tt - 3:SUBLANES + tt, :]


def _mixer_call(x, hist, h0, wts, *, n_total, row_offset, sb, tt, chunk, seg_len, seg_stride, chain,
                emit_vn, alias_in=None):
    nb, t_len, _ = x.shape
    nbb = nb // sb
    ntt = t_len // tt
    rows = sb * tt
    blk0 = row_offset // rows
    n_seg = rows // seg_len

    def full(arr):
        nd = arr.ndim
        return pl.BlockSpec(arr.shape, lambda b, t, _nd=nd: (0,) * _nd)

    in_specs = [
        pl.BlockSpec((sb, tt, D_MODEL), lambda b, t: (b, t, 0)),
        pl.BlockSpec((sb, CONV_WIDTH - 1, D_B), lambda b, t: (b, 0, 0)),
        pl.BlockSpec((1, sb, D_B), lambda b, t: (b, 0, 0)),
    ] + [full(w) for w in wts]
    args = [x, hist, h0] + list(wts)
    io_alias = {}
    if alias_in is not None:
        for k, arr in enumerate(alias_in):
            io_alias[len(args)] = k
            in_specs.append(pl.BlockSpec(memory_space=pl.ANY))
            args.append(arr)

    out_shape = [
        jax.ShapeDtypeStruct((n_total, D_MODEL), F32),
        jax.ShapeDtypeStruct((n_total, D_MODEL), BF16),
        jax.ShapeDtypeStruct((SUBLANES, n_total), F32),
        jax.ShapeDtypeStruct((nbb * ntt, N_EXPERTS, LANES), F32),
        jax.ShapeDtypeStruct((nb, CONV_WIDTH - 1, D_B), F32),
        jax.ShapeDtypeStruct((nbb, sb, D_B), F32),
    ]
    out_specs = [
        pl.BlockSpec((rows, D_MODEL), lambda b, t: (blk0 + b * ntt + t, 0)),
        pl.BlockSpec((rows, D_MODEL), lambda b, t: (blk0 + b * ntt + t, 0)),
        pl.BlockSpec((SUBLANES, rows), lambda b, t: (0, blk0 + b * ntt + t)),
        pl.BlockSpec((1, N_EXPERTS, LANES), lambda b, t: (b * ntt + t, 0, 0)),
        pl.BlockSpec((sb, CONV_WIDTH - 1, D_B), lambda b, t: (b, 0, 0)),
        pl.BlockSpec((1, sb, D_B), lambda b, t: (b, 0, 0)),
    ]
    if emit_vn:
        out_shape.append(jax.ShapeDtypeStruct((nb, t_len, D_A), F32))
        out_specs.append(pl.BlockSpec((sb, tt, D_A), lambda b, t: (b, t, 0)))

    kern = functools.partial(_mixer_kernel, sb=sb, tt=tt, chunk=chunk, seg_len=seg_len, seg_stride=seg_stride,
                             chain=chain, emit_vn=emit_vn, aliased=alias_in is not None)
    return pl.pallas_call(
        kern,
        grid=(nbb, ntt),
        in_specs=in_specs,
        out_specs=out_specs,
        out_shape=out_shape,
        scratch_shapes=[
            pltpu.VMEM((sb, SUBLANES + tt, D_B), F32),
            pltpu.VMEM((D_B // LANES, n_seg * seg_stride, LANES), F32),
            pltpu.VMEM((D_B // LANES, n_seg * seg_stride, LANES), F32),
            pltpu.VMEM((rows, D_MODEL), BF16),
            pltpu.VMEM((1, D_B), F32),
        ],
        input_output_aliases=io_alias,
        compiler_params=pltpu.CompilerParams(dimension_semantics=("arbitrary", "arbitrary"),
                                             vmem_limit_bytes=VMEM_LIMIT),
        name="mixer_chain" if chain else "mixer_step",
    )(*args)


def _tile_slots(route_ref, locoff_ref, upper_ref, u):
    lanes = slice(u * TOKEN_TILE, (u + 1) * TOKEN_TILE)
    e0 = route_ref[0:1, lanes]
    e1 = route_ref[1:2, lanes]
    iota_e = lax.broadcasted_iota(jnp.int32, (N_EXPERTS, TOKEN_TILE), 0).astype(F32)
    oh0 = (iota_e == e0).astype(F32)
    oh1 = (iota_e == e1).astype(F32)
    c0 = _dot(oh0.astype(BF16), upper_ref[...])
    c1 = _dot(oh1.astype(BF16), upper_ref[...])
    cnt0 = jnp.sum(oh0, axis=1, keepdims=True)
    base0 = locoff_ref[u]
    base1 = base0 + cnt0
    p0 = jnp.sum(oh0 * (base0 + c0), axis=0, keepdims=True)
    p1 = jnp.sum(oh1 * (base1 + c1), axis=0, keepdims=True)
    return p0.astype(jnp.int32), p1.astype(jnp.int32)


def _by_fill(ntot_ref, step, body):
    most = ntot_ref[step * TILES_PER_STEP]
    for u in range(1, TILES_PER_STEP):
        most = jnp.maximum(most, ntot_ref[step * TILES_PER_STEP + u])
    fits = most * SUBLANES <= SLOT_ROWS_COMMON

    @pl.when(fits)
    def _():
        body(SLOT_ROWS_COMMON)

    @pl.when(jnp.logical_not(fits))
    def _():
        body(SLOT_ROWS)


def _copy_ops(tile, n_tiles, nops_ref, oloc_ref, odst_ref, make_copy):
    base = 0
    for k, (chunks, cap) in enumerate(COPY_CLASSES):
        n = nops_ref[k * n_tiles + tile]

        def per_pair(i, _, chunks=chunks, cap=cap, base=base, n=n):
            idx = base + tile * cap + 2 * i
            make_copy(pl.multiple_of(oloc_ref[idx], SUBLANES), pl.multiple_of(odst_ref[idx], SUBLANES),
                      chunks * SUBLANES)

            @pl.when(2 * i + 1 < n)
            def _():
                make_copy(pl.multiple_of(oloc_ref[idx + 1], SUBLANES), pl.multiple_of(odst_ref[idx + 1], SUBLANES),
                          chunks * SUBLANES)
            return 0

        assert cap % 2 == 0
        lax.fori_loop(0, (n + 1) // 2, per_pair, 0)
        base += n_tiles * cap


def _dispatch_kernel(nops_ref, oloc_ref, odst_ref, ntot_ref, tail_ref, ntail_ref,
                     h2_ref, route_ref, locoff_ref, upper_ref, xb_ref, slots_ref, buf_ref, zero_ref, sem, zsem,
                     *, n_tiles):
    i = pl.program_id(0)
    n_steps = pl.num_programs(0)
    slot = i % 2

    def chunk_copy(s, u, loc, dst, rows=SUBLANES):
        return pltpu.make_async_copy(buf_ref.at[s, u, pl.ds(loc, rows), :],
                                     xb_ref.at[pl.ds(dst, rows), :], sem.at[s])

    def wait_step(step, s):
        for u in range(TILES_PER_STEP):
            n_rows = ntot_ref[step * TILES_PER_STEP + u] * SUBLANES

            @pl.when(n_rows > 0)
            def _():
                pltpu.make_async_copy(buf_ref.at[s, u, pl.ds(0, n_rows), :], xb_ref.at[pl.ds(0, n_rows), :],
                                      sem.at[s]).wait()

    @pl.when(i >= 2)
    def _():
        wait_step(i - 2, slot)

    def sort_tiles(n_rows):
        iota_s = lax.broadcasted_iota(jnp.int32, (n_rows, TOKEN_TILE), 0)
        iota8 = lax.broadcasted_iota(jnp.int32, (SUBLANES, TOKEN_TILE), 0)
        for u in range(TILES_PER_STEP):
            lanes = slice(u * TOKEN_TILE, (u + 1) * TOKEN_TILE)
            p0, p1 = _tile_slots(route_ref, locoff_ref, upper_ref, u)
            slots_ref[:, lanes] = jnp.where(iota8 == 0, p0, jnp.where(iota8 == 1, p1, 0))
            hit0 = iota_s == p0
            hit1 = iota_s == p1
            perm = jnp.where(hit0 | hit1, 1.0, 0.0).astype(BF16)
            sorted_rows = _dot(perm, h2_ref[u * TOKEN_TILE:(u + 1) * TOKEN_TILE, :])
            buf_ref[slot, u, 0:n_rows, 0:PACK_COLS] = _pack_bf16_pairs(sorted_rows)
            gates = jnp.where(hit0, route_ref[2:3, lanes], 0.0) + jnp.where(hit1, route_ref[3:4, lanes], 0.0)
            gcol = jnp.sum(gates, axis=1, keepdims=True)
            buf_ref[slot, u, 0:n_rows, PACK_COLS:XB_COLS] = lax.bitcast_convert_type(
                jnp.broadcast_to(gcol, (n_rows, LANES)), U32)

    _by_fill(ntot_ref, i, sort_tiles)

    for u in range(TILES_PER_STEP):
        _copy_ops(i * TILES_PER_STEP + u, n_tiles, nops_ref, oloc_ref, odst_ref,
                  lambda loc, dst, rows, u=u: chunk_copy(slot, u, loc, dst, rows).start())

    @pl.when(i == n_steps - 1)
    def _():
        zero_ref[...] = jnp.zeros_like(zero_ref)

        def zero_copy(dst, rows, k):
            return pltpu.make_async_copy(zero_ref.at[pl.ds(0, rows), :], xb_ref.at[pl.ds(dst, rows), :], zsem.at[k])

        def per_expert(e, tot):
            n_big, n_small = tot
            n = ntail_ref[e]
            d0 = tail_ref[e]
            nb = n // ZERO_CHUNKS
            ns = n - nb * ZERO_CHUNKS

            def big(c, _):
                zero_copy(pl.multiple_of(d0 + c * ZERO_ROWS, SUBLANES), ZERO_ROWS, 0).start()
                return 0

            def small(c, _):
                zero_copy(pl.multiple_of(d0 + nb * ZERO_ROWS + c * SUBLANES, SUBLANES), SUBLANES, 1).start()
                return 0

            lax.fori_loop(0, nb, big, 0)
            lax.fori_loop(0, ns, small, 0)
            return n_big + nb, n_small + ns

        n_big, n_small = lax.fori_loop(0, N_EXPERTS, per_expert, (0, 0))

        @pl.when(i >= 1)
        def _():
            wait_step(i - 1, 1 - slot)

        wait_step(i, slot)

        def wait_big(c, _):
            zero_copy(0, ZERO_ROWS, 0).wait()
            return 0

        def wait_small(c, _):
            zero_copy(0, SUBLANES, 1).wait()
            return 0

        lax.fori_loop(0, n_big, wait_big, 0)
        lax.fori_loop(0, n_small, wait_small, 0)


def _dispatch_call(h2, route, locoff, upper, ops, ntot, tail, ntail, *, n_tiles, p_rows):
    grid_spec = pltpu.PrefetchScalarGridSpec(
        num_scalar_prefetch=6,
        grid=(n_tiles // TILES_PER_STEP,),
        in_specs=[
            pl.BlockSpec((STEP_TOKENS, D_MODEL), lambda i, *_: (i, 0)),
            pl.BlockSpec((SUBLANES, STEP_TOKENS), lambda i, *_: (0, i)),
            pl.BlockSpec((TILES_PER_STEP, N_EXPERTS, 1), lambda i, *_: (i, 0, 0)),
            pl.BlockSpec((TOKEN_TILE, TOKEN_TILE), lambda i, *_: (0, 0)),
        ],
        out_specs=[pl.BlockSpec(memory_space=pl.ANY),
                   pl.BlockSpec((SUBLANES, STEP_TOKENS), lambda i, *_: (0, i))],
        scratch_shapes=[
            pltpu.VMEM((2, TILES_PER_STEP, SLOT_ROWS, XB_COLS), U32),
            pltpu.VMEM((ZERO_ROWS, XB_COLS), U32),
            pltpu.SemaphoreType.DMA((2,)),
            pltpu.SemaphoreType.DMA((2,)),
        ],
    )
    return pl.pallas_call(
        functools.partial(_dispatch_kernel, n_tiles=n_tiles),
        grid_spec=grid_spec,
        out_shape=[jax.ShapeDtypeStruct((p_rows, XB_COLS), U32),
                   jax.ShapeDtypeStruct((SUBLANES, n_tiles * TOKEN_TILE), jnp.int32)],
        compiler_params=pltpu.CompilerParams(dimension_semantics=("arbitrary",), vmem_limit_bytes=VMEM_LIMIT),
        name="dispatch",
    )(*ops, ntot, tail, ntail, h2, route, locoff, upper)


def _ffn_kernel(nitems_ref, istart_ref, iunits_ref, iexp_ref, ifirst_ref, iwslot_ref, inext_ref,
                xb_ref, wg_ref, wu_ref, wd_ref, yb_ref,
                xin_ref, yout_ref, wgf_ref, wuf_ref, wdf_ref, wgb_ref, wub_ref, wdb_ref, sem_in, sem_out, sem_w):
    n_items = nitems_ref[0]
    sizes = tuple(k * FFN_UNIT for k in range(FFN_ITEM_UNITS, 0, -1))

    def rows_in(t, s, rows):
        start = pl.multiple_of(istart_ref[t], FFN_UNIT)
        return pltpu.make_async_copy(xb_ref.at[pl.ds(start, rows), :], xin_ref.at[s, pl.ds(0, rows), :], sem_in.at[s])

    def rows_out(t, s, rows):
        start = pl.multiple_of(istart_ref[t], FFN_UNIT)
        return pltpu.make_async_copy(yout_ref.at[s, pl.ds(0, rows), :], yb_ref.at[pl.ds(start, rows), :],
                                     sem_out.at[s])

    def start_by_size(make, t, s):
        for rows in sizes:
            @pl.when(iunits_ref[t] * FFN_UNIT == rows)
            def _():
                make(t, s, rows).start()

    def weight_copies(e, ws):
        return (pltpu.make_async_copy(wg_ref.at[e], wgf_ref.at[ws], sem_w.at[ws]),
                pltpu.make_async_copy(wu_ref.at[e], wuf_ref.at[ws], sem_w.at[ws]),
                pltpu.make_async_copy(wd_ref.at[e], wdf_ref.at[ws], sem_w.at[ws]))

    def compute(s, rows):
        xb = _unpack_bf16_pairs(xin_ref[s, 0:rows, 0:PACK_COLS])
        gate = lax.bitcast_convert_type(xin_ref[s, 0:rows, PACK_COLS:PACK_COLS + 1], F32)
        a = _dot(xb, wgb_ref[...])
        u = _dot(xb, wub_ref[...])
        mid = (a * _sigmoid(a) * u).astype(BF16)
        y = _dot(mid, wdb_ref[...]) * gate
        yout_ref[s, 0:rows, :] = _pack_bf16_pairs(y.astype(BF16).astype(F32))

    @pl.when(n_items > 0)
    def _():
        for c in weight_copies(iexp_ref[0], 0):
            c.start()
        start_by_size(rows_in, 0, 0)

    def per_item(t, _):
        s = t % 2

        @pl.when(t + 1 < n_items)
        def _():
            start_by_size(rows_in, t + 1, 1 - s)

        @pl.when(ifirst_ref[t] == 1)
        def _():
            ws = iwslot_ref[t]
            for c in weight_copies(0, ws):
                c.wait()

            @pl.when(inext_ref[t] >= 0)
            def _():
                for c in weight_copies(inext_ref[t], 1 - ws):
                    c.start()

            wgb_ref[...] = wgf_ref[ws].astype(BF16)
            wub_ref[...] = wuf_ref[ws].astype(BF16)
            wdb_ref[...] = wdf_ref[ws].astype(BF16)

        rows_in(t, s, iunits_ref[t] * FFN_UNIT).wait()

        @pl.when(t >= 2)
        def _():
            rows_out(t - 2, s, iunits_ref[t - 2] * FFN_UNIT).wait()

        for rows in sizes:
            @pl.when(iunits_ref[t] * FFN_UNIT == rows)
            def _():
                compute(s, rows)

        start_by_size(rows_out, t, s)
        return 0

    lax.fori_loop(0, n_items, per_item, 0)

    for back in (2, 1):
        @pl.when(n_items >= back)
        def _():
            t = n_items - back
            rows_out(t, t % 2, iunits_ref[t] * FFN_UNIT).wait()


def _ffn_call(xb, wg, wu, wd, items, *, p_rows):
    any_spec = pl.BlockSpec(memory_space=pl.ANY)
    grid_spec = pltpu.PrefetchScalarGridSpec(
        num_scalar_prefetch=len(items),
        grid=(1,),
        in_specs=[any_spec, any_spec, any_spec, any_spec],
        out_specs=any_spec,
        scratch_shapes=[
            pltpu.VMEM((2, FFN_BLOCK, XB_COLS), U32),
            pltpu.VMEM((2, FFN_BLOCK, PACK_COLS), U32),
            pltpu.VMEM((2, D_MODEL, D_EXPERT), F32),
            pltpu.VMEM((2, D_MODEL, D_EXPERT), F32),
            pltpu.VMEM((2, D_EXPERT, D_MODEL), F32),
            pltpu.VMEM((D_MODEL, D_EXPERT), BF16),
            pltpu.VMEM((D_MODEL, D_EXPERT), BF16),
            pltpu.VMEM((D_EXPERT, D_MODEL), BF16),
            pltpu.SemaphoreType.DMA((2,)),
            pltpu.SemaphoreType.DMA((2,)),
            pltpu.SemaphoreType.DMA((2,)),
        ],
    )
    return pl.pallas_call(
        _ffn_kernel,
        grid_spec=grid_spec,
        out_shape=jax.ShapeDtypeStruct((p_rows, PACK_COLS), U32),
        compiler_params=pltpu.CompilerParams(dimension_semantics=("arbitrary",), vmem_limit_bytes=VMEM_LIMIT),
        name="expert_ffn",
    )(*items, xb, wg, wu, wd)


def _combine_kernel(nops_ref, oloc_ref, odst_ref, ntot_ref, y1_ref, slots_ref, fg_ref,
                    yb_ref, yp_ref, ys_ref, buf_ref, sem, *, n_tiles):
    i = pl.program_id(0)
    n_steps = pl.num_programs(0)
    slot = i % 2

    def chunk_copy(s, u, loc, src, rows):
        return pltpu.make_async_copy(yb_ref.at[pl.ds(src, rows), :],
                                     buf_ref.at[s, u, pl.ds(loc, rows), :], sem.at[s])

    def issue_step(step, s):
        for u in range(TILES_PER_STEP):
            _copy_ops(step * TILES_PER_STEP + u, n_tiles, nops_ref, oloc_ref, odst_ref,
                      lambda loc, src, rows, u=u: chunk_copy(s, u, loc, src, rows).start())

    @pl.when(i == 0)
    def _():
        buf_ref[...] = jnp.zeros_like(buf_ref)
        issue_step(0, 0)

    @pl.when(i + 1 < n_steps)
    def _():
        issue_step(i + 1, 1 - slot)

    def unsort_tiles(n_slots):
        iota_s = lax.broadcasted_iota(jnp.int32, (n_slots, TOKEN_TILE), 0)
        perms = []
        for u in range(TILES_PER_STEP):
            lanes = slice(u * TOKEN_TILE, (u + 1) * TOKEN_TILE)
            p0 = slots_ref[0:1, lanes]
            p1 = slots_ref[1:2, lanes]
            perms.append(jnp.where((iota_s == p0) | (iota_s == p1), 1.0, 0.0).astype(BF16))

        for u in range(TILES_PER_STEP):
            n_rows = ntot_ref[i * TILES_PER_STEP + u] * SUBLANES

            @pl.when(n_rows > 0)
            def _():
                pltpu.make_async_copy(yb_ref.at[pl.ds(0, n_rows), :], buf_ref.at[slot, u, pl.ds(0, n_rows), :],
                                      sem.at[slot]).wait()

        for u in range(TILES_PER_STEP):
            rs = slice(u * TOKEN_TILE, (u + 1) * TOKEN_TILE)
            yb = _unpack_bf16_pairs(buf_ref[slot, u, 0:n_slots, :])
            moe = lax.dot_general(perms[u], yb, (((0,), (0,)), ((), ())), preferred_element_type=F32)
            y_tile = _rms(y1_ref[rs, :] + moe, fg_ref[...])
            yp_ref[rs, :] = y_tile

        @pl.when(i == n_steps - 1)
        def _():
            ys_ref[...] = y_tile

    _by_fill(ntot_ref, i, unsort_tiles)


def _combine_call(y1, slots, fg, yb, ops, ntot, *, n_tiles, n_prompt, n_sample):
    n_steps = n_tiles // TILES_PER_STEP
    assert n_sample == TOKEN_TILE and n_prompt + n_sample == n_steps * STEP_TOKENS
    grid_spec = pltpu.PrefetchScalarGridSpec(
        num_scalar_prefetch=4,
        grid=(n_steps,),
        in_specs=[
            pl.BlockSpec((STEP_TOKENS, D_MODEL), lambda i, *_: (i, 0)),
            pl.BlockSpec((SUBLANES, STEP_TOKENS), lambda i, *_: (0, i)),
            pl.BlockSpec((1, D_MODEL), lambda i, *_: (0, 0)),
            pl.BlockSpec(memory_space=pl.ANY),
        ],
        out_specs=[
            pl.BlockSpec((STEP_TOKENS, D_MODEL), lambda i, *_: (i, 0)),
            pl.BlockSpec((n_sample, D_MODEL), lambda i, *_: (0, 0)),
        ],
        scratch_shapes=[
            pltpu.VMEM((2, TILES_PER_STEP, SLOT_ROWS, PACK_COLS), U32),
            pltpu.SemaphoreType.DMA((2,)),
        ],
    )
    return pl.pallas_call(
        functools.partial(_combine_kernel, n_tiles=n_tiles),
        grid_spec=grid_spec,
        out_shape=[jax.ShapeDtypeStruct((n_prompt, D_MODEL), F32), jax.ShapeDtypeStruct((n_sample, D_MODEL), F32)],
        compiler_params=pltpu.CompilerParams(dimension_semantics=("arbitrary",), vmem_limit_bytes=VMEM_LIMIT),
        name="combine",
    )(*ops, ntot, y1, slots, fg, yb)


def _block_diag(w):
    h, d, _ = w.shape
    eye = jnp.eye(h, dtype=w.dtype)
    return (eye[:, None, :, None] * w[:, :, None, :]).reshape(h * d, h * d)


def _head_blocks(w):
    half = B_HEADS // 2
    return jnp.stack([_block_diag(w[:half]), _block_diag(w[half:])]).astype(BF16)


def _mixer_weights(l, length, reps, norm1_g, w_in, gmlp_ln_g, gmlp_ln_b, gmlp_w_s, gmlp_b_s, conv_w, conv_b,
                   lru_w_a, lru_b_a, lru_w_x, lru_b_x, lru_lambda, w_out, norm2_g,
                   router_group_w, router_group_b, router_expert_w, router_expert_b):
    mask = jnp.tril(jnp.ones((length, length), dtype=bool))
    ws = jnp.where(mask, gmlp_w_s[l][:, :length, :length], 0.0)
    wsm = jnp.concatenate([jnp.pad(ws, ((0, 0), (0, 0), (j * length, (reps - 1 - j) * length)))
                           for j in range(reps)], axis=1)
    bs = jnp.tile(gmlp_b_s[l][:, :length], (1, reps))
    bsx = jnp.repeat(bs.T, A_GROUP_DIM, axis=1)
    wr = jnp.concatenate([router_expert_w[l].T, router_group_w[l].T,
                          jnp.zeros((ROUTER_ROWS - N_EXPERTS - N_GROUPS, D_MODEL), F32)], axis=0)
    br = jnp.concatenate([router_expert_b[l], router_group_b[l],
                          jnp.zeros((ROUTER_ROWS - N_EXPERTS - N_GROUPS,), F32)]).reshape(ROUTER_ROWS, 1)
    return [
        norm1_g[l].reshape(1, D_MODEL), w_in[l].astype(BF16),
        gmlp_ln_g[l].reshape(1, D_A), gmlp_ln_b[l].reshape(1, D_A), wsm.astype(BF16), bsx,
        conv_w[l], conv_b[l].reshape(1, D_B),
        _head_blocks(lru_w_a[l]), lru_b_a[l].reshape(1, D_B),
        _head_blocks(lru_w_x[l]), lru_b_x[l].reshape(1, D_B),
        lru_lambda[l].reshape(1, D_B), w_out[l].astype(BF16), norm2_g[l].reshape(1, D_MODEL),
        wr.astype(BF16), br,
    ]


def _routing_tables(cnt, n_items_max):
    n_tiles = cnt.shape[0]
    seg = (cnt + SUBLANES - 1) // SUBLANES * SUBLANES
    nch = seg // SUBLANES
    tot = jnp.sum(seg, axis=0)
    padded = (tot + FFN_UNIT - 1) // FFN_UNIT * FFN_UNIT
    e_before = jnp.arange(N_EXPERTS)[None, :] < jnp.arange(N_EXPERTS)[:, None]
    t_before = jnp.arange(n_tiles)[None, :] < jnp.arange(n_tiles)[:, None]
    pad_start = jnp.sum(jnp.where(e_before, padded[None, :], 0), axis=1)
    doff = pad_start[None, :] + jnp.sum(jnp.where(t_before[:, :, None], seg[None, :, :], 0), axis=1)
    locoff = jnp.sum(jnp.where(e_before[None, :, :], seg[:, None, :], 0), axis=2)
    ntot = jnp.sum(nch, axis=1)
    big = COPY_CLASSES[0][0]
    sizes = jnp.array([chunks for chunks, _ in COPY_CLASSES], jnp.int32)
    whole = nch // big
    done = whole * big * SUBLANES
    rest = nch - whole * big
    is_big = (sizes == big)[:, None, None]
    m = jnp.where(is_big, whole[None], (rest[None] == sizes[:, None, None]).astype(jnp.int32))
    loc0 = jnp.where(is_big, locoff[None], (locoff + done)[None])
    dst0 = jnp.where(is_big, doff[None], (doff + done)[None])
    start = jnp.sum(jnp.where(e_before[None, None], m[:, :, None, :], 0), axis=3)
    s = jnp.arange(COPY_SLOTS)
    owner = jnp.sum((start + m)[:, :, None, :] <= s[None, None, :, None], axis=3)
    own = owner[..., None] == jnp.arange(N_EXPERTS)
    step = (sizes * SUBLANES)[:, None, None, None] * (s[None, None, :, None] - start[:, :, None, :])
    olocs = jnp.sum(jnp.where(own, loc0[:, :, None, :] + step, 0), axis=3)
    odsts = jnp.sum(jnp.where(own, dst0[:, :, None, :] + step, 0), axis=3)
    nops = jnp.sum(m, axis=2)
    tail = pad_start + tot
    ntail = (padded - tot) // SUBLANES
    units = padded // FFN_UNIT
    n_it = (units + FFN_ITEM_UNITS - 1) // FFN_ITEM_UNITS
    it_start = jnp.sum(jnp.where(e_before, n_it[None, :], 0), axis=1)
    t = jnp.arange(n_items_max)
    e_ids = jnp.arange(N_EXPERTS)
    owner = jnp.minimum(jnp.sum((it_start + n_it)[None, :] <= t[:, None], axis=1), N_EXPERTS - 1)
    own = owner[:, None] == e_ids[None, :]
    pick = lambda v: jnp.sum(jnp.where(own, v[None, :], 0), axis=1)
    j = t - pick(it_start)
    istart = pick(pad_start) + j * FFN_BLOCK
    iunits = jnp.clip(pick(units) - FFN_ITEM_UNITS * j, 1, FFN_ITEM_UNITS)
    ordinal = jnp.sum(jnp.where(e_before, (n_it > 0).astype(jnp.int32)[None, :], 0), axis=1)
    later = (e_ids[None, :] > e_ids[:, None]) & (n_it > 0)[None, :]
    nxt = jnp.min(jnp.where(later, e_ids[None, :], N_EXPERTS), axis=1)
    nxt = jnp.where(nxt == N_EXPERTS, -1, nxt)
    i32 = lambda a: a.astype(jnp.int32)
    items = (i32(jnp.sum(n_it).reshape(1)), i32(istart), i32(iunits), i32(owner), i32(j == 0),
             i32(pick(ordinal) % 2), i32(pick(nxt)))
    ops = (i32(nops.reshape(-1)), i32(olocs.reshape(-1)), i32(odsts.reshape(-1)))
    return ops, i32(ntot), i32(tail), i32(ntail), locoff.astype(F32)[:, :, None], items


def kernel(x_prompt, x_sample, state_conv, state_rglru, norm1_g, w_in, gmlp_ln_g, gmlp_ln_b, gmlp_w_s, gmlp_b_s,
           conv_w, conv_b, lru_w_a, lru_b_a, lru_w_x, lru_b_x, lru_lambda, w_out, norm2_g,
           router_group_w, router_group_b, router_expert_w, router_expert_b,
           expert_w_gate, expert_w_up, expert_w_down, final_norm_g):
    depth = w_in.shape[0]
    nb, t_len, _ = x_prompt.shape
    db, dt, _ = x_sample.shape
    n_prompt = nb * t_len
    n_sample = db * dt
    assert t_len % PROMPT_TILE == 0 and n_prompt % TOKEN_TILE == 0 and n_sample == TOKEN_TILE
    n_total = n_prompt + n_sample
    assert n_total % STEP_TOKENS == 0
    n_tiles = n_total // TOKEN_TILE
    p_tiles = n_prompt // TOKEN_TILE
    p_rows = 2 * n_total + (SUBLANES - 1) * N_EXPERTS * n_tiles + N_EXPERTS * (FFN_UNIT - 1)
    p_rows = -(-p_rows // FFN_UNIT) * FFN_UNIT
    n_items_max = p_rows // FFN_BLOCK + N_EXPERTS
    upper = jnp.triu(jnp.ones((TOKEN_TILE, TOKEN_TILE), BF16), 1)

    xp, xs = x_prompt, x_sample
    conv_p, h_p, v_s, conv_s, h_s = [], [], [], [], []
    for l in range(depth):
        lw = (norm1_g, w_in, gmlp_ln_g, gmlp_ln_b, gmlp_w_s, gmlp_b_s, conv_w, conv_b, lru_w_a, lru_b_a,
              lru_w_x, lru_b_x, lru_lambda, w_out, norm2_g, router_group_w, router_group_b,
              router_expert_w, router_expert_b)
        wts_p = _mixer_weights(l, GMLP_CHUNK, 1, *lw)
        wts_s = _mixer_weights(l, dt, db, *lw)
        zero_conv = jnp.zeros((nb, CONV_WIDTH - 1, D_B), F32)
        zero_h = jnp.zeros((nb, 1, D_B), F32)
        y1, h2, route, cnt_p, cp, hp = _mixer_call(
            xp, zero_conv, zero_h, wts_p, n_total=n_total, row_offset=0, sb=1, tt=PROMPT_TILE, chunk=GMLP_CHUNK, seg_len=PROMPT_SEG, seg_stride=PROMPT_SEG + SUBLANES, chain=True, emit_vn=False)
        y1, h2, route, cnt_s, cs, hs, vs = _mixer_call(
            xs, state_conv[l], state_rglru[l][None], wts_s, n_total=n_total, row_offset=n_prompt, sb=db,
            tt=dt, chunk=db * dt, seg_len=dt, seg_stride=dt + SUBLANES, chain=False, emit_vn=True,
            alias_in=(y1, h2, route))
        conv_p.append(cp)
        h_p.append(hp[:, 0])
        v_s.append(vs)
        conv_s.append(cs)
        h_s.append(hs[0])

        sub = PROMPT_TILE // TOKEN_TILE
        cnt = jnp.concatenate([
            jnp.swapaxes(cnt_p[:, :, :sub], 1, 2).reshape(p_tiles, N_EXPERTS),
            cnt_s[:, :, 0]], axis=0).astype(jnp.int32)
        ops, ntot, tail, ntail, locoff, items = _routing_tables(cnt, n_items_max)

        xb, slots = _dispatch_call(h2, route, locoff, upper, ops, ntot, tail, ntail, n_tiles=n_tiles,
                                   p_rows=p_rows)
        yb = _ffn_call(xb, expert_w_gate[l], expert_w_up[l], expert_w_down[l], items, p_rows=p_rows)
        assert l == depth - 1, "deeper stacks need an un-normalised combine between layers"
        fg = final_norm_g.reshape(1, D_MODEL)
        yp, ysm = _combine_call(y1, slots, fg, yb, ops, ntot, n_tiles=n_tiles, n_prompt=n_prompt,
                                n_sample=n_sample)
        xp = yp.reshape(nb, t_len, D_MODEL)
        xs = ysm.reshape(db, dt, D_MODEL)

    return (xp, xs, jnp.stack(conv_p), jnp.stack(h_p), jnp.stack(v_s), jnp.stack(conv_s), jnp.stack(h_s))
```

```python
import functools

import jax
import jax.numpy as jnp
from jax import lax
from jax.experimental import pallas as pl
from jax.experimental.pallas import tpu as pltpu

D_MODEL = 1024
D_A = 512
D_B = 512
A_GROUPS = 4
A_GROUP_DIM = 128
GMLP_CHUNK = 128
B_HEADS = 8
CONV_WIDTH = 4
LRU_C = 8.0
N_GROUPS = 4
EXPERTS_PER_GROUP = 8
N_EXPERTS = 32
D_EXPERT = 512
EPS = 1e-6
TINY = 1e-30

SUBLANES = 8
LANES = 128
ROUTER_ROWS = 40
TOKEN_TILE = 256
PROMPT_TILE = 1024
PROMPT_SEG = PROMPT_TILE // SUBLANES
FFN_UNIT = 256
FFN_ITEM_UNITS = 4
FFN_BLOCK = FFN_ITEM_UNITS * FFN_UNIT
ZERO_CHUNKS = 8
ZERO_ROWS = ZERO_CHUNKS * SUBLANES
SLOT_ROWS = 2 * TOKEN_TILE + N_EXPERTS * SUBLANES
SLOT_ROWS_COMMON = 2 * TOKEN_TILE + 9 * 16
TILE_CHUNKS = SLOT_ROWS // SUBLANES
COPY_SLOTS = N_EXPERTS
COPY_CLASSES = ((4, COPY_SLOTS), (3, COPY_SLOTS), (2, COPY_SLOTS), (1, COPY_SLOTS))
assert TILE_CHUNKS // 4 <= COPY_SLOTS
TILES_PER_STEP = 3
STEP_TOKENS = TILES_PER_STEP * TOKEN_TILE
PACK_COLS = D_MODEL // 2
XB_COLS = PACK_COLS + LANES
U32 = jnp.uint32
HI_MASK = 0xFFFF0000
VMEM_BYTES_V7X = 64 * 1024 * 1024
VMEM_LIMIT = VMEM_BYTES_V7X - 8 * 1024 * 1024

BF16 = jnp.bfloat16
F32 = jnp.float32


def _rms(x, g):
    return x * lax.rsqrt(jnp.mean(x * x, axis=-1, keepdims=True) + EPS) * g


def _dot(a, b):
    return jnp.dot(a, b, preferred_element_type=F32)


def _sigmoid(x):
    return 1.0 / (1.0 + jnp.exp(-x))


def _pack_bf16_pairs(x):
    bits = lax.bitcast_convert_type(x, U32)
    return (bits[:, PACK_COLS:] & U32(HI_MASK)) | (bits[:, :PACK_COLS] >> 16)


def _unpack_bf16_pairs(w):
    lo = lax.bitcast_convert_type(w << 16, F32).astype(BF16)
    hi = lax.bitcast_convert_type(w & U32(HI_MASK), F32).astype(BF16)
    return jnp.concatenate([lo, hi], axis=1)


def _mixer_kernel(x_ref, hist_ref, h0_ref, n1g_ref, win_ref, lng_ref, lnb_ref, wsm_ref, bsx_ref,
                  cw_ref, cb_ref, wa_ref, ba_ref, wx_ref, bx_ref, lam_ref, wout_ref, n2g_ref,
                  wr_ref, br_ref, *rest, sb, tt, chunk, seg_len, seg_stride, chain, emit_vn, aliased):
    if aliased:
        rest = rest[3:]
    if emit_vn:
        (y1_ref, h2_ref, route_ref, cnt_ref, conv_ref, hlast_ref, vn_ref,
         xs_ref, sa_ref, sb_ref, mix_ref, hc_ref) = rest
    else:
        (y1_ref, h2_ref, route_ref, cnt_ref, conv_ref, hlast_ref,
         xs_ref, sa_ref, sb_ref, mix_ref, hc_ref) = rest
        vn_ref = None
    t_idx = pl.program_id(1)

    @pl.when(t_idx == 0)
    def _():
        xs_ref[:, SUBLANES - 3:SUBLANES, :] = hist_ref[...]
        hc_ref[...] = jnp.zeros_like(hc_ref)

    @pl.when(t_idx != 0)
    def _():
        xs_ref[:, SUBLANES - 3:SUBLANES, :] = xs_ref[:, SUBLANES + tt - 3:SUBLANES + tt, :]

    rows = sb * tt
    n_seg = rows // seg_len
    half = D_B // 2

    x = x_ref[...].reshape(rows, D_MODEL)
    hb = _rms(x, n1g_ref[...]).astype(BF16)

    v_a = _dot(hb, win_ref[:, D_A:2 * D_A])
    x_b = _dot(hb, win_ref[:, 2 * D_A:2 * D_A + D_B])
    mixed = []
    for g in range(A_GROUPS):
        cols = slice(g * A_GROUP_DIM, (g + 1) * A_GROUP_DIM)
        vg = v_a[:, cols]
        mu = jnp.mean(vg, axis=-1, keepdims=True)
        dv = vg - mu
        var = jnp.mean(dv * dv, axis=-1, keepdims=True)
        vn = dv * lax.rsqrt(var + EPS) * lng_ref[:, cols] + lnb_ref[:, cols]
        if vn_ref is not None:
            vn_ref[:, :, cols] = vn.reshape(sb, tt, A_GROUP_DIM)
        vnb = vn.astype(BF16)
        mixed.append([_dot(wsm_ref[g], vnb[c * chunk:(c + 1) * chunk]) for c in range(rows // chunk)])
    u_a = _dot(hb, win_ref[:, 0:D_A])

    xs_ref[:, SUBLANES:SUBLANES + tt, :] = x_b.reshape(sb, tt, D_B)
    xc = cb_ref[...] + x_b * cw_ref[3:4, :]
    for j in range(1, CONV_WIDTH):
        shifted = xs_ref[:, SUBLANES - j:SUBLANES - j + tt, :].reshape(rows, D_B)
        xc = xc + shifted * cw_ref[3 - j:4 - j, :]

    xcb = xc.astype(BF16)
    r_lin = jnp.concatenate([_dot(xcb[:, :half], wa_ref[0]), _dot(xcb[:, half:], wa_ref[1])], axis=1)
    i_lin = jnp.concatenate([_dot(xcb[:, :half], wx_ref[0]), _dot(xcb[:, half:], wx_ref[1])], axis=1)

    for g in range(A_GROUPS):
        cols = slice(g * A_GROUP_DIM, (g + 1) * A_GROUP_DIM)
        for c in range(rows // chunk):
            rs = slice(c * chunk, (c + 1) * chunk)
            s = mixed[g][c] + bsx_ref[:, cols]
            mix_ref[rs, cols] = (u_a[rs, cols] * s).astype(BF16)

    g_b = _dot(hb, win_ref[:, 2 * D_A + D_B:])
    r = _sigmoid(r_lin + ba_ref[...])
    i_g = _sigmoid(i_lin + bx_ref[...])
    log_a = (-LRU_C * r) * jax.nn.softplus(-lam_ref[...])
    a = jnp.exp(log_a)
    y_gain = 1.0 - a * a
    gain = y_gain * lax.rsqrt(jnp.maximum(y_gain, TINY))
    bterm = gain * i_g * xc

    n_lb = D_B // LANES
    for j in range(n_seg):
        for k in range(n_lb):
            lc = slice(k * LANES, (k + 1) * LANES)
            dst = slice(j * seg_stride, j * seg_stride + seg_len)
            sa_ref[k, dst, :] = a[j * seg_len:(j + 1) * seg_len, lc]
            sb_ref[k, dst, :] = bterm[j * seg_len:(j + 1) * seg_len, lc]

    y_a = x + _dot(mix_ref[:, 0:D_A], wout_ref[0:D_A, :])
    gel = jax.nn.gelu(g_b)

    for grp in range(n_seg // SUBLANES):
        base = grp * SUBLANES * seg_stride
        bs = slice(grp * SUBLANES, (grp + 1) * SUBLANES)
        if chain:
            h_init = tuple(jnp.zeros((SUBLANES, LANES), F32) for _ in range(n_lb))
        else:
            h_init = tuple(h0_ref[0, bs, k * LANES:(k + 1) * LANES] for k in range(n_lb))
        a_init = tuple(jnp.ones((SUBLANES, LANES), F32) for _ in range(n_lb))

        def step(i, carry, base=base):
            hs, acs = carry
            idx = pl.ds(base + i, SUBLANES, stride=seg_stride)
            new_h, new_a = [], []
            for k in range(n_lb):
                av = sa_ref[k, idx, :]
                h = av * hs[k] + sb_ref[k, idx, :]
                sb_ref[k, idx, :] = h
                new_h.append(h)
                if chain:
                    ac = av * acs[k]
                    sa_ref[k, idx, :] = ac
                    new_a.append(ac)
                else:
                    new_a.append(acs[k])
            return tuple(new_h), tuple(new_a)

        carry = (h_init, a_init)
        for i in range(seg_len):
            carry = step(i, carry)
        h_end, a_end = carry

        for k in range(n_lb):
            lc = slice(k * LANES, (k + 1) * LANES)
            mc = slice(D_A + k * LANES, D_A + (k + 1) * LANES)
            if chain:
                h_in = hc_ref[:, lc]
                for j in range(SUBLANES):
                    seg = slice(base + j * seg_stride, base + j * seg_stride + seg_len)
                    rs = slice((grp * SUBLANES + j) * seg_len, (grp * SUBLANES + j + 1) * seg_len)
                    h_seg = sb_ref[k, seg, :] + sa_ref[k, seg, :] * h_in
                    mix_ref[rs, mc] = (h_seg * gel[rs, lc]).astype(BF16)
                    h_in = h_end[k][j:j + 1] + a_end[k][j:j + 1] * h_in
                hc_ref[:, lc] = h_in
                hlast_ref[0, :, lc] = h_in
            else:
                for j in range(SUBLANES):
                    seg = slice(base + j * seg_stride, base + j * seg_stride + seg_len)
                    rs = slice((grp * SUBLANES + j) * seg_len, (grp * SUBLANES + j + 1) * seg_len)
                    mix_ref[rs, mc] = (sb_ref[k, seg, :] * gel[rs, lc]).astype(BF16)
                hlast_ref[0, bs, lc] = h_end[k]

    y1 = y_a + _dot(mix_ref[:, D_A:], wout_ref[D_A:, :])
    y1_ref[...] = y1
    h2 = _rms(y1, n2g_ref[...]).astype(BF16)
    h2_ref[...] = h2

    logits = lax.dot_general(wr_ref[...], h2, (((1,), (1,)), ((), ())), preferred_element_type=F32) + br_ref[...]
    gl = [logits[N_EXPERTS + k:N_EXPERTS + k + 1, :] for k in range(N_GROUPS)]
    gmax = jnp.maximum(jnp.maximum(gl[0], gl[1]), jnp.maximum(gl[2], gl[3]))
    grp_idx = jnp.where(gl[0] == gmax, 0, jnp.where(gl[1] == gmax, 1, jnp.where(gl[2] == gmax, 2, 3)))
    gsum = (jnp.exp(gl[0] - gmax) + jnp.exp(gl[1] - gmax)) + (jnp.exp(gl[2] - gmax) + jnp.exp(gl[3] - gmax))
    p_grp = 1.0 / gsum
    e_in = jnp.where(grp_idx == 0, logits[0:8],
                     jnp.where(grp_idx == 1, logits[8:16], jnp.where(grp_idx == 2, logits[16:24], logits[24:32])))
    iota8 = lax.broadcasted_iota(jnp.int32, (EXPERTS_PER_GROUP, rows), 0)
    m1 = jnp.max(e_in, axis=0, keepdims=True)
    i1 = jnp.min(jnp.where(e_in == m1, iota8, EXPERTS_PER_GROUP), axis=0, keepdims=True)
    e_rest = jnp.where(iota8 == i1, -jnp.inf, e_in)
    m2 = jnp.max(e_rest, axis=0, keepdims=True)
    i2 = jnp.min(jnp.where(e_rest == m2, iota8, EXPERTS_PER_GROUP), axis=0, keepdims=True)
    t2 = jnp.exp(m2 - m1)
    den = 1.0 + t2
    gate0 = p_grp * (1.0 / den)
    gate1 = p_grp * (t2 / den)
    e0 = (grp_idx * EXPERTS_PER_GROUP + i1).astype(F32)
    e1 = (grp_idx * EXPERTS_PER_GROUP + i2).astype(F32)
    route_ref[...] = jnp.where(iota8 == 0, e0, jnp.where(iota8 == 1, e1, jnp.where(
        iota8 == 2, gate0, jnp.where(iota8 == 3, gate1, 0.0))))

    iota_e = lax.broadcasted_iota(jnp.int32, (N_EXPERTS, rows), 0).astype(F32)
    onehot = (iota_e == e0).astype(F32) + (iota_e == e1).astype(F32)
    lane = lax.broadcasted_iota(jnp.int32, (N_EXPERTS, LANES), 1)
    cnt = jnp.zeros((N_EXPERTS, LANES), F32)
    for s in range(rows // TOKEN_TILE):
        c_s = jnp.sum(onehot[:, s * TOKEN_TILE:(s + 1) * TOKEN_TILE], axis=1, keepdims=True)
        cnt = cnt + jnp.where(lane == s, c_s, 0.0)
    cnt_ref[...] = cnt.reshape(1, N_EXPERTS, LANES)
    conv_ref[...] = xs_ref[:, SUBLANES + tt - 3:SUBLANES + tt, :]


def _mixer_call(x, hist, h0, wts, *, n_total, row_offset, sb, tt, chunk, seg_len, seg_stride, chain,
                emit_vn, alias_in=None):
    nb, t_len, _ = x.shape
    nbb = nb // sb
    ntt = t_len // tt
    rows = sb * tt
    blk0 = row_offset // rows
    n_seg = rows // seg_len

    def full(arr):
        nd = arr.ndim
        return pl.BlockSpec(arr.shape, lambda b, t, _nd=nd: (0,) * _nd)

    in_specs = [
        pl.BlockSpec((sb, tt, D_MODEL), lambda b, t: (b, t, 0)),
        pl.BlockSpec((sb, CONV_WIDTH - 1, D_B), lambda b, t: (b, 0, 0)),
        pl.BlockSpec((1, sb, D_B), lambda b, t: (b, 0, 0)),
    ] + [full(w) for w in wts]
    args = [x, hist, h0] + list(wts)
    io_alias = {}
    if alias_in is not None:
        for k, arr in enumerate(alias_in):
            io_alias[len(args)] = k
            in_specs.append(pl.BlockSpec(memory_space=pl.ANY))
            args.append(arr)

    out_shape = [
        jax.ShapeDtypeStruct((n_total, D_MODEL), F32),
        jax.ShapeDtypeStruct((n_total, D_MODEL), BF16),
        jax.ShapeDtypeStruct((SUBLANES, n_total), F32),
        jax.ShapeDtypeStruct((nbb * ntt, N_EXPERTS, LANES), F32),
        jax.ShapeDtypeStruct((nb, CONV_WIDTH - 1, D_B), F32),
        jax.ShapeDtypeStruct((nbb, sb, D_B), F32),
    ]
    out_specs = [
        pl.BlockSpec((rows, D_MODEL), lambda b, t: (blk0 + b * ntt + t, 0)),
        pl.BlockSpec((rows, D_MODEL), lambda b, t: (blk0 + b * ntt + t, 0)),
        pl.BlockSpec((SUBLANES, rows), lambda b, t: (0, blk0 + b * ntt + t)),
        pl.BlockSpec((1, N_EXPERTS, LANES), lambda b, t: (b * ntt + t, 0, 0)),
        pl.BlockSpec((sb, CONV_WIDTH - 1, D_B), lambda b, t: (b, 0, 0)),
        pl.BlockSpec((1, sb, D_B), lambda b, t: (b, 0, 0)),
    ]
    if emit_vn:
        out_shape.append(jax.ShapeDtypeStruct((nb, t_len, D_A), F32))
        out_specs.append(pl.BlockSpec((sb, tt, D_A), lambda b, t: (b, t, 0)))

    kern = functools.partial(_mixer_kernel, sb=sb, tt=tt, chunk=chunk, seg_len=seg_len, seg_stride=seg_stride,
                             chain=chain, emit_vn=emit_vn, aliased=alias_in is not None)
    return pl.pallas_call(
        kern,
        grid=(nbb, ntt),
        in_specs=in_specs,
        out_specs=out_specs,
        out_shape=out_shape,
        scratch_shapes=[
            pltpu.VMEM((sb, SUBLANES + tt, D_B), F32),
            pltpu.VMEM((D_B // LANES, n_seg * seg_stride, LANES), F32),
            pltpu.VMEM((D_B // LANES, n_seg * seg_stride, LANES), F32),
            pltpu.VMEM((rows, D_MODEL), BF16),
            pltpu.VMEM((1, D_B), F32),
        ],
        input_output_aliases=io_alias,
        compiler_params=pltpu.CompilerParams(dimension_semantics=("arbitrary", "arbitrary"),
                                             vmem_limit_bytes=VMEM_LIMIT),
        name="mixer_chain" if chain else "mixer_step",
    )(*args)


def _tile_slots(route_ref, locoff_ref, upper_ref, u):
    lanes = slice(u * TOKEN_TILE, (u + 1) * TOKEN_TILE)
    e0 = route_ref[0:1, lanes]
    e1 = route_ref[1:2, lanes]
    iota_e = lax.broadcasted_iota(jnp.int32, (N_EXPERTS, TOKEN_TILE), 0).astype(F32)
    oh0 = (iota_e == e0).astype(F32)
    oh1 = (iota_e == e1).astype(F32)
    c0 = _dot(oh0.astype(BF16), upper_ref[...])
    c1 = _dot(oh1.astype(BF16), upper_ref[...])
    cnt0 = jnp.sum(oh0, axis=1, keepdims=True)
    base0 = locoff_ref[u]
    base1 = base0 + cnt0
    p0 = jnp.sum(oh0 * (base0 + c0), axis=0, keepdims=True)
    p1 = jnp.sum(oh1 * (base1 + c1), axis=0, keepdims=True)
    return p0.astype(jnp.int32), p1.astype(jnp.int32)


def _by_fill(ntot_ref, step, body):
    most = ntot_ref[step * TILES_PER_STEP]
    for u in range(1, TILES_PER_STEP):
        most = jnp.maximum(most, ntot_ref[step * TILES_PER_STEP + u])
    fits = most * SUBLANES <= SLOT_ROWS_COMMON

    @pl.when(fits)
    def _():
        body(SLOT_ROWS_COMMON)

    @pl.when(jnp.logical_not(fits))
    def _():
        body(SLOT_ROWS)


def _copy_ops(tile, n_tiles, nops_ref, oloc_ref, odst_ref, make_copy):
    base = 0
    for k, (chunks, cap) in enumerate(COPY_CLASSES):
        n = nops_ref[k * n_tiles + tile]

        def per_pair(i, _, chunks=chunks, cap=cap, base=base, n=n):
            idx = base + tile * cap + 2 * i
            make_copy(pl.multiple_of(oloc_ref[idx], SUBLANES), pl.multiple_of(odst_ref[idx], SUBLANES),
                      chunks * SUBLANES)

            @pl.when(2 * i + 1 < n)
            def _():
                make_copy(pl.multiple_of(oloc_ref[idx + 1], SUBLANES), pl.multiple_of(odst_ref[idx + 1], SUBLANES),
                          chunks * SUBLANES)
            return 0

        assert cap % 2 == 0
        lax.fori_loop(0, (n + 1) // 2, per_pair, 0)
        base += n_tiles * cap


def _dispatch_kernel(nops_ref, oloc_ref, odst_ref, ntot_ref, tail_ref, ntail_ref,
                     h2_ref, route_ref, locoff_ref, upper_ref, xb_ref, slots_ref, buf_ref, zero_ref, sem, zsem,
                     *, n_tiles):
    i = pl.program_id(0)
    n_steps = pl.num_programs(0)
    slot = i % 2

    def chunk_copy(s, u, loc, dst, rows=SUBLANES):
        return pltpu.make_async_copy(buf_ref.at[s, u, pl.ds(loc, rows), :],
                                     xb_ref.at[pl.ds(dst, rows), :], sem.at[s])

    def wait_step(step, s):
        for u in range(TILES_PER_STEP):
            n_rows = ntot_ref[step * TILES_PER_STEP + u] * SUBLANES

            @pl.when(n_rows > 0)
            def _():
                pltpu.make_async_copy(buf_ref.at[s, u, pl.ds(0, n_rows), :], xb_ref.at[pl.ds(0, n_rows), :],
                                      sem.at[s]).wait()

    @pl.when(i >= 2)
    def _():
        wait_step(i - 2, slot)

    def sort_tiles(n_rows):
        iota_s = lax.broadcasted_iota(jnp.int32, (n_rows, TOKEN_TILE), 0)
        iota8 = lax.broadcasted_iota(jnp.int32, (SUBLANES, TOKEN_TILE), 0)
        for u in range(TILES_PER_STEP):
            lanes = slice(u * TOKEN_TILE, (u + 1) * TOKEN_TILE)
            p0, p1 = _tile_slots(route_ref, locoff_ref, upper_ref, u)
            slots_ref[:, lanes] = jnp.where(iota8 == 0, p0, jnp.where(iota8 == 1, p1, 0))
            hit0 = iota_s == p0
            hit1 = iota_s == p1
            perm = jnp.where(hit0 | hit1, 1.0, 0.0).astype(BF16)
            sorted_rows = _dot(perm, h2_ref[u * TOKEN_TILE:(u + 1) * TOKEN_TILE, :])
            buf_ref[slot, u, 0:n_rows, 0:PACK_COLS] = _pack_bf16_pairs(sorted_rows)
            gates = jnp.where(hit0, route_ref[2:3, lanes], 0.0) + jnp.where(hit1, route_ref[3:4, lanes], 0.0)
            gcol = jnp.sum(gates, axis=1, keepdims=True)
            buf_ref[slot, u, 0:n_rows, PACK_COLS:XB_COLS] = lax.bitcast_convert_type(
                jnp.broadcast_to(gcol, (n_rows, LANES)), U32)

    _by_fill(ntot_ref, i, sort_tiles)

    for u in range(TILES_PER_STEP):
        _copy_ops(i * TILES_PER_STEP + u, n_tiles, nops_ref, oloc_ref, odst_ref,
                  lambda loc, dst, rows, u=u: chunk_copy(slot, u, loc, dst, rows).start())

    @pl.when(i == n_steps - 1)
    def _():
        zero_ref[...] = jnp.zeros_like(zero_ref)

        def zero_copy(dst, rows, k):
            return pltpu.make_async_copy(zero_ref.at[pl.ds(0, rows), :], xb_ref.at[pl.ds(dst, rows), :], zsem.at[k])

        def per_expert(e, tot):
            n_big, n_small = tot
            n = ntail_ref[e]
            d0 = tail_ref[e]
            nb = n // ZERO_CHUNKS
            ns = n - nb * ZERO_CHUNKS

            def big(c, _):
                zero_copy(pl.multiple_of(d0 + c * ZERO_ROWS, SUBLANES), ZERO_ROWS, 0).start()
                return 0

            def small(c, _):
                zero_copy(pl.multiple_of(d0 + nb * ZERO_ROWS + c * SUBLANES, SUBLANES), SUBLANES, 1).start()
                return 0

            lax.fori_loop(0, nb, big, 0)
            lax.fori_loop(0, ns, small, 0)
            return n_big + nb, n_small + ns

        n_big, n_small = lax.fori_loop(0, N_EXPERTS, per_expert, (0, 0))

        @pl.when(i >= 1)
        def _():
            wait_step(i - 1, 1 - slot)

        wait_step(i, slot)

        def wait_big(c, _):
            zero_copy(0, ZERO_ROWS, 0).wait()
            return 0

        def wait_small(c, _):
            zero_copy(0, SUBLANES, 1).wait()
            return 0

        lax.fori_loop(0, n_big, wait_big, 0)
        lax.fori_loop(0, n_small, wait_small, 0)


def _dispatch_call(h2, route, locoff, upper, ops, ntot, tail, ntail, *, n_tiles, p_rows):
    grid_spec = pltpu.PrefetchScalarGridSpec(
        num_scalar_prefetch=6,
        grid=(n_tiles // TILES_PER_STEP,),
        in_specs=[
            pl.BlockSpec((STEP_TOKENS, D_MODEL), lambda i, *_: (i, 0)),
            pl.BlockSpec((SUBLANES, STEP_TOKENS), lambda i, *_: (0, i)),
            pl.BlockSpec((TILES_PER_STEP, N_EXPERTS, 1), lambda i, *_: (i, 0, 0)),
            pl.BlockSpec((TOKEN_TILE, TOKEN_TILE), lambda i, *_: (0, 0)),
        ],
        out_specs=[pl.BlockSpec(memory_space=pl.ANY),
                   pl.BlockSpec((SUBLANES, STEP_TOKENS), lambda i, *_: (0, i))],
        scratch_shapes=[
            pltpu.VMEM((2, TILES_PER_STEP, SLOT_ROWS, XB_COLS), U32),
            pltpu.VMEM((ZERO_ROWS, XB_COLS), U32),
            pltpu.SemaphoreType.DMA((2,)),
            pltpu.SemaphoreType.DMA((2,)),
        ],
    )
    return pl.pallas_call(
        functools.partial(_dispatch_kernel, n_tiles=n_tiles),
        grid_spec=grid_spec,
        out_shape=[jax.ShapeDtypeStruct((p_rows, XB_COLS), U32),
                   jax.ShapeDtypeStruct((SUBLANES, n_tiles * TOKEN_TILE), jnp.int32)],
        compiler_params=pltpu.CompilerParams(dimension_semantics=("arbitrary",), vmem_limit_bytes=VMEM_LIMIT),
        name="dispatch",
    )(*ops, ntot, tail, ntail, h2, route, locoff, upper)


def _ffn_kernel(nitems_ref, istart_ref, iunits_ref, iexp_ref, ifirst_ref, iwslot_ref, inext_ref,
                xb_ref, wg_ref, wu_ref, wd_ref, yb_ref,
                xin_ref, yout_ref, wgf_ref, wuf_ref, wdf_ref, wgb_ref, wub_ref, wdb_ref, sem_in, sem_out, sem_w):
    n_items = nitems_ref[0]
    sizes = tuple(k * FFN_UNIT for k in range(FFN_ITEM_UNITS, 0, -1))

    def rows_in(t, s, rows):
        start = pl.multiple_of(istart_ref[t], FFN_UNIT)
        return pltpu.make_async_copy(xb_ref.at[pl.ds(start, rows), :], xin_ref.at[s, pl.ds(0, rows), :], sem_in.at[s])

    def rows_out(t, s, rows):
        start = pl.multiple_of(istart_ref[t], FFN_UNIT)
        return pltpu.make_async_copy(yout_ref.at[s, pl.ds(0, rows), :], yb_ref.at[pl.ds(start, rows), :],
                                     sem_out.at[s])

    def start_by_size(make, t, s):
        for rows in sizes:
            @pl.when(iunits_ref[t] * FFN_UNIT == rows)
            def _():
                make(t, s, rows).start()

    def weight_copies(e, ws):
        return (pltpu.make_async_copy(wg_ref.at[e], wgf_ref.at[ws], sem_w.at[ws]),
                pltpu.make_async_copy(wu_ref.at[e], wuf_ref.at[ws], sem_w.at[ws]),
                pltpu.make_async_copy(wd_ref.at[e], wdf_ref.at[ws], sem_w.at[ws]))

    def compute(s, rows):
        xb = _unpack_bf16_pairs(xin_ref[s, 0:rows, 0:PACK_COLS])
        gate = lax.bitcast_convert_type(xin_ref[s, 0:rows, PACK_COLS:PACK_COLS + 1], F32)
        a = _dot(xb, wgb_ref[...])
        u = _dot(xb, wub_ref[...])
        mid = (a * _sigmoid(a) * u).astype(BF16)
        y = _dot(mid, wdb_ref[...]) * gate
        yout_ref[s, 0:rows, :] = _pack_bf16_pairs(y.astype(BF16).astype(F32))

    @pl.when(n_items > 0)
    def _():
        for c in weight_copies(iexp_ref[0], 0):
            c.start()
        start_by_size(rows_in, 0, 0)

    def per_item(t, _):
        s = t % 2

        @pl.when(t + 1 < n_items)
        def _():
            start_by_size(rows_in, t + 1, 1 - s)

        @pl.when(ifirst_ref[t] == 1)
        def _():
            ws = iwslot_ref[t]
            for c in weight_copies(0, ws):
                c.wait()

            @pl.when(inext_ref[t] >= 0)
            def _():
                for c in weight_copies(inext_ref[t], 1 - ws):
                    c.start()

            wgb_ref[...] = wgf_ref[ws].astype(BF16)
            wub_ref[...] = wuf_ref[ws].astype(BF16)
            wdb_ref[...] = wdf_ref[ws].astype(BF16)

        rows_in(t, s, iunits_ref[t] * FFN_UNIT).wait()

        @pl.when(t >= 2)
        def _():
            rows_out(t - 2, s, iunits_ref[t - 2] * FFN_UNIT).wait()

        for rows in sizes:
            @pl.when(iunits_ref[t] * FFN_UNIT == rows)
            def _():
                compute(s, rows)

        start_by_size(rows_out, t, s)
        return 0

    lax.fori_loop(0, n_items, per_item, 0)

    for back in (2, 1):
        @pl.when(n_items >= back)
        def _():
            t = n_items - back
            rows_out(t, t % 2, iunits_ref[t] * FFN_UNIT).wait()


def _ffn_call(xb, wg, wu, wd, items, *, p_rows):
    any_spec = pl.BlockSpec(memory_space=pl.ANY)
    grid_spec = pltpu.PrefetchScalarGridSpec(
        num_scalar_prefetch=len(items),
        grid=(1,),
        in_specs=[any_spec, any_spec, any_spec, any_spec],
        out_specs=any_spec,
        scratch_shapes=[
            pltpu.VMEM((2, FFN_BLOCK, XB_COLS), U32),
            pltpu.VMEM((2, FFN_BLOCK, PACK_COLS), U32),
            pltpu.VMEM((2, D_MODEL, D_EXPERT), F32),
            pltpu.VMEM((2, D_MODEL, D_EXPERT), F32),
            pltpu.VMEM((2, D_EXPERT, D_MODEL), F32),
            pltpu.VMEM((D_MODEL, D_EXPERT), BF16),
            pltpu.VMEM((D_MODEL, D_EXPERT), BF16),
            pltpu.VMEM((D_EXPERT, D_MODEL), BF16),
            pltpu.SemaphoreType.DMA((2,)),
            pltpu.SemaphoreType.DMA((2,)),
            pltpu.SemaphoreType.DMA((2,)),
        ],
    )
    return pl.pallas_call(
        _ffn_kernel,
        grid_spec=grid_spec,
        out_shape=jax.ShapeDtypeStruct((p_rows, PACK_COLS), U32),
        compiler_params=pltpu.CompilerParams(dimension_semantics=("arbitrary",), vmem_limit_bytes=VMEM_LIMIT),
        name="expert_ffn",
    )(*items, xb, wg, wu, wd)


def _combine_kernel(nops_ref, oloc_ref, odst_ref, ntot_ref, y1_ref, slots_ref, fg_ref,
                    yb_ref, yp_ref, ys_ref, buf_ref, sem, *, n_tiles):
    i = pl.program_id(0)
    n_steps = pl.num_programs(0)
    slot = i % 2

    def chunk_copy(s, u, loc, src, rows):
        return pltpu.make_async_copy(yb_ref.at[pl.ds(src, rows), :],
                                     buf_ref.at[s, u, pl.ds(loc, rows), :], sem.at[s])

    def issue_step(step, s):
        for u in range(TILES_PER_STEP):
            _copy_ops(step * TILES_PER_STEP + u, n_tiles, nops_ref, oloc_ref, odst_ref,
                      lambda loc, src, rows, u=u: chunk_copy(s, u, loc, src, rows).start())

    @pl.when(i == 0)
    def _():
        buf_ref[...] = jnp.zeros_like(buf_ref)
        issue_step(0, 0)

    @pl.when(i + 1 < n_steps)
    def _():
        issue_step(i + 1, 1 - slot)

    iota_s = lax.broadcasted_iota(jnp.int32, (SLOT_ROWS, TOKEN_TILE), 0)
    perms = []
    for u in range(TILES_PER_STEP):
        lanes = slice(u * TOKEN_TILE, (u + 1) * TOKEN_TILE)
        p0 = slots_ref[0:1, lanes]
        p1 = slots_ref[1:2, lanes]
        perms.append(jnp.where((iota_s == p0) | (iota_s == p1), 1.0, 0.0).astype(BF16))

    for u in range(TILES_PER_STEP):
        n_rows = ntot_ref[i * TILES_PER_STEP + u] * SUBLANES

        @pl.when(n_rows > 0)
        def _():
            pltpu.make_async_copy(yb_ref.at[pl.ds(0, n_rows), :], buf_ref.at[slot, u, pl.ds(0, n_rows), :],
                                  sem.at[slot]).wait()

    for u in range(TILES_PER_STEP):
        rs = slice(u * TOKEN_TILE, (u + 1) * TOKEN_TILE)
        yb = _unpack_bf16_pairs(buf_ref[slot, u])
        moe = lax.dot_general(perms[u], yb, (((0,), (0,)), ((), ())), preferred_element_type=F32)
        y_tile = _rms(y1_ref[rs, :] + moe, fg_ref[...])
        yp_ref[rs, :] = y_tile

    @pl.when(i == n_steps - 1)
    def _():
        ys_ref[...] = y_tile


def _combine_call(y1, slots, fg, yb, ops, ntot, *, n_tiles, n_prompt, n_sample):
    n_steps = n_tiles // TILES_PER_STEP
    assert n_sample == TOKEN_TILE and n_prompt + n_sample == n_steps * STEP_TOKENS
    grid_spec = pltpu.PrefetchScalarGridSpec(
        num_scalar_prefetch=4,
        grid=(n_steps,),
        in_specs=[
            pl.BlockSpec((STEP_TOKENS, D_MODEL), lambda i, *_: (i, 0)),
            pl.BlockSpec((SUBLANES, STEP_TOKENS), lambda i, *_: (0, i)),
            pl.BlockSpec((1, D_MODEL), lambda i, *_: (0, 0)),
            pl.BlockSpec(memory_space=pl.ANY),
        ],
        out_specs=[
            pl.BlockSpec((STEP_TOKENS, D_MODEL), lambda i, *_: (i, 0)),
            pl.BlockSpec((n_sample, D_MODEL), lambda i, *_: (0, 0)),
        ],
        scratch_shapes=[
            pltpu.VMEM((2, TILES_PER_STEP, SLOT_ROWS, PACK_COLS), U32),
            pltpu.SemaphoreType.DMA((2,)),
        ],
    )
    return pl.pallas_call(
        functools.partial(_combine_kernel, n_tiles=n_tiles),
        grid_spec=grid_spec,
        out_shape=[jax.ShapeDtypeStruct((n_prompt, D_MODEL), F32), jax.ShapeDtypeStruct((n_sample, D_MODEL), F32)],
        compiler_params=pltpu.CompilerParams(dimension_semantics=("arbitrary",), vmem_limit_bytes=VMEM_LIMIT),
        name="combine",
    )(*ops, ntot, y1, slots, fg, yb)


def _block_diag(w):
    h, d, _ = w.shape
    eye = jnp.eye(h, dtype=w.dtype)
    return (eye[:, None, :, None] * w[:, :, None, :]).reshape(h * d, h * d)


def _head_blocks(w):
    half = B_HEADS // 2
    return jnp.stack([_block_diag(w[:half]), _block_diag(w[half:])]).astype(BF16)


def _mixer_weights(l, length, reps, norm1_g, w_in, gmlp_ln_g, gmlp_ln_b, gmlp_w_s, gmlp_b_s, conv_w, conv_b,
                   lru_w_a, lru_b_a, lru_w_x, lru_b_x, lru_lambda, w_out, norm2_g,
                   router_group_w, router_group_b, router_expert_w, router_expert_b):
    mask = jnp.tril(jnp.ones((length, length), dtype=bool))
    ws = jnp.where(mask, gmlp_w_s[l][:, :length, :length], 0.0)
    wsm = jnp.concatenate([jnp.pad(ws, ((0, 0), (0, 0), (j * length, (reps - 1 - j) * length)))
                           for j in range(reps)], axis=1)
    bs = jnp.tile(gmlp_b_s[l][:, :length], (1, reps))
    bsx = jnp.repeat(bs.T, A_GROUP_DIM, axis=1)
    wr = jnp.concatenate([router_expert_w[l].T, router_group_w[l].T,
                          jnp.zeros((ROUTER_ROWS - N_EXPERTS - N_GROUPS, D_MODEL), F32)], axis=0)
    br = jnp.concatenate([router_expert_b[l], router_group_b[l],
                          jnp.zeros((ROUTER_ROWS - N_EXPERTS - N_GROUPS,), F32)]).reshape(ROUTER_ROWS, 1)
    return [
        norm1_g[l].reshape(1, D_MODEL), w_in[l].astype(BF16),
        gmlp_ln_g[l].reshape(1, D_A), gmlp_ln_b[l].reshape(1, D_A), wsm.astype(BF16), bsx,
        conv_w[l], conv_b[l].reshape(1, D_B),
        _head_blocks(lru_w_a[l]), lru_b_a[l].reshape(1, D_B),
        _head_blocks(lru_w_x[l]), lru_b_x[l].reshape(1, D_B),
        lru_lambda[l].reshape(1, D_B), w_out[l].astype(BF16), norm2_g[l].reshape(1, D_MODEL),
        wr.astype(BF16), br,
    ]


def _routing_tables(cnt, n_items_max):
    n_tiles = cnt.shape[0]
    seg = (cnt + SUBLANES - 1) // SUBLANES * SUBLANES
    nch = seg // SUBLANES
    tot = jnp.sum(seg, axis=0)
    padded = (tot + FFN_UNIT - 1) // FFN_UNIT * FFN_UNIT
    e_before = jnp.arange(N_EXPERTS)[None, :] < jnp.arange(N_EXPERTS)[:, None]
    t_before = jnp.arange(n_tiles)[None, :] < jnp.arange(n_tiles)[:, None]
    pad_start = jnp.sum(jnp.where(e_before, padded[None, :], 0), axis=1)
    doff = pad_start[None, :] + jnp.sum(jnp.where(t_before[:, :, None], seg[None, :, :], 0), axis=1)
    locoff = jnp.sum(jnp.where(e_before[None, :, :], seg[:, None, :], 0), axis=2)
    ntot = jnp.sum(nch, axis=1)
    big = COPY_CLASSES[0][0]
    sizes = jnp.array([chunks for chunks, _ in COPY_CLASSES], jnp.int32)
    whole = nch // big
    done = whole * big * SUBLANES
    rest = nch - whole * big
    is_big = (sizes == big)[:, None, None]
    m = jnp.where(is_big, whole[None], (rest[None] == sizes[:, None, None]).astype(jnp.int32))
    loc0 = jnp.where(is_big, locoff[None], (locoff + done)[None])
    dst0 = jnp.where(is_big, doff[None], (doff + done)[None])
    start = jnp.sum(jnp.where(e_before[None, None], m[:, :, None, :], 0), axis=3)
    s = jnp.arange(COPY_SLOTS)
    owner = jnp.sum((start + m)[:, :, None, :] <= s[None, None, :, None], axis=3)
    own = owner[..., None] == jnp.arange(N_EXPERTS)
    step = (sizes * SUBLANES)[:, None, None, None] * (s[None, None, :, None] - start[:, :, None, :])
    olocs = jnp.sum(jnp.where(own, loc0[:, :, None, :] + step, 0), axis=3)
    odsts = jnp.sum(jnp.where(own, dst0[:, :, None, :] + step, 0), axis=3)
    nops = jnp.sum(m, axis=2)
    tail = pad_start + tot
    ntail = (padded - tot) // SUBLANES
    units = padded // FFN_UNIT
    n_it = (units + FFN_ITEM_UNITS - 1) // FFN_ITEM_UNITS
    it_start = jnp.sum(jnp.where(e_before, n_it[None, :], 0), axis=1)
    t = jnp.arange(n_items_max)
    e_ids = jnp.arange(N_EXPERTS)
    owner = jnp.minimum(jnp.sum((it_start + n_it)[None, :] <= t[:, None], axis=1), N_EXPERTS - 1)
    own = owner[:, None] == e_ids[None, :]
    pick = lambda v: jnp.sum(jnp.where(own, v[None, :], 0), axis=1)
    j = t - pick(it_start)
    istart = pick(pad_start) + j * FFN_BLOCK
    iunits = jnp.clip(pick(units) - FFN_ITEM_UNITS * j, 1, FFN_ITEM_UNITS)
    ordinal = jnp.sum(jnp.where(e_before, (n_it > 0).astype(jnp.int32)[None, :], 0), axis=1)
    later = (e_ids[None, :] > e_ids[:, None]) & (n_it > 0)[None, :]
    nxt = jnp.min(jnp.where(later, e_ids[None, :], N_EXPERTS), axis=1)
    nxt = jnp.where(nxt == N_EXPERTS, -1, nxt)
    i32 = lambda a: a.astype(jnp.int32)
    items = (i32(jnp.sum(n_it).reshape(1)), i32(istart), i32(iunits), i32(owner), i32(j == 0),
             i32(pick(ordinal) % 2), i32(pick(nxt)))
    ops = (i32(nops.reshape(-1)), i32(olocs.reshape(-1)), i32(odsts.reshape(-1)))
    return ops, i32(ntot), i32(tail), i32(ntail), locoff.astype(F32)[:, :, None], items


def kernel(x_prompt, x_sample, state_conv, state_rglru, norm1_g, w_in, gmlp_ln_g, gmlp_ln_b, gmlp_w_s, gmlp_b_s,
           conv_w, conv_b, lru_w_a, lru_b_a, lru_w_x, lru_b_x, lru_lambda, w_out, norm2_g,
           router_group_w, router_group_b, router_expert_w, router_expert_b,
           expert_w_gate, expert_w_up, expert_w_down, final_norm_g):
    depth = w_in.shape[0]
    nb, t_len, _ = x_prompt.shape
    db, dt, _ = x_sample.shape
    n_prompt = nb * t_len
    n_sample = db * dt
    assert t_len % PROMPT_TILE == 0 and n_prompt % TOKEN_TILE == 0 and n_sample == TOKEN_TILE
    n_total = n_prompt + n_sample
    assert n_total % STEP_TOKENS == 0
    n_tiles = n_total // TOKEN_TILE
    p_tiles = n_prompt // TOKEN_TILE
    p_rows = 2 * n_total + (SUBLANES - 1) * N_EXPERTS * n_tiles + N_EXPERTS * (FFN_UNIT - 1)
    p_rows = -(-p_rows // FFN_UNIT) * FFN_UNIT
    n_items_max = p_rows // FFN_BLOCK + N_EXPERTS
    upper = jnp.triu(jnp.ones((TOKEN_TILE, TOKEN_TILE), BF16), 1)

    xp, xs = x_prompt, x_sample
    conv_p, h_p, v_s, conv_s, h_s = [], [], [], [], []
    for l in range(depth):
        lw = (norm1_g, w_in, gmlp_ln_g, gmlp_ln_b, gmlp_w_s, gmlp_b_s, conv_w, conv_b, lru_w_a, lru_b_a,
              lru_w_x, lru_b_x, lru_lambda, w_out, norm2_g, router_group_w, router_group_b,
              router_expert_w, router_expert_b)
        wts_p = _mixer_weights(l, GMLP_CHUNK, 1, *lw)
        wts_s = _mixer_weights(l, dt, db, *lw)
        zero_conv = jnp.zeros((nb, CONV_WIDTH - 1, D_B), F32)
        zero_h = jnp.zeros((nb, 1, D_B), F32)
        y1, h2, route, cnt_p, cp, hp = _mixer_call(
            xp, zero_conv, zero_h, wts_p, n_total=n_total, row_offset=0, sb=1, tt=PROMPT_TILE, chunk=GMLP_CHUNK, seg_len=PROMPT_SEG, seg_stride=PROMPT_SEG + SUBLANES, chain=True, emit_vn=False)
        y1, h2, route, cnt_s, cs, hs, vs = _mixer_call(
            xs, state_conv[l], state_rglru[l][None], wts_s, n_total=n_total, row_offset=n_prompt, sb=db,
            tt=dt, chunk=db * dt, seg_len=dt, seg_stride=dt + SUBLANES, chain=False, emit_vn=True,
            alias_in=(y1, h2, route))
        conv_p.append(cp)
        h_p.append(hp[:, 0])
        v_s.append(vs)
        conv_s.append(cs)
        h_s.append(hs[0])

        sub = PROMPT_TILE // TOKEN_TILE
        cnt = jnp.concatenate([
            jnp.swapaxes(cnt_p[:, :, :sub], 1, 2).reshape(p_tiles, N_EXPERTS),
            cnt_s[:, :, 0]], axis=0).astype(jnp.int32)
        ops, ntot, tail, ntail, locoff, items = _routing_tables(cnt, n_items_max)

        xb, slots = _dispatch_call(h2, route, locoff, upper, ops, ntot, tail, ntail, n_tiles=n_tiles,
                                   p_rows=p_rows)
        yb = _ffn_call(xb, expert_w_gate[l], expert_w_up[l], expert_w_down[l], items, p_rows=p_rows)
        assert l == depth - 1, "deeper stacks need an un-normalised combine between layers"
        fg = final_norm_g.reshape(1, D_MODEL)
        yp, ysm = _combine_call(y1, slots, fg, yb, ops, ntot, n_tiles=n_tiles, n_prompt=n_prompt,
                                n_sample=n_sample)
        xp = yp.reshape(nb, t_len, D_MODEL)
        xs = ysm.reshape(db, dt, D_MODEL)

    return (xp, xs, jnp.stack(conv_p), jnp.stack(h_p), jnp.stack(v_s), jnp.stack(conv_s), jnp.stack(h_s))
```

```python
import functools

import jax
import jax.numpy as jnp
from jax import lax
from jax.experimental import pallas as pl
from jax.experimental.pallas import tpu as pltpu

D_MODEL = 1024
D_A = 512
D_B = 512
A_GROUPS = 4
A_GROUP_DIM = 128
GMLP_CHUNK = 128
B_HEADS = 8
CONV_WIDTH = 4
LRU_C = 8.0
N_GROUPS = 4
EXPERTS_PER_GROUP = 8
N_EXPERTS = 32
D_EXPERT = 512
EPS = 1e-6
TINY = 1e-30

SUBLANES = 8
LANES = 128
ROUTER_ROWS = 40
TOKEN_TILE = 256
PROMPT_TILE = 1024
PROMPT_SEG = PROMPT_TILE // SUBLANES
FFN_UNIT = 256
FFN_ITEM_UNITS = 4
FFN_BLOCK = FFN_ITEM_UNITS * FFN_UNIT
ZERO_CHUNKS = 8
ZERO_ROWS = ZERO_CHUNKS * SUBLANES
SLOT_ROWS = 2 * TOKEN_TILE + N_EXPERTS * SUBLANES
SLOT_ROWS_COMMON = 2 * TOKEN_TILE + 9 * 16
TILE_CHUNKS = SLOT_ROWS // SUBLANES
COPY_SLOTS = N_EXPERTS
COPY_CLASSES = ((4, COPY_SLOTS), (3, COPY_SLOTS), (2, COPY_SLOTS), (1, COPY_SLOTS))
assert TILE_CHUNKS // 4 <= COPY_SLOTS
TILES_PER_STEP = 3
STEP_TOKENS = TILES_PER_STEP * TOKEN_TILE
PACK_COLS = D_MODEL // 2
XB_COLS = PACK_COLS + LANES
U32 = jnp.uint32
HI_MASK = 0xFFFF0000
VMEM_BYTES_V7X = 64 * 1024 * 1024
VMEM_LIMIT = VMEM_BYTES_V7X - 8 * 1024 * 1024

BF16 = jnp.bfloat16
F32 = jnp.float32


def _rms(x, g):
    return x * lax.rsqrt(jnp.mean(x * x, axis=-1, keepdims=True) + EPS) * g


def _dot(a, b):
    return jnp.dot(a, b, preferred_element_type=F32)


def _sigmoid(x):
    return 1.0 / (1.0 + jnp.exp(-x))


def _pack_bf16_pairs(x):
    bits = lax.bitcast_convert_type(x, U32)
    return (bits[:, PACK_COLS:] & U32(HI_MASK)) | (bits[:, :PACK_COLS] >> 16)


def _unpack_bf16_pairs(w):
    lo = lax.bitcast_convert_type(w << 16, F32).astype(BF16)
    hi = lax.bitcast_convert_type(w & U32(HI_MASK), F32).astype(BF16)
    return jnp.concatenate([lo, hi], axis=1)


def _mixer_kernel(x_ref, hist_ref, h0_ref, n1g_ref, win_ref, lng_ref, lnb_ref, wsm_ref, bsx_ref,
                  cw_ref, cb_ref, wa_ref, ba_ref, wx_ref, bx_ref, lam_ref, wout_ref, n2g_ref,
                  wr_ref, br_ref, *rest, sb, tt, chunk, seg_len, seg_stride, chain, emit_vn, aliased):
    if aliased:
        rest = rest[3:]
    if emit_vn:
        (y1_ref, h2_ref, route_ref, cnt_ref, conv_ref, hlast_ref, vn_ref,
         xs_ref, sa_ref, sb_ref, mix_ref, hc_ref) = rest
    else:
        (y1_ref, h2_ref, route_ref, cnt_ref, conv_ref, hlast_ref,
         xs_ref, sa_ref, sb_ref, mix_ref, hc_ref) = rest
        vn_ref = None
    t_idx = pl.program_id(1)

    @pl.when(t_idx == 0)
    def _():
        xs_ref[:, SUBLANES - 3:SUBLANES, :] = hist_ref[...]
        hc_ref[...] = jnp.zeros_like(hc_ref)

    @pl.when(t_idx != 0)
    def _():
        xs_ref[:, SUBLANES - 3:SUBLANES, :] = xs_ref[:, SUBLANES + tt - 3:SUBLANES + tt, :]

    rows = sb * tt
    n_seg = rows // seg_len
    half = D_B // 2

    x = x_ref[...].reshape(rows, D_MODEL)
    hb = _rms(x, n1g_ref[...]).astype(BF16)

    v_a = _dot(hb, win_ref[:, D_A:2 * D_A])
    x_b = _dot(hb, win_ref[:, 2 * D_A:2 * D_A + D_B])
    mixed = []
    for g in range(A_GROUPS):
        cols = slice(g * A_GROUP_DIM, (g + 1) * A_GROUP_DIM)
        vg = v_a[:, cols]
        mu = jnp.mean(vg, axis=-1, keepdims=True)
        dv = vg - mu
        var = jnp.mean(dv * dv, axis=-1, keepdims=True)
        vn = dv * lax.rsqrt(var + EPS) * lng_ref[:, cols] + lnb_ref[:, cols]
        if vn_ref is not None:
            vn_ref[:, :, cols] = vn.reshape(sb, tt, A_GROUP_DIM)
        vnb = vn.astype(BF16)
        mixed.append([_dot(wsm_ref[g], vnb[c * chunk:(c + 1) * chunk]) for c in range(rows // chunk)])
    u_a = _dot(hb, win_ref[:, 0:D_A])

    xs_ref[:, SUBLANES:SUBLANES + tt, :] = x_b.reshape(sb, tt, D_B)
    xc = cb_ref[...] + x_b * cw_ref[3:4, :]
    for j in range(1, CONV_WIDTH):
        shifted = xs_ref[:, SUBLANES - j:SUBLANES - j + tt, :].reshape(rows, D_B)
        xc = xc + shifted * cw_ref[3 - j:4 - j, :]

    xcb = xc.astype(BF16)
    r_lin = jnp.concatenate([_dot(xcb[:, :half], wa_ref[0]), _dot(xcb[:, half:], wa_ref[1])], axis=1)
    i_lin = jnp.concatenate([_dot(xcb[:, :half], wx_ref[0]), _dot(xcb[:, half:], wx_ref[1])], axis=1)

    for g in range(A_GROUPS):
        cols = slice(g * A_GROUP_DIM, (g + 1) * A_GROUP_DIM)
        for c in range(rows // chunk):
            rs = slice(c * chunk, (c + 1) * chunk)
            s = mixed[g][c] + bsx_ref[:, cols]
            mix_ref[rs, cols] = (u_a[rs, cols] * s).astype(BF16)

    g_b = _dot(hb, win_ref[:, 2 * D_A + D_B:])
    r = _sigmoid(r_lin + ba_ref[...])
    i_g = _sigmoid(i_lin + bx_ref[...])
    log_a = (-LRU_C * r) * jax.nn.softplus(-lam_ref[...])
    a = jnp.exp(log_a)
    y_gain = 1.0 - a * a
    gain = y_gain * lax.rsqrt(jnp.maximum(y_gain, TINY))
    bterm = gain * i_g * xc

    n_lb = D_B // LANES
    for j in range(n_seg):
        for k in range(n_lb):
            lc = slice(k * LANES, (k + 1) * LANES)
            dst = slice(j * seg_stride, j * seg_stride + seg_len)
            sa_ref[k, dst, :] = a[j * seg_len:(j + 1) * seg_len, lc]
            sb_ref[k, dst, :] = bterm[j * seg_len:(j + 1) * seg_len, lc]

    y_a = x + _dot(mix_ref[:, 0:D_A], wout_ref[0:D_A, :])
    gel = jax.nn.gelu(g_b)

    for grp in range(n_seg // SUBLANES):
        base = grp * SUBLANES * seg_stride
        bs = slice(grp * SUBLANES, (grp + 1) * SUBLANES)
        if chain:
            h_init = tuple(jnp.zeros((SUBLANES, LANES), F32) for _ in range(n_lb))
        else:
            h_init = tuple(h0_ref[0, bs, k * LANES:(k + 1) * LANES] for k in range(n_lb))
        a_init = tuple(jnp.ones((SUBLANES, LANES), F32) for _ in range(n_lb))

        def step(i, carry, base=base):
            hs, acs = carry
            idx = pl.ds(base + i, SUBLANES, stride=seg_stride)
            new_h, new_a = [], []
            for k in range(n_lb):
                av = sa_ref[k, idx, :]
                h = av * hs[k] + sb_ref[k, idx, :]
                sb_ref[k, idx, :] = h
                new_h.append(h)
                if chain:
                    ac = av * acs[k]
                    sa_ref[k, idx, :] = ac
                    new_a.append(ac)
                else:
                    new_a.append(acs[k])
            return tuple(new_h), tuple(new_a)

        carry = (h_init, a_init)
        for i in range(seg_len):
            carry = step(i, carry)
        h_end, a_end = carry

        for k in range(n_lb):
            lc = slice(k * LANES, (k + 1) * LANES)
            mc = slice(D_A + k * LANES, D_A + (k + 1) * LANES)
            if chain:
                h_in = hc_ref[:, lc]
                for j in range(SUBLANES):
                    seg = slice(base + j * seg_stride, base + j * seg_stride + seg_len)
                    rs = slice((grp * SUBLANES + j) * seg_len, (grp * SUBLANES + j + 1) * seg_len)
                    h_seg = sb_ref[k, seg, :] + sa_ref[k, seg, :] * h_in
                    mix_ref[rs, mc] = (h_seg * gel[rs, lc]).astype(BF16)
                    h_in = h_end[k][j:j + 1] + a_end[k][j:j + 1] * h_in
                hc_ref[:, lc] = h_in
                hlast_ref[0, :, lc] = h_in
            else:
                for j in range(SUBLANES):
                    seg = slice(base + j * seg_stride, base + j * seg_stride + seg_len)
                    rs = slice((grp * SUBLANES + j) * seg_len, (grp * SUBLANES + j + 1) * seg_len)
                    mix_ref[rs, mc] = (sb_ref[k, seg, :] * gel[rs, lc]).astype(BF16)
                hlast_ref[0, bs, lc] = h_end[k]

    y1 = y_a + _dot(mix_ref[:, D_A:], wout_ref[D_A:, :])
    y1_ref[...] = y1
    h2 = _rms(y1, n2g_ref[...]).astype(BF16)
    h2_ref[...] = h2

    logits = lax.dot_general(wr_ref[...], h2, (((1,), (1,)), ((), ())), preferred_element_type=F32) + br_ref[...]
    gl = [logits[N_EXPERTS + k:N_EXPERTS + k + 1, :] for k in range(N_GROUPS)]
    gmax = jnp.maximum(jnp.maximum(gl[0], gl[1]), jnp.maximum(gl[2], gl[3]))
    grp_idx = jnp.where(gl[0] == gmax, 0, jnp.where(gl[1] == gmax, 1, jnp.where(gl[2] == gmax, 2, 3)))
    gsum = (jnp.exp(gl[0] - gmax) + jnp.exp(gl[1] - gmax)) + (jnp.exp(gl[2] - gmax) + jnp.exp(gl[3] - gmax))
    p_grp = 1.0 / gsum
    e_in = jnp.where(grp_idx == 0, logits[0:8],
                     jnp.where(grp_idx == 1, logits[8:16], jnp.where(grp_idx == 2, logits[16:24], logits[24:32])))
    iota8 = lax.broadcasted_iota(jnp.int32, (EXPERTS_PER_GROUP, rows), 0)
    m1 = jnp.max(e_in, axis=0, keepdims=True)
    i1 = jnp.min(jnp.where(e_in == m1, iota8, EXPERTS_PER_GROUP), axis=0, keepdims=True)
    e_rest = jnp.where(iota8 == i1, -jnp.inf, e_in)
    m2 = jnp.max(e_rest, axis=0, keepdims=True)
    i2 = jnp.min(jnp.where(e_rest == m2, iota8, EXPERTS_PER_GROUP), axis=0, keepdims=True)
    t2 = jnp.exp(m2 - m1)
    den = 1.0 + t2
    gate0 = p_grp * (1.0 / den)
    gate1 = p_grp * (t2 / den)
    e0 = (grp_idx * EXPERTS_PER_GROUP + i1).astype(F32)
    e1 = (grp_idx * EXPERTS_PER_GROUP + i2).astype(F32)
    route_ref[...] = jnp.where(iota8 == 0, e0, jnp.where(iota8 == 1, e1, jnp.where(
        iota8 == 2, gate0, jnp.where(iota8 == 3, gate1, 0.0))))

    iota_e = lax.broadcasted_iota(jnp.int32, (N_EXPERTS, rows), 0).astype(F32)
    onehot = (iota_e == e0).astype(F32) + (iota_e == e1).astype(F32)
    lane = lax.broadcasted_iota(jnp.int32, (N_EXPERTS, LANES), 1)
    cnt = jnp.zeros((N_EXPERTS, LANES), F32)
    for s in range(rows // TOKEN_TILE):
        c_s = jnp.sum(onehot[:, s * TOKEN_TILE:(s + 1) * TOKEN_TILE], axis=1, keepdims=True)
        cnt = cnt + jnp.where(lane == s, c_s, 0.0)
    cnt_ref[...] = cnt.reshape(1, N_EXPERTS, LANES)
    conv_ref[...] = xs_ref[:, SUBLANES + tt - 3:SUBLANES + tt, :]


def _mixer_call(x, hist, h0, wts, *, n_total, row_offset, sb, tt, chunk, seg_len, seg_stride, chain,
                emit_vn, alias_in=None):
    nb, t_len, _ = x.shape
    nbb = nb // sb
    ntt = t_len // tt
    rows = sb * tt
    blk0 = row_offset // rows
    n_seg = rows // seg_len

    def full(arr):
        nd = arr.ndim
        return pl.BlockSpec(arr.shape, lambda b, t, _nd=nd: (0,) * _nd)

    in_specs = [
        pl.BlockSpec((sb, tt, D_MODEL), lambda b, t: (b, t, 0)),
        pl.BlockSpec((sb, CONV_WIDTH - 1, D_B), lambda b, t: (b, 0, 0)),
        pl.BlockSpec((1, sb, D_B), lambda b, t: (b, 0, 0)),
    ] + [full(w) for w in wts]
    args = [x, hist, h0] + list(wts)
    io_alias = {}
    if alias_in is not None:
        for k, arr in enumerate(alias_in):
            io_alias[len(args)] = k
            in_specs.append(pl.BlockSpec(memory_space=pl.ANY))
            args.append(arr)

    out_shape = [
        jax.ShapeDtypeStruct((n_total, D_MODEL), F32),
        jax.ShapeDtypeStruct((n_total, D_MODEL), BF16),
        jax.ShapeDtypeStruct((SUBLANES, n_total), F32),
        jax.ShapeDtypeStruct((nbb * ntt, N_EXPERTS, LANES), F32),
        jax.ShapeDtypeStruct((nb, CONV_WIDTH - 1, D_B), F32),
        jax.ShapeDtypeStruct((nbb, sb, D_B), F32),
    ]
    out_specs = [
        pl.BlockSpec((rows, D_MODEL), lambda b, t: (blk0 + b * ntt + t, 0)),
        pl.BlockSpec((rows, D_MODEL), lambda b, t: (blk0 + b * ntt + t, 0)),
        pl.BlockSpec((SUBLANES, rows), lambda b, t: (0, blk0 + b * ntt + t)),
        pl.BlockSpec((1, N_EXPERTS, LANES), lambda b, t: (b * ntt + t, 0, 0)),
        pl.BlockSpec((sb, CONV_WIDTH - 1, D_B), lambda b, t: (b, 0, 0)),
        pl.BlockSpec((1, sb, D_B), lambda b, t: (b, 0, 0)),
    ]
    if emit_vn:
        out_shape.append(jax.ShapeDtypeStruct((nb, t_len, D_A), F32))
        out_specs.append(pl.BlockSpec((sb, tt, D_A), lambda b, t: (b, t, 0)))

    kern = functools.partial(_mixer_kernel, sb=sb, tt=tt, chunk=chunk, seg_len=seg_len, seg_stride=seg_stride,
                             chain=chain, emit_vn=emit_vn, aliased=alias_in is not None)
    return pl.pallas_call(
        kern,
        grid=(nbb, ntt),
        in_specs=in_specs,
        out_specs=out_specs,
        out_shape=out_shape,
        scratch_shapes=[
            pltpu.VMEM((sb, SUBLANES + tt, D_B), F32),
            pltpu.VMEM((D_B // LANES, n_seg * seg_stride, LANES), F32),
            pltpu.VMEM((D_B // LANES, n_seg * seg_stride, LANES), F32),
            pltpu.VMEM((rows, D_MODEL), BF16),
            pltpu.VMEM((1, D_B), F32),
        ],
        input_output_aliases=io_alias,
        compiler_params=pltpu.CompilerParams(dimension_semantics=("arbitrary", "arbitrary"),
                                             vmem_limit_bytes=VMEM_LIMIT),
        name="mixer_chain" if chain else "mixer_step",
    )(*args)


def _tile_slots(route_ref, locoff_ref, upper_ref, u):
    lanes = slice(u * TOKEN_TILE, (u + 1) * TOKEN_TILE)
    e0 = route_ref[0:1, lanes]
    e1 = route_ref[1:2, lanes]
    iota_e = lax.broadcasted_iota(jnp.int32, (N_EXPERTS, TOKEN_TILE), 0).astype(F32)
    oh0 = (iota_e == e0).astype(F32)
    oh1 = (iota_e == e1).astype(F32)
    c0 = _dot(oh0.astype(BF16), upper_ref[...])
    c1 = _dot(oh1.astype(BF16), upper_ref[...])
    cnt0 = jnp.sum(oh0, axis=1, keepdims=True)
    base0 = locoff_ref[u]
    base1 = base0 + cnt0
    p0 = jnp.sum(oh0 * (base0 + c0), axis=0, keepdims=True)
    p1 = jnp.sum(oh1 * (base1 + c1), axis=0, keepdims=True)
    return p0.astype(jnp.int32), p1.astype(jnp.int32)


def _by_fill(ntot_ref, step, body):
    most = ntot_ref[step * TILES_PER_STEP]
    for u in range(1, TILES_PER_STEP):
        most = jnp.maximum(most, ntot_ref[step * TILES_PER_STEP + u])
    fits = most * SUBLANES <= SLOT_ROWS_COMMON

    @pl.when(fits)
    def _():
        body(SLOT_ROWS_COMMON)

    @pl.when(jnp.logical_not(fits))
    def _():
        body(SLOT_ROWS)


def _copy_ops(tile, n_tiles, nops_ref, oloc_ref, odst_ref, make_copy):
    base = 0
    for k, (chunks, cap) in enumerate(COPY_CLASSES):
        n = nops_ref[k * n_tiles + tile]

        def per_pair(i, _, chunks=chunks, cap=cap, base=base, n=n):
            idx = base + tile * cap + 2 * i
            make_copy(pl.multiple_of(oloc_ref[idx], SUBLANES), pl.multiple_of(odst_ref[idx], SUBLANES),
                      chunks * SUBLANES, 0)

            @pl.when(2 * i + 1 < n)
            def _():
                make_copy(pl.multiple_of(oloc_ref[idx + 1], SUBLANES), pl.multiple_of(odst_ref[idx + 1], SUBLANES),
                          chunks * SUBLANES, 1)
            return 0

        assert cap % 2 == 0
        lax.fori_loop(0, (n + 1) // 2, per_pair, 0)
        base += n_tiles * cap


def _dispatch_kernel(nops_ref, oloc_ref, odst_ref, ntot_ref, tail_ref, ntail_ref,
                     h2_ref, route_ref, locoff_ref, upper_ref, xb_ref, slots_ref, buf_ref, zero_ref, sem, zsem,
                     *, n_tiles):
    i = pl.program_id(0)
    n_steps = pl.num_programs(0)
    slot = i % 2

    def chunk_copy(s, u, loc, dst, rows=SUBLANES):
        return pltpu.make_async_copy(buf_ref.at[s, u, pl.ds(loc, rows), :],
                                     xb_ref.at[pl.ds(dst, rows), :], sem.at[s])

    def wait_step(step, s):
        for u in range(TILES_PER_STEP):
            n_rows = ntot_ref[step * TILES_PER_STEP + u] * SUBLANES

            @pl.when(n_rows > 0)
            def _():
                pltpu.make_async_copy(buf_ref.at[s, u, pl.ds(0, n_rows), :], xb_ref.at[pl.ds(0, n_rows), :],
                                      sem.at[s]).wait()

    @pl.when(i >= 2)
    def _():
        wait_step(i - 2, slot)

    def sort_tiles(n_rows):
        iota_s = lax.broadcasted_iota(jnp.int32, (n_rows, TOKEN_TILE), 0)
        iota8 = lax.broadcasted_iota(jnp.int32, (SUBLANES, TOKEN_TILE), 0)
        for u in range(TILES_PER_STEP):
            lanes = slice(u * TOKEN_TILE, (u + 1) * TOKEN_TILE)
            p0, p1 = _tile_slots(route_ref, locoff_ref, upper_ref, u)
            slots_ref[:, lanes] = jnp.where(iota8 == 0, p0, jnp.where(iota8 == 1, p1, 0))
            hit0 = iota_s == p0
            hit1 = iota_s == p1
            perm = jnp.where(hit0 | hit1, 1.0, 0.0).astype(BF16)
            sorted_rows = _dot(perm, h2_ref[u * TOKEN_TILE:(u + 1) * TOKEN_TILE, :])
            buf_ref[slot, u, 0:n_rows, 0:PACK_COLS] = _pack_bf16_pairs(sorted_rows)
            gates = jnp.where(hit0, route_ref[2:3, lanes], 0.0) + jnp.where(hit1, route_ref[3:4, lanes], 0.0)
            gcol = jnp.sum(gates, axis=1, keepdims=True)
            buf_ref[slot, u, 0:n_rows, PACK_COLS:XB_COLS] = lax.bitcast_convert_type(
                jnp.broadcast_to(gcol, (n_rows, LANES)), U32)

    _by_fill(ntot_ref, i, sort_tiles)

    for u in range(TILES_PER_STEP):
        _copy_ops(i * TILES_PER_STEP + u, n_tiles, nops_ref, oloc_ref, odst_ref,
                  lambda loc, dst, rows, prio, u=u: chunk_copy(slot, u, loc, dst, rows).start(priority=prio))

    @pl.when(i == n_steps - 1)
    def _():
        zero_ref[...] = jnp.zeros_like(zero_ref)

        def zero_copy(dst, rows, k):
            return pltpu.make_async_copy(zero_ref.at[pl.ds(0, rows), :], xb_ref.at[pl.ds(dst, rows), :], zsem.at[k])

        def per_expert(e, tot):
            n_big, n_small = tot
            n = ntail_ref[e]
            d0 = tail_ref[e]
            nb = n // ZERO_CHUNKS
            ns = n - nb * ZERO_CHUNKS

            def big(c, _):
                zero_copy(pl.multiple_of(d0 + c * ZERO_ROWS, SUBLANES), ZERO_ROWS, 0).start()
                return 0

            def small(c, _):
                zero_copy(pl.multiple_of(d0 + nb * ZERO_ROWS + c * SUBLANES, SUBLANES), SUBLANES, 1).start()
                return 0

            lax.fori_loop(0, nb, big, 0)
            lax.fori_loop(0, ns, small, 0)
            return n_big + nb, n_small + ns

        n_big, n_small = lax.fori_loop(0, N_EXPERTS, per_expert, (0, 0))

        @pl.when(i >= 1)
        def _():
            wait_step(i - 1, 1 - slot)

        wait_step(i, slot)

        def wait_big(c, _):
            zero_copy(0, ZERO_ROWS, 0).wait()
            return 0

        def wait_small(c, _):
            zero_copy(0, SUBLANES, 1).wait()
            return 0

        lax.fori_loop(0, n_big, wait_big, 0)
        lax.fori_loop(0, n_small, wait_small, 0)


def _dispatch_call(h2, route, locoff, upper, ops, ntot, tail, ntail, *, n_tiles, p_rows):
    grid_spec = pltpu.PrefetchScalarGridSpec(
        num_scalar_prefetch=6,
        grid=(n_tiles // TILES_PER_STEP,),
        in_specs=[
            pl.BlockSpec((STEP_TOKENS, D_MODEL), lambda i, *_: (i, 0)),
            pl.BlockSpec((SUBLANES, STEP_TOKENS), lambda i, *_: (0, i)),
            pl.BlockSpec((TILES_PER_STEP, N_EXPERTS, 1), lambda i, *_: (i, 0, 0)),
            pl.BlockSpec((TOKEN_TILE, TOKEN_TILE), lambda i, *_: (0, 0)),
        ],
        out_specs=[pl.BlockSpec(memory_space=pl.ANY),
                   pl.BlockSpec((SUBLANES, STEP_TOKENS), lambda i, *_: (0, i))],
        scratch_shapes=[
            pltpu.VMEM((2, TILES_PER_STEP, SLOT_ROWS, XB_COLS), U32),
            pltpu.VMEM((ZERO_ROWS, XB_COLS), U32),
            pltpu.SemaphoreType.DMA((2,)),
            pltpu.SemaphoreType.DMA((2,)),
        ],
    )
    return pl.pallas_call(
        functools.partial(_dispatch_kernel, n_tiles=n_tiles),
        grid_spec=grid_spec,
        out_shape=[jax.ShapeDtypeStruct((p_rows, XB_COLS), U32),
                   jax.ShapeDtypeStruct((SUBLANES, n_tiles * TOKEN_TILE), jnp.int32)],
        compiler_params=pltpu.CompilerParams(dimension_semantics=("arbitrary",), vmem_limit_bytes=VMEM_LIMIT),
        name="dispatch",
    )(*ops, ntot, tail, ntail, h2, route, locoff, upper)


def _ffn_kernel(nitems_ref, istart_ref, iunits_ref, iexp_ref, ifirst_ref, iwslot_ref, inext_ref,
                xb_ref, wg_ref, wu_ref, wd_ref, yb_ref,
                xin_ref, yout_ref, wgf_ref, wuf_ref, wdf_ref, wgb_ref, wub_ref, wdb_ref, sem_in, sem_out, sem_w):
    n_items = nitems_ref[0]
    sizes = tuple(k * FFN_UNIT for k in range(FFN_ITEM_UNITS, 0, -1))

    def rows_in(t, s, rows):
        start = pl.multiple_of(istart_ref[t], FFN_UNIT)
        return pltpu.make_async_copy(xb_ref.at[pl.ds(start, rows), :], xin_ref.at[s, pl.ds(0, rows), :], sem_in.at[s])

    def rows_out(t, s, rows):
        start = pl.multiple_of(istart_ref[t], FFN_UNIT)
        return pltpu.make_async_copy(yout_ref.at[s, pl.ds(0, rows), :], yb_ref.at[pl.ds(start, rows), :],
                                     sem_out.at[s])

    def start_by_size(make, t, s):
        for rows in sizes:
            @pl.when(iunits_ref[t] * FFN_UNIT == rows)
            def _():
                make(t, s, rows).start()

    def weight_copies(e, ws):
        return (pltpu.make_async_copy(wg_ref.at[e], wgf_ref.at[ws], sem_w.at[ws]),
                pltpu.make_async_copy(wu_ref.at[e], wuf_ref.at[ws], sem_w.at[ws]),
                pltpu.make_async_copy(wd_ref.at[e], wdf_ref.at[ws], sem_w.at[ws]))

    def compute(s, rows):
        xb = _unpack_bf16_pairs(xin_ref[s, 0:rows, 0:PACK_COLS])
        gate = lax.bitcast_convert_type(xin_ref[s, 0:rows, PACK_COLS:PACK_COLS + 1], F32)
        a = _dot(xb, wgb_ref[...])
        u = _dot(xb, wub_ref[...])
        mid = (a * _sigmoid(a) * u).astype(BF16)
        y = _dot(mid, wdb_ref[...]) * gate
        yout_ref[s, 0:rows, :] = _pack_bf16_pairs(y.astype(BF16).astype(F32))

    @pl.when(n_items > 0)
    def _():
        for c in weight_copies(iexp_ref[0], 0):
            c.start()
        start_by_size(rows_in, 0, 0)

    def per_item(t, _):
        s = t % 2

        @pl.when(t + 1 < n_items)
        def _():
            start_by_size(rows_in, t + 1, 1 - s)

        @pl.when(ifirst_ref[t] == 1)
        def _():
            ws = iwslot_ref[t]
            for c in weight_copies(0, ws):
                c.wait()

            @pl.when(inext_ref[t] >= 0)
            def _():
                for c in weight_copies(inext_ref[t], 1 - ws):
                    c.start()

            wgb_ref[...] = wgf_ref[ws].astype(BF16)
            wub_ref[...] = wuf_ref[ws].astype(BF16)
            wdb_ref[...] = wdf_ref[ws].astype(BF16)

        rows_in(t, s, iunits_ref[t] * FFN_UNIT).wait()

        @pl.when(t >= 2)
        def _():
            rows_out(t - 2, s, iunits_ref[t - 2] * FFN_UNIT).wait()

        for rows in sizes:
            @pl.when(iunits_ref[t] * FFN_UNIT == rows)
            def _():
                compute(s, rows)

        start_by_size(rows_out, t, s)
        return 0

    lax.fori_loop(0, n_items, per_item, 0)

    for back in (2, 1):
        @pl.when(n_items >= back)
        def _():
            t = n_items - back
            rows_out(t, t % 2, iunits_ref[t] * FFN_UNIT).wait()


def _ffn_call(xb, wg, wu, wd, items, *, p_rows):
    any_spec = pl.BlockSpec(memory_space=pl.ANY)
    grid_spec = pltpu.PrefetchScalarGridSpec(
        num_scalar_prefetch=len(items),
        grid=(1,),
        in_specs=[any_spec, any_spec, any_spec, any_spec],
        out_specs=any_spec,
        scratch_shapes=[
            pltpu.VMEM((2, FFN_BLOCK, XB_COLS), U32),
            pltpu.VMEM((2, FFN_BLOCK, PACK_COLS), U32),
            pltpu.VMEM((2, D_MODEL, D_EXPERT), F32),
            pltpu.VMEM((2, D_MODEL, D_EXPERT), F32),
            pltpu.VMEM((2, D_EXPERT, D_MODEL), F32),
            pltpu.VMEM((D_MODEL, D_EXPERT), BF16),
            pltpu.VMEM((D_MODEL, D_EXPERT), BF16),
            pltpu.VMEM((D_EXPERT, D_MODEL), BF16),
            pltpu.SemaphoreType.DMA((2,)),
            pltpu.SemaphoreType.DMA((2,)),
            pltpu.SemaphoreType.DMA((2,)),
        ],
    )
    return pl.pallas_call(
        _ffn_kernel,
        grid_spec=grid_spec,
        out_shape=jax.ShapeDtypeStruct((p_rows, PACK_COLS), U32),
        compiler_params=pltpu.CompilerParams(dimension_semantics=("arbitrary",), vmem_limit_bytes=VMEM_LIMIT),
        name="expert_ffn",
    )(*items, xb, wg, wu, wd)


def _combine_kernel(nops_ref, oloc_ref, odst_ref, ntot_ref, y1_ref, slots_ref, fg_ref,
                    yb_ref, yp_ref, ys_ref, buf_ref, sem, *, n_tiles):
    i = pl.program_id(0)
    n_steps = pl.num_programs(0)
    slot = i % 2

    def chunk_copy(s, u, loc, src, rows):
        return pltpu.make_async_copy(yb_ref.at[pl.ds(src, rows), :],
                                     buf_ref.at[s, u, pl.ds(loc, rows), :], sem.at[s])

    def issue_step(step, s):
        for u in range(TILES_PER_STEP):
            _copy_ops(step * TILES_PER_STEP + u, n_tiles, nops_ref, oloc_ref, odst_ref,
                      lambda loc, src, rows, prio, u=u: chunk_copy(s, u, loc, src, rows).start(priority=prio))

    @pl.when(i == 0)
    def _():
        buf_ref[...] = jnp.zeros_like(buf_ref)
        issue_step(0, 0)

    @pl.when(i + 1 < n_steps)
    def _():
        issue_step(i + 1, 1 - slot)

    def unsort_tiles(n_slots):
        iota_s = lax.broadcasted_iota(jnp.int32, (n_slots, TOKEN_TILE), 0)
        perms = []
        for u in range(TILES_PER_STEP):
            lanes = slice(u * TOKEN_TILE, (u + 1) * TOKEN_TILE)
            p0 = slots_ref[0:1, lanes]
            p1 = slots_ref[1:2, lanes]
            perms.append(jnp.where((iota_s == p0) | (iota_s == p1), 1.0, 0.0).astype(BF16))

        for u in range(TILES_PER_STEP):
            n_rows = ntot_ref[i * TILES_PER_STEP + u] * SUBLANES

            @pl.when(n_rows > 0)
            def _():
                pltpu.make_async_copy(yb_ref.at[pl.ds(0, n_rows), :], buf_ref.at[slot, u, pl.ds(0, n_rows), :],
                                      sem.at[slot]).wait()

        for u in range(TILES_PER_STEP):
            rs = slice(u * TOKEN_TILE, (u + 1) * TOKEN_TILE)
            yb = _unpack_bf16_pairs(buf_ref[slot, u, 0:n_slots, :])
            moe = lax.dot_general(perms[u], yb, (((0,), (0,)), ((), ())), preferred_element_type=F32)
            y_tile = _rms(y1_ref[rs, :] + moe, fg_ref[...])
            yp_ref[rs, :] = y_tile

        @pl.when(i == n_steps - 1)
        def _():
            ys_ref[...] = y_tile

    _by_fill(ntot_ref, i, unsort_tiles)


def _combine_call(y1, slots, fg, yb, ops, ntot, *, n_tiles, n_prompt, n_sample):
    n_steps = n_tiles // TILES_PER_STEP
    assert n_sample == TOKEN_TILE and n_prompt + n_sample == n_steps * STEP_TOKENS
    grid_spec = pltpu.PrefetchScalarGridSpec(
        num_scalar_prefetch=4,
        grid=(n_steps,),
        in_specs=[
            pl.BlockSpec((STEP_TOKENS, D_MODEL), lambda i, *_: (i, 0)),
            pl.BlockSpec((SUBLANES, STEP_TOKENS), lambda i, *_: (0, i)),
            pl.BlockSpec((1, D_MODEL), lambda i, *_: (0, 0)),
            pl.BlockSpec(memory_space=pl.ANY),
        ],
        out_specs=[
            pl.BlockSpec((STEP_TOKENS, D_MODEL), lambda i, *_: (i, 0)),
            pl.BlockSpec((n_sample, D_MODEL), lambda i, *_: (0, 0)),
        ],
        scratch_shapes=[
            pltpu.VMEM((2, TILES_PER_STEP, SLOT_ROWS, PACK_COLS), U32),
            pltpu.SemaphoreType.DMA((2,)),
        ],
    )
    return pl.pallas_call(
        functools.partial(_combine_kernel, n_tiles=n_tiles),
        grid_spec=grid_spec,
        out_shape=[jax.ShapeDtypeStruct((n_prompt, D_MODEL), F32), jax.ShapeDtypeStruct((n_sample, D_MODEL), F32)],
        compiler_params=pltpu.CompilerParams(dimension_semantics=("arbitrary",), vmem_limit_bytes=VMEM_LIMIT),
        name="combine",
    )(*ops, ntot, y1, slots, fg, yb)


def _block_diag(w):
    h, d, _ = w.shape
    eye = jnp.eye(h, dtype=w.dtype)
    return (eye[:, None, :, None] * w[:, :, None, :]).reshape(h * d, h * d)


def _head_blocks(w):
    half = B_HEADS // 2
    return jnp.stack([_block_diag(w[:half]), _block_diag(w[half:])]).astype(BF16)


def _mixer_weights(l, length, reps, norm1_g, w_in, gmlp_ln_g, gmlp_ln_b, gmlp_w_s, gmlp_b_s, conv_w, conv_b,
                   lru_w_a, lru_b_a, lru_w_x, lru_b_x, lru_lambda, w_out, norm2_g,
                   router_group_w, router_group_b, router_expert_w, router_expert_b):
    mask = jnp.tril(jnp.ones((length, length), dtype=bool))
    ws = jnp.where(mask, gmlp_w_s[l][:, :length, :length], 0.0)
    wsm = jnp.concatenate([jnp.pad(ws, ((0, 0), (0, 0), (j * length, (reps - 1 - j) * length)))
                           for j in range(reps)], axis=1)
    bs = jnp.tile(gmlp_b_s[l][:, :length], (1, reps))
    bsx = jnp.repeat(bs.T, A_GROUP_DIM, axis=1)
    wr = jnp.concatenate([router_expert_w[l].T, router_group_w[l].T,
                          jnp.zeros((ROUTER_ROWS - N_EXPERTS - N_GROUPS, D_MODEL), F32)], axis=0)
    br = jnp.concatenate([router_expert_b[l], router_group_b[l],
                          jnp.zeros((ROUTER_ROWS - N_EXPERTS - N_GROUPS,), F32)]).reshape(ROUTER_ROWS, 1)
    return [
        norm1_g[l].reshape(1, D_MODEL), w_in[l].astype(BF16),
        gmlp_ln_g[l].reshape(1, D_A), gmlp_ln_b[l].reshape(1, D_A), wsm.astype(BF16), bsx,
        conv_w[l], conv_b[l].reshape(1, D_B),
        _head_blocks(lru_w_a[l]), lru_b_a[l].reshape(1, D_B),
        _head_blocks(lru_w_x[l]), lru_b_x[l].reshape(1, D_B),
        lru_lambda[l].reshape(1, D_B), w_out[l].astype(BF16), norm2_g[l].reshape(1, D_MODEL),
        wr.astype(BF16), br,
    ]


def _routing_tables(cnt, n_items_max):
    n_tiles = cnt.shape[0]
    seg = (cnt + SUBLANES - 1) // SUBLANES * SUBLANES
    nch = seg // SUBLANES
    tot = jnp.sum(seg, axis=0)
    padded = (tot + FFN_UNIT - 1) // FFN_UNIT * FFN_UNIT
    e_before = jnp.arange(N_EXPERTS)[None, :] < jnp.arange(N_EXPERTS)[:, None]
    t_before = jnp.arange(n_tiles)[None, :] < jnp.arange(n_tiles)[:, None]
    pad_start = jnp.sum(jnp.where(e_before, padded[None, :], 0), axis=1)
    doff = pad_start[None, :] + jnp.sum(jnp.where(t_before[:, :, None], seg[None, :, :], 0), axis=1)
    locoff = jnp.sum(jnp.where(e_before[None, :, :], seg[:, None, :], 0), axis=2)
    ntot = jnp.sum(nch, axis=1)
    big = COPY_CLASSES[0][0]
    sizes = jnp.array([chunks for chunks, _ in COPY_CLASSES], jnp.int32)
    whole = nch // big
    done = whole * big * SUBLANES
    rest = nch - whole * big
    is_big = (sizes == big)[:, None, None]
    m = jnp.where(is_big, whole[None], (rest[None] == sizes[:, None, None]).astype(jnp.int32))
    loc0 = jnp.where(is_big, locoff[None], (locoff + done)[None])
    dst0 = jnp.where(is_big, doff[None], (doff + done)[None])
    start = jnp.sum(jnp.where(e_before[None, None], m[:, :, None, :], 0), axis=3)
    s = jnp.arange(COPY_SLOTS)
    owner = jnp.sum((start + m)[:, :, None, :] <= s[None, None, :, None], axis=3)
    own = owner[..., None] == jnp.arange(N_EXPERTS)
    step = (sizes * SUBLANES)[:, None, None, None] * (s[None, None, :, None] - start[:, :, None, :])
    olocs = jnp.sum(jnp.where(own, loc0[:, :, None, :] + step, 0), axis=3)
    odsts = jnp.sum(jnp.where(own, dst0[:, :, None, :] + step, 0), axis=3)
    nops = jnp.sum(m, axis=2)
    tail = pad_start + tot
    ntail = (padded - tot) // SUBLANES
    units = padded // FFN_UNIT
    n_it = (units + FFN_ITEM_UNITS - 1) // FFN_ITEM_UNITS
    it_start = jnp.sum(jnp.where(e_before, n_it[None, :], 0), axis=1)
    t = jnp.arange(n_items_max)
    e_ids = jnp.arange(N_EXPERTS)
    owner = jnp.minimum(jnp.sum((it_start + n_it)[None, :] <= t[:, None], axis=1), N_EXPERTS - 1)
    own = owner[:, None] == e_ids[None, :]
    pick = lambda v: jnp.sum(jnp.where(own, v[None, :], 0), axis=1)
    j = t - pick(it_start)
    istart = pick(pad_start) + j * FFN_BLOCK
    iunits = jnp.clip(pick(units) - FFN_ITEM_UNITS * j, 1, FFN_ITEM_UNITS)
    ordinal = jnp.sum(jnp.where(e_before, (n_it > 0).astype(jnp.int32)[None, :], 0), axis=1)
    later = (e_ids[None, :] > e_ids[:, None]) & (n_it > 0)[None, :]
    nxt = jnp.min(jnp.where(later, e_ids[None, :], N_EXPERTS), axis=1)
    nxt = jnp.where(nxt == N_EXPERTS, -1, nxt)
    i32 = lambda a: a.astype(jnp.int32)
    items = (i32(jnp.sum(n_it).reshape(1)), i32(istart), i32(iunits), i32(owner), i32(j == 0),
             i32(pick(ordinal) % 2), i32(pick(nxt)))
    ops = (i32(nops.reshape(-1)), i32(olocs.reshape(-1)), i32(odsts.reshape(-1)))
    return ops, i32(ntot), i32(tail), i32(ntail), locoff.astype(F32)[:, :, None], items


def kernel(x_prompt, x_sample, state_conv, state_rglru, norm1_g, w_in, gmlp_ln_g, gmlp_ln_b, gmlp_w_s, gmlp_b_s,
           conv_w, conv_b, lru_w_a, lru_b_a, lru_w_x, lru_b_x, lru_lambda, w_out, norm2_g,
           router_group_w, router_group_b, router_expert_w, router_expert_b,
           expert_w_gate, expert_w_up, expert_w_down, final_norm_g):
    depth = w_in.shape[0]
    nb, t_len, _ = x_prompt.shape
    db, dt, _ = x_sample.shape
    n_prompt = nb * t_len
    n_sample = db * dt
    assert t_len % PROMPT_TILE == 0 and n_prompt % TOKEN_TILE == 0 and n_sample == TOKEN_TILE
    n_total = n_prompt + n_sample
    assert n_total % STEP_TOKENS == 0
    n_tiles = n_total // TOKEN_TILE
    p_tiles = n_prompt // TOKEN_TILE
    p_rows = 2 * n_total + (SUBLANES - 1) * N_EXPERTS * n_tiles + N_EXPERTS * (FFN_UNIT - 1)
    p_rows = -(-p_rows // FFN_UNIT) * FFN_UNIT
    n_items_max = p_rows // FFN_BLOCK + N_EXPERTS
    upper = jnp.triu(jnp.ones((TOKEN_TILE, TOKEN_TILE), BF16), 1)

    xp, xs = x_prompt, x_sample
    conv_p, h_p, v_s, conv_s, h_s = [], [], [], [], []
    for l in range(depth):
        lw = (norm1_g, w_in, gmlp_ln_g, gmlp_ln_b, gmlp_w_s, gmlp_b_s, conv_w, conv_b, lru_w_a, lru_b_a,
              lru_w_x, lru_b_x, lru_lambda, w_out, norm2_g, router_group_w, router_group_b,
              router_expert_w, router_expert_b)
        wts_p = _mixer_weights(l, GMLP_CHUNK, 1, *lw)
        wts_s = _mixer_weights(l, dt, db, *lw)
        zero_conv = jnp.zeros((nb, CONV_WIDTH - 1, D_B), F32)
        zero_h = jnp.zeros((nb, 1, D_B), F32)
        y1, h2, route, cnt_p, cp, hp = _mixer_call(
            xp, zero_conv, zero_h, wts_p, n_total=n_total, row_offset=0, sb=1, tt=PROMPT_TILE, chunk=GMLP_CHUNK, seg_len=PROMPT_SEG, seg_stride=PROMPT_SEG + SUBLANES, chain=True, emit_vn=False)
        y1, h2, route, cnt_s, cs, hs, vs = _mixer_call(
            xs, state_conv[l], state_rglru[l][None], wts_s, n_total=n_total, row_offset=n_prompt, sb=db,
            tt=dt, chunk=db * dt, seg_len=dt, seg_stride=dt + SUBLANES, chain=False, emit_vn=True,
            alias_in=(y1, h2, route))
        conv_p.append(cp)
        h_p.append(hp[:, 0])
        v_s.append(vs)
        conv_s.append(cs)
        h_s.append(hs[0])

        sub = PROMPT_TILE // TOKEN_TILE
        cnt = jnp.concatenate([
            jnp.swapaxes(cnt_p[:, :, :sub], 1, 2).reshape(p_tiles, N_EXPERTS),
            cnt_s[:, :, 0]], axis=0).astype(jnp.int32)
        ops, ntot, tail, ntail, locoff, items = _routing_tables(cnt, n_items_max)

        xb, slots = _dispatch_call(h2, route, locoff, upper, ops, ntot, tail, ntail, n_tiles=n_tiles,
                                   p_rows=p_rows)
        yb = _ffn_call(xb, expert_w_gate[l], expert_w_up[l], expert_w_down[l], items, p_rows=p_rows)
        assert l == depth - 1, "deeper stacks need an un-normalised combine between layers"
        fg = final_norm_g.reshape(1, D_MODEL)
        yp, ysm = _combine_call(y1, slots, fg, yb, ops, ntot, n_tiles=n_tiles, n_prompt=n_prompt,
                                n_sample=n_sample)
        xp = yp.reshape(nb, t_len, D_MODEL)
        xs = ysm.reshape(db, dt, D_MODEL)

    return (xp, xs, jnp.stack(conv_p), jnp.stack(h_p), jnp.stack(v_s), jnp.stack(conv_s), jnp.stack(h_s))
```

```python
import functools

import jax
import jax.numpy as jnp
from jax import lax
from jax.experimental import pallas as pl
from jax.experimental.pallas import tpu as pltpu

D_MODEL = 1024
D_A = 512
D_B = 512
A_GROUPS = 4
A_GROUP_DIM = 128
GMLP_CHUNK = 128
B_HEADS = 8
CONV_WIDTH = 4
LRU_C = 8.0
N_GROUPS = 4
EXPERTS_PER_GROUP = 8
N_EXPERTS = 32
D_EXPERT = 512
EPS = 1e-6
TINY = 1e-30

SUBLANES = 8
LANES = 128
ROUTER_ROWS = 40
TOKEN_TILE = 256
PROMPT_TILE = 1024
PROMPT_SEG = PROMPT_TILE // SUBLANES
FFN_UNIT = 256
FFN_ITEM_UNITS = 4
FFN_BLOCK = FFN_ITEM_UNITS * FFN_UNIT
ZERO_CHUNKS = 8
ZERO_ROWS = ZERO_CHUNKS * SUBLANES
SLOT_ROWS = 2 * TOKEN_TILE + N_EXPERTS * SUBLANES
SLOT_ROWS_COMMON = 2 * TOKEN_TILE + 9 * 16
TILE_CHUNKS = SLOT_ROWS // SUBLANES
COPY_SLOTS = N_EXPERTS
COPY_CLASSES = ((4, COPY_SLOTS), (3, COPY_SLOTS), (2, COPY_SLOTS), (1, COPY_SLOTS))
assert TILE_CHUNKS // 4 <= COPY_SLOTS
TILES_PER_STEP = 3
STEP_TOKENS = TILES_PER_STEP * TOKEN_TILE
PACK_COLS = D_MODEL // 2
XB_COLS = PACK_COLS + LANES
U32 = jnp.uint32
HI_MASK = 0xFFFF0000
VMEM_BYTES_V7X = 64 * 1024 * 1024
VMEM_LIMIT = VMEM_BYTES_V7X - 8 * 1024 * 1024

BF16 = jnp.bfloat16
F32 = jnp.float32


def _rms(x, g):
    return x * lax.rsqrt(jnp.mean(x * x, axis=-1, keepdims=True) + EPS) * g


def _dot(a, b):
    return jnp.dot(a, b, preferred_element_type=F32)


def _sigmoid(x):
    return 1.0 / (1.0 + jnp.exp(-x))


def _pack_bf16_pairs(x):
    bits = lax.bitcast_convert_type(x, U32)
    return (bits[:, PACK_COLS:] & U32(HI_MASK)) | (bits[:, :PACK_COLS] >> 16)


def _unpack_bf16_pairs(w):
    lo = lax.bitcast_convert_type(w << 16, F32).astype(BF16)
    hi = lax.bitcast_convert_type(w & U32(HI_MASK), F32).astype(BF16)
    return jnp.concatenate([lo, hi], axis=1)


def _mixer_kernel(x_ref, hist_ref, h0_ref, n1g_ref, win_ref, lng_ref, lnb_ref, wsm_ref, bsx_ref,
                  cw_ref, cb_ref, wa_ref, ba_ref, wx_ref, bx_ref, lam_ref, wout_ref, n2g_ref,
                  wr_ref, br_ref, *rest, sb, tt, chunk, seg_len, seg_stride, chain, emit_vn, aliased):
    if aliased:
        rest = rest[3:]
    if emit_vn:
        (y1_ref, h2_ref, route_ref, cnt_ref, conv_ref, hlast_ref, vn_ref,
         xs_ref, sa_ref, sb_ref, mix_ref, hc_ref) = rest
    else:
        (y1_ref, h2_ref, route_ref, cnt_ref, conv_ref, hlast_ref,
         xs_ref, sa_ref, sb_ref, mix_ref, hc_ref) = rest
        vn_ref = None
    t_idx = pl.program_id(1)

    @pl.when(t_idx == 0)
    def _():
        xs_ref[:, SUBLANES - 3:SUBLANES, :] = hist_ref[...]
        hc_ref[...] = jnp.zeros_like(hc_ref)

    @pl.when(t_idx != 0)
    def _():
        xs_ref[:, SUBLANES - 3:SUBLANES, :] = xs_ref[:, SUBLANES + tt - 3:SUBLANES + tt, :]

    rows = sb * tt
    n_seg = rows // seg_len
    half = D_B // 2

    x = x_ref[...].reshape(rows, D_MODEL)
    hb = _rms(x, n1g_ref[...]).astype(BF16)

    v_a = _dot(hb, win_ref[:, D_A:2 * D_A])
    x_b = _dot(hb, win_ref[:, 2 * D_A:2 * D_A + D_B])
    mixed = []
    for g in range(A_GROUPS):
        cols = slice(g * A_GROUP_DIM, (g + 1) * A_GROUP_DIM)
        vg = v_a[:, cols]
        mu = jnp.mean(vg, axis=-1, keepdims=True)
        dv = vg - mu
        var = jnp.mean(dv * dv, axis=-1, keepdims=True)
        vn = dv * lax.rsqrt(var + EPS) * lng_ref[:, cols] + lnb_ref[:, cols]
        if vn_ref is not None:
            vn_ref[:, :, cols] = vn.reshape(sb, tt, A_GROUP_DIM)
        vnb = vn.astype(BF16)
        mixed.append([_dot(wsm_ref[g], vnb[c * chunk:(c + 1) * chunk]) for c in range(rows // chunk)])
    u_a = _dot(hb, win_ref[:, 0:D_A])

    xs_ref[:, SUBLANES:SUBLANES + tt, :] = x_b.reshape(sb, tt, D_B)
    xc = cb_ref[...] + x_b * cw_ref[3:4, :]
    for j in range(1, CONV_WIDTH):
        shifted = xs_ref[:, SUBLANES - j:SUBLANES - j + tt, :].reshape(rows, D_B)
        xc = xc + shifted * cw_ref[3 - j:4 - j, :]

    xcb = xc.astype(BF16)
    r_lin = jnp.concatenate([_dot(xcb[:, :half], wa_ref[0]), _dot(xcb[:, half:], wa_ref[1])], axis=1)
    i_lin = jnp.concatenate([_dot(xcb[:, :half], wx_ref[0]), _dot(xcb[:, half:], wx_ref[1])], axis=1)

    for g in range(A_GROUPS):
        cols = slice(g * A_GROUP_DIM, (g + 1) * A_GROUP_DIM)
        for c in range(rows // chunk):
            rs = slice(c * chunk, (c + 1) * chunk)
            s = mixed[g][c] + bsx_ref[:, cols]
            mix_ref[rs, cols] = (u_a[rs, cols] * s).astype(BF16)

    g_b = _dot(hb, win_ref[:, 2 * D_A + D_B:])
    r = _sigmoid(r_lin + ba_ref[...])
    i_g = _sigmoid(i_lin + bx_ref[...])
    log_a = (-LRU_C * r) * jax.nn.softplus(-lam_ref[...])
    a = jnp.exp(log_a)
    y_gain = 1.0 - a * a
    gain = y_gain * lax.rsqrt(jnp.maximum(y_gain, TINY))
    bterm = gain * i_g * xc

    n_lb = D_B // LANES
    for j in range(n_seg):
        for k in range(n_lb):
            lc = slice(k * LANES, (k + 1) * LANES)
            dst = slice(j * seg_stride, j * seg_stride + seg_len)
            sa_ref[k, dst, :] = a[j * seg_len:(j + 1) * seg_len, lc]
            sb_ref[k, dst, :] = bterm[j * seg_len:(j + 1) * seg_len, lc]

    y_a = x + _dot(mix_ref[:, 0:D_A], wout_ref[0:D_A, :])
    gel = jax.nn.gelu(g_b)

    for grp in range(n_seg // SUBLANES):
        base = grp * SUBLANES * seg_stride
        bs = slice(grp * SUBLANES, (grp + 1) * SUBLANES)
        if chain:
            h_init = tuple(jnp.zeros((SUBLANES, LANES), F32) for _ in range(n_lb))
        else:
            h_init = tuple(h0_ref[0, bs, k * LANES:(k + 1) * LANES] for k in range(n_lb))
        a_init = tuple(jnp.ones((SUBLANES, LANES), F32) for _ in range(n_lb))

        def step(i, carry, base=base):
            hs, acs = carry
            idx = pl.ds(base + i, SUBLANES, stride=seg_stride)
            new_h, new_a = [], []
            for k in range(n_lb):
                av = sa_ref[k, idx, :]
                h = av * hs[k] + sb_ref[k, idx, :]
                sb_ref[k, idx, :] = h
                new_h.append(h)
                if chain:
                    ac = av * acs[k]
                    sa_ref[k, idx, :] = ac
                    new_a.append(ac)
                else:
                    new_a.append(acs[k])
            return tuple(new_h), tuple(new_a)

        carry = (h_init, a_init)
        for i in range(seg_len):
            carry = step(i, carry)
        h_end, a_end = carry

        for k in range(n_lb):
            lc = slice(k * LANES, (k + 1) * LANES)
            mc = slice(D_A + k * LANES, D_A + (k + 1) * LANES)
            if chain:
                h_in = hc_ref[:, lc]
                for j in range(SUBLANES):
                    seg = slice(base + j * seg_stride, base + j * seg_stride + seg_len)
                    rs = slice((grp * SUBLANES + j) * seg_len, (grp * SUBLANES + j + 1) * seg_len)
                    h_seg = sb_ref[k, seg, :] + sa_ref[k, seg, :] * h_in
                    mix_ref[rs, mc] = (h_seg * gel[rs, lc]).astype(BF16)
                    h_in = h_end[k][j:j + 1] + a_end[k][j:j + 1] * h_in
                hc_ref[:, lc] = h_in
                hlast_ref[0, :, lc] = h_in
            else:
                for j in range(SUBLANES):
                    seg = slice(base + j * seg_stride, base + j * seg_stride + seg_len)
                    rs = slice((grp * SUBLANES + j) * seg_len, (grp * SUBLANES + j + 1) * seg_len)
                    mix_ref[rs, mc] = (sb_ref[k, seg, :] * gel[rs, lc]).astype(BF16)
                hlast_ref[0, bs, lc] = h_end[k]

    y1 = y_a + _dot(mix_ref[:, D_A:], wout_ref[D_A:, :])
    y1_ref[...] = y1
    h2 = _rms(y1, n2g_ref[...]).astype(BF16)
    h2_ref[...] = h2

    logits = lax.dot_general(wr_ref[...], h2, (((1,), (1,)), ((), ())), preferred_element_type=F32) + br_ref[...]
    gl = [logits[N_EXPERTS + k:N_EXPERTS + k + 1, :] for k in range(N_GROUPS)]
    gmax = jnp.maximum(jnp.maximum(gl[0], gl[1]), jnp.maximum(gl[2], gl[3]))
    grp_idx = jnp.where(gl[0] == gmax, 0, jnp.where(gl[1] == gmax, 1, jnp.where(gl[2] == gmax, 2, 3)))
    gsum = (jnp.exp(gl[0] - gmax) + jnp.exp(gl[1] - gmax)) + (jnp.exp(gl[2] - gmax) + jnp.exp(gl[3] - gmax))
    p_grp = 1.0 / gsum
    e_in = jnp.where(grp_idx == 0, logits[0:8],
                     jnp.where(grp_idx == 1, logits[8:16], jnp.where(grp_idx == 2, logits[16:24], logits[24:32])))
    iota8 = lax.broadcasted_iota(jnp.int32, (EXPERTS_PER_GROUP, rows), 0)
    m1 = jnp.max(e_in, axis=0, keepdims=True)
    i1 = jnp.min(jnp.where(e_in == m1, iota8, EXPERTS_PER_GROUP), axis=0, keepdims=True)
    e_rest = jnp.where(iota8 == i1, -jnp.inf, e_in)
    m2 = jnp.max(e_rest, axis=0, keepdims=True)
    i2 = jnp.min(jnp.where(e_rest == m2, iota8, EXPERTS_PER_GROUP), axis=0, keepdims=True)
    t2 = jnp.exp(m2 - m1)
    den = 1.0 + t2
    gate0 = p_grp * (1.0 / den)
    gate1 = p_grp * (t2 / den)
    e0 = (grp_idx * EXPERTS_PER_GROUP + i1).astype(F32)
    e1 = (grp_idx * EXPERTS_PER_GROUP + i2).astype(F32)
    route_ref[...] = jnp.where(iota8 == 0, e0, jnp.where(iota8 == 1, e1, jnp.where(
        iota8 == 2, gate0, jnp.where(iota8 == 3, gate1, 0.0))))

    iota_e = lax.broadcasted_iota(jnp.int32, (N_EXPERTS, rows), 0).astype(F32)
    onehot = (iota_e == e0).astype(F32) + (iota_e == e1).astype(F32)
    lane = lax.broadcasted_iota(jnp.int32, (N_EXPERTS, LANES), 1)
    cnt = jnp.zeros((N_EXPERTS, LANES), F32)
    for s in range(rows // TOKEN_TILE):
        c_s = jnp.sum(onehot[:, s * TOKEN_TILE:(s + 1) * TOKEN_TILE], axis=1, keepdims=True)
        cnt = cnt + jnp.where(lane == s, c_s, 0.0)
    cnt_ref[...] = cnt.reshape(1, N_EXPERTS, LANES)
    conv_ref[...] = xs_ref[:, SUBLANES + tt - 3:SUBLANES + tt, :]


def _mixer_call(x, hist, h0, wts, *, n_total, row_offset, sb, tt, chunk, seg_len, seg_stride, chain,
                emit_vn, alias_in=None):
    nb, t_len, _ = x.shape
    nbb = nb // sb
    ntt = t_len // tt
    rows = sb * tt
    blk0 = row_offset // rows
    n_seg = rows // seg_len

    def full(arr):
        nd = arr.ndim
        return pl.BlockSpec(arr.shape, lambda b, t, _nd=nd: (0,) * _nd)

    in_specs = [
        pl.BlockSpec((sb, tt, D_MODEL), lambda b, t: (b, t, 0)),
        pl.BlockSpec((sb, CONV_WIDTH - 1, D_B), lambda b, t: (b, 0, 0)),
        pl.BlockSpec((1, sb, D_B), lambda b, t: (b, 0, 0)),
    ] + [full(w) for w in wts]
    args = [x, hist, h0] + list(wts)
    io_alias = {}
    if alias_in is not None:
        for k, arr in enumerate(alias_in):
            io_alias[len(args)] = k
            in_specs.append(pl.BlockSpec(memory_space=pl.ANY))
            args.append(arr)

    out_shape = [
        jax.ShapeDtypeStruct((n_total, D_MODEL), F32),
        jax.ShapeDtypeStruct((n_total, D_MODEL), BF16),
        jax.ShapeDtypeStruct((SUBLANES, n_total), F32),
        jax.ShapeDtypeStruct((nbb * ntt, N_EXPERTS, LANES), F32),
        jax.ShapeDtypeStruct((nb, CONV_WIDTH - 1, D_B), F32),
        jax.ShapeDtypeStruct((nbb, sb, D_B), F32),
    ]
    out_specs = [
        pl.BlockSpec((rows, D_MODEL), lambda b, t: (blk0 + b * ntt + t, 0)),
        pl.BlockSpec((rows, D_MODEL), lambda b, t: (blk0 + b * ntt + t, 0)),
        pl.BlockSpec((SUBLANES, rows), lambda b, t: (0, blk0 + b * ntt + t)),
        pl.BlockSpec((1, N_EXPERTS, LANES), lambda b, t: (b * ntt + t, 0, 0)),
        pl.BlockSpec((sb, CONV_WIDTH - 1, D_B), lambda b, t: (b, 0, 0)),
        pl.BlockSpec((1, sb, D_B), lambda b, t: (b, 0, 0)),
    ]
    if emit_vn:
        out_shape.append(jax.ShapeDtypeStruct((nb, t_len, D_A), F32))
        out_specs.append(pl.BlockSpec((sb, tt, D_A), lambda b, t: (b, t, 0)))

    kern = functools.partial(_mixer_kernel, sb=sb, tt=tt, chunk=chunk, seg_len=seg_len, seg_stride=seg_stride,
                             chain=chain, emit_vn=emit_vn, aliased=alias_in is not None)
    return pl.pallas_call(
        kern,
        grid=(nbb, ntt),
        in_specs=in_specs,
        out_specs=out_specs,
        out_shape=out_shape,
        scratch_shapes=[
            pltpu.VMEM((sb, SUBLANES + tt, D_B), F32),
            pltpu.VMEM((D_B // LANES, n_seg * seg_stride, LANES), F32),
            pltpu.VMEM((D_B // LANES, n_seg * seg_stride, LANES), F32),
            pltpu.VMEM((rows, D_MODEL), BF16),
            pltpu.VMEM((1, D_B), F32),
        ],
        input_output_aliases=io_alias,
        compiler_params=pltpu.CompilerParams(dimension_semantics=("arbitrary", "arbitrary"),
                                             vmem_limit_bytes=VMEM_LIMIT),
        name="mixer_chain" if chain else "mixer_step",
    )(*args)


def _tile_slots(route_ref, locoff_ref, upper_ref, u):
    lanes = slice(u * TOKEN_TILE, (u + 1) * TOKEN_TILE)
    e0 = route_ref[0:1, lanes]
    e1 = route_ref[1:2, lanes]
    iota_e = lax.broadcasted_iota(jnp.int32, (N_EXPERTS, TOKEN_TILE), 0).astype(F32)
    oh0 = (iota_e == e0).astype(F32)
    oh1 = (iota_e == e1).astype(F32)
    c0 = _dot(oh0.astype(BF16), upper_ref[...])
    c1 = _dot(oh1.astype(BF16), upper_ref[...])
    cnt0 = jnp.sum(oh0, axis=1, keepdims=True)
    base0 = locoff_ref[u]
    base1 = base0 + cnt0
    p0 = jnp.sum(oh0 * (base0 + c0), axis=0, keepdims=True)
    p1 = jnp.sum(oh1 * (base1 + c1), axis=0, keepdims=True)
    return p0.astype(jnp.int32), p1.astype(jnp.int32)


def _by_fill(ntot_ref, step, body):
    most = ntot_ref[step * TILES_PER_STEP]
    for u in range(1, TILES_PER_STEP):
        most = jnp.maximum(most, ntot_ref[step * TILES_PER_STEP + u])
    fits = most * SUBLANES <= SLOT_ROWS_COMMON

    @pl.when(fits)
    def _():
        body(SLOT_ROWS_COMMON)

    @pl.when(jnp.logical_not(fits))
    def _():
        body(SLOT_ROWS)


def _copy_ops(tile, n_tiles, nops_ref, oloc_ref, odst_ref, make_copy):
    base = 0
    for k, (chunks, cap) in enumerate(COPY_CLASSES):
        n = nops_ref[k * n_tiles + tile]

        def per_pair(i, _, chunks=chunks, cap=cap, base=base, n=n):
            idx = base + tile * cap + 2 * i
            make_copy(pl.multiple_of(oloc_ref[idx], SUBLANES), pl.multiple_of(odst_ref[idx], SUBLANES),
                      chunks * SUBLANES)

            @pl.when(2 * i + 1 < n)
            def _():
                make_copy(pl.multiple_of(oloc_ref[idx + 1], SUBLANES), pl.multiple_of(odst_ref[idx + 1], SUBLANES),
                          chunks * SUBLANES)
            return 0

        assert cap % 2 == 0
        lax.fori_loop(0, (n + 1) // 2, per_pair, 0)
        base += n_tiles * cap


def _dispatch_kernel(nops_ref, oloc_ref, odst_ref, ntot_ref, tail_ref, ntail_ref,
                     h2_ref, route_ref, locoff_ref, upper_ref, xb_ref, slots_ref, buf_ref, zero_ref, sem, zsem,
                     *, n_tiles):
    i = pl.program_id(0)
    n_steps = pl.num_programs(0)
    slot = i % 2

    def chunk_copy(s, u, loc, dst, rows=SUBLANES):
        return pltpu.make_async_copy(buf_ref.at[s, u, pl.ds(loc, rows), :],
                                     xb_ref.at[pl.ds(dst, rows), :], sem.at[s])

    def wait_step(step, s):
        for u in range(TILES_PER_STEP):
            n_rows = ntot_ref[step * TILES_PER_STEP + u] * SUBLANES

            @pl.when(n_rows > 0)
            def _():
                pltpu.make_async_copy(buf_ref.at[s, u, pl.ds(0, n_rows), :], xb_ref.at[pl.ds(0, n_rows), :],
                                      sem.at[s]).wait()

    @pl.when(i >= 2)
    def _():
        wait_step(i - 2, slot)

    def sort_tiles(n_rows):
        iota_s = lax.broadcasted_iota(jnp.int32, (n_rows, TOKEN_TILE), 0)
        iota8 = lax.broadcasted_iota(jnp.int32, (SUBLANES, TOKEN_TILE), 0)
        for u in range(TILES_PER_STEP):
            lanes = slice(u * TOKEN_TILE, (u + 1) * TOKEN_TILE)
            p0, p1 = _tile_slots(route_ref, locoff_ref, upper_ref, u)
            slots_ref[:, lanes] = jnp.where(iota8 == 0, p0, jnp.where(iota8 == 1, p1, 0))
            hit0 = iota_s == p0
            hit1 = iota_s == p1
            perm = jnp.where(hit0 | hit1, 1.0, 0.0).astype(BF16)
            sorted_rows = _dot(perm, h2_ref[u * TOKEN_TILE:(u + 1) * TOKEN_TILE, :])
            buf_ref[slot, u, 0:n_rows, 0:PACK_COLS] = _pack_bf16_pairs(sorted_rows)
            gates = jnp.where(hit0, route_ref[2:3, lanes], 0.0) + jnp.where(hit1, route_ref[3:4, lanes], 0.0)
            gcol = jnp.sum(gates, axis=1, keepdims=True)
            buf_ref[slot, u, 0:n_rows, PACK_COLS:XB_COLS] = lax.bitcast_convert_type(
                jnp.broadcast_to(gcol, (n_rows, LANES)), U32)

    _by_fill(ntot_ref, i, sort_tiles)

    for u in range(TILES_PER_STEP):
        _copy_ops(i * TILES_PER_STEP + u, n_tiles, nops_ref, oloc_ref, odst_ref,
                  lambda loc, dst, rows, u=u: chunk_copy(slot, u, loc, dst, rows).start())

    @pl.when(i == n_steps - 1)
    def _():
        zero_ref[...] = jnp.zeros_like(zero_ref)

        def zero_copy(dst, rows, k):
            return pltpu.make_async_copy(zero_ref.at[pl.ds(0, rows), :], xb_ref.at[pl.ds(dst, rows), :], zsem.at[k])

        def per_expert(e, tot):
            n_big, n_small = tot
            n = ntail_ref[e]
            d0 = tail_ref[e]
            nb = n // ZERO_CHUNKS
            ns = n - nb * ZERO_CHUNKS

            def big(c, _):
                zero_copy(pl.multiple_of(d0 + c * ZERO_ROWS, SUBLANES), ZERO_ROWS, 0).start()
                return 0

            def small(c, _):
                zero_copy(pl.multiple_of(d0 + nb * ZERO_ROWS + c * SUBLANES, SUBLANES), SUBLANES, 1).start()
                return 0

            lax.fori_loop(0, nb, big, 0)
            lax.fori_loop(0, ns, small, 0)
            return n_big + nb, n_small + ns

        n_big, n_small = lax.fori_loop(0, N_EXPERTS, per_expert, (0, 0))

        @pl.when(i >= 1)
        def _():
            wait_step(i - 1, 1 - slot)

        wait_step(i, slot)

        def wait_big(c, _):
            zero_copy(0, ZERO_ROWS, 0).wait()
            return 0

        def wait_small(c, _):
            zero_copy(0, SUBLANES, 1).wait()
            return 0

        lax.fori_loop(0, n_big, wait_big, 0)
        lax.fori_loop(0, n_small, wait_small, 0)


def _dispatch_call(h2, route, locoff, upper, ops, ntot, tail, ntail, *, n_tiles, p_rows):
    grid_spec = pltpu.PrefetchScalarGridSpec(
        num_scalar_prefetch=6,
        grid=(n_tiles // TILES_PER_STEP,),
        in_specs=[
            pl.BlockSpec((STEP_TOKENS, D_MODEL), lambda i, *_: (i, 0)),
            pl.BlockSpec((SUBLANES, STEP_TOKENS), lambda i, *_: (0, i)),
            pl.BlockSpec((TILES_PER_STEP, N_EXPERTS, 1), lambda i, *_: (i, 0, 0)),
            pl.BlockSpec((TOKEN_TILE, TOKEN_TILE), lambda i, *_: (0, 0)),
        ],
        out_specs=[pl.BlockSpec(memory_space=pl.ANY),
                   pl.BlockSpec((SUBLANES, STEP_TOKENS), lambda i, *_: (0, i))],
        scratch_shapes=[
            pltpu.VMEM((2, TILES_PER_STEP, SLOT_ROWS, XB_COLS), U32),
            pltpu.VMEM((ZERO_ROWS, XB_COLS), U32),
            pltpu.SemaphoreType.DMA((2,)),
            pltpu.SemaphoreType.DMA((2,)),
        ],
    )
    return pl.pallas_call(
        functools.partial(_dispatch_kernel, n_tiles=n_tiles),
        grid_spec=grid_spec,
        out_shape=[jax.ShapeDtypeStruct((p_rows, XB_COLS), U32),
                   jax.ShapeDtypeStruct((SUBLANES, n_tiles * TOKEN_TILE), jnp.int32)],
        compiler_params=pltpu.CompilerParams(dimension_semantics=("arbitrary",), vmem_limit_bytes=VMEM_LIMIT),
        name="dispatch",
    )(*ops, ntot, tail, ntail, h2, route, locoff, upper)


def _ffn_kernel(nitems_ref, istart_ref, iunits_ref, iexp_ref, ifirst_ref, iwslot_ref, inext_ref,
                xb_ref, wg_ref, wu_ref, wd_ref, yb_ref,
                xin_ref, yout_ref, wgf_ref, wuf_ref, wdf_ref, wgb_ref, wub_ref, wdb_ref, sem_in, sem_out, sem_w):
    n_items = nitems_ref[0]
    sizes = tuple(k * FFN_UNIT for k in range(FFN_ITEM_UNITS, 0, -1))

    def rows_in(t, s, rows):
        start = pl.multiple_of(istart_ref[t], FFN_UNIT)
        return pltpu.make_async_copy(xb_ref.at[pl.ds(start, rows), :], xin_ref.at[s, pl.ds(0, rows), :], sem_in.at[s])

    def rows_out(t, s, rows):
        start = pl.multiple_of(istart_ref[t], FFN_UNIT)
        return pltpu.make_async_copy(yout_ref.at[s, pl.ds(0, rows), :], yb_ref.at[pl.ds(start, rows), :],
                                     sem_out.at[s])

    def start_by_size(make, t, s):
        for rows in sizes:
            @pl.when(iunits_ref[t] * FFN_UNIT == rows)
            def _():
                make(t, s, rows).start()

    def weight_copies(e, ws):
        return (pltpu.make_async_copy(wg_ref.at[e], wgf_ref.at[ws], sem_w.at[ws]),
                pltpu.make_async_copy(wu_ref.at[e], wuf_ref.at[ws], sem_w.at[ws]),
                pltpu.make_async_copy(wd_ref.at[e], wdf_ref.at[ws], sem_w.at[ws]))

    def compute(s, rows):
        xb = _unpack_bf16_pairs(xin_ref[s, 0:rows, 0:PACK_COLS])
        gate = lax.bitcast_convert_type(xin_ref[s, 0:rows, PACK_COLS:PACK_COLS + 1], F32)
        a = _dot(xb, wgb_ref[...])
        u = _dot(xb, wub_ref[...])
        mid = (a * _sigmoid(a) * u).astype(BF16)
        y = _dot(mid, wdb_ref[...]) * gate
        yout_ref[s, 0:rows, :] = _pack_bf16_pairs(y.astype(BF16).astype(F32))

    @pl.when(n_items > 0)
    def _():
        for c in weight_copies(iexp_ref[0], 0):
            c.start(priority=1)
        start_by_size(rows_in, 0, 0)

    def per_item(t, _):
        s = t % 2

        @pl.when(t + 1 < n_items)
        def _():
            start_by_size(rows_in, t + 1, 1 - s)

        @pl.when(ifirst_ref[t] == 1)
        def _():
            ws = iwslot_ref[t]
            for c in weight_copies(0, ws):
                c.wait()

            @pl.when(inext_ref[t] >= 0)
            def _():
                for c in weight_copies(inext_ref[t], 1 - ws):
                    c.start(priority=1)

            wgb_ref[...] = wgf_ref[ws].astype(BF16)
            wub_ref[...] = wuf_ref[ws].astype(BF16)
            wdb_ref[...] = wdf_ref[ws].astype(BF16)

        rows_in(t, s, iunits_ref[t] * FFN_UNIT).wait()

        @pl.when(t >= 2)
        def _():
            rows_out(t - 2, s, iunits_ref[t - 2] * FFN_UNIT).wait()

        for rows in sizes:
            @pl.when(iunits_ref[t] * FFN_UNIT == rows)
            def _():
                compute(s, rows)

        start_by_size(rows_out, t, s)
        return 0

    lax.fori_loop(0, n_items, per_item, 0)

    for back in (2, 1):
        @pl.when(n_items >= back)
        def _():
            t = n_items - back
            rows_out(t, t % 2, iunits_ref[t] * FFN_UNIT).wait()


def _ffn_call(xb, wg, wu, wd, items, *, p_rows):
    any_spec = pl.BlockSpec(memory_space=pl.ANY)
    grid_spec = pltpu.PrefetchScalarGridSpec(
        num_scalar_prefetch=len(items),
        grid=(1,),
        in_specs=[any_spec, any_spec, any_spec, any_spec],
        out_specs=any_spec,
        scratch_shapes=[
            pltpu.VMEM((2, FFN_BLOCK, XB_COLS), U32),
            pltpu.VMEM((2, FFN_BLOCK, PACK_COLS), U32),
            pltpu.VMEM((2, D_MODEL, D_EXPERT), F32),
            pltpu.VMEM((2, D_MODEL, D_EXPERT), F32),
            pltpu.VMEM((2, D_EXPERT, D_MODEL), F32),
            pltpu.VMEM((D_MODEL, D_EXPERT), BF16),
            pltpu.VMEM((D_MODEL, D_EXPERT), BF16),
            pltpu.VMEM((D_EXPERT, D_MODEL), BF16),
            pltpu.SemaphoreType.DMA((2,)),
            pltpu.SemaphoreType.DMA((2,)),
            pltpu.SemaphoreType.DMA((2,)),
        ],
    )
    return pl.pallas_call(
        _ffn_kernel,
        grid_spec=grid_spec,
        out_shape=jax.ShapeDtypeStruct((p_rows, PACK_COLS), U32),
        compiler_params=pltpu.CompilerParams(dimension_semantics=("arbitrary",), vmem_limit_bytes=VMEM_LIMIT),
        name="expert_ffn",
    )(*items, xb, wg, wu, wd)


def _combine_kernel(nops_ref, oloc_ref, odst_ref, ntot_ref, y1_ref, slots_ref, fg_ref,
                    yb_ref, yp_ref, ys_ref, buf_ref, sem, *, n_tiles):
    i = pl.program_id(0)
    n_steps = pl.num_programs(0)
    slot = i % 2

    def chunk_copy(s, u, loc, src, rows):
        return pltpu.make_async_copy(yb_ref.at[pl.ds(src, rows), :],
                                     buf_ref.at[s, u, pl.ds(loc, rows), :], sem.at[s])

    def issue_step(step, s):
        for u in range(TILES_PER_STEP):
            _copy_ops(step * TILES_PER_STEP + u, n_tiles, nops_ref, oloc_ref, odst_ref,
                      lambda loc, src, rows, u=u: chunk_copy(s, u, loc, src, rows).start())

    @pl.when(i == 0)
    def _():
        buf_ref[...] = jnp.zeros_like(buf_ref)
        issue_step(0, 0)

    @pl.when(i + 1 < n_steps)
    def _():
        issue_step(i + 1, 1 - slot)

    def unsort_tiles(n_slots):
        iota_s = lax.broadcasted_iota(jnp.int32, (n_slots, TOKEN_TILE), 0)
        perms = []
        for u in range(TILES_PER_STEP):
            lanes = slice(u * TOKEN_TILE, (u + 1) * TOKEN_TILE)
            p0 = slots_ref[0:1, lanes]
            p1 = slots_ref[1:2, lanes]
            perms.append(jnp.where((iota_s == p0) | (iota_s == p1), 1.0, 0.0).astype(BF16))

        for u in range(TILES_PER_STEP):
            n_rows = ntot_ref[i * TILES_PER_STEP + u] * SUBLANES

            @pl.when(n_rows > 0)
            def _():
                pltpu.make_async_copy(yb_ref.at[pl.ds(0, n_rows), :], buf_ref.at[slot, u, pl.ds(0, n_rows), :],
                                      sem.at[slot]).wait()

        for u in range(TILES_PER_STEP):
            rs = slice(u * TOKEN_TILE, (u + 1) * TOKEN_TILE)
            yb = _unpack_bf16_pairs(buf_ref[slot, u, 0:n_slots, :])
            moe = lax.dot_general(perms[u], yb, (((0,), (0,)), ((), ())), preferred_element_type=F32)
            y_tile = _rms(y1_ref[rs, :] + moe, fg_ref[...])
            yp_ref[rs, :] = y_tile

        @pl.when(i == n_steps - 1)
        def _():
            ys_ref[...] = y_tile

    _by_fill(ntot_ref, i, unsort_tiles)


def _combine_call(y1, slots, fg, yb, ops, ntot, *, n_tiles, n_prompt, n_sample):
    n_steps = n_tiles // TILES_PER_STEP
    assert n_sample == TOKEN_TILE and n_prompt + n_sample == n_steps * STEP_TOKENS
    grid_spec = pltpu.PrefetchScalarGridSpec(
        num_scalar_prefetch=4,
        grid=(n_steps,),
        in_specs=[
            pl.BlockSpec((STEP_TOKENS, D_MODEL), lambda i, *_: (i, 0)),
            pl.BlockSpec((SUBLANES, STEP_TOKENS), lambda i, *_: (0, i)),
            pl.BlockSpec((1, D_MODEL), lambda i, *_: (0, 0)),
            pl.BlockSpec(memory_space=pl.ANY),
        ],
        out_specs=[
            pl.BlockSpec((STEP_TOKENS, D_MODEL), lambda i, *_: (i, 0)),
            pl.BlockSpec((n_sample, D_MODEL), lambda i, *_: (0, 0)),
        ],
        scratch_shapes=[
            pltpu.VMEM((2, TILES_PER_STEP, SLOT_ROWS, PACK_COLS), U32),
            pltpu.SemaphoreType.DMA((2,)),
        ],
    )
    return pl.pallas_call(
        functools.partial(_combine_kernel, n_tiles=n_tiles),
        grid_spec=grid_spec,
        out_shape=[jax.ShapeDtypeStruct((n_prompt, D_MODEL), F32), jax.ShapeDtypeStruct((n_sample, D_MODEL), F32)],
        compiler_params=pltpu.CompilerParams(dimension_semantics=("arbitrary",), vmem_limit_bytes=VMEM_LIMIT),
        name="combine",
    )(*ops, ntot, y1, slots, fg, yb)


def _block_diag(w):
    h, d, _ = w.shape
    eye = jnp.eye(h, dtype=w.dtype)
    return (eye[:, None, :, None] * w[:, :, None, :]).reshape(h * d, h * d)


def _head_blocks(w):
    half = B_HEADS // 2
    return jnp.stack([_block_diag(w[:half]), _block_diag(w[half:])]).astype(BF16)


def _mixer_weights(l, length, reps, norm1_g, w_in, gmlp_ln_g, gmlp_ln_b, gmlp_w_s, gmlp_b_s, conv_w, conv_b,
                   lru_w_a, lru_b_a, lru_w_x, lru_b_x, lru_lambda, w_out, norm2_g,
                   router_group_w, router_group_b, router_expert_w, router_expert_b):
    mask = jnp.tril(jnp.ones((length, length), dtype=bool))
    ws = jnp.where(mask, gmlp_w_s[l][:, :length, :length], 0.0)
    wsm = jnp.concatenate([jnp.pad(ws, ((0, 0), (0, 0), (j * length, (reps - 1 - j) * length)))
                           for j in range(reps)], axis=1)
    bs = jnp.tile(gmlp_b_s[l][:, :length], (1, reps))
    bsx = jnp.repeat(bs.T, A_GROUP_DIM, axis=1)
    wr = jnp.concatenate([router_expert_w[l].T, router_group_w[l].T,
                          jnp.zeros((ROUTER_ROWS - N_EXPERTS - N_GROUPS, D_MODEL), F32)], axis=0)
    br = jnp.concatenate([router_expert_b[l], router_group_b[l],
                          jnp.zeros((ROUTER_ROWS - N_EXPERTS - N_GROUPS,), F32)]).reshape(ROUTER_ROWS, 1)
    return [
        norm1_g[l].reshape(1, D_MODEL), w_in[l].astype(BF16),
        gmlp_ln_g[l].reshape(1, D_A), gmlp_ln_b[l].reshape(1, D_A), wsm.astype(BF16), bsx,
        conv_w[l], conv_b[l].reshape(1, D_B),
        _head_blocks(lru_w_a[l]), lru_b_a[l].reshape(1, D_B),
        _head_blocks(lru_w_x[l]), lru_b_x[l].reshape(1, D_B),
        lru_lambda[l].reshape(1, D_B), w_out[l].astype(BF16), norm2_g[l].reshape(1, D_MODEL),
        wr.astype(BF16), br,
    ]


def _routing_tables(cnt, n_items_max):
    n_tiles = cnt.shape[0]
    seg = (cnt + SUBLANES - 1) // SUBLANES * SUBLANES
    nch = seg // SUBLANES
    tot = jnp.sum(seg, axis=0)
    padded = (tot + FFN_UNIT - 1) // FFN_UNIT * FFN_UNIT
    e_before = jnp.arange(N_EXPERTS)[None, :] < jnp.arange(N_EXPERTS)[:, None]
    t_before = jnp.arange(n_tiles)[None, :] < jnp.arange(n_tiles)[:, None]
    pad_start = jnp.sum(jnp.where(e_before, padded[None, :], 0), axis=1)
    doff = pad_start[None, :] + jnp.sum(jnp.where(t_before[:, :, None], seg[None, :, :], 0), axis=1)
    locoff = jnp.sum(jnp.where(e_before[None, :, :], seg[:, None, :], 0), axis=2)
    ntot = jnp.sum(nch, axis=1)
    big = COPY_CLASSES[0][0]
    sizes = jnp.array([chunks for chunks, _ in COPY_CLASSES], jnp.int32)
    whole = nch // big
    done = whole * big * SUBLANES
    rest = nch - whole * big
    is_big = (sizes == big)[:, None, None]
    m = jnp.where(is_big, whole[None], (rest[None] == sizes[:, None, None]).astype(jnp.int32))
    loc0 = jnp.where(is_big, locoff[None], (locoff + done)[None])
    dst0 = jnp.where(is_big, doff[None], (doff + done)[None])
    start = jnp.sum(jnp.where(e_before[None, None], m[:, :, None, :], 0), axis=3)
    s = jnp.arange(COPY_SLOTS)
    owner = jnp.sum((start + m)[:, :, None, :] <= s[None, None, :, None], axis=3)
    own = owner[..., None] == jnp.arange(N_EXPERTS)
    step = (sizes * SUBLANES)[:, None, None, None] * (s[None, None, :, None] - start[:, :, None, :])
    olocs = jnp.sum(jnp.where(own, loc0[:, :, None, :] + step, 0), axis=3)
    odsts = jnp.sum(jnp.where(own, dst0[:, :, None, :] + step, 0), axis=3)
    nops = jnp.sum(m, axis=2)
    tail = pad_start + tot
    ntail = (padded - tot) // SUBLANES
    units = padded // FFN_UNIT
    n_it = (units + FFN_ITEM_UNITS - 1) // FFN_ITEM_UNITS
    it_start = jnp.sum(jnp.where(e_before, n_it[None, :], 0), axis=1)
    t = jnp.arange(n_items_max)
    e_ids = jnp.arange(N_EXPERTS)
    owner = jnp.minimum(jnp.sum((it_start + n_it)[None, :] <= t[:, None], axis=1), N_EXPERTS - 1)
    own = owner[:, None] == e_ids[None, :]
    pick = lambda v: jnp.sum(jnp.where(own, v[None, :], 0), axis=1)
    j = t - pick(it_start)
    istart = pick(pad_start) + j * FFN_BLOCK
    iunits = jnp.clip(pick(units) - FFN_ITEM_UNITS * j, 1, FFN_ITEM_UNITS)
    ordinal = jnp.sum(jnp.where(e_before, (n_it > 0).astype(jnp.int32)[None, :], 0), axis=1)
    later = (e_ids[None, :] > e_ids[:, None]) & (n_it > 0)[None, :]
    nxt = jnp.min(jnp.where(later, e_ids[None, :], N_EXPERTS), axis=1)
    nxt = jnp.where(nxt == N_EXPERTS, -1, nxt)
    i32 = lambda a: a.astype(jnp.int32)
    items = (i32(jnp.sum(n_it).reshape(1)), i32(istart), i32(iunits), i32(owner), i32(j == 0),
             i32(pick(ordinal) % 2), i32(pick(nxt)))
    ops = (i32(nops.reshape(-1)), i32(olocs.reshape(-1)), i32(odsts.reshape(-1)))
    return ops, i32(ntot), i32(tail), i32(ntail), locoff.astype(F32)[:, :, None], items


def kernel(x_prompt, x_sample, state_conv, state_rglru, norm1_g, w_in, gmlp_ln_g, gmlp_ln_b, gmlp_w_s, gmlp_b_s,
           conv_w, conv_b, lru_w_a, lru_b_a, lru_w_x, lru_b_x, lru_lambda, w_out, norm2_g,
           router_group_w, router_group_b, router_expert_w, router_expert_b,
           expert_w_gate, expert_w_up, expert_w_down, final_norm_g):
    depth = w_in.shape[0]
    nb, t_len, _ = x_prompt.shape
    db, dt, _ = x_sample.shape
    n_prompt = nb * t_len
    n_sample = db * dt
    assert t_len % PROMPT_TILE == 0 and n_prompt % TOKEN_TILE == 0 and n_sample == TOKEN_TILE
    n_total = n_prompt + n_sample
    assert n_total % STEP_TOKENS == 0
    n_tiles = n_total // TOKEN_TILE
    p_tiles = n_prompt // TOKEN_TILE
    p_rows = 2 * n_total + (SUBLANES - 1) * N_EXPERTS * n_tiles + N_EXPERTS * (FFN_UNIT - 1)
    p_rows = -(-p_rows // FFN_UNIT) * FFN_UNIT
    n_items_max = p_rows // FFN_BLOCK + N_EXPERTS
    upper = jnp.triu(jnp.ones((TOKEN_TILE, TOKEN_TILE), BF16), 1)

    xp, xs = x_prompt, x_sample
    conv_p, h_p, v_s, conv_s, h_s = [], [], [], [], []
    for l in range(depth):
        lw = (norm1_g, w_in, gmlp_ln_g, gmlp_ln_b, gmlp_w_s, gmlp_b_s, conv_w, conv_b, lru_w_a, lru_b_a,
              lru_w_x, lru_b_x, lru_lambda, w_out, norm2_g, router_group_w, router_group_b,
              router_expert_w, router_expert_b)
        wts_p = _mixer_weights(l, GMLP_CHUNK, 1, *lw)
        wts_s = _mixer_weights(l, dt, db, *lw)
        zero_conv = jnp.zeros((nb, CONV_WIDTH - 1, D_B), F32)
        zero_h = jnp.zeros((nb, 1, D_B), F32)
        y1, h2, route, cnt_p, cp, hp = _mixer_call(
            xp, zero_conv, zero_h, wts_p, n_total=n_total, row_offset=0, sb=1, tt=PROMPT_TILE, chunk=GMLP_CHUNK, seg_len=PROMPT_SEG, seg_stride=PROMPT_SEG + SUBLANES, chain=True, emit_vn=False)
        y1, h2, route, cnt_s, cs, hs, vs = _mixer_call(
            xs, state_conv[l], state_rglru[l][None], wts_s, n_total=n_total, row_offset=n_prompt, sb=db,
            tt=dt, chunk=db * dt, seg_len=dt, seg_stride=dt + SUBLANES, chain=False, emit_vn=True,
            alias_in=(y1, h2, route))
        conv_p.append(cp)
        h_p.append(hp[:, 0])
        v_s.append(vs)
        conv_s.append(cs)
        h_s.append(hs[0])

        sub = PROMPT_TILE // TOKEN_TILE
        cnt = jnp.concatenate([
            jnp.swapaxes(cnt_p[:, :, :sub], 1, 2).reshape(p_tiles, N_EXPERTS),
            cnt_s[:, :, 0]], axis=0).astype(jnp.int32)
        ops, ntot, tail, ntail, locoff, items = _routing_tables(cnt, n_items_max)

        xb, slots = _dispatch_call(h2, route, locoff, upper, ops, ntot, tail, ntail, n_tiles=n_tiles,
                                   p_rows=p_rows)
        yb = _ffn_call(xb, expert_w_gate[l], expert_w_up[l], expert_w_down[l], items, p_rows=p_rows)
        assert l == depth - 1, "deeper stacks need an un-normalised combine between layers"
        fg = final_norm_g.reshape(1, D_MODEL)
        yp, ysm = _combine_call(y1, slots, fg, yb, ops, ntot, n_tiles=n_tiles, n_prompt=n_prompt,
                                n_sample=n_sample)
        xp = yp.reshape(nb, t_len, D_MODEL)
        xs = ysm.reshape(db, dt, D_MODEL)

    return (xp, xs, jnp.stack(conv_p), jnp.stack(h_p), jnp.stack(v_s), jnp.stack(conv_s), jnp.stack(h_s))
```
